```python
import math
import jax, jax.numpy as jnp
from jax import lax
import numpy as np

D_MODEL = 1024
BATCH = 8
SEQ = 2048
DEPTH = 2

CHUNK = 64
Q_BLOCK = 128
HEAD_DIM = 64
MEM_LEN = 256
MEM_HEADS = 4
MEM_W = MEM_HEADS * HEAD_DIM
MIX_W = D_MODEL
BR_W = MIX_W - MEM_W
A_HEADS = BR_W // HEAD_DIM
B_HEADS = BR_W // (2 * HEAD_DIM)
LORA_W = 64
LORA_A = 64
N_A = DEPTH // 2
N_B = DEPTH - N_A
A_SHIFT = 3 * BR_W + LORA_W + LORA_A
A_IN = A_SHIFT + BR_W + 2 * MEM_W
B_IN = 2 * BR_W + 2 * MEM_W
NORM_EPS = 1e-6
LNX_EPS = 64e-5

kernel_name = 'hybrid_rwkv7_diffattn_yoco'


def rms_norm(x, g):
    xf = x.astype(jnp.float32)
    y = xf * lax.rsqrt(jnp.mean(xf * xf, axis=-1, keepdims=True) + NORM_EPS)
    return (y * g.astype(jnp.float32)).astype(x.dtype)


def token_shift(z):
    return jnp.pad(z, ((0, 0), (1, 0), (0, 0)))[:, :-1]


def alibi_slopes(n):
    def pow2(m):
        start = 2.0 ** (-8.0 / m)
        return [start ** (i + 1) for i in range(m)]
    if math.log2(n).is_integer():
        s = pow2(n)
    else:
        c = 2 ** int(math.floor(math.log2(n)))
        s = pow2(c) + pow2(2 * c)[0::2][: n - c]
    return jnp.asarray(np.array(s, dtype=np.float32))


def rwkv7_time_mix(z, mu, w0, w2, a0, a2, k_k, k_a, r_k, lnx_w, lnx_b):
    B, S, _ = z.shape
    z = z.astype(jnp.float32)
    z = z + (token_shift(z) - z) * mu
    r, k, v, wd, ad = jnp.split(z, [BR_W, 2 * BR_W, 3 * BR_W, 3 * BR_W + LORA_W], axis=-1)
    w = -jax.nn.softplus(-(w0 + jnp.tanh(wd) @ w2)) - 0.5
    decay = jnp.exp(-jnp.exp(w))
    a = jax.nn.sigmoid(a0 + ad @ a2)
    heads = lambda t: t.reshape(B, S, A_HEADS, HEAD_DIM)
    kk = heads(k * k_k)
    kk = kk * lax.rsqrt(jnp.maximum(jnp.sum(kk * kk, axis=-1, keepdims=True), 1e-12))
    k = k * (1.0 + (a - 1.0) * k_a)
    r, k, v, decay, a = heads(r), heads(k), heads(v), heads(decay), heads(a)

    def step(state, inp):
        r_t, w_t, k_t, v_t, kk_t, a_t = inp
        sa = jnp.einsum('bhvk,bhk->bhv', state, -kk_t)
        state = (state * w_t[:, :, None, :]
                 + sa[..., None] * (kk_t * a_t)[:, :, None, :]
                 + v_t[..., None] * k_t[:, :, None, :])
        y_t = jnp.einsum('bhvk,bhk->bhv', state, r_t)
        return state, y_t

    tm = lambda t: jnp.moveaxis(t, 1, 0)
    state0 = jnp.zeros((B, A_HEADS, HEAD_DIM, HEAD_DIM), jnp.float32)
    _, y = lax.scan(step, state0, (tm(r), tm(decay), tm(k), tm(v), tm(kk), tm(a)))
    y = jnp.moveaxis(y, 0, 1)
    mean = jnp.mean(y, axis=-1, keepdims=True)
    var = jnp.mean(jnp.square(y - mean), axis=-1, keepdims=True)
    y = ((y - mean) * lax.rsqrt(var + LNX_EPS)).reshape(B, S, BR_W) * lnx_w + lnx_b
    bonus = jnp.sum(r * k * r_k, axis=-1, keepdims=True) * v
    return y + bonus.reshape(B, S, BR_W)


def diff_attention(q, k_sh, v_sh, lq1, lk1, lq2, lk2, subln, lam_init):
    B, S, _ = q.shape
    q = q.reshape(B, S, B_HEADS, 2, HEAD_DIM) * (HEAD_DIM ** -0.5)
    k = k_sh.reshape(B, S, B_HEADS, 2, HEAD_DIM)
    v = v_sh.reshape(B, S, B_HEADS, 2 * HEAD_DIM)
    lq1f, lk1f = lq1.astype(jnp.float32), lk1.astype(jnp.float32)
    lq2f, lk2f = lq2.astype(jnp.float32), lk2.astype(jnp.float32)
    lam = jnp.exp(jnp.sum(lq1f * lk1f)) - jnp.exp(jnp.sum(lq2f * lk2f)) + lam_init
    slopes = alibi_slopes(B_HEADS)
    pos = jnp.arange(S)
    nb = S // Q_BLOCK
    qb = jnp.moveaxis(q.reshape(B, nb, Q_BLOCK, B_HEADS, 2, HEAD_DIM), 1, 0)

    def block(args):
        q_blk, i = args
        t = i * Q_BLOCK + jnp.arange(Q_BLOCK)
        s = jnp.einsum('bqhcd,bkhcd->bhcqk', q_blk, k, preferred_element_type=jnp.float32)
        dist = jnp.abs(t[:, None] - pos[None, :]).astype(jnp.float32)
        bias = -slopes[:, None, None] * dist
        allowed = (pos[None, :] // CHUNK) <= (t[:, None] // CHUNK)
        s = jnp.where(allowed, s + bias[None, :, None], -jnp.inf)
        p = jax.nn.softmax(s, axis=-1)
        attn = (p[:, :, 0] - lam * p[:, :, 1]).astype(v.dtype)
        return jnp.einsum('bhqk,bkhe->bqhe', attn, v)

    o = lax.map(block, (qb, jnp.arange(nb)))
    o = jnp.moveaxis(o, 0, 1).reshape(B, S, B_HEADS, 2 * HEAD_DIM)
    o = rms_norm(o, subln) * (1.0 - lam_init)
    return o.reshape(B, S, BR_W)


def memory_cross_attention(q, mem_n, w_mem_kv):
    B, S, _ = q.shape
    k, v = jnp.split(mem_n @ w_mem_kv, 2, axis=-1)
    k = k.reshape(B, -1, MEM_HEADS, HEAD_DIM)
    v = v.reshape(B, -1, MEM_HEADS, HEAD_DIM)
    q = q.reshape(B, S, MEM_HEADS, HEAD_DIM)
    s = jnp.einsum('bshd,bmhd->bhsm', q, k, preferred_element_type=jnp.float32) * (HEAD_DIM ** -0.5)
    p = jax.nn.softmax(s, axis=-1).astype(v.dtype)
    o = jnp.einsum('bhsm,bmhd->bshd', p, v)
    return o.reshape(B, S, MEM_W)


def setup_inputs(seed: int = 0) -> dict:
    key = jax.random.key(seed)
    ks = jax.random.split(key, 32)
    nrm = lambda k, shape, scale: scale * jax.random.normal(k, shape, jnp.float32)
    gain = lambda k, shape: 1.0 + 0.05 * jax.random.normal(k, shape, jnp.float32)
    return {
        'x': jax.random.normal(ks[0], (BATCH, SEQ, D_MODEL), jnp.float32),
        'mem': jax.random.normal(ks[1], (BATCH, MEM_LEN, D_MODEL), jnp.float32),
        'pre_norm': gain(ks[2], (DEPTH, D_MODEL)),
        'post_norm': gain(ks[3], (DEPTH, D_MODEL)),
        'w_out': nrm(ks[4], (DEPTH, MIX_W, D_MODEL), MIX_W ** -0.5),
        'mem_norm': gain(ks[5], (DEPTH, D_MODEL)),
        'w_mem_kv': nrm(ks[6], (DEPTH, D_MODEL, 2 * MEM_W), D_MODEL ** -0.5),
        'a_w_in': nrm(ks[7], (N_A, D_MODEL, A_IN), D_MODEL ** -0.5),
        'a_shift_mu': jax.random.uniform(ks[8], (N_A, A_SHIFT), jnp.float32),
        'a_w0': jax.random.uniform(ks[9], (N_A, BR_W), jnp.float32, minval=-6.0, maxval=0.0),
        'a_w2': nrm(ks[10], (N_A, LORA_W, BR_W), 0.5 * LORA_W ** -0.5),
        'a_a0': nrm(ks[11], (N_A, BR_W), 0.1),
        'a_a2': nrm(ks[12], (N_A, LORA_A, BR_W), 0.5 * LORA_A ** -0.5),
        'a_k_k': 0.85 + 0.05 * jax.random.normal(ks[13], (N_A, BR_W), jnp.float32),
        'a_k_a': gain(ks[14], (N_A, BR_W)),
        'a_r_k': nrm(ks[15], (N_A, A_HEADS, HEAD_DIM), 0.1),
        'a_lnx_w': gain(ks[16], (N_A, BR_W)),
        'a_lnx_b': nrm(ks[17], (N_A, BR_W), 0.02),
        'kv_norm': gain(ks[18], (D_MODEL,)),
        'w_kv': nrm(ks[19], (D_MODEL, 2 * BR_W), D_MODEL ** -0.5),
        'b_w_in': nrm(ks[20], (N_B, D_MODEL, B_IN), D_MODEL ** -0.5),
        'b_lam_q1': nrm(ks[21], (N_B, HEAD_DIM), 0.1),
        'b_lam_k1': nrm(ks[22], (N_B, HEAD_DIM), 0.1),
        'b_lam_q2': nrm(ks[23], (N_B, HEAD_DIM), 0.1),
        'b_lam_k2': nrm(ks[24], (N_B, HEAD_DIM), 0.1),
        'b_subln': gain(ks[25], (N_B, 2 * HEAD_DIM)),
    }


def reference(x, mem, pre_norm, post_norm, w_out, mem_norm, w_mem_kv,
              a_w_in, a_shift_mu, a_w0, a_w2, a_a0, a_a2, a_k_k, a_k_a, a_r_k, a_lnx_w, a_lnx_b,
              kv_norm, w_kv,
              b_w_in, b_lam_q1, b_lam_k1, b_lam_q2, b_lam_k2, b_subln):
    k_sh = None
    v_sh = None
    for l in range(DEPTH):
        h = rms_norm(x, pre_norm[l])
        mem_n = rms_norm(mem, mem_norm[l])
        if l < N_A:
            i = l
            proj = h @ a_w_in[i]
            z, gate, q_mem, g_mem = jnp.split(
                proj, [A_SHIFT, A_SHIFT + BR_W, A_SHIFT + BR_W + MEM_W], axis=-1)
            y_br = rwkv7_time_mix(z, a_shift_mu[i], a_w0[i], a_w2[i], a_a0[i], a_a2[i],
                                  a_k_k[i], a_k_a[i], a_r_k[i], a_lnx_w[i], a_lnx_b[i])
        else:
            if l == N_A:
                k_sh, v_sh = jnp.split(rms_norm(x, kv_norm) @ w_kv, 2, axis=-1)
            i = l - N_A
            proj = h @ b_w_in[i]
            q, gate, q_mem, g_mem = jnp.split(
                proj, [BR_W, 2 * BR_W, 2 * BR_W + MEM_W], axis=-1)
            lam_init = 0.8 - 0.6 * math.exp(-0.3 * l)
            y_br = diff_attention(q, k_sh, v_sh, b_lam_q1[i], b_lam_k1[i], b_lam_q2[i],
                                  b_lam_k2[i], b_subln[i], lam_init)
        y_mem = memory_cross_attention(q_mem, mem_n, w_mem_kv[l])
        y = jnp.concatenate([(y_br * jax.nn.silu(gate.astype(jnp.float32))).astype(x.dtype),
                             y_mem * jax.nn.silu(g_mem)], axis=-1) @ w_out[l]
        x = x + rms_norm(y, post_norm[l])
    return x
```

```python
import functools
import math

import numpy as np
import jax
import jax.numpy as jnp
from jax import lax
from jax.experimental import pallas as pl
from jax.experimental.pallas import tpu as pltpu

F32 = jnp.float32
BF16 = jnp.bfloat16

HEAD_DIM = 64
LANES = 128
MEM_HEADS = 4
MEM_W = MEM_HEADS * HEAD_DIM
LORA_W = 64
ATT_CHUNK = 64
RWKV_CHUNK = 64
NORM_EPS = 1e-6
LNX_EPS = 64e-5
MASK_VALUE = -1e30
VMEM_LIMIT_BYTES = 56 * 1024 * 1024


def _dot(a, b):
    return jnp.dot(a, b, preferred_element_type=F32)


def _dot_nt(a, b):
    return lax.dot_general(a, b, (((1,), (1,)), ((), ())), preferred_element_type=F32)


def _dot_tn(a, b):
    return lax.dot_general(a, b, (((0,), (0,)), ((), ())), preferred_element_type=F32)


def _split(x, pieces):
    out = []
    rem = x
    for i in range(pieces):
        p = rem.astype(BF16)
        out.append(p)
        if i + 1 < pieces:
            rem = rem - p.astype(F32)
    return out


def _mm(a, b, fn=_dot, pa=1, pb=1):
    aps = _split(a, pa)
    bps = _split(b, pb)
    order = max(pa, pb)
    acc = None
    for i, ap in enumerate(aps):
        for j, bp in enumerate(bps):
            if i + j < order:
                t = fn(ap, bp)
                acc = t if acc is None else acc + t
    return acc


def _rms(x, g):
    ms = jnp.mean(x * x, axis=-1, keepdims=True)
    return (x * lax.rsqrt(ms + NORM_EPS)) * g


def _silu(x):
    return x / (1.0 + jnp.exp(-x))


def _norm_proj_body(x_ref, g_ref, w_ref, *o_refs, segments):
    x = x_ref[...]
    ms = jnp.mean(x * x, axis=-1, keepdims=True)
    xn = x * lax.rsqrt(ms + NORM_EPS)
    hs = {}
    for o_ref, (gi, lo, hi) in zip(o_refs, segments):
        if gi not in hs:
            hs[gi] = (xn * g_ref[gi:gi + 1, :]).astype(BF16)
        o_ref[...] = _dot(hs[gi], w_ref[:, lo:hi])


def _norm_proj(x2d, gains, w_bf16, segments, tm):
    m, d = x2d.shape
    n = w_bf16.shape[1]
    ng = gains.shape[0]
    out_shape = [jax.ShapeDtypeStruct((m, hi - lo), F32) for _, lo, hi in segments]
    out_specs = [pl.BlockSpec((tm, hi - lo), lambda i: (i, 0)) for _, lo, hi in segments]
    return pl.pallas_call(
        functools.partial(_norm_proj_body, segments=tuple(segments)),
        grid=(m // tm,),
        in_specs=[
            pl.BlockSpec((tm, d), lambda i: (i, 0)),
            pl.BlockSpec((ng, d), lambda i: (0, 0)),
            pl.BlockSpec((d, n), lambda i: (0, 0)),
        ],
        out_specs=out_specs,
        out_shape=out_shape,
        compiler_params=pltpu.CompilerParams(
            dimension_semantics=("arbitrary",), vmem_limit_bytes=VMEM_LIMIT_BYTES),
        name="norm_proj",
    )(x2d, gains, w_bf16)


def _mem_kv_body(mem_ref, g_ref, w_ref, k_ref, v_ref):
    h = _rms(mem_ref[0], g_ref[...]).astype(BF16)
    kv = _dot(h, w_ref[...])
    k_ref[0] = kv[:, :MEM_W]
    v_ref[0] = kv[:, MEM_W:]


def _mem_kv(mem, g, w_bf16):
    b, ml, d = mem.shape
    return pl.pallas_call(
        _mem_kv_body,
        grid=(b,),
        in_specs=[
            pl.BlockSpec((1, ml, d), lambda i: (i, 0, 0)),
            pl.BlockSpec((1, d), lambda i: (0, 0)),
            pl.BlockSpec((d, 2 * MEM_W), lambda i: (0, 0)),
        ],
        out_specs=[pl.BlockSpec((1, ml, MEM_W), lambda i: (i, 0, 0))] * 2,
        out_shape=[jax.ShapeDtypeStruct((b, ml, MEM_W), F32)] * 2,
        compiler_params=pltpu.CompilerParams(
            dimension_semantics=("arbitrary",), vmem_limit_bytes=VMEM_LIMIT_BYTES),
        name="mem_kv",
    )(mem, g, w_bf16)


_PP_MU_R, _PP_MU_K, _PP_MU_V, _PP_W0, _PP_A0, _PP_KK, _PP_KA, _PP_RK, _PP_LNW, _PP_LNB = range(10)
_PP_ROWS = 16


def _rwkv_body(zr_ref, zk_ref, zv_ref, zwa_ref, pp_ref, muwa_ref, w2a2_ref, y_ref,
               s_ref, prev_ref, *, nchunks):
    c = RWKV_CHUNK
    lane = lax.broadcasted_iota(jnp.int32, (c, LANES), 1)
    row = lax.broadcasted_iota(jnp.int32, (c, LANES), 0)
    head0 = lane < HEAD_DIM
    scol = jnp.where(head0, lane, lane - HEAD_DIM)
    strict = scol < row
    incl = scol <= row
    r2 = lax.broadcasted_iota(jnp.int32, (LANES, LANES), 0)
    c2 = lax.broadcasted_iota(jnp.int32, (LANES, LANES), 1)
    blockmask = (r2 < HEAD_DIM) == (c2 < HEAD_DIM)
    blockones = jnp.where(blockmask, 1.0, 0.0).astype(BF16)
    tr = lax.broadcasted_iota(jnp.int32, (c, c), 0)
    tc = lax.broadcasted_iota(jnp.int32, (c, c), 1)
    tril_ones = jnp.where(tc <= tr, 1.0, 0.0).astype(BF16)
    first_row = row == 0

    def bd(x):
        return jnp.concatenate([jnp.where(head0, x, 0.0), jnp.where(head0, 0.0, x)], axis=0)

    def seg_sum(x):
        hi, lo = _split(x, 2)
        return _dot(hi, blockones) + _dot(lo, blockones)

    pp = pp_ref[0]
    prow = lambda i: pp[i:i + 1, :]
    mu_r, mu_k, mu_v = prow(_PP_MU_R), prow(_PP_MU_K), prow(_PP_MU_V)
    w0, a0 = prow(_PP_W0), prow(_PP_A0)
    k_k, k_a, r_k = prow(_PP_KK), prow(_PP_KA), prow(_PP_RK)
    lnx_w, lnx_b = prow(_PP_LNW), prow(_PP_LNB)
    mu_wa = muwa_ref[...]
    w2a2 = w2a2_ref[0]

    s_ref[...] = jnp.zeros_like(s_ref)
    prev_ref[...] = jnp.zeros_like(prev_ref)

    def shift_mix(ref, slot, sl, mu):
        z = ref[0, sl, :]
        zp = pltpu.roll(z, 1, axis=0)
        zp = jnp.where(first_row, prev_ref[slot:slot + 1, :], zp)
        prev_ref[slot:slot + 1, :] = z[c - 1:c, :]
        return z + (zp - z) * mu

    def chunk(ci, carry):
        sl = pl.ds(pl.multiple_of(ci * c, c), c)
        r = shift_mix(zr_ref, 0, sl, mu_r)
        k = shift_mix(zk_ref, 1, sl, mu_k)
        v = shift_mix(zv_ref, 2, sl, mu_v)
        wa = shift_mix(zwa_ref, 3, sl, mu_wa)

        lora = _mm(jnp.where(head0, jnp.tanh(wa), wa), w2a2, _dot, 2, 2)
        wlog = w0 + lora[:, :LANES]
        nw = -wlog
        w = -(jnp.maximum(nw, 0.0) + jnp.log(1.0 + jnp.exp(-jnp.abs(nw)))) - 0.5
        logw = -jnp.exp(w)
        a = 1.0 / (1.0 + jnp.exp(-(a0 + lora[:, LANES:])))

        kk = k * k_k
        kk = kk * lax.rsqrt(jnp.maximum(seg_sum(kk * kk), 1e-12))
        kmod = k * (1.0 + (a - 1.0) * k_a)
        alpha = -kk
        beta = kk * a

        l3 = jnp.concatenate(_split(logw, 3), axis=1)
        cum3 = _dot(tril_ones, l3)
        cum = cum3[:, :LANES] + cum3[:, LANES:2 * LANES] + cum3[:, 2 * LANES:]
        cum_ex = cum - logw
        cum_c = cum[c - 1:c, :]
        e_neg = jnp.exp(-cum)
        at = alpha * jnp.exp(cum_ex)
        rt = r * jnp.exp(cum)
        bt = beta * e_neg
        kt = kmod * e_neg
        e_end = jnp.exp(cum_c - cum)
        bt_end = beta * e_end
        kt_end = kmod * e_end
        p_c = jnp.exp(cum_c)

        lhs = jnp.concatenate([at, rt], axis=0)
        xb = _mm(lhs, bd(bt), _dot_nt)
        xk = _mm(lhs, bd(kt), _dot_nt)
        a_ab = jnp.where(strict, xb[:c], 0.0)
        a_rb = jnp.where(incl, xb[c:], 0.0)
        a_ak = jnp.where(strict, xk[:c], 0.0)
        a_rk = jnp.where(incl, xk[c:], 0.0)

        av = _mm(jnp.concatenate([a_ak, a_rk], axis=0), bd(v))
        x1 = at
        x2 = av[:c]
        apow = a_ab
        nfac = int(math.log2(c))
        for i in range(nfac):
            d = _mm(apow, jnp.concatenate([bd(x1), bd(x2)], axis=1))
            x1 = x1 + d[:, :LANES]
            x2 = x2 + d[:, LANES:]
            if i + 1 < nfac:
                apow = _mm(apow, bd(apow))
        wmat, u0 = x1, x2

        d2 = _mm(a_rb, jnp.concatenate([bd(wmat), bd(u0)], axis=1))
        rp = rt + d2[:, :LANES]
        y0 = d2[:, LANES:] + av[c:]

        s_bd = s_ref[...]
        y = _mm(rp, s_bd, _dot_nt) + y0

        gp = jnp.where(blockmask, _mm(wmat, bt_end, _dot_tn), 0.0)
        npart = jnp.where(
            blockmask,
            _mm(jnp.concatenate([u0, v], axis=0), jnp.concatenate([bt_end, kt_end], axis=0), _dot_tn),
            0.0)
        s_ref[...] = s_bd * p_c + _mm(s_bd, gp) + npart

        mean = seg_sum(y) * (1.0 / HEAD_DIM)
        yc = y - mean
        var = seg_sum(yc * yc) * (1.0 / HEAD_DIM)
        yn = yc * lax.rsqrt(var + LNX_EPS) * lnx_w + lnx_b
        bonus = seg_sum(r * kmod * r_k) * v
        y_ref[0, sl, :] = yn + bonus
        return carry

    lax.fori_loop(0, nchunks, chunk, 0)


def _rwkv(z3d, pp, mu_wa, w2a2):
    b, s, a_shift = z3d.shape
    br_w = (a_shift - 2 * LORA_W) // 3
    npairs = br_w // LANES
    zspec = lambda off: pl.BlockSpec((1, s, LANES), lambda bi, hp: (bi, 0, off + hp))
    return pl.pallas_call(
        functools.partial(_rwkv_body, nchunks=s // RWKV_CHUNK),
        grid=(b, npairs),
        in_specs=[
            zspec(0), zspec(npairs), zspec(2 * npairs),
            pl.BlockSpec((1, s, LANES), lambda bi, hp: (bi, 0, 3 * npairs)),
            pl.BlockSpec((1, _PP_ROWS, LANES), lambda bi, hp: (hp, 0, 0)),
            pl.BlockSpec((1, LANES), lambda bi, hp: (0, 0)),
            pl.BlockSpec((1, LANES, 2 * LANES), lambda bi, hp: (hp, 0, 0)),
        ],
        out_specs=pl.BlockSpec((1, s, LANES), lambda bi, hp: (bi, 0, hp)),
        out_shape=jax.ShapeDtypeStruct((b, s, br_w), F32),
        scratch_shapes=[pltpu.VMEM((LANES, LANES), F32), pltpu.VMEM((8, LANES), F32)],
        compiler_params=pltpu.CompilerParams(
            dimension_semantics=("arbitrary", "arbitrary"), vmem_limit_bytes=VMEM_LIMIT_BYTES),
        name="rwkv7_scan",
    )(z3d, z3d, z3d, z3d, pp, mu_wa, w2a2)


def _alibi_slopes(n):
    def pow2(m):
        start = 2.0 ** (-8.0 / m)
        return [start ** (i + 1) for i in range(m)]
    if math.log2(n).is_integer():
        return pow2(n)
    cl = 2 ** int(math.floor(math.log2(n)))
    return pow2(cl) + pow2(2 * cl)[0::2][: n - cl]


def _diff_attn_body(q_ref, k_ref, v_ref, slope_ref, lamp_ref, sub_ref, o_ref,
                    acc1_ref, acc2_ref, *, tq, lam_init):
    qi = pl.program_id(2)
    tk = tq
    lane = lax.broadcasted_iota(jnp.int32, (tq, LANES), 1)
    head0 = lane < HEAD_DIM
    q = q_ref[0] * (HEAD_DIM ** -0.5)
    q1 = jnp.where(head0, q, 0.0).astype(BF16)
    q2 = jnp.where(head0, 0.0, q).astype(BF16)
    slope = slope_ref[0]
    slope_col = slope[:, :1]
    rr = lax.broadcasted_iota(jnp.int32, (tq, tk), 0)
    cc = lax.broadcasted_iota(jnp.int32, (tq, tk), 1)
    rel = (rr - cc).astype(F32)

    acc1_ref[...] = jnp.zeros_like(acc1_ref)
    acc2_ref[...] = jnp.zeros_like(acc2_ref)

    def tile(j, carry, diagonal):
        m1, l1, m2, l2 = carry
        ksl = pl.ds(pl.multiple_of(j * tk, tk), tk)
        kt = k_ref[0, ksl, :].astype(BF16)
        vt = v_ref[0, ksl, :].astype(BF16)
        off = ((qi - j) * tq).astype(F32)
        bias = -slope_col * jnp.abs(rel + off)
        s1 = _dot_nt(q1, kt) + bias
        s2 = _dot_nt(q2, kt) + bias
        if diagonal:
            allowed = (cc // ATT_CHUNK) <= (rr // ATT_CHUNK)
            s1 = jnp.where(allowed, s1, MASK_VALUE)
            s2 = jnp.where(allowed, s2, MASK_VALUE)

        def update(s, m, l, acc_ref):
            m_new = jnp.maximum(m, jnp.max(s, axis=-1, keepdims=True))
            scale = jnp.exp(m - m_new)
            p = jnp.exp(s - m_new)
            l_new = scale * l + jnp.sum(p, axis=-1, keepdims=True)
            acc_ref[...] = scale * acc_ref[...] + _dot(p.astype(BF16), vt)
            return m_new, l_new

        m1, l1 = update(s1, m1, l1, acc1_ref)
        m2, l2 = update(s2, m2, l2, acc2_ref)
        return m1, l1, m2, l2

    init = (jnp.full((tq, 1), MASK_VALUE, F32), jnp.zeros((tq, 1), F32),
            jnp.full((tq, 1), MASK_VALUE, F32), jnp.zeros((tq, 1), F32))
    carry = lax.fori_loop(0, qi, lambda j, cr: tile(j, cr, False), init)
    m1, l1, m2, l2 = tile(qi, carry, True)

    lp = lamp_ref[...]
    lam = (jnp.exp(jnp.sum(lp[0:1] * lp[1:2], axis=-1, keepdims=True))
           - jnp.exp(jnp.sum(lp[2:3] * lp[3:4], axis=-1, keepdims=True)) + lam_init)
    o = acc1_ref[...] / l1 - lam * (acc2_ref[...] / l2)
    o_ref[0] = _rms(o, sub_ref[...]) * (1.0 - lam_init)


def _diff_attn(q3d, k3d, v3d, slopes, lam_params, subln, lam_init, tq):
    b, s, br_w = q3d.shape
    nh = br_w // LANES
    return pl.pallas_call(
        functools.partial(_diff_attn_body, tq=tq, lam_init=lam_init),
        grid=(b, nh, s // tq),
        in_specs=[
            pl.BlockSpec((1, tq, LANES), lambda bi, h, qi: (bi, qi, h)),
            pl.BlockSpec((1, s, LANES), lambda bi, h, qi: (bi, 0, h)),
            pl.BlockSpec((1, s, LANES), lambda bi, h, qi: (bi, 0, h)),
            pl.BlockSpec((1, 1, LANES), lambda bi, h, qi: (h, 0, 0)),
            pl.BlockSpec((4, HEAD_DIM), lambda bi, h, qi: (0, 0)),
            pl.BlockSpec((1, LANES), lambda bi, h, qi: (0, 0)),
        ],
        out_specs=pl.BlockSpec((1, tq, LANES), lambda bi, h, qi: (bi, qi, h)),
        out_shape=jax.ShapeDtypeStruct((b, s, br_w), F32),
        scratch_shapes=[pltpu.VMEM((tq, LANES), F32), pltpu.VMEM((tq, LANES), F32)],
        compiler_params=pltpu.CompilerParams(
            dimension_semantics=("arbitrary", "arbitrary", "arbitrary"),
            vmem_limit_bytes=VMEM_LIMIT_BYTES),
        name="diff_attn",
    )(q3d, k3d, v3d, slopes, lam_params, subln)


def _out_body(ybr_ref, gate_ref, qm_ref, gm_ref, x_ref, km_ref, vm_ref, w_ref, pn_ref, o_ref, *, br_w):
    tm = x_ref.shape[0]
    lane = lax.broadcasted_iota(jnp.int32, (tm, MEM_W), 1)
    qm = qm_ref[...]
    km = km_ref[0].astype(BF16)
    vm = vm_ref[0].astype(BF16)
    y_mem = jnp.zeros((tm, MEM_W), F32)
    for h in range(MEM_HEADS):
        in_head = (lane >= h * HEAD_DIM) & (lane < (h + 1) * HEAD_DIM)
        qh = jnp.where(in_head, qm, 0.0).astype(BF16)
        s = _dot_nt(qh, km) * (HEAD_DIM ** -0.5)
        p = jnp.exp(s - jnp.max(s, axis=-1, keepdims=True))
        l = jnp.sum(p, axis=-1, keepdims=True)
        oh = _dot(p.astype(BF16), vm)
        y_mem = y_mem + jnp.where(in_head, oh / l, 0.0)
    y_mem = (y_mem * _silu(gm_ref[...])).astype(BF16)
    y_br = (ybr_ref[...] * _silu(gate_ref[...])).astype(BF16)
    y = _dot(y_br, w_ref[:br_w, :]) + _dot(y_mem, w_ref[br_w:, :])
    o_ref[...] = x_ref[...] + _rms(y, pn_ref[...])


def _out_proj(ybr, gate, qm, gm, x2d, k_mem, v_mem, w_bf16, post_g, seq, tm):
    m, d = x2d.shape
    br_w = ybr.shape[1]
    ml = k_mem.shape[1]
    per_b = seq // tm
    row = lambda w: pl.BlockSpec((tm, w), lambda i: (i, 0))
    return pl.pallas_call(
        functools.partial(_out_body, br_w=br_w),
        grid=(m // tm,),
        in_specs=[
            row(br_w), row(br_w), row(MEM_W), row(MEM_W), row(d),
            pl.BlockSpec((1, ml, MEM_W), lambda i: (i // per_b, 0, 0)),
            pl.BlockSpec((1, ml, MEM_W), lambda i: (i // per_b, 0, 0)),
            pl.BlockSpec((d, d), lambda i: (0, 0)),
            pl.BlockSpec((1, d), lambda i: (0, 0)),
        ],
        out_specs=row(d),
        out_shape=jax.ShapeDtypeStruct((m, d), F32),
        compiler_params=pltpu.CompilerParams(
            dimension_semantics=("arbitrary",), vmem_limit_bytes=VMEM_LIMIT_BYTES),
        name="out_proj",
    )(ybr, gate, qm, gm, x2d, k_mem, v_mem, w_bf16, post_g)


def _rwkv_params(mu, w0, w2, a0, a2, k_k, k_a, r_k, lnx_w, lnx_b, br_w):
    npairs = br_w // LANES
    rows = [mu[:br_w], mu[br_w:2 * br_w], mu[2 * br_w:3 * br_w], w0, a0, k_k, k_a,
            r_k.reshape(-1), lnx_w, lnx_b]
    pp = jnp.stack([r.reshape(npairs, LANES) for r in rows], axis=1)
    pp = jnp.pad(pp, ((0, 0), (0, _PP_ROWS - len(rows)), (0, 0)))
    mu_wa = mu[3 * br_w:].reshape(1, LANES)
    w2p = w2.reshape(LORA_W, npairs, LANES).transpose(1, 0, 2)
    a2p = a2.reshape(LORA_W, npairs, LANES).transpose(1, 0, 2)
    zeros = jnp.zeros_like(w2p)
    w2a2 = jnp.concatenate(
        [jnp.concatenate([w2p, zeros], axis=2), jnp.concatenate([zeros, a2p], axis=2)], axis=1)
    return pp, mu_wa, w2a2


def kernel(x, mem, pre_norm, post_norm, w_out, mem_norm, w_mem_kv, a_w_in, a_shift_mu, a_w0, a_w2,
           a_a0, a_a2, a_k_k, a_k_a, a_r_k, a_lnx_w, a_lnx_b, kv_norm, w_kv, b_w_in, b_lam_q1,
           b_lam_k1, b_lam_q2, b_lam_k2, b_subln):
    bsz, seq, d = x.shape
    depth = pre_norm.shape[0]
    n_a = a_w_in.shape[0]
    br_w = d - MEM_W
    a_shift = 3 * br_w + 2 * LORA_W
    m = bsz * seq
    tm = 256
    x2d = x.reshape(m, d)
    slopes = jnp.asarray(
        np.repeat(np.array(_alibi_slopes(br_w // LANES), np.float32)[:, None, None], LANES, axis=2))

    for l in range(depth):
        k_mem, v_mem = _mem_kv(mem, mem_norm[l].reshape(1, d), w_mem_kv[l].astype(BF16))
        if l < n_a:
            i = l
            segs = [(0, 0, a_shift), (0, a_shift, a_shift + br_w),
                    (0, a_shift + br_w, a_shift + br_w + MEM_W),
                    (0, a_shift + br_w + MEM_W, a_shift + br_w + 2 * MEM_W)]
            z, gate, q_mem, g_mem = _norm_proj(
                x2d, pre_norm[l].reshape(1, d), a_w_in[i].astype(BF16), segs, tm)
            pp, mu_wa, w2a2 = _rwkv_params(
                a_shift_mu[i], a_w0[i], a_w2[i], a_a0[i], a_a2[i], a_k_k[i], a_k_a[i], a_r_k[i],
                a_lnx_w[i], a_lnx_b[i], br_w)
            y_br = _rwkv(z.reshape(bsz, seq, a_shift), pp, mu_wa, w2a2).reshape(m, br_w)
        else:
            i = l - n_a
            if l == n_a:
                w_cat = jnp.concatenate([b_w_in[i], w_kv], axis=1).astype(BF16)
                gains = jnp.stack([pre_norm[l], kv_norm], axis=0)
                segs = [(0, 0, br_w), (0, br_w, 2 * br_w), (0, 2 * br_w, 2 * br_w + MEM_W),
                        (0, 2 * br_w + MEM_W, 2 * br_w + 2 * MEM_W),
                        (1, 2 * br_w + 2 * MEM_W, 3 * br_w + 2 * MEM_W),
                        (1, 3 * br_w + 2 * MEM_W, 4 * br_w + 2 * MEM_W)]
                q, gate, q_mem, g_mem, k_sh, v_sh = _norm_proj(x2d, gains, w_cat, segs, tm)
                k_sh = k_sh.reshape(bsz, seq, br_w)
                v_sh = v_sh.reshape(bsz, seq, br_w)
            else:
                segs = [(0, 0, br_w), (0, br_w, 2 * br_w), (0, 2 * br_w, 2 * br_w + MEM_W),
                        (0, 2 * br_w + MEM_W, 2 * br_w + 2 * MEM_W)]
                q, gate, q_mem, g_mem = _norm_proj(
                    x2d, pre_norm[l].reshape(1, d), b_w_in[i].astype(BF16), segs, tm)
            lam_init = 0.8 - 0.6 * math.exp(-0.3 * l)
            lam_params = jnp.stack([b_lam_q1[i], b_lam_k1[i], b_lam_q2[i], b_lam_k2[i]], axis=0)
            y_br = _diff_attn(q.reshape(bsz, seq, br_w), k_sh, v_sh, slopes, lam_params,
                              b_subln[i].reshape(1, LANES), lam_init, 256).reshape(m, br_w)
        x2d = _out_proj(y_br, gate, q_mem, g_mem, x2d, k_mem, v_mem, w_out[l].astype(BF16),
                        post_norm[l].reshape(1, d), seq, tm)
    return x2d.reshape(bsz, seq, d)
```

```python
import functools
import math

import numpy as np
import jax
import jax.numpy as jnp
from jax import lax
from jax.experimental import pallas as pl
from jax.experimental.pallas import tpu as pltpu

F32 = jnp.float32
BF16 = jnp.bfloat16

HEAD_DIM = 64
LANES = 128
MEM_HEADS = 4
MEM_W = MEM_HEADS * HEAD_DIM
LORA_W = 64
ATT_CHUNK = 64
RWKV_CHUNK = 64
RWKV_GROUP = 8
NORM_EPS = 1e-6
LNX_EPS = 64e-5
MASK_VALUE = -1e30
VMEM_LIMIT_BYTES = 56 * 1024 * 1024


def _dot(a, b):
    return jnp.dot(a, b, preferred_element_type=F32)


def _dot_nt(a, b):
    return lax.dot_general(a, b, (((1,), (1,)), ((), ())), preferred_element_type=F32)


def _dot_tn(a, b):
    return lax.dot_general(a, b, (((0,), (0,)), ((), ())), preferred_element_type=F32)


def _split(x, pieces):
    out = []
    rem = x
    for i in range(pieces):
        p = rem.astype(BF16)
        out.append(p)
        if i + 1 < pieces:
            rem = rem - p.astype(F32)
    return out


def _mm(a, b, fn=_dot, pa=1, pb=1):
    aps = _split(a, pa)
    bps = _split(b, pb)
    order = max(pa, pb)
    acc = None
    for i, ap in enumerate(aps):
        for j, bp in enumerate(bps):
            if i + j < order:
                t = fn(ap, bp)
                acc = t if acc is None else acc + t
    return acc


def _rms(x, g):
    ms = jnp.mean(x * x, axis=-1, keepdims=True)
    return (x * lax.rsqrt(ms + NORM_EPS)) * g


def _silu(x):
    return x / (1.0 + jnp.exp(-x))


def _norm_proj_body(x_ref, g_ref, w_ref, *o_refs, segments):
    x = x_ref[...]
    ms = jnp.mean(x * x, axis=-1, keepdims=True)
    xn = x * lax.rsqrt(ms + NORM_EPS)
    hs = {}
    for o_ref, (gi, lo, hi) in zip(o_refs, segments):
        if gi not in hs:
            hs[gi] = (xn * g_ref[gi:gi + 1, :]).astype(BF16)
        o_ref[...] = _dot(hs[gi], w_ref[:, lo:hi])


def _norm_proj(x2d, gains, w_bf16, segments, tm):
    m, d = x2d.shape
    n = w_bf16.shape[1]
    ng = gains.shape[0]
    out_shape = [jax.ShapeDtypeStruct((m, hi - lo), F32) for _, lo, hi in segments]
    out_specs = [pl.BlockSpec((tm, hi - lo), lambda i: (i, 0)) for _, lo, hi in segments]
    return pl.pallas_call(
        functools.partial(_norm_proj_body, segments=tuple(segments)),
        grid=(m // tm,),
        in_specs=[
            pl.BlockSpec((tm, d), lambda i: (i, 0)),
            pl.BlockSpec((ng, d), lambda i: (0, 0)),
            pl.BlockSpec((d, n), lambda i: (0, 0)),
        ],
        out_specs=out_specs,
        out_shape=out_shape,
        compiler_params=pltpu.CompilerParams(
            dimension_semantics=("arbitrary",), vmem_limit_bytes=VMEM_LIMIT_BYTES),
        name="norm_proj",
    )(x2d, gains, w_bf16)


def _mem_kv_body(mem_ref, g_ref, w_ref, k_ref, v_ref):
    h = _rms(mem_ref[0], g_ref[...]).astype(BF16)
    kv = _dot(h, w_ref[...])
    k_ref[0] = kv[:, :MEM_W]
    v_ref[0] = kv[:, MEM_W:]


def _mem_kv(mem, g, w_bf16):
    b, ml, d = mem.shape
    return pl.pallas_call(
        _mem_kv_body,
        grid=(b,),
        in_specs=[
            pl.BlockSpec((1, ml, d), lambda i: (i, 0, 0)),
            pl.BlockSpec((1, d), lambda i: (0, 0)),
            pl.BlockSpec((d, 2 * MEM_W), lambda i: (0, 0)),
        ],
        out_specs=[pl.BlockSpec((1, ml, MEM_W), lambda i: (i, 0, 0))] * 2,
        out_shape=[jax.ShapeDtypeStruct((b, ml, MEM_W), F32)] * 2,
        compiler_params=pltpu.CompilerParams(
            dimension_semantics=("arbitrary",), vmem_limit_bytes=VMEM_LIMIT_BYTES),
        name="mem_kv",
    )(mem, g, w_bf16)


_PP_MU_R, _PP_MU_K, _PP_MU_V, _PP_W0, _PP_A0, _PP_KK, _PP_KA, _PP_RK, _PP_LNW, _PP_LNB = range(10)
_PP_ROWS = 16


def _rwkv_body(zr_ref, zk_ref, zv_ref, zwa_ref, pp_ref, muwa_ref, w2a2_ref, y_ref,
               s_ref, prev_ref, *, ngroups, group):
    c = RWKV_CHUNK
    rows = group * c
    lane = lax.broadcasted_iota(jnp.int32, (c, LANES), 1)
    row = lax.broadcasted_iota(jnp.int32, (c, LANES), 0)
    head0 = lane < HEAD_DIM
    scol = jnp.where(head0, lane, lane - HEAD_DIM)
    strict = scol < row
    incl = scol <= row
    r2 = lax.broadcasted_iota(jnp.int32, (LANES, LANES), 0)
    c2 = lax.broadcasted_iota(jnp.int32, (LANES, LANES), 1)
    blockmask = (r2 < HEAD_DIM) == (c2 < HEAD_DIM)
    blockones = jnp.where(blockmask, 1.0, 0.0).astype(BF16)
    tr = lax.broadcasted_iota(jnp.int32, (rows, rows), 0)
    tc = lax.broadcasted_iota(jnp.int32, (rows, rows), 1)
    tril_ones = jnp.where((tc <= tr) & (tc // c == tr // c), 1.0, 0.0).astype(BF16)
    slab_lane = lax.broadcasted_iota(jnp.int32, (rows, LANES), 1)
    slab_head0 = slab_lane < HEAD_DIM
    slab_first = lax.broadcasted_iota(jnp.int32, (rows, LANES), 0) == 0

    def bd(x):
        return jnp.concatenate([jnp.where(head0, x, 0.0), jnp.where(head0, 0.0, x)], axis=0)

    def seg_sum(x):
        hi, lo = _split(x, 2)
        return _dot(hi, blockones) + _dot(lo, blockones)

    pp = pp_ref[0]
    prow = lambda i: pp[i:i + 1, :]
    mu_r, mu_k, mu_v = prow(_PP_MU_R), prow(_PP_MU_K), prow(_PP_MU_V)
    w0, a0 = prow(_PP_W0), prow(_PP_A0)
    k_k, k_a, r_k = prow(_PP_KK), prow(_PP_KA), prow(_PP_RK)
    lnx_w, lnx_b = prow(_PP_LNW), prow(_PP_LNB)
    mu_wa = muwa_ref[...]
    w2a2 = w2a2_ref[0]

    s_ref[...] = jnp.zeros_like(s_ref)
    prev_ref[...] = jnp.zeros_like(prev_ref)

    def shift_mix(ref, slot, sl, mu):
        z = ref[0, sl, :]
        zp = pltpu.roll(z, 1, axis=0)
        zp = jnp.where(slab_first, prev_ref[slot:slot + 1, :], zp)
        prev_ref[slot:slot + 1, :] = z[rows - 1:rows, :]
        return z + (zp - z) * mu

    def chunk_terms(at, rt, bt, kt, bt_end, kt_end, v):
        n = range(len(at))
        lhs = [jnp.concatenate([at[g], rt[g]], axis=0) for g in n]
        xb = [_mm(lhs[g], bd(bt[g]), _dot_nt) for g in n]
        xk = [_mm(lhs[g], bd(kt[g]), _dot_nt) for g in n]
        a_rb = [jnp.where(incl, xb[g][c:], 0.0) for g in n]
        akrk = [jnp.concatenate([jnp.where(strict, xk[g][:c], 0.0), jnp.where(incl, xk[g][c:], 0.0)], axis=0)
                for g in n]
        av = [_mm(akrk[g], bd(v[g])) for g in n]
        x1 = list(at)
        x2 = [av[g][:c] for g in n]
        apow = [jnp.where(strict, xb[g][:c], 0.0) for g in n]
        nfac = int(math.log2(c))
        for i in range(nfac):
            d = [_mm(apow[g], jnp.concatenate([bd(x1[g]), bd(x2[g])], axis=1)) for g in n]
            x1 = [x1[g] + d[g][:, :LANES] for g in n]
            x2 = [x2[g] + d[g][:, LANES:] for g in n]
            if i + 1 < nfac:
                apow = [_mm(apow[g], bd(apow[g])) for g in n]
        wmat, u0 = x1, x2

        d2 = [_mm(a_rb[g], jnp.concatenate([bd(wmat[g]), bd(u0[g])], axis=1)) for g in n]
        rp = [rt[g] + d2[g][:, :LANES] for g in n]
        y0 = [d2[g][:, LANES:] + av[g][c:] for g in n]
        gp = [jnp.where(blockmask, _mm(wmat[g], bt_end[g], _dot_tn), 0.0) for g in n]
        npart = [
            jnp.where(
                blockmask,
                _mm(jnp.concatenate([u0[g], v[g]], axis=0),
                    jnp.concatenate([bt_end[g], kt_end[g]], axis=0), _dot_tn),
                0.0)
            for g in n]
        return rp, y0, gp, npart

    def group_step(gi, carry):
        sl = pl.ds(pl.multiple_of(gi * rows, rows), rows)
        r = shift_mix(zr_ref, 0, sl, mu_r)
        k = shift_mix(zk_ref, 1, sl, mu_k)
        v = shift_mix(zv_ref, 2, sl, mu_v)
        wa = shift_mix(zwa_ref, 3, sl, mu_wa)

        lora = _mm(jnp.where(slab_head0, jnp.tanh(wa), wa), w2a2, _dot, 2, 2)
        wlog = w0 + lora[:, :LANES]
        nw = -wlog
        w = -(jnp.maximum(nw, 0.0) + jnp.log(1.0 + jnp.exp(-jnp.abs(nw)))) - 0.5
        logw = -jnp.exp(w)
        a = 1.0 / (1.0 + jnp.exp(-(a0 + lora[:, LANES:])))

        kk = k * k_k
        kk = kk * lax.rsqrt(jnp.maximum(seg_sum(kk * kk), 1e-12))
        kmod = k * (1.0 + (a - 1.0) * k_a)
        alpha = -kk
        beta = kk * a

        l3 = jnp.concatenate(_split(logw, 3), axis=1)
        cum3 = _dot(tril_ones, l3)
        cum = cum3[:, :LANES] + cum3[:, LANES:2 * LANES] + cum3[:, 2 * LANES:]
        e_neg = jnp.exp(-cum)
        at = alpha * jnp.exp(cum - logw)
        rt = r * jnp.exp(cum)
        bt = beta * e_neg
        kt = kmod * e_neg

        chunks = [slice(g * c, (g + 1) * c) for g in range(group)]
        cum_c = [cum[cs.stop - 1:cs.stop, :] for cs in chunks]
        e_end = [jnp.exp(cum_c[g] - cum[cs]) for g, cs in enumerate(chunks)]
        per = lambda x: [x[cs] for cs in chunks]
        rp, y0, gp, npart = chunk_terms(
            per(at), per(rt), per(bt), per(kt),
            [beta[cs] * e_end[g] for g, cs in enumerate(chunks)],
            [kmod[cs] * e_end[g] for g, cs in enumerate(chunks)], per(v))

        s_bd = s_ref[...]
        ys = []
        for g in range(group):
            ys.append(_mm(rp[g], s_bd, _dot_nt) + y0[g])
            s_bd = s_bd * jnp.exp(cum_c[g]) + _mm(s_bd, gp[g]) + npart[g]
        s_ref[...] = s_bd
        y = jnp.concatenate(ys, axis=0)

        mean = seg_sum(y) * (1.0 / HEAD_DIM)
        yc = y - mean
        var = seg_sum(yc * yc) * (1.0 / HEAD_DIM)
        yn = yc * lax.rsqrt(var + LNX_EPS) * lnx_w + lnx_b
        bonus = seg_sum(r * kmod * r_k) * v
        y_ref[0, sl, :] = yn + bonus
        return carry

    lax.fori_loop(0, ngroups, group_step, 0)


def _rwkv(z3d, pp, mu_wa, w2a2):
    b, s, a_shift = z3d.shape
    br_w = (a_shift - 2 * LORA_W) // 3
    npairs = br_w // LANES
    zspec = lambda off: pl.BlockSpec((1, s, LANES), lambda bi, hp: (bi, 0, off + hp))
    return pl.pallas_call(
        functools.partial(_rwkv_body, ngroups=s // (RWKV_CHUNK * RWKV_GROUP), group=RWKV_GROUP),
        grid=(b, npairs),
        in_specs=[
            zspec(0), zspec(npairs), zspec(2 * npairs),
            pl.BlockSpec((1, s, LANES), lambda bi, hp: (bi, 0, 3 * npairs)),
            pl.BlockSpec((1, _PP_ROWS, LANES), lambda bi, hp: (hp, 0, 0)),
            pl.BlockSpec((1, LANES), lambda bi, hp: (0, 0)),
            pl.BlockSpec((1, LANES, 2 * LANES), lambda bi, hp: (hp, 0, 0)),
        ],
        out_specs=pl.BlockSpec((1, s, LANES), lambda bi, hp: (bi, 0, hp)),
        out_shape=jax.ShapeDtypeStruct((b, s, br_w), F32),
        scratch_shapes=[pltpu.VMEM((LANES, LANES), F32), pltpu.VMEM((8, LANES), F32)],
        compiler_params=pltpu.CompilerParams(
            dimension_semantics=("arbitrary", "arbitrary"), vmem_limit_bytes=VMEM_LIMIT_BYTES),
        name="rwkv7_scan",
    )(z3d, z3d, z3d, z3d, pp, mu_wa, w2a2)


def _alibi_slopes(n):
    def pow2(m):
        start = 2.0 ** (-8.0 / m)
        return [start ** (i + 1) for i in range(m)]
    if math.log2(n).is_integer():
        return pow2(n)
    cl = 2 ** int(math.floor(math.log2(n)))
    return pow2(cl) + pow2(2 * cl)[0::2][: n - cl]


def _diff_attn_body(q_ref, k_ref, v_ref, slope_ref, lamp_ref, sub_ref, o_ref,
                    acc1_ref, acc2_ref, *, tq, lam_init):
    qi = pl.program_id(2)
    tk = tq
    lane = lax.broadcasted_iota(jnp.int32, (tq, LANES), 1)
    head0 = lane < HEAD_DIM
    q = q_ref[0] * (HEAD_DIM ** -0.5)
    q1 = jnp.where(head0, q, 0.0).astype(BF16)
    q2 = jnp.where(head0, 0.0, q).astype(BF16)
    slope = slope_ref[0]
    slope_col = slope[:, :1]
    rr = lax.broadcasted_iota(jnp.int32, (tq, tk), 0)
    cc = lax.broadcasted_iota(jnp.int32, (tq, tk), 1)
    rel = (rr - cc).astype(F32)

    acc1_ref[...] = jnp.zeros_like(acc1_ref)
    acc2_ref[...] = jnp.zeros_like(acc2_ref)

    def tile(j, carry, diagonal):
        m1, l1, m2, l2 = carry
        ksl = pl.ds(pl.multiple_of(j * tk, tk), tk)
        kt = k_ref[0, ksl, :].astype(BF16)
        vt = v_ref[0, ksl, :].astype(BF16)
        off = ((qi - j) * tq).astype(F32)
        bias = -slope_col * jnp.abs(rel + off)
        s1 = _dot_nt(q1, kt) + bias
        s2 = _dot_nt(q2, kt) + bias
        if diagonal:
            allowed = (cc // ATT_CHUNK) <= (rr // ATT_CHUNK)
            s1 = jnp.where(allowed, s1, MASK_VALUE)
            s2 = jnp.where(allowed, s2, MASK_VALUE)

        def update(s, m, l, acc_ref):
            m_new = jnp.maximum(m, jnp.max(s, axis=-1, keepdims=True))
            scale = jnp.exp(m - m_new)
            p = jnp.exp(s - m_new)
            l_new = scale * l + jnp.sum(p, axis=-1, keepdims=True)
            acc_ref[...] = scale * acc_ref[...] + _dot(p.astype(BF16), vt)
            return m_new, l_new

        m1, l1 = update(s1, m1, l1, acc1_ref)
        m2, l2 = update(s2, m2, l2, acc2_ref)
        return m1, l1, m2, l2

    init = (jnp.full((tq, 1), MASK_VALUE, F32), jnp.zeros((tq, 1), F32),
            jnp.full((tq, 1), MASK_VALUE, F32), jnp.zeros((tq, 1), F32))
    carry = lax.fori_loop(0, qi, lambda j, cr: tile(j, cr, False), init)
    m1, l1, m2, l2 = tile(qi, carry, True)

    lp = lamp_ref[...]
    lam = (jnp.exp(jnp.sum(lp[0:1] * lp[1:2], axis=-1, keepdims=True))
           - jnp.exp(jnp.sum(lp[2:3] * lp[3:4], axis=-1, keepdims=True)) + lam_init)
    o = acc1_ref[...] / l1 - lam * (acc2_ref[...] / l2)
    o_ref[0] = _rms(o, sub_ref[...]) * (1.0 - lam_init)


def _diff_attn(q3d, k3d, v3d, slopes, lam_params, subln, lam_init, tq):
    b, s, br_w = q3d.shape
    nh = br_w // LANES
    return pl.pallas_call(
        functools.partial(_diff_attn_body, tq=tq, lam_init=lam_init),
        grid=(b, nh, s // tq),
        in_specs=[
            pl.BlockSpec((1, tq, LANES), lambda bi, h, qi: (bi, qi, h)),
            pl.BlockSpec((1, s, LANES), lambda bi, h, qi: (bi, 0, h)),
            pl.BlockSpec((1, s, LANES), lambda bi, h, qi: (bi, 0, h)),
            pl.BlockSpec((1, 1, LANES), lambda bi, h, qi: (h, 0, 0)),
            pl.BlockSpec((4, HEAD_DIM), lambda bi, h, qi: (0, 0)),
            pl.BlockSpec((1, LANES), lambda bi, h, qi: (0, 0)),
        ],
        out_specs=pl.BlockSpec((1, tq, LANES), lambda bi, h, qi: (bi, qi, h)),
        out_shape=jax.ShapeDtypeStruct((b, s, br_w), F32),
        scratch_shapes=[pltpu.VMEM((tq, LANES), F32), pltpu.VMEM((tq, LANES), F32)],
        compiler_params=pltpu.CompilerParams(
            dimension_semantics=("arbitrary", "arbitrary", "arbitrary"),
            vmem_limit_bytes=VMEM_LIMIT_BYTES),
        name="diff_attn",
    )(q3d, k3d, v3d, slopes, lam_params, subln)


def _out_body(ybr_ref, gate_ref, qm_ref, gm_ref, x_ref, km_ref, vm_ref, w_ref, pn_ref, o_ref, *, br_w):
    tm = x_ref.shape[0]
    lane = lax.broadcasted_iota(jnp.int32, (tm, MEM_W), 1)
    qm = qm_ref[...]
    km = km_ref[0].astype(BF16)
    vm = vm_ref[0].astype(BF16)
    y_mem = jnp.zeros((tm, MEM_W), F32)
    for h in range(MEM_HEADS):
        in_head = (lane >= h * HEAD_DIM) & (lane < (h + 1) * HEAD_DIM)
        qh = jnp.where(in_head, qm, 0.0).astype(BF16)
        s = _dot_nt(qh, km) * (HEAD_DIM ** -0.5)
        p = jnp.exp(s - jnp.max(s, axis=-1, keepdims=True))
        l = jnp.sum(p, axis=-1, keepdims=True)
        oh = _dot(p.astype(BF16), vm)
        y_mem = y_mem + jnp.where(in_head, oh / l, 0.0)
    y_mem = (y_mem * _silu(gm_ref[...])).astype(BF16)
    y_br = (ybr_ref[...] * _silu(gate_ref[...])).astype(BF16)
    y = _dot(y_br, w_ref[:br_w, :]) + _dot(y_mem, w_ref[br_w:, :])
    o_ref[...] = x_ref[...] + _rms(y, pn_ref[...])


def _out_proj(ybr, gate, qm, gm, x2d, k_mem, v_mem, w_bf16, post_g, seq, tm):
    m, d = x2d.shape
    br_w = ybr.shape[1]
    ml = k_mem.shape[1]
    per_b = seq // tm
    row = lambda w: pl.BlockSpec((tm, w), lambda i: (i, 0))
    return pl.pallas_call(
        functools.partial(_out_body, br_w=br_w),
        grid=(m // tm,),
        in_specs=[
            row(br_w), row(br_w), row(MEM_W), row(MEM_W), row(d),
            pl.BlockSpec((1, ml, MEM_W), lambda i: (i // per_b, 0, 0)),
            pl.BlockSpec((1, ml, MEM_W), lambda i: (i // per_b, 0, 0)),
            pl.BlockSpec((d, d), lambda i: (0, 0)),
            pl.BlockSpec((1, d), lambda i: (0, 0)),
        ],
        out_specs=row(d),
        out_shape=jax.ShapeDtypeStruct((m, d), F32),
        compiler_params=pltpu.CompilerParams(
            dimension_semantics=("arbitrary",), vmem_limit_bytes=VMEM_LIMIT_BYTES),
        name="out_proj",
    )(ybr, gate, qm, gm, x2d, k_mem, v_mem, w_bf16, post_g)


def _rwkv_params(mu, w0, w2, a0, a2, k_k, k_a, r_k, lnx_w, lnx_b, br_w):
    npairs = br_w // LANES
    rows = [mu[:br_w], mu[br_w:2 * br_w], mu[2 * br_w:3 * br_w], w0, a0, k_k, k_a,
            r_k.reshape(-1), lnx_w, lnx_b]
    pp = jnp.stack([r.reshape(npairs, LANES) for r in rows], axis=1)
    pp = jnp.pad(pp, ((0, 0), (0, _PP_ROWS - len(rows)), (0, 0)))
    mu_wa = mu[3 * br_w:].reshape(1, LANES)
    w2p = w2.reshape(LORA_W, npairs, LANES).transpose(1, 0, 2)
    a2p = a2.reshape(LORA_W, npairs, LANES).transpose(1, 0, 2)
    zeros = jnp.zeros_like(w2p)
    w2a2 = jnp.concatenate(
        [jnp.concatenate([w2p, zeros], axis=2), jnp.concatenate([zeros, a2p], axis=2)], axis=1)
    return pp, mu_wa, w2a2


def kernel(x, mem, pre_norm, post_norm, w_out, mem_norm, w_mem_kv, a_w_in, a_shift_mu, a_w0, a_w2,
           a_a0, a_a2, a_k_k, a_k_a, a_r_k, a_lnx_w, a_lnx_b, kv_norm, w_kv, b_w_in, b_lam_q1,
           b_lam_k1, b_lam_q2, b_lam_k2, b_subln):
    bsz, seq, d = x.shape
    depth = pre_norm.shape[0]
    n_a = a_w_in.shape[0]
    br_w = d - MEM_W
    a_shift = 3 * br_w + 2 * LORA_W
    m = bsz * seq
    tm = 256
    x2d = x.reshape(m, d)
    slopes = jnp.asarray(
        np.repeat(np.array(_alibi_slopes(br_w // LANES), np.float32)[:, None, None], LANES, axis=2))

    for l in range(depth):
        k_mem, v_mem = _mem_kv(mem, mem_norm[l].reshape(1, d), w_mem_kv[l].astype(BF16))
        if l < n_a:
            i = l
            segs = [(0, 0, a_shift), (0, a_shift, a_shift + br_w),
                    (0, a_shift + br_w, a_shift + br_w + MEM_W),
                    (0, a_shift + br_w + MEM_W, a_shift + br_w + 2 * MEM_W)]
            z, gate, q_mem, g_mem = _norm_proj(
                x2d, pre_norm[l].reshape(1, d), a_w_in[i].astype(BF16), segs, tm)
            pp, mu_wa, w2a2 = _rwkv_params(
                a_shift_mu[i], a_w0[i], a_w2[i], a_a0[i], a_a2[i], a_k_k[i], a_k_a[i], a_r_k[i],
                a_lnx_w[i], a_lnx_b[i], br_w)
            y_br = _rwkv(z.reshape(bsz, seq, a_shift), pp, mu_wa, w2a2).reshape(m, br_w)
        else:
            i = l - n_a
            if l == n_a:
                w_cat = jnp.concatenate([b_w_in[i], w_kv], axis=1).astype(BF16)
                gains = jnp.stack([pre_norm[l], kv_norm], axis=0)
                segs = [(0, 0, br_w), (0, br_w, 2 * br_w), (0, 2 * br_w, 2 * br_w + MEM_W),
                        (0, 2 * br_w + MEM_W, 2 * br_w + 2 * MEM_W),
                        (1, 2 * br_w + 2 * MEM_W, 3 * br_w + 2 * MEM_W),
                        (1, 3 * br_w + 2 * MEM_W, 4 * br_w + 2 * MEM_W)]
                q, gate, q_mem, g_mem, k_sh, v_sh = _norm_proj(x2d, gains, w_cat, segs, tm)
                k_sh = k_sh.reshape(bsz, seq, br_w)
                v_sh = v_sh.reshape(bsz, seq, br_w)
            else:
                segs = [(0, 0, br_w), (0, br_w, 2 * br_w), (0, 2 * br_w, 2 * br_w + MEM_W),
                        (0, 2 * br_w + MEM_W, 2 * br_w + 2 * MEM_W)]
                q, gate, q_mem, g_mem = _norm_proj(
                    x2d, pre_norm[l].reshape(1, d), b_w_in[i].astype(BF16), segs, tm)
            lam_init = 0.8 - 0.6 * math.exp(-0.3 * l)
            lam_params = jnp.stack([b_lam_q1[i], b_lam_k1[i], b_lam_q2[i], b_lam_k2[i]], axis=0)
            y_br = _diff_attn(q.reshape(bsz, seq, br_w), k_sh, v_sh, slopes, lam_params,
                              b_subln[i].reshape(1, LANES), lam_init, 256).reshape(m, br_w)
        x2d = _out_proj(y_br, gate, q_mem, g_mem, x2d, k_mem, v_mem, w_out[l].astype(BF16),
                        post_norm[l].reshape(1, d), seq, tm)
    return x2d.reshape(bsz, seq, d)
```

```python
import functools
import math

import numpy as np
import jax
import jax.numpy as jnp
from jax import lax
from jax.experimental import pallas as pl
from jax.experimental.pallas import tpu as pltpu

F32 = jnp.float32
BF16 = jnp.bfloat16

HEAD_DIM = 64
LANES = 128
MEM_HEADS = 4
MEM_W = MEM_HEADS * HEAD_DIM
LORA_W = 64
ATT_CHUNK = 64
ATT_Q_TILE = 512
ATT_KV_TILE = 256
RWKV_CHUNK = 64
RWKV_GROUP = 8
NORM_EPS = 1e-6
LNX_EPS = 64e-5
MASK_VALUE = -1e30
VMEM_LIMIT_BYTES = 56 * 1024 * 1024


def _dot(a, b):
    return jnp.dot(a, b, preferred_element_type=F32)


def _dot_nt(a, b):
    return lax.dot_general(a, b, (((1,), (1,)), ((), ())), preferred_element_type=F32)


def _dot_tn(a, b):
    return lax.dot_general(a, b, (((0,), (0,)), ((), ())), preferred_element_type=F32)


def _split(x, pieces):
    out = []
    rem = x
    for i in range(pieces):
        p = rem.astype(BF16)
        out.append(p)
        if i + 1 < pieces:
            rem = rem - p.astype(F32)
    return out


def _mm(a, b, fn=_dot, pa=1, pb=1):
    aps = _split(a, pa)
    bps = _split(b, pb)
    order = max(pa, pb)
    acc = None
    for i, ap in enumerate(aps):
        for j, bp in enumerate(bps):
            if i + j < order:
                t = fn(ap, bp)
                acc = t if acc is None else acc + t
    return acc


def _rms(x, g):
    ms = jnp.mean(x * x, axis=-1, keepdims=True)
    return (x * lax.rsqrt(ms + NORM_EPS)) * g


def _silu(x):
    return x / (1.0 + jnp.exp(-x))


def _norm_proj_body(x_ref, g_ref, w_ref, *o_refs, segments):
    x = x_ref[...]
    ms = jnp.mean(x * x, axis=-1, keepdims=True)
    xn = x * lax.rsqrt(ms + NORM_EPS)
    hs = {}
    for o_ref, (gi, lo, hi) in zip(o_refs, segments):
        if gi not in hs:
            hs[gi] = (xn * g_ref[gi:gi + 1, :]).astype(BF16)
        o_ref[...] = _dot(hs[gi], w_ref[:, lo:hi])


def _norm_proj(x2d, gains, w_bf16, segments, tm):
    m, d = x2d.shape
    n = w_bf16.shape[1]
    ng = gains.shape[0]
    out_shape = [jax.ShapeDtypeStruct((m, hi - lo), F32) for _, lo, hi in segments]
    out_specs = [pl.BlockSpec((tm, hi - lo), lambda i: (i, 0)) for _, lo, hi in segments]
    return pl.pallas_call(
        functools.partial(_norm_proj_body, segments=tuple(segments)),
        grid=(m // tm,),
        in_specs=[
            pl.BlockSpec((tm, d), lambda i: (i, 0)),
            pl.BlockSpec((ng, d), lambda i: (0, 0)),
            pl.BlockSpec((d, n), lambda i: (0, 0)),
        ],
        out_specs=out_specs,
        out_shape=out_shape,
        compiler_params=pltpu.CompilerParams(
            dimension_semantics=("arbitrary",), vmem_limit_bytes=VMEM_LIMIT_BYTES),
        name="norm_proj",
    )(x2d, gains, w_bf16)


def _mem_kv_body(mem_ref, g_ref, w_ref, k_ref, v_ref):
    h = _rms(mem_ref[0], g_ref[...]).astype(BF16)
    kv = _dot(h, w_ref[...])
    k_ref[0] = kv[:, :MEM_W]
    v_ref[0] = kv[:, MEM_W:]


def _mem_kv(mem, g, w_bf16):
    b, ml, d = mem.shape
    return pl.pallas_call(
        _mem_kv_body,
        grid=(b,),
        in_specs=[
            pl.BlockSpec((1, ml, d), lambda i: (i, 0, 0)),
            pl.BlockSpec((1, d), lambda i: (0, 0)),
            pl.BlockSpec((d, 2 * MEM_W), lambda i: (0, 0)),
        ],
        out_specs=[pl.BlockSpec((1, ml, MEM_W), lambda i: (i, 0, 0))] * 2,
        out_shape=[jax.ShapeDtypeStruct((b, ml, MEM_W), F32)] * 2,
        compiler_params=pltpu.CompilerParams(
            dimension_semantics=("arbitrary",), vmem_limit_bytes=VMEM_LIMIT_BYTES),
        name="mem_kv",
    )(mem, g, w_bf16)


_PP_MU_R, _PP_MU_K, _PP_MU_V, _PP_W0, _PP_A0, _PP_KK, _PP_KA, _PP_RK, _PP_LNW, _PP_LNB = range(10)
_PP_ROWS = 16


def _rwkv_body(zr_ref, zk_ref, zv_ref, zwa_ref, pp_ref, muwa_ref, w2a2_ref, y_ref,
               s_ref, prev_ref, *, ngroups, group):
    c = RWKV_CHUNK
    rows = group * c
    lane = lax.broadcasted_iota(jnp.int32, (c, LANES), 1)
    row = lax.broadcasted_iota(jnp.int32, (c, LANES), 0)
    head0 = lane < HEAD_DIM
    scol = jnp.where(head0, lane, lane - HEAD_DIM)
    strict = scol < row
    incl = scol <= row
    r2 = lax.broadcasted_iota(jnp.int32, (LANES, LANES), 0)
    c2 = lax.broadcasted_iota(jnp.int32, (LANES, LANES), 1)
    blockmask = (r2 < HEAD_DIM) == (c2 < HEAD_DIM)
    blockones = jnp.where(blockmask, 1.0, 0.0).astype(BF16)
    tr = lax.broadcasted_iota(jnp.int32, (rows, rows), 0)
    tc = lax.broadcasted_iota(jnp.int32, (rows, rows), 1)
    tril_ones = jnp.where((tc <= tr) & (tc // c == tr // c), 1.0, 0.0).astype(BF16)
    slab_lane = lax.broadcasted_iota(jnp.int32, (rows, LANES), 1)
    slab_head0 = slab_lane < HEAD_DIM
    slab_first = lax.broadcasted_iota(jnp.int32, (rows, LANES), 0) == 0

    def bd(x):
        return jnp.concatenate([jnp.where(head0, x, 0.0), jnp.where(head0, 0.0, x)], axis=0)

    def seg_sum(x):
        hi, lo = _split(x, 2)
        return _dot(hi, blockones) + _dot(lo, blockones)

    pp = pp_ref[0]
    prow = lambda i: pp[i:i + 1, :]
    mu_r, mu_k, mu_v = prow(_PP_MU_R), prow(_PP_MU_K), prow(_PP_MU_V)
    w0, a0 = prow(_PP_W0), prow(_PP_A0)
    k_k, k_a, r_k = prow(_PP_KK), prow(_PP_KA), prow(_PP_RK)
    lnx_w, lnx_b = prow(_PP_LNW), prow(_PP_LNB)
    mu_wa = muwa_ref[...]
    w2a2 = w2a2_ref[0]

    s_ref[...] = jnp.zeros_like(s_ref)
    prev_ref[...] = jnp.zeros_like(prev_ref)

    def shift_mix(ref, slot, sl, mu):
        z = ref[0, sl, :]
        zp = pltpu.roll(z, 1, axis=0)
        zp = jnp.where(slab_first, prev_ref[slot:slot + 1, :], zp)
        prev_ref[slot:slot + 1, :] = z[rows - 1:rows, :]
        return z + (zp - z) * mu

    def chunk_terms(at, rt, bt, kt, bt_end, kt_end, v):
        n = range(len(at))
        lhs = [jnp.concatenate([at[g], rt[g]], axis=0) for g in n]
        xb = [_mm(lhs[g], bd(bt[g]), _dot_nt) for g in n]
        xk = [_mm(lhs[g], bd(kt[g]), _dot_nt) for g in n]
        a_rb = [jnp.where(incl, xb[g][c:], 0.0) for g in n]
        akrk = [jnp.concatenate([jnp.where(strict, xk[g][:c], 0.0), jnp.where(incl, xk[g][c:], 0.0)], axis=0)
                for g in n]
        av = [_mm(akrk[g], bd(v[g])) for g in n]
        x1 = list(at)
        x2 = [av[g][:c] for g in n]
        apow = [jnp.where(strict, xb[g][:c], 0.0) for g in n]
        nfac = int(math.log2(c))
        for i in range(nfac):
            d = [_mm(apow[g], jnp.concatenate([bd(x1[g]), bd(x2[g])], axis=1)) for g in n]
            x1 = [x1[g] + d[g][:, :LANES] for g in n]
            x2 = [x2[g] + d[g][:, LANES:] for g in n]
            if i + 1 < nfac:
                apow = [_mm(apow[g], bd(apow[g])) for g in n]
        wmat, u0 = x1, x2

        d2 = [_mm(a_rb[g], jnp.concatenate([bd(wmat[g]), bd(u0[g])], axis=1)) for g in n]
        rp = [rt[g] + d2[g][:, :LANES] for g in n]
        y0 = [d2[g][:, LANES:] + av[g][c:] for g in n]
        gp = [jnp.where(blockmask, _mm(wmat[g], bt_end[g], _dot_tn), 0.0) for g in n]
        npart = [
            jnp.where(
                blockmask,
                _mm(jnp.concatenate([u0[g], v[g]], axis=0),
                    jnp.concatenate([bt_end[g], kt_end[g]], axis=0), _dot_tn),
                0.0)
            for g in n]
        return rp, y0, gp, npart

    def group_step(gi, carry):
        sl = pl.ds(pl.multiple_of(gi * rows, rows), rows)
        r = shift_mix(zr_ref, 0, sl, mu_r)
        k = shift_mix(zk_ref, 1, sl, mu_k)
        v = shift_mix(zv_ref, 2, sl, mu_v)
        wa = shift_mix(zwa_ref, 3, sl, mu_wa)

        lora = _mm(jnp.where(slab_head0, jnp.tanh(wa), wa), w2a2, _dot, 2, 2)
        wlog = w0 + lora[:, :LANES]
        nw = -wlog
        w = -(jnp.maximum(nw, 0.0) + jnp.log(1.0 + jnp.exp(-jnp.abs(nw)))) - 0.5
        logw = -jnp.exp(w)
        a = 1.0 / (1.0 + jnp.exp(-(a0 + lora[:, LANES:])))

        kk = k * k_k
        kk = kk * lax.rsqrt(jnp.maximum(seg_sum(kk * kk), 1e-12))
        kmod = k * (1.0 + (a - 1.0) * k_a)
        alpha = -kk
        beta = kk * a

        l3 = jnp.concatenate(_split(logw, 3), axis=1)
        cum3 = _dot(tril_ones, l3)
        cum = cum3[:, :LANES] + cum3[:, LANES:2 * LANES] + cum3[:, 2 * LANES:]
        e_neg = jnp.exp(-cum)
        at = alpha * jnp.exp(cum - logw)
        rt = r * jnp.exp(cum)
        bt = beta * e_neg
        kt = kmod * e_neg

        chunks = [slice(g * c, (g + 1) * c) for g in range(group)]
        cum_c = [cum[cs.stop - 1:cs.stop, :] for cs in chunks]
        e_end = [jnp.exp(cum_c[g] - cum[cs]) for g, cs in enumerate(chunks)]
        per = lambda x: [x[cs] for cs in chunks]
        rp, y0, gp, npart = chunk_terms(
            per(at), per(rt), per(bt), per(kt),
            [beta[cs] * e_end[g] for g, cs in enumerate(chunks)],
            [kmod[cs] * e_end[g] for g, cs in enumerate(chunks)], per(v))

        s_bd = s_ref[...]
        ys = []
        for g in range(group):
            ys.append(_mm(rp[g], s_bd, _dot_nt) + y0[g])
            s_bd = s_bd * jnp.exp(cum_c[g]) + _mm(s_bd, gp[g]) + npart[g]
        s_ref[...] = s_bd
        y = jnp.concatenate(ys, axis=0)

        mean = seg_sum(y) * (1.0 / HEAD_DIM)
        yc = y - mean
        var = seg_sum(yc * yc) * (1.0 / HEAD_DIM)
        yn = yc * lax.rsqrt(var + LNX_EPS) * lnx_w + lnx_b
        bonus = seg_sum(r * kmod * r_k) * v
        y_ref[0, sl, :] = yn + bonus
        return carry

    lax.fori_loop(0, ngroups, group_step, 0)


def _rwkv(z3d, pp, mu_wa, w2a2):
    b, s, a_shift = z3d.shape
    br_w = (a_shift - 2 * LORA_W) // 3
    npairs = br_w // LANES
    zspec = lambda off: pl.BlockSpec((1, s, LANES), lambda bi, hp: (bi, 0, off + hp))
    return pl.pallas_call(
        functools.partial(_rwkv_body, ngroups=s // (RWKV_CHUNK * RWKV_GROUP), group=RWKV_GROUP),
        grid=(b, npairs),
        in_specs=[
            zspec(0), zspec(npairs), zspec(2 * npairs),
            pl.BlockSpec((1, s, LANES), lambda bi, hp: (bi, 0, 3 * npairs)),
            pl.BlockSpec((1, _PP_ROWS, LANES), lambda bi, hp: (hp, 0, 0)),
            pl.BlockSpec((1, LANES), lambda bi, hp: (0, 0)),
            pl.BlockSpec((1, LANES, 2 * LANES), lambda bi, hp: (hp, 0, 0)),
        ],
        out_specs=pl.BlockSpec((1, s, LANES), lambda bi, hp: (bi, 0, hp)),
        out_shape=jax.ShapeDtypeStruct((b, s, br_w), F32),
        scratch_shapes=[pltpu.VMEM((LANES, LANES), F32), pltpu.VMEM((8, LANES), F32)],
        compiler_params=pltpu.CompilerParams(
            dimension_semantics=("arbitrary", "arbitrary"), vmem_limit_bytes=VMEM_LIMIT_BYTES),
        name="rwkv7_scan",
    )(z3d, z3d, z3d, z3d, pp, mu_wa, w2a2)


def _alibi_slopes(n):
    def pow2(m):
        start = 2.0 ** (-8.0 / m)
        return [start ** (i + 1) for i in range(m)]
    if math.log2(n).is_integer():
        return pow2(n)
    cl = 2 ** int(math.floor(math.log2(n)))
    return pow2(cl) + pow2(2 * cl)[0::2][: n - cl]


def _diff_attn_body(q_ref, k_ref, v_ref, slope_ref, lamp_ref, sub_ref, o_ref,
                    kb_ref, vt_ref, sa_ref, sb_ref, acc_ref, *, tq, lam_init):
    qi = pl.program_id(2)
    tk = ATT_KV_TILE
    band = tq // tk
    assert band == 2
    nkv = k_ref.shape[1] // tk
    log2e = math.log2(math.e)

    @pl.when(qi == 0)
    def _():
        for t in range(nkv):
            rows = slice(t * tk, (t + 1) * tk)
            kb_ref[rows, :] = k_ref[0, rows, :].astype(BF16)
            vt_ref[:, rows] = v_ref[0, rows, :].T.astype(BF16)

    lane = lax.broadcasted_iota(jnp.int32, (tq, LANES), 1)
    head0 = lane < HEAD_DIM
    q = q_ref[0] * (HEAD_DIM ** -0.5 * log2e)
    qcat = jnp.concatenate([jnp.where(head0, q, 0.0), jnp.where(head0, 0.0, q)], axis=0).astype(BF16)
    slope2 = slope_ref[0][:, :1] * log2e
    key = lax.broadcasted_iota(jnp.int32, (tk, 2 * tq), 0)
    lane2 = lax.broadcasted_iota(jnp.int32, (tk, 2 * tq), 1)
    qry = jnp.where(lane2 < tq, lane2, lane2 - tq)
    keyf = key.astype(F32)
    qryf = qry.astype(F32)
    acc_ref[...] = jnp.zeros_like(acc_ref)

    def scores(j, dst_ref):
        ksl = pl.ds(pl.multiple_of(j * tk, tk), tk)
        dst_ref[...] = _dot_nt(kb_ref[ksl, :], qcat)

    def softmax_pv(j, src_ref, carry, band_index):
        m, l = carry
        vt = vt_ref[:, pl.ds(pl.multiple_of(j * tk, tk), tk)]
        if band_index is None:
            base = slope2 * (j * tk).astype(F32)
            s = src_ref[...] + slope2 * keyf
        else:
            base = slope2 * (qi * tq).astype(F32)
            kpos = key + band_index * tk
            kposf = keyf + float(band_index * tk)
            allowed = (kpos // ATT_CHUNK) <= (qry // ATT_CHUNK)
            s = jnp.where(allowed, src_ref[...] + slope2 * (qryf - jnp.abs(qryf - kposf)), MASK_VALUE)
        m_new = jnp.maximum(m, jnp.max(s, axis=0, keepdims=True) + base)
        scale = jnp.exp2(m - m_new)
        p = jnp.exp2(s + (base - m_new))
        l_new = scale * l + jnp.sum(p, axis=0, keepdims=True)
        acc_ref[...] = scale * acc_ref[...] + _dot(vt, p.astype(BF16))
        return m_new, l_new

    def pair(i, carry):
        scores(2 * i + 1, sb_ref)
        carry = softmax_pv(2 * i, sa_ref, carry, None)
        scores(2 * i + 2, sa_ref)
        return softmax_pv(2 * i + 1, sb_ref, carry, None)

    carry = (jnp.full((1, 2 * tq), MASK_VALUE, F32), jnp.zeros((1, 2 * tq), F32))
    scores(0, sa_ref)
    carry = lax.fori_loop(0, qi, pair, carry)
    scores(qi * band + 1, sb_ref)
    carry = softmax_pv(qi * band, sa_ref, carry, 0)
    _, l = softmax_pv(qi * band + 1, sb_ref, carry, 1)

    lp = lamp_ref[...]
    lam = (jnp.exp(jnp.sum(lp[0:1] * lp[1:2], axis=-1, keepdims=True))
           - jnp.exp(jnp.sum(lp[2:3] * lp[3:4], axis=-1, keepdims=True)) + lam_init)
    on = acc_ref[...] / l
    o = on[:, :tq] - lam * on[:, tq:]
    ms = jnp.mean(o * o, axis=0, keepdims=True)
    o = o * lax.rsqrt(ms + NORM_EPS) * sub_ref[...] * (1.0 - lam_init)
    o_ref[0] = o.T


def _diff_attn(q3d, k3d, v3d, slopes, lam_params, subln_col, lam_init, tq):
    b, s, br_w = q3d.shape
    nh = br_w // LANES
    tk = ATT_KV_TILE
    return pl.pallas_call(
        functools.partial(_diff_attn_body, tq=tq, lam_init=lam_init),
        grid=(b, nh, s // tq),
        in_specs=[
            pl.BlockSpec((1, tq, LANES), lambda bi, h, qi: (bi, qi, h)),
            pl.BlockSpec((1, s, LANES), lambda bi, h, qi: (bi, 0, h)),
            pl.BlockSpec((1, s, LANES), lambda bi, h, qi: (bi, 0, h)),
            pl.BlockSpec((1, 1, LANES), lambda bi, h, qi: (h, 0, 0)),
            pl.BlockSpec((4, HEAD_DIM), lambda bi, h, qi: (0, 0)),
            pl.BlockSpec((LANES, 1), lambda bi, h, qi: (0, 0)),
        ],
        out_specs=pl.BlockSpec((1, tq, LANES), lambda bi, h, qi: (bi, qi, h)),
        out_shape=jax.ShapeDtypeStruct((b, s, br_w), F32),
        scratch_shapes=[pltpu.VMEM((s, LANES), BF16), pltpu.VMEM((LANES, s), BF16),
                        pltpu.VMEM((tk, 2 * tq), F32), pltpu.VMEM((tk, 2 * tq), F32),
                        pltpu.VMEM((LANES, 2 * tq), F32)],
        compiler_params=pltpu.CompilerParams(
            dimension_semantics=("arbitrary", "arbitrary", "arbitrary"),
            vmem_limit_bytes=VMEM_LIMIT_BYTES),
        name="diff_attn",
    )(q3d, k3d, v3d, slopes, lam_params, subln_col)


def _out_body(ybr_ref, gate_ref, qm_ref, gm_ref, x_ref, km_ref, vm_ref, w_ref, pn_ref, o_ref, *, br_w):
    tm = x_ref.shape[0]
    lane = lax.broadcasted_iota(jnp.int32, (tm, MEM_W), 1)
    qm = qm_ref[...]
    km = km_ref[0].astype(BF16)
    vm = vm_ref[0].astype(BF16)
    y_mem = jnp.zeros((tm, MEM_W), F32)
    for h in range(MEM_HEADS):
        in_head = (lane >= h * HEAD_DIM) & (lane < (h + 1) * HEAD_DIM)
        qh = jnp.where(in_head, qm, 0.0).astype(BF16)
        s = _dot_nt(qh, km) * (HEAD_DIM ** -0.5)
        p = jnp.exp(s - jnp.max(s, axis=-1, keepdims=True))
        l = jnp.sum(p, axis=-1, keepdims=True)
        oh = _dot(p.astype(BF16), vm)
        y_mem = y_mem + jnp.where(in_head, oh / l, 0.0)
    y_mem = (y_mem * _silu(gm_ref[...])).astype(BF16)
    y_br = (ybr_ref[...] * _silu(gate_ref[...])).astype(BF16)
    y = _dot(y_br, w_ref[:br_w, :]) + _dot(y_mem, w_ref[br_w:, :])
    o_ref[...] = x_ref[...] + _rms(y, pn_ref[...])


def _out_proj(ybr, gate, qm, gm, x2d, k_mem, v_mem, w_bf16, post_g, seq, tm):
    m, d = x2d.shape
    br_w = ybr.shape[1]
    ml = k_mem.shape[1]
    per_b = seq // tm
    row = lambda w: pl.BlockSpec((tm, w), lambda i: (i, 0))
    return pl.pallas_call(
        functools.partial(_out_body, br_w=br_w),
        grid=(m // tm,),
        in_specs=[
            row(br_w), row(br_w), row(MEM_W), row(MEM_W), row(d),
            pl.BlockSpec((1, ml, MEM_W), lambda i: (i // per_b, 0, 0)),
            pl.BlockSpec((1, ml, MEM_W), lambda i: (i // per_b, 0, 0)),
            pl.BlockSpec((d, d), lambda i: (0, 0)),
            pl.BlockSpec((1, d), lambda i: (0, 0)),
        ],
        out_specs=row(d),
        out_shape=jax.ShapeDtypeStruct((m, d), F32),
        compiler_params=pltpu.CompilerParams(
            dimension_semantics=("arbitrary",), vmem_limit_bytes=VMEM_LIMIT_BYTES),
        name="out_proj",
    )(ybr, gate, qm, gm, x2d, k_mem, v_mem, w_bf16, post_g)


def _rwkv_params(mu, w0, w2, a0, a2, k_k, k_a, r_k, lnx_w, lnx_b, br_w):
    npairs = br_w // LANES
    rows = [mu[:br_w], mu[br_w:2 * br_w], mu[2 * br_w:3 * br_w], w0, a0, k_k, k_a,
            r_k.reshape(-1), lnx_w, lnx_b]
    pp = jnp.stack([r.reshape(npairs, LANES) for r in rows], axis=1)
    pp = jnp.pad(pp, ((0, 0), (0, _PP_ROWS - len(rows)), (0, 0)))
    mu_wa = mu[3 * br_w:].reshape(1, LANES)
    w2p = w2.reshape(LORA_W, npairs, LANES).transpose(1, 0, 2)
    a2p = a2.reshape(LORA_W, npairs, LANES).transpose(1, 0, 2)
    zeros = jnp.zeros_like(w2p)
    w2a2 = jnp.concatenate(
        [jnp.concatenate([w2p, zeros], axis=2), jnp.concatenate([zeros, a2p], axis=2)], axis=1)
    return pp, mu_wa, w2a2


def kernel(x, mem, pre_norm, post_norm, w_out, mem_norm, w_mem_kv, a_w_in, a_shift_mu, a_w0, a_w2,
           a_a0, a_a2, a_k_k, a_k_a, a_r_k, a_lnx_w, a_lnx_b, kv_norm, w_kv, b_w_in, b_lam_q1,
           b_lam_k1, b_lam_q2, b_lam_k2, b_subln):
    bsz, seq, d = x.shape
    depth = pre_norm.shape[0]
    n_a = a_w_in.shape[0]
    br_w = d - MEM_W
    a_shift = 3 * br_w + 2 * LORA_W
    m = bsz * seq
    tm = 256
    x2d = x.reshape(m, d)
    slopes = jnp.asarray(
        np.repeat(np.array(_alibi_slopes(br_w // LANES), np.float32)[:, None, None], LANES, axis=2))

    for l in range(depth):
        k_mem, v_mem = _mem_kv(mem, mem_norm[l].reshape(1, d), w_mem_kv[l].astype(BF16))
        if l < n_a:
            i = l
            segs = [(0, 0, a_shift), (0, a_shift, a_shift + br_w),
                    (0, a_shift + br_w, a_shift + br_w + MEM_W),
                    (0, a_shift + br_w + MEM_W, a_shift + br_w + 2 * MEM_W)]
            z, gate, q_mem, g_mem = _norm_proj(
                x2d, pre_norm[l].reshape(1, d), a_w_in[i].astype(BF16), segs, tm)
            pp, mu_wa, w2a2 = _rwkv_params(
                a_shift_mu[i], a_w0[i], a_w2[i], a_a0[i], a_a2[i], a_k_k[i], a_k_a[i], a_r_k[i],
                a_lnx_w[i], a_lnx_b[i], br_w)
            y_br = _rwkv(z.reshape(bsz, seq, a_shift), pp, mu_wa, w2a2).reshape(m, br_w)
        else:
            i = l - n_a
            if l == n_a:
                w_cat = jnp.concatenate([b_w_in[i], w_kv], axis=1).astype(BF16)
                gains = jnp.stack([pre_norm[l], kv_norm], axis=0)
                segs = [(0, 0, br_w), (0, br_w, 2 * br_w), (0, 2 * br_w, 2 * br_w + MEM_W),
                        (0, 2 * br_w + MEM_W, 2 * br_w + 2 * MEM_W),
                        (1, 2 * br_w + 2 * MEM_W, 3 * br_w + 2 * MEM_W),
                        (1, 3 * br_w + 2 * MEM_W, 4 * br_w + 2 * MEM_W)]
                q, gate, q_mem, g_mem, k_sh, v_sh = _norm_proj(x2d, gains, w_cat, segs, tm)
                k_sh = k_sh.reshape(bsz, seq, br_w)
                v_sh = v_sh.reshape(bsz, seq, br_w)
            else:
                segs = [(0, 0, br_w), (0, br_w, 2 * br_w), (0, 2 * br_w, 2 * br_w + MEM_W),
                        (0, 2 * br_w + MEM_W, 2 * br_w + 2 * MEM_W)]
                q, gate, q_mem, g_mem = _norm_proj(
                    x2d, pre_norm[l].reshape(1, d), b_w_in[i].astype(BF16), segs, tm)
            lam_init = 0.8 - 0.6 * math.exp(-0.3 * l)
            lam_params = jnp.stack([b_lam_q1[i], b_lam_k1[i], b_lam_q2[i], b_lam_k2[i]], axis=0)
            y_br = _diff_attn(q.reshape(bsz, seq, br_w), k_sh, v_sh, slopes, lam_params,
                              b_subln[i].reshape(LANES, 1), lam_init, ATT_Q_TILE).reshape(m, br_w)
        x2d = _out_proj(y_br, gate, q_mem, g_mem, x2d, k_mem, v_mem, w_out[l].astype(BF16),
                        post_norm[l].reshape(1, d), seq, tm)
    return x2d.reshape(bsz, seq, d)
```

```python
import functools
import math

import numpy as np
import jax
import jax.numpy as jnp
from jax import lax
from jax.experimental import pallas as pl
from jax.experimental.pallas import tpu as pltpu

F32 = jnp.float32
BF16 = jnp.bfloat16

HEAD_DIM = 64
LANES = 128
MEM_HEADS = 4
MEM_W = MEM_HEADS * HEAD_DIM
LORA_W = 64
ATT_CHUNK = 64
ATT_Q_TILE = 512
ATT_KV_TILE = 256
ATT_BIAS_PIECES = 3
ATT_V_PAD = 16
RWKV_CHUNK = 64
RWKV_GROUP = 8
RWKV_STAGGER = 2
NORM_EPS = 1e-6
LNX_EPS = 64e-5
MASK_VALUE = -1e30
VMEM_LIMIT_BYTES = 56 * 1024 * 1024


def _dot(a, b):
    return jnp.dot(a, b, preferred_element_type=F32)


def _dot_nt(a, b):
    return lax.dot_general(a, b, (((1,), (1,)), ((), ())), preferred_element_type=F32)


def _dot_tn(a, b):
    return lax.dot_general(a, b, (((0,), (0,)), ((), ())), preferred_element_type=F32)


def _split(x, pieces):
    out = []
    rem = x
    for i in range(pieces):
        p = rem.astype(BF16)
        out.append(p)
        if i + 1 < pieces:
            rem = rem - p.astype(F32)
    return out


def _mm(a, b, fn=_dot, pa=1, pb=1):
    aps = _split(a, pa)
    bps = _split(b, pb)
    order = max(pa, pb)
    acc = None
    for i, ap in enumerate(aps):
        for j, bp in enumerate(bps):
            if i + j < order:
                t = fn(ap, bp)
                acc = t if acc is None else acc + t
    return acc


def _rms(x, g):
    ms = jnp.mean(x * x, axis=-1, keepdims=True)
    return (x * lax.rsqrt(ms + NORM_EPS)) * g


def _silu(x):
    return x / (1.0 + jnp.exp(-x))


def _norm_proj_body(x_ref, g_ref, w_ref, *o_refs, segments):
    x = x_ref[...]
    ms = jnp.mean(x * x, axis=-1, keepdims=True)
    xn = x * lax.rsqrt(ms + NORM_EPS)
    hs = {}
    for o_ref, (gi, lo, hi) in zip(o_refs, segments):
        if gi not in hs:
            hs[gi] = (xn * g_ref[gi:gi + 1, :]).astype(BF16)
        o_ref[...] = _dot(hs[gi], w_ref[:, lo:hi])


def _norm_proj(x2d, gains, w_bf16, segments, tm):
    m, d = x2d.shape
    n = w_bf16.shape[1]
    ng = gains.shape[0]
    out_shape = [jax.ShapeDtypeStruct((m, hi - lo), F32) for _, lo, hi in segments]
    out_specs = [pl.BlockSpec((tm, hi - lo), lambda i: (i, 0)) for _, lo, hi in segments]
    return pl.pallas_call(
        functools.partial(_norm_proj_body, segments=tuple(segments)),
        grid=(m // tm,),
        in_specs=[
            pl.BlockSpec((tm, d), lambda i: (i, 0)),
            pl.BlockSpec((ng, d), lambda i: (0, 0)),
            pl.BlockSpec((d, n), lambda i: (0, 0)),
        ],
        out_specs=out_specs,
        out_shape=out_shape,
        compiler_params=pltpu.CompilerParams(
            dimension_semantics=("arbitrary",), vmem_limit_bytes=VMEM_LIMIT_BYTES),
        name="norm_proj",
    )(x2d, gains, w_bf16)


def _mem_kv_body(mem_ref, g_ref, w_ref, k_ref, v_ref):
    h = _rms(mem_ref[0], g_ref[...]).astype(BF16)
    kv = _dot(h, w_ref[...])
    k_ref[0] = kv[:, :MEM_W]
    v_ref[0] = kv[:, MEM_W:]


def _mem_kv(mem, g, w_bf16):
    b, ml, d = mem.shape
    return pl.pallas_call(
        _mem_kv_body,
        grid=(b,),
        in_specs=[
            pl.BlockSpec((1, ml, d), lambda i: (i, 0, 0)),
            pl.BlockSpec((1, d), lambda i: (0, 0)),
            pl.BlockSpec((d, 2 * MEM_W), lambda i: (0, 0)),
        ],
        out_specs=[pl.BlockSpec((1, ml, MEM_W), lambda i: (i, 0, 0))] * 2,
        out_shape=[jax.ShapeDtypeStruct((b, ml, MEM_W), F32)] * 2,
        compiler_params=pltpu.CompilerParams(
            dimension_semantics=("arbitrary",), vmem_limit_bytes=VMEM_LIMIT_BYTES),
        name="mem_kv",
    )(mem, g, w_bf16)


_PP_MU_R, _PP_MU_K, _PP_MU_V, _PP_W0, _PP_A0, _PP_KK, _PP_KA, _PP_RK, _PP_LNW, _PP_LNB = range(10)
_PP_ROWS = 16


def _rwkv_body(zr_ref, zk_ref, zv_ref, zwa_ref, pp_ref, muwa_ref, w2a2_ref, y_ref,
               s_ref, prev_ref, *, nsteps, stagger, group):
    c = RWKV_CHUNK
    rows = group * c
    lane = lax.broadcasted_iota(jnp.int32, (c, LANES), 1)
    row = lax.broadcasted_iota(jnp.int32, (c, LANES), 0)
    head0 = lane < HEAD_DIM
    scol = jnp.where(head0, lane, lane - HEAD_DIM)
    strict = scol < row
    incl = scol <= row
    r2 = lax.broadcasted_iota(jnp.int32, (LANES, LANES), 0)
    c2 = lax.broadcasted_iota(jnp.int32, (LANES, LANES), 1)
    blockmask = (r2 < HEAD_DIM) == (c2 < HEAD_DIM)
    blockones = jnp.where(blockmask, 1.0, 0.0).astype(BF16)
    eye = r2 == c2
    tr = lax.broadcasted_iota(jnp.int32, (rows, rows), 0)
    tc = lax.broadcasted_iota(jnp.int32, (rows, rows), 1)
    tril_ones = jnp.where((tc <= tr) & (tc // c == tr // c), 1.0, 0.0).astype(BF16)
    slab_lane = lax.broadcasted_iota(jnp.int32, (rows, LANES), 1)
    slab_head0 = slab_lane < HEAD_DIM
    slab_first = lax.broadcasted_iota(jnp.int32, (rows, LANES), 0) == 0

    def bd(x):
        return jnp.concatenate([jnp.where(head0, x, 0.0), jnp.where(head0, 0.0, x)], axis=0)

    def seg_sum(x):
        return _dot(x.astype(BF16), blockones)

    pp = pp_ref[0]
    prow = lambda i: pp[i:i + 1, :]
    mu_r, mu_k, mu_v = prow(_PP_MU_R), prow(_PP_MU_K), prow(_PP_MU_V)
    w0, a0 = prow(_PP_W0), prow(_PP_A0)
    k_k, k_a, r_k = prow(_PP_KK), prow(_PP_KA), prow(_PP_RK)
    lnx_w, lnx_b = prow(_PP_LNW), prow(_PP_LNB)
    mu_wa = muwa_ref[...]
    w2a2 = w2a2_ref[0]

    s_ref[...] = jnp.zeros_like(s_ref)
    prev_ref[...] = jnp.zeros_like(prev_ref)

    def shift_mix(ref, slot, sl, mu):
        z = ref[0, sl, :]
        zp = pltpu.roll(z, 1, axis=0)
        zp = jnp.where(slab_first, prev_ref[slot:slot + 1, :], zp)
        prev_ref[slot:slot + 1, :] = z[rows - 1:rows, :]
        return z + (zp - z) * mu


    def prepare(sl):
        r = shift_mix(zr_ref, 0, sl, mu_r)
        k = shift_mix(zk_ref, 1, sl, mu_k)
        yield
        v = shift_mix(zv_ref, 2, sl, mu_v)
        wa = shift_mix(zwa_ref, 3, sl, mu_wa)
        lora = _mm(jnp.where(slab_head0, jnp.tanh(wa), wa), w2a2)
        yield
        wlog = w0 + lora[:, :LANES]
        nw = -wlog
        w = -(jnp.maximum(nw, 0.0) + jnp.log(1.0 + jnp.exp(-jnp.abs(nw)))) - 0.5
        logw = -jnp.exp(w)
        a = 1.0 / (1.0 + jnp.exp(-(a0 + lora[:, LANES:])))
        kk = k * k_k
        kk = kk * lax.rsqrt(jnp.maximum(seg_sum(kk * kk), 1e-12))
        yield
        kmod = k * (1.0 + (a - 1.0) * k_a)
        alpha = -kk
        beta = kk * a
        cum2 = _dot(tril_ones, jnp.concatenate(_split(logw, 2), axis=1))
        cum = cum2[:, :LANES] + cum2[:, LANES:]
        yield
        e_neg = jnp.exp(-cum)
        at = alpha * jnp.exp(cum - logw)
        rt = r * jnp.exp(cum)
        bt = beta * e_neg
        kt = kmod * e_neg
        yield
        chunks = [slice(g * c, (g + 1) * c) for g in range(group)]
        cum_c = [cum[cs.stop - 1:cs.stop, :] for cs in chunks]
        e_end = [jnp.exp(cum_c[g] - cum[cs]) for g, cs in enumerate(chunks)]
        per = lambda x: [x[cs] for cs in chunks]
        return dict(
            at=per(at), rt=per(rt), bt=per(bt), kt=per(kt), v=per(v),
            bt_end=[beta[cs] * e_end[g] for g, cs in enumerate(chunks)],
            kt_end=[kmod[cs] * e_end[g] for g, cs in enumerate(chunks)],
            p_c=[jnp.exp(x) for x in cum_c], v_slab=v, rk=r * kmod * r_k)

    def chains(p):
        at, rt, bt, kt, bt_end, kt_end, v = (p[x] for x in ("at", "rt", "bt", "kt", "bt_end", "kt_end", "v"))
        n = range(group)
        lhs = [jnp.concatenate([at[g], rt[g]], axis=0) for g in n]
        xbk = [_mm(lhs[g], jnp.concatenate([bd(bt[g]), bd(kt[g])], axis=0), _dot_nt) for g in n]
        yield
        a_rb = [jnp.where(incl, xbk[g][c:, :LANES], 0.0) for g in n]
        akrk = [jnp.concatenate([jnp.where(strict, xbk[g][:c, LANES:], 0.0),
                                 jnp.where(incl, xbk[g][c:, LANES:], 0.0)], axis=0) for g in n]
        av = [_mm(akrk[g], bd(v[g])) for g in n]
        yield
        x1 = list(at)
        x2 = [av[g][:c] for g in n]
        apow = [jnp.where(strict, xbk[g][:c, :LANES], 0.0) for g in n]
        nfac = int(math.log2(c))
        for i in range(nfac):
            last = i + 1 == nfac
            rhs = [[bd(x1[g]), bd(x2[g])] + ([] if last else [bd(apow[g])]) for g in n]
            d = [_mm(apow[g], jnp.concatenate(rhs[g], axis=1)) for g in n]
            x1 = [x1[g] + d[g][:, :LANES] for g in n]
            x2 = [x2[g] + d[g][:, LANES:2 * LANES] for g in n]
            if not last:
                apow = [d[g][:, 2 * LANES:] for g in n]
            yield
        wmat, u0 = x1, x2
        d2 = [_mm(a_rb[g], jnp.concatenate([bd(wmat[g]), bd(u0[g])], axis=1)) for g in n]
        rp = [rt[g] + d2[g][:, :LANES] for g in n]
        y0 = [d2[g][:, LANES:] + av[g][c:] for g in n]
        yield
        gp = [jnp.where(blockmask, _mm(wmat[g], bt_end[g], _dot_tn), 0.0) for g in n]
        npart = [
            jnp.where(
                blockmask,
                _mm(jnp.concatenate([u0[g], v[g]], axis=0),
                    jnp.concatenate([bt_end[g], kt_end[g]], axis=0), _dot_tn),
                0.0)
            for g in n]
        yield
        spans = [[(jnp.where(eye, gp[g] + p["p_c"][g], gp[g]), npart[g]) for g in n]]
        while len(spans[-1]) > 1:
            prev = spans[-1]
            nxt = []
            for i in range(0, len(prev), 2):
                (ma, na), (mb, nb) = prev[i], prev[i + 1]
                prod = _mm(jnp.concatenate([ma, na], axis=0), mb)
                nxt.append((prod[:LANES], prod[LANES:] + nb))
            spans.append(nxt)
            yield
        return dict(rp=rp, y0=y0, spans=spans, v_slab=p["v_slab"], rk=p["rk"])

    def finish(t, sl):
        spans = t["spans"]
        states = {0: s_ref[...]}
        top = len(spans) - 1
        m_all, n_all = spans[top][0]
        s_ref[...] = _mm(states[0], m_all) + n_all
        yield
        for level in range(top, 0, -1):
            width = 1 << level
            for lo in range(0, group, width):
                ma, na = spans[level - 1][lo >> (level - 1)]
                states[lo + width // 2] = _mm(states[lo], ma) + na
            yield
        y = jnp.concatenate(
            [_mm(t["rp"][g], states[g], _dot_nt) + t["y0"][g] for g in range(group)], axis=0)
        yield
        mean = seg_sum(y) * (1.0 / HEAD_DIM)
        yc = y - mean
        var = seg_sum(yc * yc) * (1.0 / HEAD_DIM)
        yield
        yn = yc * lax.rsqrt(var + LNX_EPS) * lnx_w + lnx_b
        bonus = seg_sum(t["rk"]) * t["v_slab"]
        y_ref[0, sl, :] = yn + bonus

    def run(*gens):
        results = [None] * len(gens)
        live = list(range(len(gens)))
        while live:
            for i in list(live):
                try:
                    next(gens[i])
                except StopIteration as stop:
                    results[i] = stop.value
                    live.remove(i)
        return results

    def step(si, carry):
        slabs = [pl.ds(pl.multiple_of((si * stagger + j) * rows, rows), rows) for j in range(stagger)]
        prepared, = run(prepare(slabs[0]))
        done = None
        for j in range(stagger):
            gens = [chains(prepared)]
            if j + 1 < stagger:
                gens.append(prepare(slabs[j + 1]))
            if done is not None:
                gens.append(finish(done, slabs[j - 1]))
            out = run(*gens)
            done = out[0]
            if j + 1 < stagger:
                prepared = out[1]
        run(finish(done, slabs[stagger - 1]))
        return carry

    lax.fori_loop(0, nsteps, step, 0)


def _rwkv(z3d, pp, mu_wa, w2a2):
    b, s, a_shift = z3d.shape
    br_w = (a_shift - 2 * LORA_W) // 3
    npairs = br_w // LANES
    zspec = lambda off: pl.BlockSpec((1, s, LANES), lambda bi, hp: (bi, 0, off + hp))
    return pl.pallas_call(
        functools.partial(_rwkv_body, nsteps=s // (RWKV_CHUNK * RWKV_GROUP * RWKV_STAGGER),
                          stagger=RWKV_STAGGER, group=RWKV_GROUP),
        grid=(b, npairs),
        in_specs=[
            zspec(0), zspec(npairs), zspec(2 * npairs),
            pl.BlockSpec((1, s, LANES), lambda bi, hp: (bi, 0, 3 * npairs)),
            pl.BlockSpec((1, _PP_ROWS, LANES), lambda bi, hp: (hp, 0, 0)),
            pl.BlockSpec((1, LANES), lambda bi, hp: (0, 0)),
            pl.BlockSpec((1, LANES, 2 * LANES), lambda bi, hp: (hp, 0, 0)),
        ],
        out_specs=pl.BlockSpec((1, s, LANES), lambda bi, hp: (bi, 0, hp)),
        out_shape=jax.ShapeDtypeStruct((b, s, br_w), F32),
        scratch_shapes=[pltpu.VMEM((LANES, LANES), F32), pltpu.VMEM((8, LANES), F32)],
        compiler_params=pltpu.CompilerParams(
            dimension_semantics=("arbitrary", "arbitrary"), vmem_limit_bytes=VMEM_LIMIT_BYTES),
        name="rwkv7_scan",
    )(z3d, z3d, z3d, z3d, pp, mu_wa, w2a2)


def _alibi_slopes(n):
    def pow2(m):
        start = 2.0 ** (-8.0 / m)
        return [start ** (i + 1) for i in range(m)]
    if math.log2(n).is_integer():
        return pow2(n)
    cl = 2 ** int(math.floor(math.log2(n)))
    return pow2(cl) + pow2(2 * cl)[0::2][: n - cl]


def _diff_attn_body(q_ref, k_ref, v_ref, slope_ref, lamp_ref, sub_ref, o_ref,
                    kb_ref, vt_ref, band_ref, sa_ref, sb_ref, acc_ref, *, tq, lam_init):
    qi = pl.program_id(2)
    tk = ATT_KV_TILE
    band = tq // tk
    assert band == 2
    nkv = k_ref.shape[1] // tk
    log2e = math.log2(math.e)
    slope2 = slope_ref[0][:, :1] * log2e
    aug_lane = lax.broadcasted_iota(jnp.int32, (tk, LANES), 1)

    @pl.when(qi == 0)
    def _():
        ones_row = lax.broadcasted_iota(jnp.int32, (ATT_V_PAD, tk), 0) == 0
        for t in range(nkv):
            rows = slice(t * tk, (t + 1) * tk)
            kb_ref[rows, :LANES] = k_ref[0, rows, :].astype(BF16)
            pos = (lax.broadcasted_iota(jnp.int32, (tk, LANES), 0) + t * tk).astype(F32)
            pieces = _split(slope2 * pos, ATT_BIAS_PIECES)
            aug = jnp.zeros((tk, LANES), F32)
            for i, piece in enumerate(pieces):
                aug = jnp.where(aug_lane == i, piece.astype(F32), aug)
            kb_ref[rows, LANES:] = aug.astype(BF16)
            vt_ref[:LANES, rows] = v_ref[0, rows, :].T.astype(BF16)
            vt_ref[LANES:, rows] = jnp.where(ones_row, 1.0, 0.0).astype(BF16)
        key = lax.broadcasted_iota(jnp.int32, (tk, 2 * tq), 0).astype(F32)
        lane2 = lax.broadcasted_iota(jnp.int32, (1, 2 * tq), 1)
        qry = jnp.where(lane2 < tq, lane2, lane2 - tq)
        qlim = ((qry // ATT_CHUNK + 1) * ATT_CHUNK).astype(F32)
        for d in range(band):
            kpos = key + float(d * tk)
            band_ref[d] = jnp.where(
                kpos < qlim, (-2.0 * slope2) * jnp.maximum(kpos - qry.astype(F32), 0.0), MASK_VALUE)

    lane = lax.broadcasted_iota(jnp.int32, (2 * tq, LANES), 1)
    head0 = lane < HEAD_DIM
    first = lax.broadcasted_iota(jnp.int32, (2 * tq, LANES), 0) < tq
    q = q_ref[0] * (HEAD_DIM ** -0.5 * log2e)
    q2x = jnp.concatenate([q, q], axis=0)
    qcat = jnp.concatenate(
        [jnp.where(first == head0, q2x, 0.0), jnp.where(lane < ATT_BIAS_PIECES, 1.0, 0.0)],
        axis=1).astype(BF16)
    acc_ref[...] = jnp.zeros_like(acc_ref)

    def scores(j, dst_ref):
        dst_ref[...] = _dot_nt(kb_ref[j * tk:(j + 1) * tk, :], qcat)

    def softmax_pv(j, src_ref, m, band_index):
        vt = vt_ref[:, j * tk:(j + 1) * tk]
        s = src_ref[...]
        if band_index is not None:
            s = s + band_ref[band_index]
        m_new = jnp.maximum(m, jnp.max(s, axis=0, keepdims=True))
        p = jnp.exp2(s - m_new)
        acc_ref[...] = jnp.exp2(m - m_new) * acc_ref[...] + _dot(vt, p.astype(BF16))
        return m_new

    for n in range(nkv // band):
        @pl.when(qi == n)
        def _(n=n):
            ntiles = (n + 1) * band
            bufs = (sa_ref, sb_ref)
            m = jnp.full((1, 2 * tq), MASK_VALUE, F32)
            scores(0, bufs[0])
            for j in range(ntiles):
                if j + 1 < ntiles:
                    scores(j + 1, bufs[(j + 1) % 2])
                m = softmax_pv(j, bufs[j % 2], m, None if j < n * band else j - n * band)

    lp = lamp_ref[...]
    lam = (jnp.exp(jnp.sum(lp[0:1] * lp[1:2], axis=-1, keepdims=True))
           - jnp.exp(jnp.sum(lp[2:3] * lp[3:4], axis=-1, keepdims=True)) + lam_init)
    on = acc_ref[:LANES, :] * (1.0 / acc_ref[LANES:LANES + 1, :])
    o = on[:, :tq] - lam * on[:, tq:]
    ms = jnp.mean(o * o, axis=0, keepdims=True)
    o = o * lax.rsqrt(ms + NORM_EPS) * sub_ref[...] * (1.0 - lam_init)
    o_ref[0] = o.T


def _diff_attn(q3d, k3d, v3d, slopes, lam_params, subln_col, lam_init, tq):
    b, s, br_w = q3d.shape
    nh = br_w // LANES
    tk = ATT_KV_TILE
    return pl.pallas_call(
        functools.partial(_diff_attn_body, tq=tq, lam_init=lam_init),
        grid=(b, nh, s // tq),
        in_specs=[
            pl.BlockSpec((1, tq, LANES), lambda bi, h, qi: (bi, qi, h)),
            pl.BlockSpec((1, s, LANES), lambda bi, h, qi: (bi, 0, h)),
            pl.BlockSpec((1, s, LANES), lambda bi, h, qi: (bi, 0, h)),
            pl.BlockSpec((1, 1, LANES), lambda bi, h, qi: (h, 0, 0)),
            pl.BlockSpec((4, HEAD_DIM), lambda bi, h, qi: (0, 0)),
            pl.BlockSpec((LANES, 1), lambda bi, h, qi: (0, 0)),
        ],
        out_specs=pl.BlockSpec((1, tq, LANES), lambda bi, h, qi: (bi, qi, h)),
        out_shape=jax.ShapeDtypeStruct((b, s, br_w), F32),
        scratch_shapes=[pltpu.VMEM((s, 2 * LANES), BF16), pltpu.VMEM((LANES + ATT_V_PAD, s), BF16),
                        pltpu.VMEM((tq // tk, tk, 2 * tq), F32),
                        pltpu.VMEM((tk, 2 * tq), F32), pltpu.VMEM((tk, 2 * tq), F32),
                        pltpu.VMEM((LANES + ATT_V_PAD, 2 * tq), F32)],
        compiler_params=pltpu.CompilerParams(
            dimension_semantics=("arbitrary", "arbitrary", "arbitrary"),
            vmem_limit_bytes=VMEM_LIMIT_BYTES),
        name="diff_attn",
    )(q3d, k3d, v3d, slopes, lam_params, subln_col)


def _out_body(ybr_ref, gate_ref, qm_ref, gm_ref, x_ref, km_ref, vm_ref, w_ref, pn_ref, o_ref, *, br_w):
    tm = x_ref.shape[0]
    lane = lax.broadcasted_iota(jnp.int32, (tm, MEM_W), 1)
    qm = qm_ref[...]
    km = km_ref[0].astype(BF16)
    vm = vm_ref[0].astype(BF16)
    y_mem = jnp.zeros((tm, MEM_W), F32)
    for h in range(MEM_HEADS):
        in_head = (lane >= h * HEAD_DIM) & (lane < (h + 1) * HEAD_DIM)
        qh = jnp.where(in_head, qm, 0.0).astype(BF16)
        s = _dot_nt(qh, km) * (HEAD_DIM ** -0.5)
        p = jnp.exp(s - jnp.max(s, axis=-1, keepdims=True))
        l = jnp.sum(p, axis=-1, keepdims=True)
        oh = _dot(p.astype(BF16), vm)
        y_mem = y_mem + jnp.where(in_head, oh / l, 0.0)
    y_mem = (y_mem * _silu(gm_ref[...])).astype(BF16)
    y_br = (ybr_ref[...] * _silu(gate_ref[...])).astype(BF16)
    y = _dot(y_br, w_ref[:br_w, :]) + _dot(y_mem, w_ref[br_w:, :])
    o_ref[...] = x_ref[...] + _rms(y, pn_ref[...])


def _out_proj(ybr, gate, qm, gm, x2d, k_mem, v_mem, w_bf16, post_g, seq, tm):
    m, d = x2d.shape
    br_w = ybr.shape[1]
    ml = k_mem.shape[1]
    per_b = seq // tm
    row = lambda w: pl.BlockSpec((tm, w), lambda i: (i, 0))
    return pl.pallas_call(
        functools.partial(_out_body, br_w=br_w),
        grid=(m // tm,),
        in_specs=[
            row(br_w), row(br_w), row(MEM_W), row(MEM_W), row(d),
            pl.BlockSpec((1, ml, MEM_W), lambda i: (i // per_b, 0, 0)),
            pl.BlockSpec((1, ml, MEM_W), lambda i: (i // per_b, 0, 0)),
            pl.BlockSpec((d, d), lambda i: (0, 0)),
            pl.BlockSpec((1, d), lambda i: (0, 0)),
        ],
        out_specs=row(d),
        out_shape=jax.ShapeDtypeStruct((m, d), F32),
        compiler_params=pltpu.CompilerParams(
            dimension_semantics=("arbitrary",), vmem_limit_bytes=VMEM_LIMIT_BYTES),
        name="out_proj",
    )(ybr, gate, qm, gm, x2d, k_mem, v_mem, w_bf16, post_g)


def _rwkv_params(mu, w0, w2, a0, a2, k_k, k_a, r_k, lnx_w, lnx_b, br_w):
    npairs = br_w // LANES
    rows = [mu[:br_w], mu[br_w:2 * br_w], mu[2 * br_w:3 * br_w], w0, a0, k_k, k_a,
            r_k.reshape(-1), lnx_w, lnx_b]
    pp = jnp.stack([r.reshape(npairs, LANES) for r in rows], axis=1)
    pp = jnp.pad(pp, ((0, 0), (0, _PP_ROWS - len(rows)), (0, 0)))
    mu_wa = mu[3 * br_w:].reshape(1, LANES)
    w2p = w2.reshape(LORA_W, npairs, LANES).transpose(1, 0, 2)
    a2p = a2.reshape(LORA_W, npairs, LANES).transpose(1, 0, 2)
    zeros = jnp.zeros_like(w2p)
    w2a2 = jnp.concatenate(
        [jnp.concatenate([w2p, zeros], axis=2), jnp.concatenate([zeros, a2p], axis=2)], axis=1)
    return pp, mu_wa, w2a2


def kernel(x, mem, pre_norm, post_norm, w_out, mem_norm, w_mem_kv, a_w_in, a_shift_mu, a_w0, a_w2,
           a_a0, a_a2, a_k_k, a_k_a, a_r_k, a_lnx_w, a_lnx_b, kv_norm, w_kv, b_w_in, b_lam_q1,
           b_lam_k1, b_lam_q2, b_lam_k2, b_subln):
    bsz, seq, d = x.shape
    depth = pre_norm.shape[0]
    n_a = a_w_in.shape[0]
    br_w = d - MEM_W
    a_shift = 3 * br_w + 2 * LORA_W
    m = bsz * seq
    tm = 256
    x2d = x.reshape(m, d)
    slopes = jnp.asarray(
        np.repeat(np.array(_alibi_slopes(br_w // LANES), np.float32)[:, None, None], LANES, axis=2))

    for l in range(depth):
        k_mem, v_mem = _mem_kv(mem, mem_norm[l].reshape(1, d), w_mem_kv[l].astype(BF16))
        if l < n_a:
            i = l
            segs = [(0, 0, a_shift), (0, a_shift, a_shift + br_w),
                    (0, a_shift + br_w, a_shift + br_w + MEM_W),
                    (0, a_shift + br_w + MEM_W, a_shift + br_w + 2 * MEM_W)]
            z, gate, q_mem, g_mem = _norm_proj(
                x2d, pre_norm[l].reshape(1, d), a_w_in[i].astype(BF16), segs, tm)
            pp, mu_wa, w2a2 = _rwkv_params(
                a_shift_mu[i], a_w0[i], a_w2[i], a_a0[i], a_a2[i], a_k_k[i], a_k_a[i], a_r_k[i],
                a_lnx_w[i], a_lnx_b[i], br_w)
            y_br = _rwkv(z.reshape(bsz, seq, a_shift), pp, mu_wa, w2a2).reshape(m, br_w)
        else:
            i = l - n_a
            if l == n_a:
                w_cat = jnp.concatenate([b_w_in[i], w_kv], axis=1).astype(BF16)
                gains = jnp.stack([pre_norm[l], kv_norm], axis=0)
                segs = [(0, 0, br_w), (0, br_w, 2 * br_w), (0, 2 * br_w, 2 * br_w + MEM_W),
                        (0, 2 * br_w + MEM_W, 2 * br_w + 2 * MEM_W),
                        (1, 2 * br_w + 2 * MEM_W, 3 * br_w + 2 * MEM_W),
                        (1, 3 * br_w + 2 * MEM_W, 4 * br_w + 2 * MEM_W)]
                q, gate, q_mem, g_mem, k_sh, v_sh = _norm_proj(x2d, gains, w_cat, segs, tm)
                k_sh = k_sh.reshape(bsz, seq, br_w)
                v_sh = v_sh.reshape(bsz, seq, br_w)
            else:
                segs = [(0, 0, br_w), (0, br_w, 2 * br_w), (0, 2 * br_w, 2 * br_w + MEM_W),
                        (0, 2 * br_w + MEM_W, 2 * br_w + 2 * MEM_W)]
                q, gate, q_mem, g_mem = _norm_proj(
                    x2d, pre_norm[l].reshape(1, d), b_w_in[i].astype(BF16), segs, tm)
            lam_init = 0.8 - 0.6 * math.exp(-0.3 * l)
            lam_params = jnp.stack([b_lam_q1[i], b_lam_k1[i], b_lam_q2[i], b_lam_k2[i]], axis=0)
            y_br = _diff_attn(q.reshape(bsz, seq, br_w), k_sh, v_sh, slopes, lam_params,
                              b_subln[i].reshape(LANES, 1), lam_init, ATT_Q_TILE).reshape(m, br_w)
        x2d = _out_proj(y_br, gate, q_mem, g_mem, x2d, k_mem, v_mem, w_out[l].astype(BF16),
                        post_norm[l].reshape(1, d), seq, tm)
    return x2d.reshape(bsz, seq, d)
```

```python
import functools
import math

import numpy as np
import jax
import jax.numpy as jnp
from jax import lax
from jax.experimental import pallas as pl
from jax.experimental.pallas import tpu as pltpu

F32 = jnp.float32
BF16 = jnp.bfloat16
ACT_DTYPE = BF16

HEAD_DIM = 64
LANES = 128
MEM_HEADS = 4
MEM_W = MEM_HEADS * HEAD_DIM
LORA_W = 64
ATT_CHUNK = 64
ATT_Q_TILE = 512
ATT_KV_TILE = 256
ATT_BIAS_PIECES = 3
ATT_V_PAD = 16
RWKV_CHUNK = 64
RWKV_GROUP = 8
RWKV_STAGGER = 2
NORM_EPS = 1e-6
LNX_EPS = 64e-5
MASK_VALUE = -1e30
VMEM_LIMIT_BYTES = 56 * 1024 * 1024


def _dot(a, b):
    return jnp.dot(a, b, preferred_element_type=F32)


def _dot_nt(a, b):
    return lax.dot_general(a, b, (((1,), (1,)), ((), ())), preferred_element_type=F32)


def _dot_tn(a, b):
    return lax.dot_general(a, b, (((0,), (0,)), ((), ())), preferred_element_type=F32)


def _split(x, pieces):
    out = []
    rem = x
    for i in range(pieces):
        p = rem.astype(BF16)
        out.append(p)
        if i + 1 < pieces:
            rem = rem - p.astype(F32)
    return out


def _mm(a, b, fn=_dot, pa=1, pb=1):
    aps = _split(a, pa)
    bps = _split(b, pb)
    order = max(pa, pb)
    acc = None
    for i, ap in enumerate(aps):
        for j, bp in enumerate(bps):
            if i + j < order:
                t = fn(ap, bp)
                acc = t if acc is None else acc + t
    return acc


def _rms(x, g):
    ms = jnp.mean(x * x, axis=-1, keepdims=True)
    return (x * lax.rsqrt(ms + NORM_EPS)) * g


def _silu(x):
    return x / (1.0 + jnp.exp(-x))


def _norm_proj_body(x_ref, g_ref, w_ref, *o_refs, segments):
    x = x_ref[...]
    ms = jnp.mean(x * x, axis=-1, keepdims=True)
    xn = x * lax.rsqrt(ms + NORM_EPS)
    hs = {}
    for o_ref, (gi, lo, hi) in zip(o_refs, segments):
        if gi not in hs:
            hs[gi] = (xn * g_ref[gi:gi + 1, :]).astype(BF16)
        o_ref[...] = _dot(hs[gi], w_ref[:, lo:hi]).astype(o_ref.dtype)


def _norm_proj(x2d, gains, w_bf16, segments, tm):
    m, d = x2d.shape
    n = w_bf16.shape[1]
    ng = gains.shape[0]
    out_shape = [jax.ShapeDtypeStruct((m, hi - lo), ACT_DTYPE) for _, lo, hi in segments]
    out_specs = [pl.BlockSpec((tm, hi - lo), lambda i: (i, 0)) for _, lo, hi in segments]
    return pl.pallas_call(
        functools.partial(_norm_proj_body, segments=tuple(segments)),
        grid=(m // tm,),
        in_specs=[
            pl.BlockSpec((tm, d), lambda i: (i, 0)),
            pl.BlockSpec((ng, d), lambda i: (0, 0)),
            pl.BlockSpec((d, n), lambda i: (0, 0)),
        ],
        out_specs=out_specs,
        out_shape=out_shape,
        compiler_params=pltpu.CompilerParams(
            dimension_semantics=("arbitrary",), vmem_limit_bytes=VMEM_LIMIT_BYTES),
        name="norm_proj",
    )(x2d, gains, w_bf16)


def _mem_kv_body(mem_ref, g_ref, w_ref, k_ref, v_ref):
    h = _rms(mem_ref[0], g_ref[...]).astype(BF16)
    kv = _dot(h, w_ref[...])
    k_ref[0] = kv[:, :MEM_W].astype(k_ref.dtype)
    v_ref[0] = kv[:, MEM_W:].astype(v_ref.dtype)


def _mem_kv(mem, g, w_bf16):
    b, ml, d = mem.shape
    return pl.pallas_call(
        _mem_kv_body,
        grid=(b,),
        in_specs=[
            pl.BlockSpec((1, ml, d), lambda i: (i, 0, 0)),
            pl.BlockSpec((1, d), lambda i: (0, 0)),
            pl.BlockSpec((d, 2 * MEM_W), lambda i: (0, 0)),
        ],
        out_specs=[pl.BlockSpec((1, ml, MEM_W), lambda i: (i, 0, 0))] * 2,
        out_shape=[jax.ShapeDtypeStruct((b, ml, MEM_W), ACT_DTYPE)] * 2,
        compiler_params=pltpu.CompilerParams(
            dimension_semantics=("arbitrary",), vmem_limit_bytes=VMEM_LIMIT_BYTES),
        name="mem_kv",
    )(mem, g, w_bf16)


_PP_MU_R, _PP_MU_K, _PP_MU_V, _PP_W0, _PP_A0, _PP_KK, _PP_KA, _PP_RK, _PP_LNW, _PP_LNB = range(10)
_PP_ROWS = 16


def _rwkv_body(zr_ref, zk_ref, zv_ref, zwa_ref, pp_ref, muwa_ref, w2a2_ref, y_ref,
               s_ref, prev_ref, *, nsteps, stagger, group):
    c = RWKV_CHUNK
    rows = group * c
    lane = lax.broadcasted_iota(jnp.int32, (c, LANES), 1)
    row = lax.broadcasted_iota(jnp.int32, (c, LANES), 0)
    head0 = lane < HEAD_DIM
    scol = jnp.where(head0, lane, lane - HEAD_DIM)
    strict = scol < row
    incl = scol <= row
    r2 = lax.broadcasted_iota(jnp.int32, (LANES, LANES), 0)
    c2 = lax.broadcasted_iota(jnp.int32, (LANES, LANES), 1)
    blockmask = (r2 < HEAD_DIM) == (c2 < HEAD_DIM)
    blockones = jnp.where(blockmask, 1.0, 0.0).astype(BF16)
    eye = r2 == c2
    tr = lax.broadcasted_iota(jnp.int32, (rows, rows), 0)
    tc = lax.broadcasted_iota(jnp.int32, (rows, rows), 1)
    tril_ones = jnp.where((tc <= tr) & (tc // c == tr // c), 1.0, 0.0).astype(BF16)
    slab_lane = lax.broadcasted_iota(jnp.int32, (rows, LANES), 1)
    slab_head0 = slab_lane < HEAD_DIM
    slab_first = lax.broadcasted_iota(jnp.int32, (rows, LANES), 0) == 0

    def bd(x):
        return jnp.concatenate([jnp.where(head0, x, 0.0), jnp.where(head0, 0.0, x)], axis=0)

    def seg_sum(x):
        return _dot(x.astype(BF16), blockones)

    pp = pp_ref[0]
    prow = lambda i: pp[i:i + 1, :]
    mu_r, mu_k, mu_v = prow(_PP_MU_R), prow(_PP_MU_K), prow(_PP_MU_V)
    w0, a0 = prow(_PP_W0), prow(_PP_A0)
    k_k, k_a, r_k = prow(_PP_KK), prow(_PP_KA), prow(_PP_RK)
    lnx_w, lnx_b = prow(_PP_LNW), prow(_PP_LNB)
    mu_wa = muwa_ref[...]
    w2a2 = w2a2_ref[0]

    s_ref[...] = jnp.zeros_like(s_ref)
    prev_ref[...] = jnp.zeros_like(prev_ref)

    def shift_mix(ref, slot, sl, mu):
        z = ref[0, sl, :].astype(F32)
        zp = pltpu.roll(z, 1, axis=0)
        zp = jnp.where(slab_first, prev_ref[slot:slot + 1, :], zp)
        prev_ref[slot:slot + 1, :] = z[rows - 1:rows, :]
        return z + (zp - z) * mu


    def prepare(sl):
        r = shift_mix(zr_ref, 0, sl, mu_r)
        k = shift_mix(zk_ref, 1, sl, mu_k)
        yield
        v = shift_mix(zv_ref, 2, sl, mu_v)
        wa = shift_mix(zwa_ref, 3, sl, mu_wa)
        lora = _mm(jnp.where(slab_head0, jnp.tanh(wa), wa), w2a2)
        yield
        wlog = w0 + lora[:, :LANES]
        nw = -wlog
        w = -(jnp.maximum(nw, 0.0) + jnp.log(1.0 + jnp.exp(-jnp.abs(nw)))) - 0.5
        logw = -jnp.exp(w)
        a = 1.0 / (1.0 + jnp.exp(-(a0 + lora[:, LANES:])))
        kk = k * k_k
        kk = kk * lax.rsqrt(jnp.maximum(seg_sum(kk * kk), 1e-12))
        yield
        kmod = k * (1.0 + (a - 1.0) * k_a)
        alpha = -kk
        beta = kk * a
        cum2 = _dot(tril_ones, jnp.concatenate(_split(logw, 2), axis=1))
        cum = cum2[:, :LANES] + cum2[:, LANES:]
        yield
        e_neg = jnp.exp(-cum)
        at = alpha * jnp.exp(cum - logw)
        rt = r * jnp.exp(cum)
        bt = beta * e_neg
        kt = kmod * e_neg
        yield
        chunks = [slice(g * c, (g + 1) * c) for g in range(group)]
        cum_c = [cum[cs.stop - 1:cs.stop, :] for cs in chunks]
        e_end = [jnp.exp(cum_c[g] - cum[cs]) for g, cs in enumerate(chunks)]
        per = lambda x: [x[cs] for cs in chunks]
        return dict(
            at=per(at), rt=per(rt), bt=per(bt), kt=per(kt), v=per(v),
            bt_end=[beta[cs] * e_end[g] for g, cs in enumerate(chunks)],
            kt_end=[kmod[cs] * e_end[g] for g, cs in enumerate(chunks)],
            p_c=[jnp.exp(x) for x in cum_c], v_slab=v, rk=r * kmod * r_k)

    def chains(p):
        at, rt, bt, kt, bt_end, kt_end, v = (p[x] for x in ("at", "rt", "bt", "kt", "bt_end", "kt_end", "v"))
        n = range(group)
        lhs = [jnp.concatenate([at[g], rt[g]], axis=0) for g in n]
        xbk = [_mm(lhs[g], jnp.concatenate([bd(bt[g]), bd(kt[g])], axis=0), _dot_nt) for g in n]
        yield
        a_rb = [jnp.where(incl, xbk[g][c:, :LANES], 0.0) for g in n]
        akrk = [jnp.concatenate([jnp.where(strict, xbk[g][:c, LANES:], 0.0),
                                 jnp.where(incl, xbk[g][c:, LANES:], 0.0)], axis=0) for g in n]
        av = [_mm(akrk[g], bd(v[g])) for g in n]
        yield
        x1 = list(at)
        x2 = [av[g][:c] for g in n]
        apow = [jnp.where(strict, xbk[g][:c, :LANES], 0.0) for g in n]
        nfac = int(math.log2(c))
        for i in range(nfac):
            last = i + 1 == nfac
            rhs = [[bd(x1[g]), bd(x2[g])] + ([] if last else [bd(apow[g])]) for g in n]
            d = [_mm(apow[g], jnp.concatenate(rhs[g], axis=1)) for g in n]
            x1 = [x1[g] + d[g][:, :LANES] for g in n]
            x2 = [x2[g] + d[g][:, LANES:2 * LANES] for g in n]
            if not last:
                apow = [d[g][:, 2 * LANES:] for g in n]
            yield
        wmat, u0 = x1, x2
        d2 = [_mm(a_rb[g], jnp.concatenate([bd(wmat[g]), bd(u0[g])], axis=1)) for g in n]
        rp = [rt[g] + d2[g][:, :LANES] for g in n]
        y0 = [d2[g][:, LANES:] + av[g][c:] for g in n]
        yield
        gp = [jnp.where(blockmask, _mm(wmat[g], bt_end[g], _dot_tn), 0.0) for g in n]
        npart = [
            jnp.where(
                blockmask,
                _mm(jnp.concatenate([u0[g], v[g]], axis=0),
                    jnp.concatenate([bt_end[g], kt_end[g]], axis=0), _dot_tn),
                0.0)
            for g in n]
        yield
        spans = [[(jnp.where(eye, gp[g] + p["p_c"][g], gp[g]), npart[g]) for g in n]]
        while len(spans[-1]) > 1:
            prev = spans[-1]
            nxt = []
            for i in range(0, len(prev), 2):
                (ma, na), (mb, nb) = prev[i], prev[i + 1]
                prod = _mm(jnp.concatenate([ma, na], axis=0), mb)
                nxt.append((prod[:LANES], prod[LANES:] + nb))
            spans.append(nxt)
            yield
        return dict(rp=rp, y0=y0, spans=spans, v_slab=p["v_slab"], rk=p["rk"])

    def finish(t, sl):
        spans = t["spans"]
        states = {0: s_ref[...]}
        top = len(spans) - 1
        m_all, n_all = spans[top][0]
        s_ref[...] = _mm(states[0], m_all) + n_all
        yield
        for level in range(top, 0, -1):
            width = 1 << level
            for lo in range(0, group, width):
                ma, na = spans[level - 1][lo >> (level - 1)]
                states[lo + width // 2] = _mm(states[lo], ma) + na
            yield
        y = jnp.concatenate(
            [_mm(t["rp"][g], states[g], _dot_nt) + t["y0"][g] for g in range(group)], axis=0)
        yield
        mean = seg_sum(y) * (1.0 / HEAD_DIM)
        yc = y - mean
        var = seg_sum(yc * yc) * (1.0 / HEAD_DIM)
        yield
        yn = yc * lax.rsqrt(var + LNX_EPS) * lnx_w + lnx_b
        bonus = seg_sum(t["rk"]) * t["v_slab"]
        y_ref[0, sl, :] = (yn + bonus).astype(y_ref.dtype)

    def run(*gens):
        results = [None] * len(gens)
        live = list(range(len(gens)))
        while live:
            for i in list(live):
                try:
                    next(gens[i])
                except StopIteration as stop:
                    results[i] = stop.value
                    live.remove(i)
        return results

    def step(si, carry):
        slabs = [pl.ds(pl.multiple_of((si * stagger + j) * rows, rows), rows) for j in range(stagger)]
        prepared, = run(prepare(slabs[0]))
        done = None
        for j in range(stagger):
            gens = [chains(prepared)]
            if j + 1 < stagger:
                gens.append(prepare(slabs[j + 1]))
            if done is not None:
                gens.append(finish(done, slabs[j - 1]))
            out = run(*gens)
            done = out[0]
            if j + 1 < stagger:
                prepared = out[1]
        run(finish(done, slabs[stagger - 1]))
        return carry

    lax.fori_loop(0, nsteps, step, 0)


def _rwkv(z3d, pp, mu_wa, w2a2):
    b, s, a_shift = z3d.shape
    br_w = (a_shift - 2 * LORA_W) // 3
    npairs = br_w // LANES
    zspec = lambda off: pl.BlockSpec((1, s, LANES), lambda bi, hp: (bi, 0, off + hp))
    return pl.pallas_call(
        functools.partial(_rwkv_body, nsteps=s // (RWKV_CHUNK * RWKV_GROUP * RWKV_STAGGER),
                          stagger=RWKV_STAGGER, group=RWKV_GROUP),
        grid=(b, npairs),
        in_specs=[
            zspec(0), zspec(npairs), zspec(2 * npairs),
            pl.BlockSpec((1, s, LANES), lambda bi, hp: (bi, 0, 3 * npairs)),
            pl.BlockSpec((1, _PP_ROWS, LANES), lambda bi, hp: (hp, 0, 0)),
            pl.BlockSpec((1, LANES), lambda bi, hp: (0, 0)),
            pl.BlockSpec((1, LANES, 2 * LANES), lambda bi, hp: (hp, 0, 0)),
        ],
        out_specs=pl.BlockSpec((1, s, LANES), lambda bi, hp: (bi, 0, hp)),
        out_shape=jax.ShapeDtypeStruct((b, s, br_w), ACT_DTYPE),
        scratch_shapes=[pltpu.VMEM((LANES, LANES), F32), pltpu.VMEM((8, LANES), F32)],
        compiler_params=pltpu.CompilerParams(
            dimension_semantics=("arbitrary", "arbitrary"), vmem_limit_bytes=VMEM_LIMIT_BYTES),
        name="rwkv7_scan",
    )(z3d, z3d, z3d, z3d, pp, mu_wa, w2a2)


def _alibi_slopes(n):
    def pow2(m):
        start = 2.0 ** (-8.0 / m)
        return [start ** (i + 1) for i in range(m)]
    if math.log2(n).is_integer():
        return pow2(n)
    cl = 2 ** int(math.floor(math.log2(n)))
    return pow2(cl) + pow2(2 * cl)[0::2][: n - cl]


def _diff_attn_body(q_ref, k_ref, v_ref, slope_ref, lamp_ref, sub_ref, o_ref,
                    kb_ref, vt_ref, band_ref, sa_ref, sb_ref, acc_ref, *, tq, lam_init):
    qi = pl.program_id(2)
    tk = ATT_KV_TILE
    band = tq // tk
    assert band == 2
    nkv = k_ref.shape[1] // tk
    log2e = math.log2(math.e)
    slope2 = slope_ref[0][:, :1] * log2e
    aug_lane = lax.broadcasted_iota(jnp.int32, (tk, LANES), 1)

    @pl.when(qi == 0)
    def _():
        ones_row = lax.broadcasted_iota(jnp.int32, (ATT_V_PAD, tk), 0) == 0
        for t in range(nkv):
            rows = slice(t * tk, (t + 1) * tk)
            kb_ref[rows, :LANES] = k_ref[0, rows, :].astype(BF16)
            pos = (lax.broadcasted_iota(jnp.int32, (tk, LANES), 0) + t * tk).astype(F32)
            pieces = _split(slope2 * pos, ATT_BIAS_PIECES)
            aug = jnp.zeros((tk, LANES), F32)
            for i, piece in enumerate(pieces):
                aug = jnp.where(aug_lane == i, piece.astype(F32), aug)
            kb_ref[rows, LANES:] = aug.astype(BF16)
            vt_ref[:LANES, rows] = v_ref[0, rows, :].astype(F32).T.astype(BF16)
            vt_ref[LANES:, rows] = jnp.where(ones_row, 1.0, 0.0).astype(BF16)
        key = lax.broadcasted_iota(jnp.int32, (tk, 2 * tq), 0).astype(F32)
        lane2 = lax.broadcasted_iota(jnp.int32, (1, 2 * tq), 1)
        qry = jnp.where(lane2 < tq, lane2, lane2 - tq)
        qlim = ((qry // ATT_CHUNK + 1) * ATT_CHUNK).astype(F32)
        for d in range(band):
            kpos = key + float(d * tk)
            band_ref[d] = jnp.where(
                kpos < qlim, (-2.0 * slope2) * jnp.maximum(kpos - qry.astype(F32), 0.0), MASK_VALUE)

    lane = lax.broadcasted_iota(jnp.int32, (2 * tq, LANES), 1)
    head0 = lane < HEAD_DIM
    first = lax.broadcasted_iota(jnp.int32, (2 * tq, LANES), 0) < tq
    q = q_ref[0].astype(F32) * (HEAD_DIM ** -0.5 * log2e)
    q2x = jnp.concatenate([q, q], axis=0)
    qcat = jnp.concatenate(
        [jnp.where(first == head0, q2x, 0.0), jnp.where(lane < ATT_BIAS_PIECES, 1.0, 0.0)],
        axis=1).astype(BF16)
    acc_ref[...] = jnp.zeros_like(acc_ref)

    def scores(j, dst_ref):
        dst_ref[...] = _dot_nt(kb_ref[j * tk:(j + 1) * tk, :], qcat)

    def softmax_pv(j, src_ref, m, band_index):
        vt = vt_ref[:, j * tk:(j + 1) * tk]
        s = src_ref[...]
        if band_index is not None:
            s = s + band_ref[band_index]
        m_new = jnp.maximum(m, jnp.max(s, axis=0, keepdims=True))
        p = jnp.exp2(s - m_new)
        acc_ref[...] = jnp.exp2(m - m_new) * acc_ref[...] + _dot(vt, p.astype(BF16))
        return m_new

    for n in range(nkv // band):
        @pl.when(qi == n)
        def _(n=n):
            ntiles = (n + 1) * band
            bufs = (sa_ref, sb_ref)
            m = jnp.full((1, 2 * tq), MASK_VALUE, F32)
            scores(0, bufs[0])
            for j in range(ntiles):
                if j + 1 < ntiles:
                    scores(j + 1, bufs[(j + 1) % 2])
                m = softmax_pv(j, bufs[j % 2], m, None if j < n * band else j - n * band)

    lp = lamp_ref[...]
    lam = (jnp.exp(jnp.sum(lp[0:1] * lp[1:2], axis=-1, keepdims=True))
           - jnp.exp(jnp.sum(lp[2:3] * lp[3:4], axis=-1, keepdims=True)) + lam_init)
    on = acc_ref[:LANES, :] * (1.0 / acc_ref[LANES:LANES + 1, :])
    o = on[:, :tq] - lam * on[:, tq:]
    ms = jnp.mean(o * o, axis=0, keepdims=True)
    o = o * lax.rsqrt(ms + NORM_EPS) * sub_ref[...] * (1.0 - lam_init)
    o_ref[0] = o.T.astype(o_ref.dtype)


def _diff_attn(q3d, k3d, v3d, slopes, lam_params, subln_col, lam_init, tq):
    b, s, br_w = q3d.shape
    nh = br_w // LANES
    tk = ATT_KV_TILE
    return pl.pallas_call(
        functools.partial(_diff_attn_body, tq=tq, lam_init=lam_init),
        grid=(b, nh, s // tq),
        in_specs=[
            pl.BlockSpec((1, tq, LANES), lambda bi, h, qi: (bi, qi, h)),
            pl.BlockSpec((1, s, LANES), lambda bi, h, qi: (bi, 0, h)),
            pl.BlockSpec((1, s, LANES), lambda bi, h, qi: (bi, 0, h)),
            pl.BlockSpec((1, 1, LANES), lambda bi, h, qi: (h, 0, 0)),
            pl.BlockSpec((4, HEAD_DIM), lambda bi, h, qi: (0, 0)),
            pl.BlockSpec((LANES, 1), lambda bi, h, qi: (0, 0)),
        ],
        out_specs=pl.BlockSpec((1, tq, LANES), lambda bi, h, qi: (bi, qi, h)),
        out_shape=jax.ShapeDtypeStruct((b, s, br_w), ACT_DTYPE),
        scratch_shapes=[pltpu.VMEM((s, 2 * LANES), BF16), pltpu.VMEM((LANES + ATT_V_PAD, s), BF16),
                        pltpu.VMEM((tq // tk, tk, 2 * tq), F32),
                        pltpu.VMEM((tk, 2 * tq), F32), pltpu.VMEM((tk, 2 * tq), F32),
                        pltpu.VMEM((LANES + ATT_V_PAD, 2 * tq), F32)],
        compiler_params=pltpu.CompilerParams(
            dimension_semantics=("arbitrary", "arbitrary", "arbitrary"),
            vmem_limit_bytes=VMEM_LIMIT_BYTES),
        name="diff_attn",
    )(q3d, k3d, v3d, slopes, lam_params, subln_col)


def _out_body(ybr_ref, gate_ref, qm_ref, gm_ref, x_ref, km_ref, vm_ref, w_ref, pn_ref, o_ref, *, br_w):
    tm = x_ref.shape[0]
    lane = lax.broadcasted_iota(jnp.int32, (tm, MEM_W), 1)
    qm = qm_ref[...].astype(F32)
    km = km_ref[0].astype(BF16)
    vm = vm_ref[0].astype(BF16)
    y_mem = jnp.zeros((tm, MEM_W), F32)
    for h in range(MEM_HEADS):
        in_head = (lane >= h * HEAD_DIM) & (lane < (h + 1) * HEAD_DIM)
        qh = jnp.where(in_head, qm, 0.0).astype(BF16)
        s = _dot_nt(qh, km) * (HEAD_DIM ** -0.5)
        p = jnp.exp(s - jnp.max(s, axis=-1, keepdims=True))
        l = jnp.sum(p, axis=-1, keepdims=True)
        oh = _dot(p.astype(BF16), vm)
        y_mem = y_mem + jnp.where(in_head, oh / l, 0.0)
    y_mem = (y_mem * _silu(gm_ref[...].astype(F32))).astype(BF16)
    y_br = (ybr_ref[...].astype(F32) * _silu(gate_ref[...].astype(F32))).astype(BF16)
    y = _dot(y_br, w_ref[:br_w, :]) + _dot(y_mem, w_ref[br_w:, :])
    o_ref[...] = x_ref[...] + _rms(y, pn_ref[...])


def _out_proj(ybr, gate, qm, gm, x2d, k_mem, v_mem, w_bf16, post_g, seq, tm):
    m, d = x2d.shape
    br_w = ybr.shape[1]
    ml = k_mem.shape[1]
    per_b = seq // tm
    row = lambda w: pl.BlockSpec((tm, w), lambda i: (i, 0))
    return pl.pallas_call(
        functools.partial(_out_body, br_w=br_w),
        grid=(m // tm,),
        in_specs=[
            row(br_w), row(br_w), row(MEM_W), row(MEM_W), row(d),
            pl.BlockSpec((1, ml, MEM_W), lambda i: (i // per_b, 0, 0)),
            pl.BlockSpec((1, ml, MEM_W), lambda i: (i // per_b, 0, 0)),
            pl.BlockSpec((d, d), lambda i: (0, 0)),
            pl.BlockSpec((1, d), lambda i: (0, 0)),
        ],
        out_specs=row(d),
        out_shape=jax.ShapeDtypeStruct((m, d), F32),
        compiler_params=pltpu.CompilerParams(
            dimension_semantics=("arbitrary",), vmem_limit_bytes=VMEM_LIMIT_BYTES),
        name="out_proj",
    )(ybr, gate, qm, gm, x2d, k_mem, v_mem, w_bf16, post_g)


def _rwkv_params(mu, w0, w2, a0, a2, k_k, k_a, r_k, lnx_w, lnx_b, br_w):
    npairs = br_w // LANES
    rows = [mu[:br_w], mu[br_w:2 * br_w], mu[2 * br_w:3 * br_w], w0, a0, k_k, k_a,
            r_k.reshape(-1), lnx_w, lnx_b]
    pp = jnp.stack([r.reshape(npairs, LANES) for r in rows], axis=1)
    pp = jnp.pad(pp, ((0, 0), (0, _PP_ROWS - len(rows)), (0, 0)))
    mu_wa = mu[3 * br_w:].reshape(1, LANES)
    w2p = w2.reshape(LORA_W, npairs, LANES).transpose(1, 0, 2)
    a2p = a2.reshape(LORA_W, npairs, LANES).transpose(1, 0, 2)
    zeros = jnp.zeros_like(w2p)
    w2a2 = jnp.concatenate(
        [jnp.concatenate([w2p, zeros], axis=2), jnp.concatenate([zeros, a2p], axis=2)], axis=1)
    return pp, mu_wa, w2a2


def kernel(x, mem, pre_norm, post_norm, w_out, mem_norm, w_mem_kv, a_w_in, a_shift_mu, a_w0, a_w2,
           a_a0, a_a2, a_k_k, a_k_a, a_r_k, a_lnx_w, a_lnx_b, kv_norm, w_kv, b_w_in, b_lam_q1,
           b_lam_k1, b_lam_q2, b_lam_k2, b_subln):
    bsz, seq, d = x.shape
    depth = pre_norm.shape[0]
    n_a = a_w_in.shape[0]
    br_w = d - MEM_W
    a_shift = 3 * br_w + 2 * LORA_W
    m = bsz * seq
    tm = 512
    x2d = x.reshape(m, d)
    slopes = jnp.asarray(
        np.repeat(np.array(_alibi_slopes(br_w // LANES), np.float32)[:, None, None], LANES, axis=2))

    for l in range(depth):
        k_mem, v_mem = _mem_kv(mem, mem_norm[l].reshape(1, d), w_mem_kv[l].astype(BF16))
        if l < n_a:
            i = l
            segs = [(0, 0, a_shift), (0, a_shift, a_shift + br_w),
                    (0, a_shift + br_w, a_shift + br_w + MEM_W),
                    (0, a_shift + br_w + MEM_W, a_shift + br_w + 2 * MEM_W)]
            z, gate, q_mem, g_mem = _norm_proj(
                x2d, pre_norm[l].reshape(1, d), a_w_in[i].astype(BF16), segs, tm)
            pp, mu_wa, w2a2 = _rwkv_params(
                a_shift_mu[i], a_w0[i], a_w2[i], a_a0[i], a_a2[i], a_k_k[i], a_k_a[i], a_r_k[i],
                a_lnx_w[i], a_lnx_b[i], br_w)
            y_br = _rwkv(z.reshape(bsz, seq, a_shift), pp, mu_wa, w2a2).reshape(m, br_w)
        else:
            i = l - n_a
            if l == n_a:
                w_cat = jnp.concatenate([b_w_in[i], w_kv], axis=1).astype(BF16)
                gains = jnp.stack([pre_norm[l], kv_norm], axis=0)
                segs = [(0, 0, br_w), (0, br_w, 2 * br_w), (0, 2 * br_w, 2 * br_w + MEM_W),
                        (0, 2 * br_w + MEM_W, 2 * br_w + 2 * MEM_W),
                        (1, 2 * br_w + 2 * MEM_W, 3 * br_w + 2 * MEM_W),
                        (1, 3 * br_w + 2 * MEM_W, 4 * br_w + 2 * MEM_W)]
                q, gate, q_mem, g_mem, k_sh, v_sh = _norm_proj(x2d, gains, w_cat, segs, tm)
                k_sh = k_sh.reshape(bsz, seq, br_w)
                v_sh = v_sh.reshape(bsz, seq, br_w)
            else:
                segs = [(0, 0, br_w), (0, br_w, 2 * br_w), (0, 2 * br_w, 2 * br_w + MEM_W),
                        (0, 2 * br_w + MEM_W, 2 * br_w + 2 * MEM_W)]
                q, gate, q_mem, g_mem = _norm_proj(
                    x2d, pre_norm[l].reshape(1, d), b_w_in[i].astype(BF16), segs, tm)
            lam_init = 0.8 - 0.6 * math.exp(-0.3 * l)
            lam_params = jnp.stack([b_lam_q1[i], b_lam_k1[i], b_lam_q2[i], b_lam_k2[i]], axis=0)
            y_br = _diff_attn(q.reshape(bsz, seq, br_w), k_sh, v_sh, slopes, lam_params,
                              b_subln[i].reshape(LANES, 1), lam_init, ATT_Q_TILE).reshape(m, br_w)
        x2d = _out_proj(y_br, gate, q_mem, g_mem, x2d, k_mem, v_mem, w_out[l].astype(BF16),
                        post_norm[l].reshape(1, d), seq, tm)
    return x2d.reshape(bsz, seq, d)
```

```python
import functools
import math

import numpy as np
import jax
import jax.numpy as jnp
from jax import lax
from jax.experimental import pallas as pl
from jax.experimental.pallas import tpu as pltpu

F32 = jnp.float32
BF16 = jnp.bfloat16
ACT_DTYPE = BF16

HEAD_DIM = 64
LANES = 128
MEM_HEADS = 4
MEM_W = MEM_HEADS * HEAD_DIM
LORA_W = 64
ATT_CHUNK = 64
ATT_Q_TILE = 512
ATT_KV_TILE = 512
ATT_BIAS_PIECES = 3
ATT_V_PAD = 16
RWKV_CHUNK = 64
RWKV_GROUP = 8
RWKV_STAGGER = 2
NORM_EPS = 1e-6
LNX_EPS = 64e-5
MASK_VALUE = -1e30
VMEM_LIMIT_BYTES = 56 * 1024 * 1024


def _dot(a, b):
    return jnp.dot(a, b, preferred_element_type=F32)


def _dot_nt(a, b):
    return lax.dot_general(a, b, (((1,), (1,)), ((), ())), preferred_element_type=F32)


def _dot_tn(a, b):
    return lax.dot_general(a, b, (((0,), (0,)), ((), ())), preferred_element_type=F32)


def _split(x, pieces):
    out = []
    rem = x
    for i in range(pieces):
        p = rem.astype(BF16)
        out.append(p)
        if i + 1 < pieces:
            rem = rem - p.astype(F32)
    return out


def _mm(a, b, fn=_dot, pa=1, pb=1):
    aps = _split(a, pa)
    bps = _split(b, pb)
    order = max(pa, pb)
    acc = None
    for i, ap in enumerate(aps):
        for j, bp in enumerate(bps):
            if i + j < order:
                t = fn(ap, bp)
                acc = t if acc is None else acc + t
    return acc


def _rms(x, g):
    ms = jnp.mean(x * x, axis=-1, keepdims=True)
    return (x * lax.rsqrt(ms + NORM_EPS)) * g


def _silu(x):
    return x / (1.0 + jnp.exp(-x))


def _norm_proj_body(x_ref, g_ref, w_ref, *o_refs, segments):
    x = x_ref[...]
    ms = jnp.mean(x * x, axis=-1, keepdims=True)
    xn = x * lax.rsqrt(ms + NORM_EPS)
    hs = {}
    for o_ref, (gi, lo, hi) in zip(o_refs, segments):
        if gi not in hs:
            hs[gi] = (xn * g_ref[gi:gi + 1, :]).astype(BF16)
        o_ref[...] = _dot(hs[gi], w_ref[:, lo:hi]).astype(o_ref.dtype)


def _norm_proj(x2d, gains, w_bf16, segments, tm):
    m, d = x2d.shape
    n = w_bf16.shape[1]
    ng = gains.shape[0]
    out_shape = [jax.ShapeDtypeStruct((m, hi - lo), ACT_DTYPE) for _, lo, hi in segments]
    out_specs = [pl.BlockSpec((tm, hi - lo), lambda i: (i, 0)) for _, lo, hi in segments]
    return pl.pallas_call(
        functools.partial(_norm_proj_body, segments=tuple(segments)),
        grid=(m // tm,),
        in_specs=[
            pl.BlockSpec((tm, d), lambda i: (i, 0)),
            pl.BlockSpec((ng, d), lambda i: (0, 0)),
            pl.BlockSpec((d, n), lambda i: (0, 0)),
        ],
        out_specs=out_specs,
        out_shape=out_shape,
        compiler_params=pltpu.CompilerParams(
            dimension_semantics=("arbitrary",), vmem_limit_bytes=VMEM_LIMIT_BYTES),
        name="norm_proj",
    )(x2d, gains, w_bf16)


def _mem_kv_body(mem_ref, g_ref, w_ref, k_ref, v_ref):
    h = _rms(mem_ref[0], g_ref[...]).astype(BF16)
    kv = _dot(h, w_ref[...])
    k_ref[0] = kv[:, :MEM_W].astype(k_ref.dtype)
    v_ref[0] = kv[:, MEM_W:].astype(v_ref.dtype)


def _mem_kv(mem, g, w_bf16):
    b, ml, d = mem.shape
    return pl.pallas_call(
        _mem_kv_body,
        grid=(b,),
        in_specs=[
            pl.BlockSpec((1, ml, d), lambda i: (i, 0, 0)),
            pl.BlockSpec((1, d), lambda i: (0, 0)),
            pl.BlockSpec((d, 2 * MEM_W), lambda i: (0, 0)),
        ],
        out_specs=[pl.BlockSpec((1, ml, MEM_W), lambda i: (i, 0, 0))] * 2,
        out_shape=[jax.ShapeDtypeStruct((b, ml, MEM_W), ACT_DTYPE)] * 2,
        compiler_params=pltpu.CompilerParams(
            dimension_semantics=("arbitrary",), vmem_limit_bytes=VMEM_LIMIT_BYTES),
        name="mem_kv",
    )(mem, g, w_bf16)


_PP_MU_R, _PP_MU_K, _PP_MU_V, _PP_W0, _PP_A0, _PP_KK, _PP_KA, _PP_RK, _PP_LNW, _PP_LNB = range(10)
_PP_ROWS = 16


def _rwkv_body(zr_ref, zk_ref, zv_ref, zwa_ref, pp_ref, muwa_ref, w2a2_ref, y_ref,
               s_ref, prev_ref, *, nsteps, stagger, group):
    c = RWKV_CHUNK
    rows = group * c
    lane = lax.broadcasted_iota(jnp.int32, (c, LANES), 1)
    row = lax.broadcasted_iota(jnp.int32, (c, LANES), 0)
    head0 = lane < HEAD_DIM
    scol = jnp.where(head0, lane, lane - HEAD_DIM)
    strict = scol < row
    incl = scol <= row
    diag = scol == row
    r2 = lax.broadcasted_iota(jnp.int32, (LANES, LANES), 0)
    c2 = lax.broadcasted_iota(jnp.int32, (LANES, LANES), 1)
    blockmask = (r2 < HEAD_DIM) == (c2 < HEAD_DIM)
    blockones = jnp.where(blockmask, 1.0, 0.0).astype(BF16)
    eye = r2 == c2
    tr = lax.broadcasted_iota(jnp.int32, (rows, rows), 0)
    tc = lax.broadcasted_iota(jnp.int32, (rows, rows), 1)
    tril_ones = jnp.where((tc <= tr) & (tc // c == tr // c), 1.0, 0.0).astype(BF16)
    slab_lane = lax.broadcasted_iota(jnp.int32, (rows, LANES), 1)
    slab_head0 = slab_lane < HEAD_DIM
    slab_first = lax.broadcasted_iota(jnp.int32, (rows, LANES), 0) == 0

    def bd(x):
        return jnp.concatenate([jnp.where(head0, x, 0.0), jnp.where(head0, 0.0, x)], axis=0)

    def seg_sum(x):
        return _dot(x.astype(BF16), blockones)

    pp = pp_ref[0]
    prow = lambda i: pp[i:i + 1, :]
    mu_r, mu_k, mu_v = prow(_PP_MU_R), prow(_PP_MU_K), prow(_PP_MU_V)
    w0, a0 = prow(_PP_W0), prow(_PP_A0)
    k_k, k_a, r_k = prow(_PP_KK), prow(_PP_KA), prow(_PP_RK)
    lnx_w, lnx_b = prow(_PP_LNW), prow(_PP_LNB)
    mu_wa = muwa_ref[...]
    w2a2 = w2a2_ref[0]

    s_ref[...] = jnp.zeros_like(s_ref)
    prev_ref[...] = jnp.zeros_like(prev_ref)

    def shift_mix(ref, slot, sl, mu):
        z = ref[0, sl, :].astype(F32)
        zp = pltpu.roll(z, 1, axis=0)
        zp = jnp.where(slab_first, prev_ref[slot:slot + 1, :], zp)
        prev_ref[slot:slot + 1, :] = z[rows - 1:rows, :]
        return z + (zp - z) * mu


    def prepare(sl):
        r = shift_mix(zr_ref, 0, sl, mu_r)
        k = shift_mix(zk_ref, 1, sl, mu_k)
        yield
        v = shift_mix(zv_ref, 2, sl, mu_v)
        wa = shift_mix(zwa_ref, 3, sl, mu_wa)
        lora = _mm(jnp.where(slab_head0, jnp.tanh(wa), wa), w2a2)
        yield
        wlog = w0 + lora[:, :LANES]
        nw = -wlog
        w = -(jnp.maximum(nw, 0.0) + jnp.log(1.0 + jnp.exp(-jnp.abs(nw)))) - 0.5
        logw = -jnp.exp(w)
        a = 1.0 / (1.0 + jnp.exp(-(a0 + lora[:, LANES:])))
        kk = k * k_k
        kk = kk * lax.rsqrt(jnp.maximum(seg_sum(kk * kk), 1e-12))
        yield
        kmod = k * (1.0 + (a - 1.0) * k_a)
        alpha = -kk
        beta = kk * a
        cum2 = _dot(tril_ones, jnp.concatenate(_split(logw, 2), axis=1))
        cum = cum2[:, :LANES] + cum2[:, LANES:]
        yield
        e_neg = jnp.exp(-cum)
        at = alpha * jnp.exp(cum - logw)
        rt = r * jnp.exp(cum)
        bt = beta * e_neg
        kt = kmod * e_neg
        yield
        chunks = [slice(g * c, (g + 1) * c) for g in range(group)]
        cum_c = [cum[cs.stop - 1:cs.stop, :] for cs in chunks]
        e_end = [jnp.exp(cum_c[g] - cum[cs]) for g, cs in enumerate(chunks)]
        per = lambda x: [x[cs] for cs in chunks]
        return dict(
            at=per(at), rt=per(rt), bt=per(bt), kt=per(kt), v=per(v),
            bt_end=[beta[cs] * e_end[g] for g, cs in enumerate(chunks)],
            kt_end=[kmod[cs] * e_end[g] for g, cs in enumerate(chunks)],
            p_c=[jnp.exp(x) for x in cum_c], v_slab=v, rk=r * kmod * r_k)

    def chains(p):
        at, rt, bt, kt, bt_end, kt_end, v = (p[x] for x in ("at", "rt", "bt", "kt", "bt_end", "kt_end", "v"))
        n = range(group)
        lhs = [jnp.concatenate([at[g], rt[g]], axis=0) for g in n]
        xbk = [_mm(lhs[g], jnp.concatenate([bd(bt[g]), bd(kt[g])], axis=0), _dot_nt) for g in n]
        yield
        a_rb = [jnp.where(incl, xbk[g][c:, :LANES], 0.0) for g in n]
        akrk = [jnp.concatenate([jnp.where(strict, xbk[g][:c, LANES:], 0.0),
                                 jnp.where(incl, xbk[g][c:, LANES:], 0.0)], axis=0) for g in n]
        av = [_mm(akrk[g], bd(v[g])) for g in n]
        yield
        apow = [jnp.where(strict, xbk[g][:c, :LANES], 0.0) for g in n]
        tinv = [jnp.where(diag, 1.0, 0.0) + apow[g] for g in n]
        nfac = int(math.log2(c))
        for i in range(1, nfac):
            rhs = [[bd(apow[g])] + ([bd(tinv[g])] if i > 1 else []) for g in n]
            d = [_mm(apow[g], jnp.concatenate(rhs[g], axis=1)) for g in n]
            if i > 1:
                tinv = [tinv[g] + d[g][:, LANES:] for g in n]
            apow = [d[g][:, :LANES] for g in n]
            yield
        tinv = [tinv[g] + _mm(apow[g], bd(tinv[g])) for g in n]
        yield
        x = [_mm(tinv[g], jnp.concatenate([bd(at[g]), bd(av[g][:c])], axis=1)) for g in n]
        wmat = [x[g][:, :LANES] for g in n]
        u0 = [x[g][:, LANES:] for g in n]
        yield
        d2 = [_mm(a_rb[g], jnp.concatenate([bd(wmat[g]), bd(u0[g])], axis=1)) for g in n]
        rp = [rt[g] + d2[g][:, :LANES] for g in n]
        y0 = [d2[g][:, LANES:] + av[g][c:] for g in n]
        yield
        gp = [jnp.where(blockmask, _mm(wmat[g], bt_end[g], _dot_tn), 0.0) for g in n]
        npart = [
            jnp.where(
                blockmask,
                _mm(jnp.concatenate([u0[g], v[g]], axis=0),
                    jnp.concatenate([bt_end[g], kt_end[g]], axis=0), _dot_tn),
                0.0)
            for g in n]
        yield
        spans = [[(jnp.where(eye, gp[g] + p["p_c"][g], gp[g]), npart[g]) for g in n]]
        while len(spans[-1]) > 1:
            prev = spans[-1]
            nxt = []
            for i in range(0, len(prev), 2):
                (ma, na), (mb, nb) = prev[i], prev[i + 1]
                prod = _mm(jnp.concatenate([ma, na], axis=0), mb)
                nxt.append((prod[:LANES], prod[LANES:] + nb))
            spans.append(nxt)
            yield
        return dict(rp=rp, y0=y0, spans=spans, v_slab=p["v_slab"], rk=p["rk"])

    def finish(t, sl):
        spans = t["spans"]
        states = {0: s_ref[...]}
        top = len(spans) - 1
        m_all, n_all = spans[top][0]
        s_ref[...] = _mm(states[0], m_all) + n_all
        yield
        for level in range(top, 0, -1):
            width = 1 << level
            for lo in range(0, group, width):
                ma, na = spans[level - 1][lo >> (level - 1)]
                states[lo + width // 2] = _mm(states[lo], ma) + na
            yield
        y = jnp.concatenate(
            [_mm(t["rp"][g], states[g], _dot_nt) + t["y0"][g] for g in range(group)], axis=0)
        yield
        mean = seg_sum(y) * (1.0 / HEAD_DIM)
        yc = y - mean
        var = seg_sum(yc * yc) * (1.0 / HEAD_DIM)
        yield
        yn = yc * lax.rsqrt(var + LNX_EPS) * lnx_w + lnx_b
        bonus = seg_sum(t["rk"]) * t["v_slab"]
        y_ref[0, sl, :] = (yn + bonus).astype(y_ref.dtype)

    def run(*gens):
        results = [None] * len(gens)
        live = list(range(len(gens)))
        while live:
            for i in list(live):
                try:
                    next(gens[i])
                except StopIteration as stop:
                    results[i] = stop.value
                    live.remove(i)
        return results

    def step(si, carry):
        slabs = [pl.ds(pl.multiple_of((si * stagger + j) * rows, rows), rows) for j in range(stagger)]
        prepared, = run(prepare(slabs[0]))
        done = None
        for j in range(stagger):
            gens = [chains(prepared)]
            if j + 1 < stagger:
                gens.append(prepare(slabs[j + 1]))
            if done is not None:
                gens.append(finish(done, slabs[j - 1]))
            out = run(*gens)
            done = out[0]
            if j + 1 < stagger:
                prepared = out[1]
        run(finish(done, slabs[stagger - 1]))
        return carry

    lax.fori_loop(0, nsteps, step, 0)


def _rwkv(z3d, pp, mu_wa, w2a2):
    b, s, a_shift = z3d.shape
    br_w = (a_shift - 2 * LORA_W) // 3
    npairs = br_w // LANES
    zspec = lambda off: pl.BlockSpec((1, s, LANES), lambda bi, hp: (bi, 0, off + hp))
    return pl.pallas_call(
        functools.partial(_rwkv_body, nsteps=s // (RWKV_CHUNK * RWKV_GROUP * RWKV_STAGGER),
                          stagger=RWKV_STAGGER, group=RWKV_GROUP),
        grid=(b, npairs),
        in_specs=[
            zspec(0), zspec(npairs), zspec(2 * npairs),
            pl.BlockSpec((1, s, LANES), lambda bi, hp: (bi, 0, 3 * npairs)),
            pl.BlockSpec((1, _PP_ROWS, LANES), lambda bi, hp: (hp, 0, 0)),
            pl.BlockSpec((1, LANES), lambda bi, hp: (0, 0)),
            pl.BlockSpec((1, LANES, 2 * LANES), lambda bi, hp: (hp, 0, 0)),
        ],
        out_specs=pl.BlockSpec((1, s, LANES), lambda bi, hp: (bi, 0, hp)),
        out_shape=jax.ShapeDtypeStruct((b, s, br_w), ACT_DTYPE),
        scratch_shapes=[pltpu.VMEM((LANES, LANES), F32), pltpu.VMEM((8, LANES), F32)],
        compiler_params=pltpu.CompilerParams(
            dimension_semantics=("arbitrary", "arbitrary"), vmem_limit_bytes=VMEM_LIMIT_BYTES),
        name="rwkv7_scan",
    )(z3d, z3d, z3d, z3d, pp, mu_wa, w2a2)


def _alibi_slopes(n):
    def pow2(m):
        start = 2.0 ** (-8.0 / m)
        return [start ** (i + 1) for i in range(m)]
    if math.log2(n).is_integer():
        return pow2(n)
    cl = 2 ** int(math.floor(math.log2(n)))
    return pow2(cl) + pow2(2 * cl)[0::2][: n - cl]


def _diff_attn_body(q_ref, k_ref, v_ref, slope_ref, lamp_ref, sub_ref, o_ref,
                    kb_ref, vt_ref, band_ref, sa_ref, sb_ref, acc_ref, *, tq, lam_init):
    bi = pl.program_id(1)
    qi = pl.program_id(2)
    tk = ATT_KV_TILE
    band = tq // tk
    nkv = k_ref.shape[1] // tk
    log2e = math.log2(math.e)
    slope2 = slope_ref[0][:, :1] * log2e
    aug_lane = lax.broadcasted_iota(jnp.int32, (tk, LANES), 1)

    @pl.when(qi == 0)
    def _():
        for t in range(nkv):
            rows = slice(t * tk, (t + 1) * tk)
            kb_ref[rows, :LANES] = k_ref[0, rows, :].astype(BF16)
            vt_ref[:LANES, rows] = v_ref[0, rows, :].astype(F32).T.astype(BF16)

    @pl.when((qi == 0) & (bi == 0))
    def _():
        ones_row = lax.broadcasted_iota(jnp.int32, (ATT_V_PAD, tk), 0) == 0
        for t in range(nkv):
            rows = slice(t * tk, (t + 1) * tk)
            pos = (lax.broadcasted_iota(jnp.int32, (tk, LANES), 0) + t * tk).astype(F32)
            pieces = _split(slope2 * pos, ATT_BIAS_PIECES)
            aug = jnp.zeros((tk, LANES), F32)
            for i, piece in enumerate(pieces):
                aug = jnp.where(aug_lane == i, piece.astype(F32), aug)
            kb_ref[rows, LANES:] = aug.astype(BF16)
            vt_ref[LANES:, rows] = jnp.where(ones_row, 1.0, 0.0).astype(BF16)
        key = lax.broadcasted_iota(jnp.int32, (tk, 2 * tq), 0).astype(F32)
        lane2 = lax.broadcasted_iota(jnp.int32, (1, 2 * tq), 1)
        qry = jnp.where(lane2 < tq, lane2, lane2 - tq)
        qlim = ((qry // ATT_CHUNK + 1) * ATT_CHUNK).astype(F32)
        for d in range(band):
            kpos = key + float(d * tk)
            band_ref[d] = jnp.where(
                kpos < qlim, (-2.0 * slope2) * jnp.maximum(kpos - qry.astype(F32), 0.0), MASK_VALUE)

    lane = lax.broadcasted_iota(jnp.int32, (2 * tq, LANES), 1)
    head0 = lane < HEAD_DIM
    first = lax.broadcasted_iota(jnp.int32, (2 * tq, LANES), 0) < tq
    q = q_ref[0].astype(F32) * (HEAD_DIM ** -0.5 * log2e)
    q2x = jnp.concatenate([q, q], axis=0)
    qcat = jnp.concatenate(
        [jnp.where(first == head0, q2x, 0.0), jnp.where(lane < ATT_BIAS_PIECES, 1.0, 0.0)],
        axis=1).astype(BF16)

    def scores(j, dst_ref):
        dst_ref[...] = _dot_nt(kb_ref[j * tk:(j + 1) * tk, :], qcat)

    def softmax_pv(j, src_ref, m, band_index):
        vt = vt_ref[:, j * tk:(j + 1) * tk]
        s = src_ref[...]
        if band_index is not None:
            s = s + band_ref[band_index]
        if m is None:
            m_new = jnp.max(s, axis=0, keepdims=True)
            acc_ref[...] = _dot(vt, jnp.exp2(s - m_new).astype(BF16))
        else:
            m_new = jnp.maximum(m, jnp.max(s, axis=0, keepdims=True))
            p = jnp.exp2(s - m_new)
            acc_ref[...] = jnp.exp2(m - m_new) * acc_ref[...] + _dot(vt, p.astype(BF16))
        return m_new

    for n in range(nkv // band):
        @pl.when(qi == n)
        def _(n=n):
            ntiles = (n + 1) * band
            bufs = (sa_ref, sb_ref)
            m = None
            scores(0, bufs[0])
            for j in range(ntiles):
                if j + 1 < ntiles:
                    scores(j + 1, bufs[(j + 1) % 2])
                m = softmax_pv(j, bufs[j % 2], m, None if j < n * band else j - n * band)

    lp = lamp_ref[...]
    lam = (jnp.exp(jnp.sum(lp[0:1] * lp[1:2], axis=-1, keepdims=True))
           - jnp.exp(jnp.sum(lp[2:3] * lp[3:4], axis=-1, keepdims=True)) + lam_init)
    on = acc_ref[:LANES, :] * (1.0 / acc_ref[LANES:LANES + 1, :])
    o = on[:, :tq] - lam * on[:, tq:]
    ms = jnp.mean(o * o, axis=0, keepdims=True)
    o = o * lax.rsqrt(ms + NORM_EPS) * sub_ref[...] * (1.0 - lam_init)
    o_ref[0] = o.T.astype(o_ref.dtype)


def _diff_attn(q3d, k3d, v3d, slopes, lam_params, subln_col, lam_init, tq):
    b, s, br_w = q3d.shape
    nh = br_w // LANES
    tk = ATT_KV_TILE
    return pl.pallas_call(
        functools.partial(_diff_attn_body, tq=tq, lam_init=lam_init),
        grid=(nh, b, s // tq),
        in_specs=[
            pl.BlockSpec((1, tq, LANES), lambda h, bi, qi: (bi, qi, h)),
            pl.BlockSpec((1, s, LANES), lambda h, bi, qi: (bi, 0, h)),
            pl.BlockSpec((1, s, LANES), lambda h, bi, qi: (bi, 0, h)),
            pl.BlockSpec((1, 1, LANES), lambda h, bi, qi: (h, 0, 0)),
            pl.BlockSpec((4, HEAD_DIM), lambda h, bi, qi: (0, 0)),
            pl.BlockSpec((LANES, 1), lambda h, bi, qi: (0, 0)),
        ],
        out_specs=pl.BlockSpec((1, tq, LANES), lambda h, bi, qi: (bi, qi, h)),
        out_shape=jax.ShapeDtypeStruct((b, s, br_w), ACT_DTYPE),
        scratch_shapes=[pltpu.VMEM((s, 2 * LANES), BF16), pltpu.VMEM((LANES + ATT_V_PAD, s), BF16),
                        pltpu.VMEM((tq // tk, tk, 2 * tq), F32),
                        pltpu.VMEM((tk, 2 * tq), F32), pltpu.VMEM((tk, 2 * tq), F32),
                        pltpu.VMEM((LANES + ATT_V_PAD, 2 * tq), F32)],
        compiler_params=pltpu.CompilerParams(
            dimension_semantics=("arbitrary", "arbitrary", "arbitrary"),
            vmem_limit_bytes=VMEM_LIMIT_BYTES),
        name="diff_attn",
    )(q3d, k3d, v3d, slopes, lam_params, subln_col)


def _out_body(ybr_ref, gate_ref, qm_ref, gm_ref, x_ref, km_ref, vm_ref, w_ref, pn_ref, o_ref, *, br_w):
    tm = x_ref.shape[0]
    lane = lax.broadcasted_iota(jnp.int32, (tm, MEM_W), 1)
    qm = qm_ref[...].astype(F32)
    km = km_ref[0].astype(BF16)
    vm = vm_ref[0].astype(BF16)
    y_mem = jnp.zeros((tm, MEM_W), F32)
    for h in range(MEM_HEADS):
        in_head = (lane >= h * HEAD_DIM) & (lane < (h + 1) * HEAD_DIM)
        qh = jnp.where(in_head, qm, 0.0).astype(BF16)
        s = _dot_nt(qh, km) * (HEAD_DIM ** -0.5)
        p = jnp.exp(s - jnp.max(s, axis=-1, keepdims=True))
        l = jnp.sum(p, axis=-1, keepdims=True)
        oh = _dot(p.astype(BF16), vm)
        y_mem = y_mem + jnp.where(in_head, oh / l, 0.0)
    y_mem = (y_mem * _silu(gm_ref[...].astype(F32))).astype(BF16)
    y_br = (ybr_ref[...].astype(F32) * _silu(gate_ref[...].astype(F32))).astype(BF16)
    y = _dot(y_br, w_ref[:br_w, :]) + _dot(y_mem, w_ref[br_w:, :])
    o_ref[...] = x_ref[...] + _rms(y, pn_ref[...])


def _out_proj(ybr, gate, qm, gm, x2d, k_mem, v_mem, w_bf16, post_g, seq, tm):
    m, d = x2d.shape
    br_w = ybr.shape[1]
    ml = k_mem.shape[1]
    per_b = seq // tm
    row = lambda w: pl.BlockSpec((tm, w), lambda i: (i, 0))
    return pl.pallas_call(
        functools.partial(_out_body, br_w=br_w),
        grid=(m // tm,),
        in_specs=[
            row(br_w), row(br_w), row(MEM_W), row(MEM_W), row(d),
            pl.BlockSpec((1, ml, MEM_W), lambda i: (i // per_b, 0, 0)),
            pl.BlockSpec((1, ml, MEM_W), lambda i: (i // per_b, 0, 0)),
            pl.BlockSpec((d, d), lambda i: (0, 0)),
            pl.BlockSpec((1, d), lambda i: (0, 0)),
        ],
        out_specs=row(d),
        out_shape=jax.ShapeDtypeStruct((m, d), F32),
        compiler_params=pltpu.CompilerParams(
            dimension_semantics=("arbitrary",), vmem_limit_bytes=VMEM_LIMIT_BYTES),
        name="out_proj",
    )(ybr, gate, qm, gm, x2d, k_mem, v_mem, w_bf16, post_g)


def _rwkv_params(mu, w0, w2, a0, a2, k_k, k_a, r_k, lnx_w, lnx_b, br_w):
    npairs = br_w // LANES
    rows = [mu[:br_w], mu[br_w:2 * br_w], mu[2 * br_w:3 * br_w], w0, a0, k_k, k_a,
            r_k.reshape(-1), lnx_w, lnx_b]
    pp = jnp.stack([r.reshape(npairs, LANES) for r in rows], axis=1)
    pp = jnp.pad(pp, ((0, 0), (0, _PP_ROWS - len(rows)), (0, 0)))
    mu_wa = mu[3 * br_w:].reshape(1, LANES)
    w2p = w2.reshape(LORA_W, npairs, LANES).transpose(1, 0, 2)
    a2p = a2.reshape(LORA_W, npairs, LANES).transpose(1, 0, 2)
    zeros = jnp.zeros_like(w2p)
    w2a2 = jnp.concatenate(
        [jnp.concatenate([w2p, zeros], axis=2), jnp.concatenate([zeros, a2p], axis=2)], axis=1)
    return pp, mu_wa, w2a2


def kernel(x, mem, pre_norm, post_norm, w_out, mem_norm, w_mem_kv, a_w_in, a_shift_mu, a_w0, a_w2,
           a_a0, a_a2, a_k_k, a_k_a, a_r_k, a_lnx_w, a_lnx_b, kv_norm, w_kv, b_w_in, b_lam_q1,
           b_lam_k1, b_lam_q2, b_lam_k2, b_subln):
    bsz, seq, d = x.shape
    depth = pre_norm.shape[0]
    n_a = a_w_in.shape[0]
    br_w = d - MEM_W
    a_shift = 3 * br_w + 2 * LORA_W
    m = bsz * seq
    tm = 512
    x2d = x.reshape(m, d)
    slopes = jnp.asarray(
        np.repeat(np.array(_alibi_slopes(br_w // LANES), np.float32)[:, None, None], LANES, axis=2))

    for l in range(depth):
        k_mem, v_mem = _mem_kv(mem, mem_norm[l].reshape(1, d), w_mem_kv[l].astype(BF16))
        if l < n_a:
            i = l
            segs = [(0, 0, a_shift), (0, a_shift, a_shift + br_w),
                    (0, a_shift + br_w, a_shift + br_w + MEM_W),
                    (0, a_shift + br_w + MEM_W, a_shift + br_w + 2 * MEM_W)]
            z, gate, q_mem, g_mem = _norm_proj(
                x2d, pre_norm[l].reshape(1, d), a_w_in[i].astype(BF16), segs, tm)
            pp, mu_wa, w2a2 = _rwkv_params(
                a_shift_mu[i], a_w0[i], a_w2[i], a_a0[i], a_a2[i], a_k_k[i], a_k_a[i], a_r_k[i],
                a_lnx_w[i], a_lnx_b[i], br_w)
            y_br = _rwkv(z.reshape(bsz, seq, a_shift), pp, mu_wa, w2a2).reshape(m, br_w)
        else:
            i = l - n_a
            if l == n_a:
                w_cat = jnp.concatenate([b_w_in[i], w_kv], axis=1).astype(BF16)
                gains = jnp.stack([pre_norm[l], kv_norm], axis=0)
                segs = [(0, 0, br_w), (0, br_w, 2 * br_w), (0, 2 * br_w, 2 * br_w + MEM_W),
                        (0, 2 * br_w + MEM_W, 2 * br_w + 2 * MEM_W),
                        (1, 2 * br_w + 2 * MEM_W, 3 * br_w + 2 * MEM_W),
                        (1, 3 * br_w + 2 * MEM_W, 4 * br_w + 2 * MEM_W)]
                q, gate, q_mem, g_mem, k_sh, v_sh = _norm_proj(x2d, gains, w_cat, segs, tm)
                k_sh = k_sh.reshape(bsz, seq, br_w)
                v_sh = v_sh.reshape(bsz, seq, br_w)
            else:
                segs = [(0, 0, br_w), (0, br_w, 2 * br_w), (0, 2 * br_w, 2 * br_w + MEM_W),
                        (0, 2 * br_w + MEM_W, 2 * br_w + 2 * MEM_W)]
                q, gate, q_mem, g_mem = _norm_proj(
                    x2d, pre_norm[l].reshape(1, d), b_w_in[i].astype(BF16), segs, tm)
            lam_init = 0.8 - 0.6 * math.exp(-0.3 * l)
            lam_params = jnp.stack([b_lam_q1[i], b_lam_k1[i], b_lam_q2[i], b_lam_k2[i]], axis=0)
            y_br = _diff_attn(q.reshape(bsz, seq, br_w), k_sh, v_sh, slopes, lam_params,
                              b_subln[i].reshape(LANES, 1), lam_init, ATT_Q_TILE).reshape(m, br_w)
        x2d = _out_proj(y_br, gate, q_mem, g_mem, x2d, k_mem, v_mem, w_out[l].astype(BF16),
                        post_norm[l].reshape(1, d), seq, tm)
    return x2d.reshape(bsz, seq, d)
```

```python
import functools
import math

import numpy as np
import jax
import jax.numpy as jnp
from jax import lax
from jax.experimental import pallas as pl
from jax.experimental.pallas import tpu as pltpu

F32 = jnp.float32
BF16 = jnp.bfloat16
ACT_DTYPE = BF16

HEAD_DIM = 64
LANES = 128
MEM_HEADS = 4
MEM_W = MEM_HEADS * HEAD_DIM
LORA_W = 64
ATT_CHUNK = 64
ATT_Q_TILE = 512
ATT_KV_TILE = 512
ATT_BIAS_PIECES = 3
ATT_V_PAD = 16
RWKV_CHUNK = 64
RWKV_GROUP = 8
RWKV_STAGGER = 2
OUT_ROWS = 1024
OUT_SUB_ROWS = 256
NORM_EPS = 1e-6
LNX_EPS = 64e-5
MASK_VALUE = -1e30
VMEM_LIMIT_BYTES = 56 * 1024 * 1024


def _dot(a, b):
    return jnp.dot(a, b, preferred_element_type=F32)


def _dot_nt(a, b):
    return lax.dot_general(a, b, (((1,), (1,)), ((), ())), preferred_element_type=F32)


def _dot_tn(a, b):
    return lax.dot_general(a, b, (((0,), (0,)), ((), ())), preferred_element_type=F32)


def _split(x, pieces):
    out = []
    rem = x
    for i in range(pieces):
        p = rem.astype(BF16)
        out.append(p)
        if i + 1 < pieces:
            rem = rem - p.astype(F32)
    return out


def _mm(a, b, fn=_dot, pa=1, pb=1):
    aps = _split(a, pa)
    bps = _split(b, pb)
    order = max(pa, pb)
    acc = None
    for i, ap in enumerate(aps):
        for j, bp in enumerate(bps):
            if i + j < order:
                t = fn(ap, bp)
                acc = t if acc is None else acc + t
    return acc


def _rms(x, g):
    ms = jnp.mean(x * x, axis=-1, keepdims=True)
    return (x * lax.rsqrt(ms + NORM_EPS)) * g


def _silu(x):
    h = 0.5 * x
    return h + h * jnp.tanh(h)


def _norm_proj_body(x_ref, g_ref, w_ref, *o_refs, segments):
    x = x_ref[...]
    ms = jnp.mean(x * x, axis=-1, keepdims=True)
    xn = x * lax.rsqrt(ms + NORM_EPS)
    hs = {}
    for o_ref, (gi, lo, hi) in zip(o_refs, segments):
        if gi not in hs:
            hs[gi] = (xn * g_ref[gi:gi + 1, :]).astype(BF16)
        o_ref[...] = _dot(hs[gi], w_ref[:, lo:hi]).astype(o_ref.dtype)


def _norm_proj(x2d, gains, w_bf16, segments, tm):
    m, d = x2d.shape
    n = w_bf16.shape[1]
    ng = gains.shape[0]
    out_shape = [jax.ShapeDtypeStruct((m, hi - lo), ACT_DTYPE) for _, lo, hi in segments]
    out_specs = [pl.BlockSpec((tm, hi - lo), lambda i: (i, 0)) for _, lo, hi in segments]
    return pl.pallas_call(
        functools.partial(_norm_proj_body, segments=tuple(segments)),
        grid=(m // tm,),
        in_specs=[
            pl.BlockSpec((tm, d), lambda i: (i, 0)),
            pl.BlockSpec((ng, d), lambda i: (0, 0)),
            pl.BlockSpec((d, n), lambda i: (0, 0)),
        ],
        out_specs=out_specs,
        out_shape=out_shape,
        compiler_params=pltpu.CompilerParams(
            dimension_semantics=("arbitrary",), vmem_limit_bytes=VMEM_LIMIT_BYTES),
        name="norm_proj",
    )(x2d, gains, w_bf16)


def _mem_kv_body(mem_ref, g_ref, w_ref, k_ref, v_ref):
    ml = mem_ref.shape[1]
    h = _rms(mem_ref[0], g_ref[...]).astype(BF16)
    kv = _dot(h, w_ref[...])
    lane = lax.broadcasted_iota(jnp.int32, (ml, MEM_W), 1)
    for hd in range(MEM_HEADS):
        in_head = (lane >= hd * HEAD_DIM) & (lane < (hd + 1) * HEAD_DIM)
        k_ref[0, hd * ml:(hd + 1) * ml, :] = jnp.where(in_head, kv[:, :MEM_W], 0.0).astype(k_ref.dtype)
        v_ref[0, hd * ml:(hd + 1) * ml, :] = jnp.where(in_head, kv[:, MEM_W:], 0.0).astype(v_ref.dtype)


def _mem_kv(mem, g, w_bf16):
    b, ml, d = mem.shape
    return pl.pallas_call(
        _mem_kv_body,
        grid=(b,),
        in_specs=[
            pl.BlockSpec((1, ml, d), lambda i: (i, 0, 0)),
            pl.BlockSpec((1, d), lambda i: (0, 0)),
            pl.BlockSpec((d, 2 * MEM_W), lambda i: (0, 0)),
        ],
        out_specs=[pl.BlockSpec((1, MEM_HEADS * ml, MEM_W), lambda i: (i, 0, 0))] * 2,
        out_shape=[jax.ShapeDtypeStruct((b, MEM_HEADS * ml, MEM_W), ACT_DTYPE)] * 2,
        compiler_params=pltpu.CompilerParams(
            dimension_semantics=("arbitrary",), vmem_limit_bytes=VMEM_LIMIT_BYTES),
        name="mem_kv",
    )(mem, g, w_bf16)


_PP_MU_R, _PP_MU_K, _PP_MU_V, _PP_W0, _PP_A0, _PP_KK, _PP_KA, _PP_RK, _PP_LNW, _PP_LNB = range(10)
_PP_ROWS = 16


def _rwkv_body(zr_ref, zk_ref, zv_ref, zwa_ref, pp_ref, muwa_ref, w2a2_ref, y_ref,
               s_ref, prev_ref, *, nsteps, stagger, group):
    c = RWKV_CHUNK
    rows = group * c
    lane = lax.broadcasted_iota(jnp.int32, (c, LANES), 1)
    row = lax.broadcasted_iota(jnp.int32, (c, LANES), 0)
    head0 = lane < HEAD_DIM
    scol = jnp.where(head0, lane, lane - HEAD_DIM)
    strict = scol < row
    incl = scol <= row
    diag = scol == row
    r2 = lax.broadcasted_iota(jnp.int32, (LANES, LANES), 0)
    c2 = lax.broadcasted_iota(jnp.int32, (LANES, LANES), 1)
    blockmask = (r2 < HEAD_DIM) == (c2 < HEAD_DIM)
    blockones = jnp.where(blockmask, 1.0, 0.0).astype(BF16)
    eye = r2 == c2
    tr = lax.broadcasted_iota(jnp.int32, (rows, rows), 0)
    tc = lax.broadcasted_iota(jnp.int32, (rows, rows), 1)
    tril_ones = jnp.where((tc <= tr) & (tc // c == tr // c), 1.0, 0.0).astype(BF16)
    slab_lane = lax.broadcasted_iota(jnp.int32, (rows, LANES), 1)
    slab_head0 = slab_lane < HEAD_DIM
    slab_first = lax.broadcasted_iota(jnp.int32, (rows, LANES), 0) == 0

    def bd(x):
        return jnp.concatenate([jnp.where(head0, x, 0.0), jnp.where(head0, 0.0, x)], axis=0)

    def seg_sum(x):
        return _dot(x.astype(BF16), blockones)

    pp = pp_ref[0]
    prow = lambda i: pp[i:i + 1, :]
    mu_r, mu_k, mu_v = prow(_PP_MU_R), prow(_PP_MU_K), prow(_PP_MU_V)
    w0, a0 = prow(_PP_W0), prow(_PP_A0)
    k_k, k_a, r_k = prow(_PP_KK), prow(_PP_KA), prow(_PP_RK)
    lnx_w, lnx_b = prow(_PP_LNW), prow(_PP_LNB)
    mu_wa = muwa_ref[...]
    w2a2 = w2a2_ref[0]

    s_ref[...] = jnp.zeros_like(s_ref)
    prev_ref[...] = jnp.zeros_like(prev_ref)

    def shift_mix(ref, slot, sl, mu):
        z = ref[0, sl, :].astype(F32)
        zp = pltpu.roll(z, 1, axis=0)
        zp = jnp.where(slab_first, prev_ref[slot:slot + 1, :], zp)
        prev_ref[slot:slot + 1, :] = z[rows - 1:rows, :]
        return z + (zp - z) * mu


    def prepare(sl):
        r = shift_mix(zr_ref, 0, sl, mu_r)
        k = shift_mix(zk_ref, 1, sl, mu_k)
        yield
        v = shift_mix(zv_ref, 2, sl, mu_v)
        wa = shift_mix(zwa_ref, 3, sl, mu_wa)
        lora = _mm(jnp.where(slab_head0, jnp.tanh(wa), wa), w2a2)
        yield
        wlog = w0 + lora[:, :LANES]
        nw = -wlog
        w = -(jnp.maximum(nw, 0.0) + jnp.log(1.0 + jnp.exp(-jnp.abs(nw)))) - 0.5
        logw = -jnp.exp(w)
        a = 1.0 / (1.0 + jnp.exp(-(a0 + lora[:, LANES:])))
        kk = k * k_k
        kk = kk * lax.rsqrt(jnp.maximum(seg_sum(kk * kk), 1e-12))
        yield
        kmod = k * (1.0 + (a - 1.0) * k_a)
        alpha = -kk
        beta = kk * a
        cum2 = _dot(tril_ones, jnp.concatenate(_split(logw, 2), axis=1))
        cum = cum2[:, :LANES] + cum2[:, LANES:]
        yield
        e_neg = jnp.exp(-cum)
        at = alpha * jnp.exp(cum - logw)
        rt = r * jnp.exp(cum)
        bt = beta * e_neg
        kt = kmod * e_neg
        yield
        chunks = [slice(g * c, (g + 1) * c) for g in range(group)]
        cum_c = [cum[cs.stop - 1:cs.stop, :] for cs in chunks]
        e_end = [jnp.exp(cum_c[g] - cum[cs]) for g, cs in enumerate(chunks)]
        per = lambda x: [x[cs] for cs in chunks]
        return dict(
            at=per(at), rt=per(rt), bt=per(bt), kt=per(kt), v=per(v),
            bt_end=[beta[cs] * e_end[g] for g, cs in enumerate(chunks)],
            kt_end=[kmod[cs] * e_end[g] for g, cs in enumerate(chunks)],
            p_c=[jnp.exp(x) for x in cum_c], v_slab=v, rk=r * kmod * r_k)

    def chains(p):
        at, rt, bt, kt, bt_end, kt_end, v = (p[x] for x in ("at", "rt", "bt", "kt", "bt_end", "kt_end", "v"))
        n = range(group)
        lhs = [jnp.concatenate([at[g], rt[g]], axis=0) for g in n]
        xbk = [_mm(lhs[g], jnp.concatenate([bd(bt[g]), bd(kt[g])], axis=0), _dot_nt) for g in n]
        yield
        a_rb = [jnp.where(incl, xbk[g][c:, :LANES], 0.0) for g in n]
        akrk = [jnp.concatenate([jnp.where(strict, xbk[g][:c, LANES:], 0.0),
                                 jnp.where(incl, xbk[g][c:, LANES:], 0.0)], axis=0) for g in n]
        av = [_mm(akrk[g], bd(v[g])) for g in n]
        yield
        apow = [jnp.where(strict, xbk[g][:c, :LANES], 0.0) for g in n]
        tinv = [jnp.where(diag, 1.0, 0.0) + apow[g] for g in n]
        nfac = int(math.log2(c))
        for i in range(1, nfac):
            rhs = [[bd(apow[g])] + ([bd(tinv[g])] if i > 1 else []) for g in n]
            d = [_mm(apow[g], jnp.concatenate(rhs[g], axis=1)) for g in n]
            if i > 1:
                tinv = [tinv[g] + d[g][:, LANES:] for g in n]
            apow = [d[g][:, :LANES] for g in n]
            yield
        tinv = [tinv[g] + _mm(apow[g], bd(tinv[g])) for g in n]
        yield
        x = [_mm(tinv[g], jnp.concatenate([bd(at[g]), bd(av[g][:c])], axis=1)) for g in n]
        wmat = [x[g][:, :LANES] for g in n]
        u0 = [x[g][:, LANES:] for g in n]
        yield
        d2 = [_mm(a_rb[g], jnp.concatenate([bd(wmat[g]), bd(u0[g])], axis=1)) for g in n]
        rp = [rt[g] + d2[g][:, :LANES] for g in n]
        y0 = [d2[g][:, LANES:] + av[g][c:] for g in n]
        yield
        gp = [jnp.where(blockmask, _mm(wmat[g], bt_end[g], _dot_tn), 0.0) for g in n]
        npart = [
            jnp.where(
                blockmask,
                _mm(jnp.concatenate([u0[g], v[g]], axis=0),
                    jnp.concatenate([bt_end[g], kt_end[g]], axis=0), _dot_tn),
                0.0)
            for g in n]
        yield
        spans = [[(jnp.where(eye, gp[g] + p["p_c"][g], gp[g]), npart[g]) for g in n]]
        while len(spans[-1]) > 1:
            prev = spans[-1]
            nxt = []
            for i in range(0, len(prev), 2):
                (ma, na), (mb, nb) = prev[i], prev[i + 1]
                prod = _mm(jnp.concatenate([ma, na], axis=0), mb)
                nxt.append((prod[:LANES], prod[LANES:] + nb))
            spans.append(nxt)
            yield
        return dict(rp=rp, y0=y0, spans=spans, v_slab=p["v_slab"], rk=p["rk"])

    def finish(t, sl):
        spans = t["spans"]
        states = {0: s_ref[...]}
        top = len(spans) - 1
        m_all, n_all = spans[top][0]
        s_ref[...] = _mm(states[0], m_all) + n_all
        yield
        for level in range(top, 0, -1):
            width = 1 << level
            for lo in range(0, group, width):
                ma, na = spans[level - 1][lo >> (level - 1)]
                states[lo + width // 2] = _mm(states[lo], ma) + na
            yield
        y = jnp.concatenate(
            [_mm(t["rp"][g], states[g], _dot_nt) + t["y0"][g] for g in range(group)], axis=0)
        yield
        mean = seg_sum(y) * (1.0 / HEAD_DIM)
        yc = y - mean
        var = seg_sum(yc * yc) * (1.0 / HEAD_DIM)
        yield
        yn = yc * lax.rsqrt(var + LNX_EPS) * lnx_w + lnx_b
        bonus = seg_sum(t["rk"]) * t["v_slab"]
        y_ref[0, sl, :] = (yn + bonus).astype(y_ref.dtype)

    def run(*gens):
        results = [None] * len(gens)
        live = list(range(len(gens)))
        while live:
            for i in list(live):
                try:
                    next(gens[i])
                except StopIteration as stop:
                    results[i] = stop.value
                    live.remove(i)
        return results

    def step(si, carry):
        slabs = [pl.ds(pl.multiple_of((si * stagger + j) * rows, rows), rows) for j in range(stagger)]
        prepared, = run(prepare(slabs[0]))
        done = None
        for j in range(stagger):
            gens = [chains(prepared)]
            if j + 1 < stagger:
                gens.append(prepare(slabs[j + 1]))
            if done is not None:
                gens.append(finish(done, slabs[j - 1]))
            out = run(*gens)
            done = out[0]
            if j + 1 < stagger:
                prepared = out[1]
        run(finish(done, slabs[stagger - 1]))
        return carry

    lax.fori_loop(0, nsteps, step, 0)


def _rwkv(z3d, pp, mu_wa, w2a2):
    b, s, a_shift = z3d.shape
    br_w = (a_shift - 2 * LORA_W) // 3
    npairs = br_w // LANES
    zspec = lambda off: pl.BlockSpec((1, s, LANES), lambda bi, hp: (bi, 0, off + hp))
    return pl.pallas_call(
        functools.partial(_rwkv_body, nsteps=s // (RWKV_CHUNK * RWKV_GROUP * RWKV_STAGGER),
                          stagger=RWKV_STAGGER, group=RWKV_GROUP),
        grid=(b, npairs),
        in_specs=[
            zspec(0), zspec(npairs), zspec(2 * npairs),
            pl.BlockSpec((1, s, LANES), lambda bi, hp: (bi, 0, 3 * npairs)),
            pl.BlockSpec((1, _PP_ROWS, LANES), lambda bi, hp: (hp, 0, 0)),
            pl.BlockSpec((1, LANES), lambda bi, hp: (0, 0)),
            pl.BlockSpec((1, LANES, 2 * LANES), lambda bi, hp: (hp, 0, 0)),
        ],
        out_specs=pl.BlockSpec((1, s, LANES), lambda bi, hp: (bi, 0, hp)),
        out_shape=jax.ShapeDtypeStruct((b, s, br_w), ACT_DTYPE),
        scratch_shapes=[pltpu.VMEM((LANES, LANES), F32), pltpu.VMEM((8, LANES), F32)],
        compiler_params=pltpu.CompilerParams(
            dimension_semantics=("arbitrary", "arbitrary"), vmem_limit_bytes=VMEM_LIMIT_BYTES),
        name="rwkv7_scan",
    )(z3d, z3d, z3d, z3d, pp, mu_wa, w2a2)


def _alibi_slopes(n):
    def pow2(m):
        start = 2.0 ** (-8.0 / m)
        return [start ** (i + 1) for i in range(m)]
    if math.log2(n).is_integer():
        return pow2(n)
    cl = 2 ** int(math.floor(math.log2(n)))
    return pow2(cl) + pow2(2 * cl)[0::2][: n - cl]


def _diff_attn_body(q_ref, k_ref, v_ref, slope_ref, lamp_ref, sub_ref, o_ref,
                    kb_ref, vt_ref, band_ref, sa_ref, sb_ref, acc_ref, *, tq, lam_init):
    bi = pl.program_id(1)
    tk = ATT_KV_TILE
    band = tq // tk
    nkv = k_ref.shape[1] // tk
    log2e = math.log2(math.e)
    slope2 = slope_ref[0][:, :1] * log2e
    aug_lane = lax.broadcasted_iota(jnp.int32, (tk, LANES), 1)

    for t in range(nkv):
        rows = slice(t * tk, (t + 1) * tk)
        kb_ref[rows, :LANES] = k_ref[0, rows, :].astype(BF16)
        vt_ref[:LANES, rows] = v_ref[0, rows, :].astype(F32).T.astype(BF16)

    @pl.when(bi == 0)
    def _():
        ones_row = lax.broadcasted_iota(jnp.int32, (ATT_V_PAD, tk), 0) == 0
        for t in range(nkv):
            rows = slice(t * tk, (t + 1) * tk)
            pos = (lax.broadcasted_iota(jnp.int32, (tk, LANES), 0) + t * tk).astype(F32)
            pieces = _split(slope2 * pos, ATT_BIAS_PIECES)
            aug = jnp.zeros((tk, LANES), F32)
            for i, piece in enumerate(pieces):
                aug = jnp.where(aug_lane == i, piece.astype(F32), aug)
            kb_ref[rows, LANES:] = aug.astype(BF16)
            vt_ref[LANES:, rows] = jnp.where(ones_row, 1.0, 0.0).astype(BF16)
        key = lax.broadcasted_iota(jnp.int32, (tk, 2 * tq), 0).astype(F32)
        lane2 = lax.broadcasted_iota(jnp.int32, (1, 2 * tq), 1)
        qry = jnp.where(lane2 < tq, lane2, lane2 - tq)
        qlim = ((qry // ATT_CHUNK + 1) * ATT_CHUNK).astype(F32)
        for d in range(band):
            kpos = key + float(d * tk)
            band_ref[d] = jnp.where(
                kpos < qlim, (-2.0 * slope2) * jnp.maximum(kpos - qry.astype(F32), 0.0), MASK_VALUE)

    lane = lax.broadcasted_iota(jnp.int32, (2 * tq, LANES), 1)
    head0 = lane < HEAD_DIM
    first = lax.broadcasted_iota(jnp.int32, (2 * tq, LANES), 0) < tq
    ones_cols = jnp.where(lane < ATT_BIAS_PIECES, 1.0, 0.0)
    lp = lamp_ref[...]
    lam = (jnp.exp(jnp.sum(lp[0:1] * lp[1:2], axis=-1, keepdims=True))
           - jnp.exp(jnp.sum(lp[2:3] * lp[3:4], axis=-1, keepdims=True)) + lam_init)

    def scores(j, qcat, dst_ref):
        dst_ref[...] = _dot_nt(kb_ref[j * tk:(j + 1) * tk, :], qcat)

    def softmax_pv(j, src_ref, m, band_index):
        vt = vt_ref[:, j * tk:(j + 1) * tk]
        s = src_ref[...]
        if band_index is not None:
            s = s + band_ref[band_index]
        if m is None:
            m_new = jnp.max(s, axis=0, keepdims=True)
            acc_ref[...] = _dot(vt, jnp.exp2(s - m_new).astype(BF16))
        else:
            m_new = jnp.maximum(m, jnp.max(s, axis=0, keepdims=True))
            p = jnp.exp2(s - m_new)
            acc_ref[...] = jnp.exp2(m - m_new) * acc_ref[...] + _dot(vt, p.astype(BF16))
        return m_new

    bufs = (sa_ref, sb_ref)
    for n in range(nkv // band):
        qrows = slice(n * tq, (n + 1) * tq)
        q = q_ref[0, qrows, :].astype(F32) * (HEAD_DIM ** -0.5 * log2e)
        q2x = jnp.concatenate([q, q], axis=0)
        qcat = jnp.concatenate([jnp.where(first == head0, q2x, 0.0), ones_cols],
                               axis=1).astype(BF16)
        ntiles = (n + 1) * band
        m = None
        scores(0, qcat, bufs[0])
        for j in range(ntiles):
            if j + 1 < ntiles:
                scores(j + 1, qcat, bufs[(j + 1) % 2])
            m = softmax_pv(j, bufs[j % 2], m, None if j < n * band else j - n * band)
        on = acc_ref[:LANES, :] * (1.0 / acc_ref[LANES:LANES + 1, :])
        o = on[:, :tq] - lam * on[:, tq:]
        ms = jnp.mean(o * o, axis=0, keepdims=True)
        o = o * lax.rsqrt(ms + NORM_EPS) * sub_ref[...] * (1.0 - lam_init)
        o_ref[0, qrows, :] = o.T.astype(o_ref.dtype)


def _diff_attn(q3d, k3d, v3d, slopes, lam_params, subln_col, lam_init, tq):
    b, s, br_w = q3d.shape
    nh = br_w // LANES
    tk = ATT_KV_TILE
    return pl.pallas_call(
        functools.partial(_diff_attn_body, tq=tq, lam_init=lam_init),
        grid=(nh, b),
        in_specs=[
            pl.BlockSpec((1, s, LANES), lambda h, bi: (bi, 0, h)),
            pl.BlockSpec((1, s, LANES), lambda h, bi: (bi, 0, h)),
            pl.BlockSpec((1, s, LANES), lambda h, bi: (bi, 0, h)),
            pl.BlockSpec((1, 1, LANES), lambda h, bi: (h, 0, 0)),
            pl.BlockSpec((4, HEAD_DIM), lambda h, bi: (0, 0)),
            pl.BlockSpec((LANES, 1), lambda h, bi: (0, 0)),
        ],
        out_specs=pl.BlockSpec((1, s, LANES), lambda h, bi: (bi, 0, h)),
        out_shape=jax.ShapeDtypeStruct((b, s, br_w), ACT_DTYPE),
        scratch_shapes=[pltpu.VMEM((s, 2 * LANES), BF16), pltpu.VMEM((LANES + ATT_V_PAD, s), BF16),
                        pltpu.VMEM((tq // tk, tk, 2 * tq), F32),
                        pltpu.VMEM((tk, 2 * tq), F32), pltpu.VMEM((tk, 2 * tq), F32),
                        pltpu.VMEM((LANES + ATT_V_PAD, 2 * tq), F32)],
        compiler_params=pltpu.CompilerParams(
            dimension_semantics=("arbitrary", "arbitrary"),
            vmem_limit_bytes=VMEM_LIMIT_BYTES),
        name="diff_attn",
    )(q3d, k3d, v3d, slopes, lam_params, subln_col)


def _out_body(ybr_ref, gate_ref, qm_ref, gm_ref, x_ref, km_ref, vm_ref, w_ref, pn_ref, o_ref, *, br_w):
    tm = x_ref.shape[0]
    ml = km_ref.shape[1] // MEM_HEADS
    subs = [slice(r * OUT_SUB_ROWS, (r + 1) * OUT_SUB_ROWS) for r in range(tm // OUT_SUB_ROWS)]
    qscale = HEAD_DIM ** -0.5 * math.log2(math.e)
    s = [_dot_nt((qm_ref[rs, :].astype(F32) * qscale).astype(BF16), km_ref[0]) for rs in subs]
    y_br = (ybr_ref[...].astype(F32) * _silu(gate_ref[...].astype(F32))).astype(BF16)
    pcat = []
    for sr in s:
        ps = []
        for hd in range(MEM_HEADS):
            sh = sr[:, hd * ml:(hd + 1) * ml]
            p = jnp.exp2(sh - jnp.max(sh, axis=-1, keepdims=True))
            ps.append((p * (1.0 / jnp.sum(p, axis=-1, keepdims=True))).astype(BF16))
        pcat.append(jnp.concatenate(ps, axis=1))
    y_mem = [_dot(pc, vm_ref[0]) for pc in pcat]
    y_mem = (jnp.concatenate(y_mem, axis=0) * _silu(gm_ref[...].astype(F32))).astype(BF16)
    y = _dot(y_br, w_ref[:br_w, :]) + _dot(y_mem, w_ref[br_w:, :])
    o_ref[...] = x_ref[...] + _rms(y, pn_ref[...])


def _out_proj(ybr, gate, qm, gm, x2d, k_mem, v_mem, w_bf16, post_g, seq, tm):
    m, d = x2d.shape
    br_w = ybr.shape[1]
    ml = k_mem.shape[1]
    per_b = seq // tm
    row = lambda w: pl.BlockSpec((tm, w), lambda i: (i, 0))
    return pl.pallas_call(
        functools.partial(_out_body, br_w=br_w),
        grid=(m // tm,),
        in_specs=[
            row(br_w), row(br_w), row(MEM_W), row(MEM_W), row(d),
            pl.BlockSpec((1, ml, MEM_W), lambda i: (i // per_b, 0, 0)),
            pl.BlockSpec((1, ml, MEM_W), lambda i: (i // per_b, 0, 0)),
            pl.BlockSpec((d, d), lambda i: (0, 0)),
            pl.BlockSpec((1, d), lambda i: (0, 0)),
        ],
        out_specs=row(d),
        out_shape=jax.ShapeDtypeStruct((m, d), F32),
        compiler_params=pltpu.CompilerParams(
            dimension_semantics=("arbitrary",), vmem_limit_bytes=VMEM_LIMIT_BYTES),
        name="out_proj",
    )(ybr, gate, qm, gm, x2d, k_mem, v_mem, w_bf16, post_g)


def _rwkv_params(mu, w0, w2, a0, a2, k_k, k_a, r_k, lnx_w, lnx_b, br_w):
    npairs = br_w // LANES
    rows = [mu[:br_w], mu[br_w:2 * br_w], mu[2 * br_w:3 * br_w], w0, a0, k_k, k_a,
            r_k.reshape(-1), lnx_w, lnx_b]
    pp = jnp.stack([r.reshape(npairs, LANES) for r in rows], axis=1)
    pp = jnp.pad(pp, ((0, 0), (0, _PP_ROWS - len(rows)), (0, 0)))
    mu_wa = mu[3 * br_w:].reshape(1, LANES)
    w2p = w2.reshape(LORA_W, npairs, LANES).transpose(1, 0, 2)
    a2p = a2.reshape(LORA_W, npairs, LANES).transpose(1, 0, 2)
    zeros = jnp.zeros_like(w2p)
    w2a2 = jnp.concatenate(
        [jnp.concatenate([w2p, zeros], axis=2), jnp.concatenate([zeros, a2p], axis=2)], axis=1)
    return pp, mu_wa, w2a2


def kernel(x, mem, pre_norm, post_norm, w_out, mem_norm, w_mem_kv, a_w_in, a_shift_mu, a_w0, a_w2,
           a_a0, a_a2, a_k_k, a_k_a, a_r_k, a_lnx_w, a_lnx_b, kv_norm, w_kv, b_w_in, b_lam_q1,
           b_lam_k1, b_lam_q2, b_lam_k2, b_subln):
    bsz, seq, d = x.shape
    depth = pre_norm.shape[0]
    n_a = a_w_in.shape[0]
    br_w = d - MEM_W
    a_shift = 3 * br_w + 2 * LORA_W
    m = bsz * seq
    tm = 512
    x2d = x.reshape(m, d)
    slopes = jnp.asarray(
        np.repeat(np.array(_alibi_slopes(br_w // LANES), np.float32)[:, None, None], LANES, axis=2))

    for l in range(depth):
        k_mem, v_mem = _mem_kv(mem, mem_norm[l].reshape(1, d), w_mem_kv[l].astype(BF16))
        if l < n_a:
            i = l
            segs = [(0, 0, a_shift), (0, a_shift, a_shift + br_w),
                    (0, a_shift + br_w, a_shift + br_w + MEM_W),
                    (0, a_shift + br_w + MEM_W, a_shift + br_w + 2 * MEM_W)]
            z, gate, q_mem, g_mem = _norm_proj(
                x2d, pre_norm[l].reshape(1, d), a_w_in[i].astype(BF16), segs, tm)
            pp, mu_wa, w2a2 = _rwkv_params(
                a_shift_mu[i], a_w0[i], a_w2[i], a_a0[i], a_a2[i], a_k_k[i], a_k_a[i], a_r_k[i],
                a_lnx_w[i], a_lnx_b[i], br_w)
            y_br = _rwkv(z.reshape(bsz, seq, a_shift), pp, mu_wa, w2a2).reshape(m, br_w)
        else:
            i = l - n_a
            if l == n_a:
                w_cat = jnp.concatenate([b_w_in[i], w_kv], axis=1).astype(BF16)
                gains = jnp.stack([pre_norm[l], kv_norm], axis=0)
                segs = [(0, 0, br_w), (0, br_w, 2 * br_w), (0, 2 * br_w, 2 * br_w + MEM_W),
                        (0, 2 * br_w + MEM_W, 2 * br_w + 2 * MEM_W),
                        (1, 2 * br_w + 2 * MEM_W, 3 * br_w + 2 * MEM_W),
                        (1, 3 * br_w + 2 * MEM_W, 4 * br_w + 2 * MEM_W)]
                q, gate, q_mem, g_mem, k_sh, v_sh = _norm_proj(x2d, gains, w_cat, segs, tm)
                k_sh = k_sh.reshape(bsz, seq, br_w)
                v_sh = v_sh.reshape(bsz, seq, br_w)
            else:
                segs = [(0, 0, br_w), (0, br_w, 2 * br_w), (0, 2 * br_w, 2 * br_w + MEM_W),
                        (0, 2 * br_w + MEM_W, 2 * br_w + 2 * MEM_W)]
                q, gate, q_mem, g_mem = _norm_proj(
                    x2d, pre_norm[l].reshape(1, d), b_w_in[i].astype(BF16), segs, tm)
            lam_init = 0.8 - 0.6 * math.exp(-0.3 * l)
            lam_params = jnp.stack([b_lam_q1[i], b_lam_k1[i], b_lam_q2[i], b_lam_k2[i]], axis=0)
            y_br = _diff_attn(q.reshape(bsz, seq, br_w), k_sh, v_sh, slopes, lam_params,
                              b_subln[i].reshape(LANES, 1), lam_init, ATT_Q_TILE).reshape(m, br_w)
        x2d = _out_proj(y_br, gate, q_mem, g_mem, x2d, k_mem, v_mem, w_out[l].astype(BF16),
                        post_norm[l].reshape(1, d), seq, OUT_ROWS)
    return x2d.reshape(bsz, seq, d)
```

```python
import functools
import math

import numpy as np
import jax
import jax.numpy as jnp
from jax import lax
from jax.experimental import pallas as pl
from jax.experimental.pallas import tpu as pltpu

F32 = jnp.float32
BF16 = jnp.bfloat16
ACT_DTYPE = BF16

HEAD_DIM = 64
LANES = 128
MEM_HEADS = 4
MEM_W = MEM_HEADS * HEAD_DIM
LORA_W = 64
ATT_CHUNK = 64
ATT_Q_TILE = 512
ATT_KV_TILE = 512
ATT_BIAS_PIECES = 3
ATT_V_PAD = 16
RWKV_CHUNK = 64
RWKV_GROUP = 16
RWKV_CUMSUM_ROWS = 256
RWKV_STAGGER = 2
OUT_ROWS = 1024
OUT_SUB_ROWS = 256
NORM_EPS = 1e-6
LNX_EPS = 64e-5
MASK_VALUE = -1e30
VMEM_LIMIT_BYTES = 56 * 1024 * 1024


def _dot(a, b):
    return jnp.dot(a, b, preferred_element_type=F32)


def _dot_nt(a, b):
    return lax.dot_general(a, b, (((1,), (1,)), ((), ())), preferred_element_type=F32)


def _dot_tn(a, b):
    return lax.dot_general(a, b, (((0,), (0,)), ((), ())), preferred_element_type=F32)


def _split(x, pieces):
    out = []
    rem = x
    for i in range(pieces):
        p = rem.astype(BF16)
        out.append(p)
        if i + 1 < pieces:
            rem = rem - p.astype(F32)
    return out


def _mm(a, b, fn=_dot, pa=1, pb=1):
    aps = _split(a, pa)
    bps = _split(b, pb)
    order = max(pa, pb)
    acc = None
    for i, ap in enumerate(aps):
        for j, bp in enumerate(bps):
            if i + j < order:
                t = fn(ap, bp)
                acc = t if acc is None else acc + t
    return acc


def _rms(x, g):
    ms = jnp.mean(x * x, axis=-1, keepdims=True)
    return (x * lax.rsqrt(ms + NORM_EPS)) * g


def _silu(x):
    h = 0.5 * x
    return h + h * jnp.tanh(h)


def _norm_proj_body(x_ref, g_ref, w_ref, *o_refs, segments):
    x = x_ref[...]
    ms = jnp.mean(x * x, axis=-1, keepdims=True)
    xn = x * lax.rsqrt(ms + NORM_EPS)
    hs = {}
    for o_ref, (gi, lo, hi) in zip(o_refs, segments):
        if gi not in hs:
            hs[gi] = (xn * g_ref[gi:gi + 1, :]).astype(BF16)
        o_ref[...] = _dot(hs[gi], w_ref[:, lo:hi]).astype(o_ref.dtype)


def _norm_proj(x2d, gains, w_bf16, segments, tm):
    m, d = x2d.shape
    n = w_bf16.shape[1]
    ng = gains.shape[0]
    out_shape = [jax.ShapeDtypeStruct((m, hi - lo), ACT_DTYPE) for _, lo, hi in segments]
    out_specs = [pl.BlockSpec((tm, hi - lo), lambda i: (i, 0)) for _, lo, hi in segments]
    return pl.pallas_call(
        functools.partial(_norm_proj_body, segments=tuple(segments)),
        grid=(m // tm,),
        in_specs=[
            pl.BlockSpec((tm, d), lambda i: (i, 0)),
            pl.BlockSpec((ng, d), lambda i: (0, 0)),
            pl.BlockSpec((d, n), lambda i: (0, 0)),
        ],
        out_specs=out_specs,
        out_shape=out_shape,
        compiler_params=pltpu.CompilerParams(
            dimension_semantics=("arbitrary",), vmem_limit_bytes=VMEM_LIMIT_BYTES),
        name="norm_proj",
    )(x2d, gains, w_bf16)


def _mem_kv_body(mem_ref, g_ref, w_ref, k_ref, v_ref):
    ml = mem_ref.shape[1]
    h = _rms(mem_ref[0], g_ref[...]).astype(BF16)
    kv = _dot(h, w_ref[...])
    lane = lax.broadcasted_iota(jnp.int32, (ml, MEM_W), 1)
    for hd in range(MEM_HEADS):
        in_head = (lane >= hd * HEAD_DIM) & (lane < (hd + 1) * HEAD_DIM)
        k_ref[0, hd * ml:(hd + 1) * ml, :] = jnp.where(in_head, kv[:, :MEM_W], 0.0).astype(k_ref.dtype)
        v_ref[0, hd * ml:(hd + 1) * ml, :] = jnp.where(in_head, kv[:, MEM_W:], 0.0).astype(v_ref.dtype)


def _mem_kv(mem, g, w_bf16):
    b, ml, d = mem.shape
    return pl.pallas_call(
        _mem_kv_body,
        grid=(b,),
        in_specs=[
            pl.BlockSpec((1, ml, d), lambda i: (i, 0, 0)),
            pl.BlockSpec((1, d), lambda i: (0, 0)),
            pl.BlockSpec((d, 2 * MEM_W), lambda i: (0, 0)),
        ],
        out_specs=[pl.BlockSpec((1, MEM_HEADS * ml, MEM_W), lambda i: (i, 0, 0))] * 2,
        out_shape=[jax.ShapeDtypeStruct((b, MEM_HEADS * ml, MEM_W), ACT_DTYPE)] * 2,
        compiler_params=pltpu.CompilerParams(
            dimension_semantics=("arbitrary",), vmem_limit_bytes=VMEM_LIMIT_BYTES),
        name="mem_kv",
    )(mem, g, w_bf16)


_PP_MU_R, _PP_MU_K, _PP_MU_V, _PP_W0, _PP_A0, _PP_KK, _PP_KA, _PP_RK, _PP_LNW, _PP_LNB = range(10)
_PP_ROWS = 16


def _rwkv_body(zr_ref, zk_ref, zv_ref, zwa_ref, pp_ref, muwa_ref, w2a2_ref, y_ref,
               s_ref, prev_ref, *, nsteps, stagger, group):
    c = RWKV_CHUNK
    rows = group * c
    lane = lax.broadcasted_iota(jnp.int32, (c, LANES), 1)
    row = lax.broadcasted_iota(jnp.int32, (c, LANES), 0)
    head0 = lane < HEAD_DIM
    scol = jnp.where(head0, lane, lane - HEAD_DIM)
    strict = scol < row
    incl = scol <= row
    diag = scol == row
    r2 = lax.broadcasted_iota(jnp.int32, (LANES, LANES), 0)
    c2 = lax.broadcasted_iota(jnp.int32, (LANES, LANES), 1)
    blockmask = (r2 < HEAD_DIM) == (c2 < HEAD_DIM)
    blockones = jnp.where(blockmask, 1.0, 0.0).astype(BF16)
    cum_rows = min(rows, RWKV_CUMSUM_ROWS)
    tr = lax.broadcasted_iota(jnp.int32, (cum_rows, cum_rows), 0)
    tc = lax.broadcasted_iota(jnp.int32, (cum_rows, cum_rows), 1)
    tril_ones = jnp.where((tc <= tr) & (tc // c == tr // c), 1.0, 0.0).astype(BF16)
    slab_lane = lax.broadcasted_iota(jnp.int32, (rows, LANES), 1)
    slab_head0 = slab_lane < HEAD_DIM
    slab_first = lax.broadcasted_iota(jnp.int32, (rows, LANES), 0) == 0

    def bd(x):
        return jnp.concatenate([jnp.where(head0, x, 0.0), jnp.where(head0, 0.0, x)], axis=0)

    def seg_sum(x):
        return _dot(x.astype(BF16), blockones)

    pp = pp_ref[0]
    prow = lambda i: pp[i:i + 1, :]
    mu_r, mu_k, mu_v = prow(_PP_MU_R), prow(_PP_MU_K), prow(_PP_MU_V)
    w0, a0 = prow(_PP_W0), prow(_PP_A0)
    k_k, k_a, r_k = prow(_PP_KK), prow(_PP_KA), prow(_PP_RK)
    lnx_w, lnx_b = prow(_PP_LNW), prow(_PP_LNB)
    mu_wa = muwa_ref[...]
    w2a2 = w2a2_ref[0]

    s_ref[...] = jnp.zeros_like(s_ref)
    prev_ref[...] = jnp.zeros_like(prev_ref)

    def shift_mix(ref, slot, sl, mu):
        z = ref[0, sl, :].astype(F32)
        zp = pltpu.roll(z, 1, axis=0)
        zp = jnp.where(slab_first, prev_ref[slot:slot + 1, :], zp)
        prev_ref[slot:slot + 1, :] = z[rows - 1:rows, :]
        return z + (zp - z) * mu


    def prepare(sl):
        r = shift_mix(zr_ref, 0, sl, mu_r)
        k = shift_mix(zk_ref, 1, sl, mu_k)
        yield
        v = shift_mix(zv_ref, 2, sl, mu_v)
        wa = shift_mix(zwa_ref, 3, sl, mu_wa)
        lora = _mm(jnp.where(slab_head0, jnp.tanh(wa), wa), w2a2)
        yield
        wlog = w0 + lora[:, :LANES]
        nw = -wlog
        w = -(jnp.maximum(nw, 0.0) + jnp.log(1.0 + jnp.exp(-jnp.abs(nw)))) - 0.5
        logw = -jnp.exp(w)
        a = 1.0 / (1.0 + jnp.exp(-(a0 + lora[:, LANES:])))
        kk = k * k_k
        kk = kk * lax.rsqrt(jnp.maximum(seg_sum(kk * kk), 1e-12))
        yield
        kmod = k * (1.0 + (a - 1.0) * k_a)
        alpha = -kk
        beta = kk * a
        l2 = jnp.concatenate(_split(logw, 2), axis=1)
        cum2 = jnp.concatenate(
            [_dot(tril_ones, l2[i:i + cum_rows]) for i in range(0, rows, cum_rows)], axis=0)
        cum = cum2[:, :LANES] + cum2[:, LANES:]
        yield
        e_neg = jnp.exp(-cum)
        at = alpha * jnp.exp(cum - logw)
        rt = r * jnp.exp(cum)
        bt = beta * e_neg
        kt = kmod * e_neg
        yield
        chunks = [slice(g * c, (g + 1) * c) for g in range(group)]
        cum_c = [cum[cs.stop - 1:cs.stop, :] for cs in chunks]
        e_end = [jnp.exp(cum_c[g] - cum[cs]) for g, cs in enumerate(chunks)]
        per = lambda x: [x[cs] for cs in chunks]
        return dict(
            at=per(at), rt=per(rt), bt=per(bt), kt=per(kt), v=per(v),
            bt_end=[beta[cs] * e_end[g] for g, cs in enumerate(chunks)],
            kt_end=[kmod[cs] * e_end[g] for g, cs in enumerate(chunks)],
            p_c=[jnp.exp(x) for x in cum_c], v_slab=v, rk=r * kmod * r_k)

    def chains(p):
        at, rt, bt, kt, bt_end, kt_end, v = (p[x] for x in ("at", "rt", "bt", "kt", "bt_end", "kt_end", "v"))
        n = range(group)
        lhs = [jnp.concatenate([at[g], rt[g]], axis=0) for g in n]
        xbk = [_mm(lhs[g], jnp.concatenate([bd(bt[g]), bd(kt[g])], axis=0), _dot_nt) for g in n]
        yield
        a_rb = [jnp.where(incl, xbk[g][c:, :LANES], 0.0) for g in n]
        akrk = [jnp.concatenate([jnp.where(strict, xbk[g][:c, LANES:], 0.0),
                                 jnp.where(incl, xbk[g][c:, LANES:], 0.0)], axis=0) for g in n]
        av = [_mm(akrk[g], bd(v[g])) for g in n]
        yield
        apow = [jnp.where(strict, xbk[g][:c, :LANES], 0.0) for g in n]
        tinv = [jnp.where(diag, 1.0, 0.0) + apow[g] for g in n]
        nfac = int(math.log2(c))
        for i in range(1, nfac):
            rhs = [[bd(apow[g])] + ([bd(tinv[g])] if i > 1 else []) for g in n]
            d = [_mm(apow[g], jnp.concatenate(rhs[g], axis=1)) for g in n]
            if i > 1:
                tinv = [tinv[g] + d[g][:, LANES:] for g in n]
            apow = [d[g][:, :LANES] for g in n]
            yield
        tinv = [tinv[g] + _mm(apow[g], bd(tinv[g])) for g in n]
        yield
        x = [_mm(tinv[g], jnp.concatenate([bd(at[g]), bd(av[g][:c])], axis=1)) for g in n]
        wmat = [x[g][:, :LANES] for g in n]
        u0 = [x[g][:, LANES:] for g in n]
        yield
        d2 = [_mm(a_rb[g], jnp.concatenate([bd(wmat[g]), bd(u0[g])], axis=1)) for g in n]
        rp = [rt[g] + d2[g][:, :LANES] for g in n]
        y0 = [d2[g][:, LANES:] + av[g][c:] for g in n]
        yield
        fold = lambda z: jnp.where(head0, z[:HEAD_DIM], z[HEAD_DIM:])
        gp = [fold(_mm(wmat[g], bt_end[g], _dot_tn)) for g in n]
        npart = [fold(_mm(jnp.concatenate([u0[g], v[g]], axis=0),
                          jnp.concatenate([bt_end[g], kt_end[g]], axis=0), _dot_tn)) for g in n]
        yield
        spans = [[(jnp.where(diag, gp[g] + p["p_c"][g], gp[g]), npart[g]) for g in n]]
        while len(spans[-1]) > 1:
            prev = spans[-1]
            nxt = []
            for i in range(0, len(prev), 2):
                (ma, na), (mb, nb) = prev[i], prev[i + 1]
                prod = _mm(jnp.concatenate([ma, na], axis=0), bd(mb))
                nxt.append((prod[:HEAD_DIM], prod[HEAD_DIM:] + nb))
            spans.append(nxt)
            yield
        return dict(rp=rp, y0=y0, spans=spans, v_slab=p["v_slab"], rk=p["rk"])

    def finish(t, sl):
        spans = t["spans"]
        states = {0: s_ref[...]}
        top = len(spans) - 1
        m_all, n_all = spans[top][0]
        s_ref[...] = _mm(states[0], bd(m_all)) + n_all
        yield
        for level in range(top, 0, -1):
            width = 1 << level
            for lo in range(0, group, width):
                ma, na = spans[level - 1][lo >> (level - 1)]
                states[lo + width // 2] = _mm(states[lo], bd(ma)) + na
            yield
        y = jnp.concatenate(
            [_mm(t["rp"][g], bd(states[g]), _dot_nt) + t["y0"][g] for g in range(group)], axis=0)
        yield
        mean = seg_sum(y) * (1.0 / HEAD_DIM)
        yc = y - mean
        var = seg_sum(yc * yc) * (1.0 / HEAD_DIM)
        yield
        yn = yc * lax.rsqrt(var + LNX_EPS) * lnx_w + lnx_b
        bonus = seg_sum(t["rk"]) * t["v_slab"]
        y_ref[0, sl, :] = (yn + bonus).astype(y_ref.dtype)

    def run(*gens):
        results = [None] * len(gens)
        live = list(range(len(gens)))
        while live:
            for i in list(live):
                try:
                    next(gens[i])
                except StopIteration as stop:
                    results[i] = stop.value
                    live.remove(i)
        return results

    def step(si, carry):
        slabs = [pl.ds(pl.multiple_of((si * stagger + j) * rows, rows), rows) for j in range(stagger)]
        prepared, = run(prepare(slabs[0]))
        done = None
        for j in range(stagger):
            gens = [chains(prepared)]
            if j + 1 < stagger:
                gens.append(prepare(slabs[j + 1]))
            if done is not None:
                gens.append(finish(done, slabs[j - 1]))
            out = run(*gens)
            done = out[0]
            if j + 1 < stagger:
                prepared = out[1]
        run(finish(done, slabs[stagger - 1]))
        return carry

    lax.fori_loop(0, nsteps, step, 0)


def _rwkv(z3d, pp, mu_wa, w2a2):
    b, s, a_shift = z3d.shape
    br_w = (a_shift - 2 * LORA_W) // 3
    npairs = br_w // LANES
    zspec = lambda off: pl.BlockSpec((1, s, LANES), lambda bi, hp: (bi, 0, off + hp))
    return pl.pallas_call(
        functools.partial(_rwkv_body, nsteps=s // (RWKV_CHUNK * RWKV_GROUP * RWKV_STAGGER),
                          stagger=RWKV_STAGGER, group=RWKV_GROUP),
        grid=(b, npairs),
        in_specs=[
            zspec(0), zspec(npairs), zspec(2 * npairs),
            pl.BlockSpec((1, s, LANES), lambda bi, hp: (bi, 0, 3 * npairs)),
            pl.BlockSpec((1, _PP_ROWS, LANES), lambda bi, hp: (hp, 0, 0)),
            pl.BlockSpec((1, LANES), lambda bi, hp: (0, 0)),
            pl.BlockSpec((1, LANES, 2 * LANES), lambda bi, hp: (hp, 0, 0)),
        ],
        out_specs=pl.BlockSpec((1, s, LANES), lambda bi, hp: (bi, 0, hp)),
        out_shape=jax.ShapeDtypeStruct((b, s, br_w), ACT_DTYPE),
        scratch_shapes=[pltpu.VMEM((HEAD_DIM, LANES), F32), pltpu.VMEM((8, LANES), F32)],
        compiler_params=pltpu.CompilerParams(
            dimension_semantics=("arbitrary", "arbitrary"), vmem_limit_bytes=VMEM_LIMIT_BYTES),
        name="rwkv7_scan",
    )(z3d, z3d, z3d, z3d, pp, mu_wa, w2a2)


def _alibi_slopes(n):
    def pow2(m):
        start = 2.0 ** (-8.0 / m)
        return [start ** (i + 1) for i in range(m)]
    if math.log2(n).is_integer():
        return pow2(n)
    cl = 2 ** int(math.floor(math.log2(n)))
    return pow2(cl) + pow2(2 * cl)[0::2][: n - cl]


def _diff_attn_body(q_ref, k_ref, v_ref, slope_ref, lamp_ref, sub_ref, o_ref,
                    kb_ref, vt_ref, band_ref, sa_ref, sb_ref, acc_ref, *, tq, lam_init):
    bi = pl.program_id(1)
    tk = ATT_KV_TILE
    band = tq // tk
    nkv = k_ref.shape[1] // tk
    log2e = math.log2(math.e)
    slope2 = slope_ref[0][:, :1] * log2e
    aug_lane = lax.broadcasted_iota(jnp.int32, (tk, LANES), 1)

    for t in range(nkv):
        rows = slice(t * tk, (t + 1) * tk)
        kb_ref[rows, :LANES] = k_ref[0, rows, :].astype(BF16)
        vt_ref[:LANES, rows] = v_ref[0, rows, :].astype(F32).T.astype(BF16)

    @pl.when(bi == 0)
    def _():
        ones_row = lax.broadcasted_iota(jnp.int32, (ATT_V_PAD, tk), 0) == 0
        for t in range(nkv):
            rows = slice(t * tk, (t + 1) * tk)
            pos = (lax.broadcasted_iota(jnp.int32, (tk, LANES), 0) + t * tk).astype(F32)
            pieces = _split(slope2 * pos, ATT_BIAS_PIECES)
            aug = jnp.zeros((tk, LANES), F32)
            for i, piece in enumerate(pieces):
                aug = jnp.where(aug_lane == i, piece.astype(F32), aug)
            kb_ref[rows, LANES:] = aug.astype(BF16)
            vt_ref[LANES:, rows] = jnp.where(ones_row, 1.0, 0.0).astype(BF16)
        key = lax.broadcasted_iota(jnp.int32, (tk, 2 * tq), 0).astype(F32)
        lane2 = lax.broadcasted_iota(jnp.int32, (1, 2 * tq), 1)
        qry = jnp.where(lane2 < tq, lane2, lane2 - tq)
        qlim = ((qry // ATT_CHUNK + 1) * ATT_CHUNK).astype(F32)
        for d in range(band):
            kpos = key + float(d * tk)
            band_ref[d] = jnp.where(
                kpos < qlim, (-2.0 * slope2) * jnp.maximum(kpos - qry.astype(F32), 0.0), MASK_VALUE)

    lane = lax.broadcasted_iota(jnp.int32, (2 * tq, LANES), 1)
    head0 = lane < HEAD_DIM
    first = lax.broadcasted_iota(jnp.int32, (2 * tq, LANES), 0) < tq
    ones_cols = jnp.where(lane < ATT_BIAS_PIECES, 1.0, 0.0)
    lp = lamp_ref[...]
    lam = (jnp.exp(jnp.sum(lp[0:1] * lp[1:2], axis=-1, keepdims=True))
           - jnp.exp(jnp.sum(lp[2:3] * lp[3:4], axis=-1, keepdims=True)) + lam_init)

    def scores(j, qcat, dst_ref):
        dst_ref[...] = _dot_nt(kb_ref[j * tk:(j + 1) * tk, :], qcat)

    def softmax_pv(j, src_ref, m, band_index):
        vt = vt_ref[:, j * tk:(j + 1) * tk]
        s = src_ref[...]
        if band_index is not None:
            s = s + band_ref[band_index]
        if m is None:
            m_new = jnp.max(s, axis=0, keepdims=True)
            acc_ref[...] = _dot(vt, jnp.exp2(s - m_new).astype(BF16))
        else:
            m_new = jnp.maximum(m, jnp.max(s, axis=0, keepdims=True))
            p = jnp.exp2(s - m_new)
            acc_ref[...] = jnp.exp2(m - m_new) * acc_ref[...] + _dot(vt, p.astype(BF16))
        return m_new

    bufs = (sa_ref, sb_ref)
    for n in range(nkv // band):
        qrows = slice(n * tq, (n + 1) * tq)
        q = q_ref[0, qrows, :].astype(F32) * (HEAD_DIM ** -0.5 * log2e)
        q2x = jnp.concatenate([q, q], axis=0)
        qcat = jnp.concatenate([jnp.where(first == head0, q2x, 0.0), ones_cols],
                               axis=1).astype(BF16)
        ntiles = (n + 1) * band
        m = None
        scores(0, qcat, bufs[0])
        for j in range(ntiles):
            if j + 1 < ntiles:
                scores(j + 1, qcat, bufs[(j + 1) % 2])
            m = softmax_pv(j, bufs[j % 2], m, None if j < n * band else j - n * band)
        on = acc_ref[:LANES, :] * (1.0 / acc_ref[LANES:LANES + 1, :])
        o = on[:, :tq] - lam * on[:, tq:]
        ms = jnp.mean(o * o, axis=0, keepdims=True)
        o = o * lax.rsqrt(ms + NORM_EPS) * sub_ref[...] * (1.0 - lam_init)
        o_ref[0, qrows, :] = o.T.astype(o_ref.dtype)


def _diff_attn(q3d, k3d, v3d, slopes, lam_params, subln_col, lam_init, tq):
    b, s, br_w = q3d.shape
    nh = br_w // LANES
    tk = ATT_KV_TILE
    return pl.pallas_call(
        functools.partial(_diff_attn_body, tq=tq, lam_init=lam_init),
        grid=(nh, b),
        in_specs=[
            pl.BlockSpec((1, s, LANES), lambda h, bi: (bi, 0, h)),
            pl.BlockSpec((1, s, LANES), lambda h, bi: (bi, 0, h)),
            pl.BlockSpec((1, s, LANES), lambda h, bi: (bi, 0, h)),
            pl.BlockSpec((1, 1, LANES), lambda h, bi: (h, 0, 0)),
            pl.BlockSpec((4, HEAD_DIM), lambda h, bi: (0, 0)),
            pl.BlockSpec((LANES, 1), lambda h, bi: (0, 0)),
        ],
        out_specs=pl.BlockSpec((1, s, LANES), lambda h, bi: (bi, 0, h)),
        out_shape=jax.ShapeDtypeStruct((b, s, br_w), ACT_DTYPE),
        scratch_shapes=[pltpu.VMEM((s, 2 * LANES), BF16), pltpu.VMEM((LANES + ATT_V_PAD, s), BF16),
                        pltpu.VMEM((tq // tk, tk, 2 * tq), F32),
                        pltpu.VMEM((tk, 2 * tq), F32), pltpu.VMEM((tk, 2 * tq), F32),
                        pltpu.VMEM((LANES + ATT_V_PAD, 2 * tq), F32)],
        compiler_params=pltpu.CompilerParams(
            dimension_semantics=("arbitrary", "arbitrary"),
            vmem_limit_bytes=VMEM_LIMIT_BYTES),
        name="diff_attn",
    )(q3d, k3d, v3d, slopes, lam_params, subln_col)


def _out_body(ybr_ref, gate_ref, qm_ref, gm_ref, x_ref, km_ref, vm_ref, w_ref, pn_ref, o_ref, *, br_w):
    tm = x_ref.shape[0]
    ml = km_ref.shape[1] // MEM_HEADS
    subs = [slice(r * OUT_SUB_ROWS, (r + 1) * OUT_SUB_ROWS) for r in range(tm // OUT_SUB_ROWS)]
    qscale = HEAD_DIM ** -0.5 * math.log2(math.e)
    s = [_dot_nt((qm_ref[rs, :].astype(F32) * qscale).astype(BF16), km_ref[0]) for rs in subs]
    y_br = (ybr_ref[...].astype(F32) * _silu(gate_ref[...].astype(F32))).astype(BF16)
    pcat = []
    for sr in s:
        ps = []
        for hd in range(MEM_HEADS):
            sh = sr[:, hd * ml:(hd + 1) * ml]
            p = jnp.exp2(sh - jnp.max(sh, axis=-1, keepdims=True))
            ps.append((p * (1.0 / jnp.sum(p, axis=-1, keepdims=True))).astype(BF16))
        pcat.append(jnp.concatenate(ps, axis=1))
    y_mem = [_dot(pc, vm_ref[0]) for pc in pcat]
    y_mem = (jnp.concatenate(y_mem, axis=0) * _silu(gm_ref[...].astype(F32))).astype(BF16)
    y = _dot(y_br, w_ref[:br_w, :]) + _dot(y_mem, w_ref[br_w:, :])
    o_ref[...] = x_ref[...] + _rms(y, pn_ref[...])


def _out_proj(ybr, gate, qm, gm, x2d, k_mem, v_mem, w_bf16, post_g, seq, tm):
    m, d = x2d.shape
    br_w = ybr.shape[1]
    ml = k_mem.shape[1]
    per_b = seq // tm
    row = lambda w: pl.BlockSpec((tm, w), lambda i: (i, 0))
    return pl.pallas_call(
        functools.partial(_out_body, br_w=br_w),
        grid=(m // tm,),
        in_specs=[
            row(br_w), row(br_w), row(MEM_W), row(MEM_W), row(d),
            pl.BlockSpec((1, ml, MEM_W), lambda i: (i // per_b, 0, 0)),
            pl.BlockSpec((1, ml, MEM_W), lambda i: (i // per_b, 0, 0)),
            pl.BlockSpec((d, d), lambda i: (0, 0)),
            pl.BlockSpec((1, d), lambda i: (0, 0)),
        ],
        out_specs=row(d),
        out_shape=jax.ShapeDtypeStruct((m, d), F32),
        compiler_params=pltpu.CompilerParams(
            dimension_semantics=("arbitrary",), vmem_limit_bytes=VMEM_LIMIT_BYTES),
        name="out_proj",
    )(ybr, gate, qm, gm, x2d, k_mem, v_mem, w_bf16, post_g)


def _rwkv_params(mu, w0, w2, a0, a2, k_k, k_a, r_k, lnx_w, lnx_b, br_w):
    npairs = br_w // LANES
    rows = [mu[:br_w], mu[br_w:2 * br_w], mu[2 * br_w:3 * br_w], w0, a0, k_k, k_a,
            r_k.reshape(-1), lnx_w, lnx_b]
    pp = jnp.stack([r.reshape(npairs, LANES) for r in rows], axis=1)
    pp = jnp.pad(pp, ((0, 0), (0, _PP_ROWS - len(rows)), (0, 0)))
    mu_wa = mu[3 * br_w:].reshape(1, LANES)
    w2p = w2.reshape(LORA_W, npairs, LANES).transpose(1, 0, 2)
    a2p = a2.reshape(LORA_W, npairs, LANES).transpose(1, 0, 2)
    zeros = jnp.zeros_like(w2p)
    w2a2 = jnp.concatenate(
        [jnp.concatenate([w2p, zeros], axis=2), jnp.concatenate([zeros, a2p], axis=2)], axis=1)
    return pp, mu_wa, w2a2


def kernel(x, mem, pre_norm, post_norm, w_out, mem_norm, w_mem_kv, a_w_in, a_shift_mu, a_w0, a_w2,
           a_a0, a_a2, a_k_k, a_k_a, a_r_k, a_lnx_w, a_lnx_b, kv_norm, w_kv, b_w_in, b_lam_q1,
           b_lam_k1, b_lam_q2, b_lam_k2, b_subln):
    bsz, seq, d = x.shape
    depth = pre_norm.shape[0]
    n_a = a_w_in.shape[0]
    br_w = d - MEM_W
    a_shift = 3 * br_w + 2 * LORA_W
    m = bsz * seq
    tm = 512
    x2d = x.reshape(m, d)
    slopes = jnp.asarray(
        np.repeat(np.array(_alibi_slopes(br_w // LANES), np.float32)[:, None, None], LANES, axis=2))

    for l in range(depth):
        k_mem, v_mem = _mem_kv(mem, mem_norm[l].reshape(1, d), w_mem_kv[l].astype(BF16))
        if l < n_a:
            i = l
            segs = [(0, 0, a_shift), (0, a_shift, a_shift + br_w),
                    (0, a_shift + br_w, a_shift + br_w + MEM_W),
                    (0, a_shift + br_w + MEM_W, a_shift + br_w + 2 * MEM_W)]
            z, gate, q_mem, g_mem = _norm_proj(
                x2d, pre_norm[l].reshape(1, d), a_w_in[i].astype(BF16), segs, tm)
            pp, mu_wa, w2a2 = _rwkv_params(
                a_shift_mu[i], a_w0[i], a_w2[i], a_a0[i], a_a2[i], a_k_k[i], a_k_a[i], a_r_k[i],
                a_lnx_w[i], a_lnx_b[i], br_w)
            y_br = _rwkv(z.reshape(bsz, seq, a_shift), pp, mu_wa, w2a2).reshape(m, br_w)
        else:
            i = l - n_a
            if l == n_a:
                w_cat = jnp.concatenate([b_w_in[i], w_kv], axis=1).astype(BF16)
                gains = jnp.stack([pre_norm[l], kv_norm], axis=0)
                segs = [(0, 0, br_w), (0, br_w, 2 * br_w), (0, 2 * br_w, 2 * br_w + MEM_W),
                        (0, 2 * br_w + MEM_W, 2 * br_w + 2 * MEM_W),
                        (1, 2 * br_w + 2 * MEM_W, 3 * br_w + 2 * MEM_W),
                        (1, 3 * br_w + 2 * MEM_W, 4 * br_w + 2 * MEM_W)]
                q, gate, q_mem, g_mem, k_sh, v_sh = _norm_proj(x2d, gains, w_cat, segs, tm)
                k_sh = k_sh.reshape(bsz, seq, br_w)
                v_sh = v_sh.reshape(bsz, seq, br_w)
            else:
                segs = [(0, 0, br_w), (0, br_w, 2 * br_w), (0, 2 * br_w, 2 * br_w + MEM_W),
                        (0, 2 * br_w + MEM_W, 2 * br_w + 2 * MEM_W)]
                q, gate, q_mem, g_mem = _norm_proj(
                    x2d, pre_norm[l].reshape(1, d), b_w_in[i].astype(BF16), segs, tm)
            lam_init = 0.8 - 0.6 * math.exp(-0.3 * l)
            lam_params = jnp.stack([b_lam_q1[i], b_lam_k1[i], b_lam_q2[i], b_lam_k2[i]], axis=0)
            y_br = _diff_attn(q.reshape(bsz, seq, br_w), k_sh, v_sh, slopes, lam_params,
                              b_subln[i].reshape(LANES, 1), lam_init, ATT_Q_TILE).reshape(m, br_w)
        x2d = _out_proj(y_br, gate, q_mem, g_mem, x2d, k_mem, v_mem, w_out[l].astype(BF16),
                        post_norm[l].reshape(1, d), seq, OUT_ROWS)
    return x2d.reshape(bsz, seq, d)
```

```python
import functools
import math

import numpy as np
import jax
import jax.numpy as jnp
from jax import lax
from jax.experimental import pallas as pl
from jax.experimental.pallas import tpu as pltpu

F32 = jnp.float32
BF16 = jnp.bfloat16
ACT_DTYPE = BF16

HEAD_DIM = 64
LANES = 128
MEM_HEADS = 4
MEM_W = MEM_HEADS * HEAD_DIM
LORA_W = 64
ATT_CHUNK = 64
ATT_Q_TILE = 512
ATT_KV_TILE = 512
ATT_BIAS_PIECES = 3
ATT_V_PAD = 16
RWKV_CHUNK = 64
RWKV_GROUP = 16
RWKV_CUMSUM_ROWS = 256
RWKV_BATCH = 2
OUT_ROWS = 1024
OUT_SUB_ROWS = 256
NORM_EPS = 1e-6
LNX_EPS = 64e-5
MASK_VALUE = -1e30
VMEM_LIMIT_BYTES = 56 * 1024 * 1024


def _dot(a, b):
    return jnp.dot(a, b, preferred_element_type=F32)


def _dot_nt(a, b):
    return lax.dot_general(a, b, (((1,), (1,)), ((), ())), preferred_element_type=F32)


def _dot_tn(a, b):
    return lax.dot_general(a, b, (((0,), (0,)), ((), ())), preferred_element_type=F32)


def _split(x, pieces):
    out = []
    rem = x
    for i in range(pieces):
        p = rem.astype(BF16)
        out.append(p)
        if i + 1 < pieces:
            rem = rem - p.astype(F32)
    return out


def _mm(a, b, fn=_dot, pa=1, pb=1):
    aps = _split(a, pa)
    bps = _split(b, pb)
    order = max(pa, pb)
    acc = None
    for i, ap in enumerate(aps):
        for j, bp in enumerate(bps):
            if i + j < order:
                t = fn(ap, bp)
                acc = t if acc is None else acc + t
    return acc


def _rms(x, g):
    ms = jnp.mean(x * x, axis=-1, keepdims=True)
    return (x * lax.rsqrt(ms + NORM_EPS)) * g


def _silu(x):
    h = 0.5 * x
    return h + h * jnp.tanh(h)


def _norm_proj_body(x_ref, g_ref, w_ref, *o_refs, segments):
    x = x_ref[...]
    ms = jnp.mean(x * x, axis=-1, keepdims=True)
    xn = x * lax.rsqrt(ms + NORM_EPS)
    hs = {}
    for o_ref, (gi, lo, hi) in zip(o_refs, segments):
        if gi not in hs:
            hs[gi] = (xn * g_ref[gi:gi + 1, :]).astype(BF16)
        o_ref[...] = _dot(hs[gi], w_ref[:, lo:hi]).astype(o_ref.dtype)


def _norm_proj(x2d, gains, w_bf16, segments, tm):
    m, d = x2d.shape
    n = w_bf16.shape[1]
    ng = gains.shape[0]
    out_shape = [jax.ShapeDtypeStruct((m, hi - lo), ACT_DTYPE) for _, lo, hi in segments]
    out_specs = [pl.BlockSpec((tm, hi - lo), lambda i: (i, 0)) for _, lo, hi in segments]
    return pl.pallas_call(
        functools.partial(_norm_proj_body, segments=tuple(segments)),
        grid=(m // tm,),
        in_specs=[
            pl.BlockSpec((tm, d), lambda i: (i, 0)),
            pl.BlockSpec((ng, d), lambda i: (0, 0)),
            pl.BlockSpec((d, n), lambda i: (0, 0)),
        ],
        out_specs=out_specs,
        out_shape=out_shape,
        compiler_params=pltpu.CompilerParams(
            dimension_semantics=("arbitrary",), vmem_limit_bytes=VMEM_LIMIT_BYTES),
        name="norm_proj",
    )(x2d, gains, w_bf16)


def _mem_kv_body(mem_ref, g_ref, w_ref, k_ref, v_ref):
    ml = mem_ref.shape[1]
    h = _rms(mem_ref[0], g_ref[...]).astype(BF16)
    kv = _dot(h, w_ref[...])
    lane = lax.broadcasted_iota(jnp.int32, (ml, MEM_W), 1)
    for hd in range(MEM_HEADS):
        in_head = (lane >= hd * HEAD_DIM) & (lane < (hd + 1) * HEAD_DIM)
        k_ref[0, hd * ml:(hd + 1) * ml, :] = jnp.where(in_head, kv[:, :MEM_W], 0.0).astype(k_ref.dtype)
        v_ref[0, hd * ml:(hd + 1) * ml, :] = jnp.where(in_head, kv[:, MEM_W:], 0.0).astype(v_ref.dtype)


def _mem_kv(mem, g, w_bf16):
    b, ml, d = mem.shape
    return pl.pallas_call(
        _mem_kv_body,
        grid=(b,),
        in_specs=[
            pl.BlockSpec((1, ml, d), lambda i: (i, 0, 0)),
            pl.BlockSpec((1, d), lambda i: (0, 0)),
            pl.BlockSpec((d, 2 * MEM_W), lambda i: (0, 0)),
        ],
        out_specs=[pl.BlockSpec((1, MEM_HEADS * ml, MEM_W), lambda i: (i, 0, 0))] * 2,
        out_shape=[jax.ShapeDtypeStruct((b, MEM_HEADS * ml, MEM_W), ACT_DTYPE)] * 2,
        compiler_params=pltpu.CompilerParams(
            dimension_semantics=("arbitrary",), vmem_limit_bytes=VMEM_LIMIT_BYTES),
        name="mem_kv",
    )(mem, g, w_bf16)


_PP_MU_R, _PP_MU_K, _PP_MU_V, _PP_W0, _PP_A0, _PP_KK, _PP_KA, _PP_RK, _PP_LNW, _PP_LNB = range(10)
_PP_ROWS = 16


def _rwkv_body(zr_ref, zk_ref, zv_ref, zwa_ref, pp_ref, muwa_ref, w2a2_ref, y_ref,
               s_ref, prev_ref, *, group):
    c = RWKV_CHUNK
    rows = group * c
    lane = lax.broadcasted_iota(jnp.int32, (c, LANES), 1)
    row = lax.broadcasted_iota(jnp.int32, (c, LANES), 0)
    head0 = lane < HEAD_DIM
    scol = jnp.where(head0, lane, lane - HEAD_DIM)
    strict = scol < row
    incl = scol <= row
    diag = scol == row
    r2 = lax.broadcasted_iota(jnp.int32, (LANES, LANES), 0)
    c2 = lax.broadcasted_iota(jnp.int32, (LANES, LANES), 1)
    blockmask = (r2 < HEAD_DIM) == (c2 < HEAD_DIM)
    blockones = jnp.where(blockmask, 1.0, 0.0).astype(BF16)
    cum_rows = min(rows, RWKV_CUMSUM_ROWS)
    tr = lax.broadcasted_iota(jnp.int32, (cum_rows, cum_rows), 0)
    tc = lax.broadcasted_iota(jnp.int32, (cum_rows, cum_rows), 1)
    tril_ones = jnp.where((tc <= tr) & (tc // c == tr // c), 1.0, 0.0).astype(BF16)
    slab_lane = lax.broadcasted_iota(jnp.int32, (rows, LANES), 1)
    slab_head0 = slab_lane < HEAD_DIM
    slab_first = lax.broadcasted_iota(jnp.int32, (rows, LANES), 0) == 0

    def bd(x):
        return jnp.concatenate([jnp.where(head0, x, 0.0), jnp.where(head0, 0.0, x)], axis=0)

    def seg_sum(x):
        return _dot(x.astype(BF16), blockones)

    pp = pp_ref[0]
    prow = lambda i: pp[i:i + 1, :]
    mu_r, mu_k, mu_v = prow(_PP_MU_R), prow(_PP_MU_K), prow(_PP_MU_V)
    w0, a0 = prow(_PP_W0), prow(_PP_A0)
    k_k, k_a, r_k = prow(_PP_KK), prow(_PP_KA), prow(_PP_RK)
    lnx_w, lnx_b = prow(_PP_LNW), prow(_PP_LNB)
    mu_wa = muwa_ref[...]
    w2a2 = w2a2_ref[0]

    s_ref[...] = jnp.zeros_like(s_ref)
    prev_ref[...] = jnp.zeros_like(prev_ref)

    def shift_mix(ref, bj, slot, sl, mu):
        z = ref[bj, sl, :].astype(F32)
        zp = pltpu.roll(z, 1, axis=0)
        zp = jnp.where(slab_first, prev_ref[bj, slot:slot + 1, :], zp)
        prev_ref[bj, slot:slot + 1, :] = z[rows - 1:rows, :]
        return z + (zp - z) * mu


    def prepare(bj, sl):
        r = shift_mix(zr_ref, bj, 0, sl, mu_r)
        k = shift_mix(zk_ref, bj, 1, sl, mu_k)
        yield
        v = shift_mix(zv_ref, bj, 2, sl, mu_v)
        wa = shift_mix(zwa_ref, bj, 3, sl, mu_wa)
        lora = _mm(jnp.where(slab_head0, jnp.tanh(wa), wa), w2a2)
        yield
        wlog = w0 + lora[:, :LANES]
        nw = -wlog
        w = -(jnp.maximum(nw, 0.0) + jnp.log(1.0 + jnp.exp(-jnp.abs(nw)))) - 0.5
        logw = -jnp.exp(w)
        a = 1.0 / (1.0 + jnp.exp(-(a0 + lora[:, LANES:])))
        kk = k * k_k
        kk = kk * lax.rsqrt(jnp.maximum(seg_sum(kk * kk), 1e-12))
        yield
        kmod = k * (1.0 + (a - 1.0) * k_a)
        alpha = -kk
        beta = kk * a
        l2 = jnp.concatenate(_split(logw, 2), axis=1)
        cum2 = jnp.concatenate(
            [_dot(tril_ones, l2[i:i + cum_rows]) for i in range(0, rows, cum_rows)], axis=0)
        cum = cum2[:, :LANES] + cum2[:, LANES:]
        yield
        e_neg = jnp.exp(-cum)
        at = alpha * jnp.exp(cum - logw)
        rt = r * jnp.exp(cum)
        bt = beta * e_neg
        kt = kmod * e_neg
        yield
        chunks = [slice(g * c, (g + 1) * c) for g in range(group)]
        cum_c = [cum[cs.stop - 1:cs.stop, :] for cs in chunks]
        e_end = [jnp.exp(cum_c[g] - cum[cs]) for g, cs in enumerate(chunks)]
        per = lambda x: [x[cs] for cs in chunks]
        return dict(
            at=per(at), rt=per(rt), bt=per(bt), kt=per(kt), v=per(v),
            bt_end=[beta[cs] * e_end[g] for g, cs in enumerate(chunks)],
            kt_end=[kmod[cs] * e_end[g] for g, cs in enumerate(chunks)],
            p_c=[jnp.exp(x) for x in cum_c], v_slab=v, rk=r * kmod * r_k)

    def chains(p):
        at, rt, bt, kt, bt_end, kt_end, v = (p[x] for x in ("at", "rt", "bt", "kt", "bt_end", "kt_end", "v"))
        n = range(group)
        lhs = [jnp.concatenate([at[g], rt[g]], axis=0) for g in n]
        xbk = [_mm(lhs[g], jnp.concatenate([bd(bt[g]), bd(kt[g])], axis=0), _dot_nt) for g in n]
        yield
        a_rb = [jnp.where(incl, xbk[g][c:, :LANES], 0.0) for g in n]
        akrk = [jnp.concatenate([jnp.where(strict, xbk[g][:c, LANES:], 0.0),
                                 jnp.where(incl, xbk[g][c:, LANES:], 0.0)], axis=0) for g in n]
        av = [_mm(akrk[g], bd(v[g])) for g in n]
        yield
        apow = [jnp.where(strict, xbk[g][:c, :LANES], 0.0) for g in n]
        tinv = [jnp.where(diag, 1.0, 0.0) + apow[g] for g in n]
        nfac = int(math.log2(c))
        for i in range(1, nfac):
            rhs = [[bd(apow[g])] + ([bd(tinv[g])] if i > 1 else []) for g in n]
            d = [_mm(apow[g], jnp.concatenate(rhs[g], axis=1)) for g in n]
            if i > 1:
                tinv = [tinv[g] + d[g][:, LANES:] for g in n]
            apow = [d[g][:, :LANES] for g in n]
            yield
        tinv = [tinv[g] + _mm(apow[g], bd(tinv[g])) for g in n]
        yield
        x = [_mm(tinv[g], jnp.concatenate([bd(at[g]), bd(av[g][:c])], axis=1)) for g in n]
        wmat = [x[g][:, :LANES] for g in n]
        u0 = [x[g][:, LANES:] for g in n]
        yield
        d2 = [_mm(a_rb[g], jnp.concatenate([bd(wmat[g]), bd(u0[g])], axis=1)) for g in n]
        rp = [rt[g] + d2[g][:, :LANES] for g in n]
        y0 = [d2[g][:, LANES:] + av[g][c:] for g in n]
        yield
        fold = lambda z: jnp.where(head0, z[:HEAD_DIM], z[HEAD_DIM:])
        gp = [fold(_mm(wmat[g], bt_end[g], _dot_tn)) for g in n]
        npart = [fold(_mm(jnp.concatenate([u0[g], v[g]], axis=0),
                          jnp.concatenate([bt_end[g], kt_end[g]], axis=0), _dot_tn)) for g in n]
        yield
        spans = [[(jnp.where(diag, gp[g] + p["p_c"][g], gp[g]), npart[g]) for g in n]]
        while len(spans[-1]) > 1:
            prev = spans[-1]
            nxt = []
            for i in range(0, len(prev), 2):
                (ma, na), (mb, nb) = prev[i], prev[i + 1]
                prod = _mm(jnp.concatenate([ma, na], axis=0), bd(mb))
                nxt.append((prod[:HEAD_DIM], prod[HEAD_DIM:] + nb))
            spans.append(nxt)
            yield
        return dict(rp=rp, y0=y0, spans=spans, v_slab=p["v_slab"], rk=p["rk"])

    def finish(t, bj, sl):
        spans = t["spans"]
        states = {0: s_ref[bj]}
        top = len(spans) - 1
        m_all, n_all = spans[top][0]
        s_ref[bj] = _mm(states[0], bd(m_all)) + n_all
        yield
        for level in range(top, 0, -1):
            width = 1 << level
            for lo in range(0, group, width):
                ma, na = spans[level - 1][lo >> (level - 1)]
                states[lo + width // 2] = _mm(states[lo], bd(ma)) + na
            yield
        y = jnp.concatenate(
            [_mm(t["rp"][g], bd(states[g]), _dot_nt) + t["y0"][g] for g in range(group)], axis=0)
        yield
        mean = seg_sum(y) * (1.0 / HEAD_DIM)
        yc = y - mean
        var = seg_sum(yc * yc) * (1.0 / HEAD_DIM)
        yield
        yn = yc * lax.rsqrt(var + LNX_EPS) * lnx_w + lnx_b
        bonus = seg_sum(t["rk"]) * t["v_slab"]
        y_ref[bj, sl, :] = (yn + bonus).astype(y_ref.dtype)

    def run(*gens):
        results = [None] * len(gens)
        live = list(range(len(gens)))
        while live:
            for i in list(live):
                try:
                    next(gens[i])
                except StopIteration as stop:
                    results[i] = stop.value
                    live.remove(i)
        return results

    nslabs = zr_ref.shape[1] // rows
    units = [(bj, slice(t * rows, (t + 1) * rows)) for t in range(nslabs) for bj in range(zr_ref.shape[0])]
    prepared, = run(prepare(*units[0]))
    done = None
    for u in range(len(units)):
        gens = [chains(prepared)]
        if u + 1 < len(units):
            gens.append(prepare(*units[u + 1]))
        if done is not None:
            gens.append(finish(done, *units[u - 1]))
        out = run(*gens)
        done = out[0]
        if u + 1 < len(units):
            prepared = out[1]
    run(finish(done, *units[-1]))


def _rwkv(z3d, pp, mu_wa, w2a2):
    b, s, a_shift = z3d.shape
    br_w = (a_shift - 2 * LORA_W) // 3
    npairs = br_w // LANES
    nb = RWKV_BATCH
    zspec = lambda off: pl.BlockSpec((nb, s, LANES), lambda bi, hp: (bi, 0, off + hp))
    return pl.pallas_call(
        functools.partial(_rwkv_body, group=RWKV_GROUP),
        grid=(b // nb, npairs),
        in_specs=[
            zspec(0), zspec(npairs), zspec(2 * npairs),
            pl.BlockSpec((nb, s, LANES), lambda bi, hp: (bi, 0, 3 * npairs)),
            pl.BlockSpec((1, _PP_ROWS, LANES), lambda bi, hp: (hp, 0, 0)),
            pl.BlockSpec((1, LANES), lambda bi, hp: (0, 0)),
            pl.BlockSpec((1, LANES, 2 * LANES), lambda bi, hp: (hp, 0, 0)),
        ],
        out_specs=pl.BlockSpec((nb, s, LANES), lambda bi, hp: (bi, 0, hp)),
        out_shape=jax.ShapeDtypeStruct((b, s, br_w), ACT_DTYPE),
        scratch_shapes=[pltpu.VMEM((nb, HEAD_DIM, LANES), F32), pltpu.VMEM((nb, 8, LANES), F32)],
        compiler_params=pltpu.CompilerParams(
            dimension_semantics=("arbitrary", "arbitrary"), vmem_limit_bytes=VMEM_LIMIT_BYTES),
        name="rwkv7_scan",
    )(z3d, z3d, z3d, z3d, pp, mu_wa, w2a2)


def _alibi_slopes(n):
    def pow2(m):
        start = 2.0 ** (-8.0 / m)
        return [start ** (i + 1) for i in range(m)]
    if math.log2(n).is_integer():
        return pow2(n)
    cl = 2 ** int(math.floor(math.log2(n)))
    return pow2(cl) + pow2(2 * cl)[0::2][: n - cl]


def _diff_attn_body(q_ref, k_ref, v_ref, slope_ref, lamp_ref, sub_ref, o_ref,
                    kb_ref, vt_ref, band_ref, sa_ref, sb_ref, acc_ref, *, tq, lam_init):
    bi = pl.program_id(1)
    tk = ATT_KV_TILE
    band = tq // tk
    nkv = k_ref.shape[1] // tk
    log2e = math.log2(math.e)
    slope2 = slope_ref[0][:, :1] * log2e
    aug_lane = lax.broadcasted_iota(jnp.int32, (tk, LANES), 1)

    for t in range(nkv):
        rows = slice(t * tk, (t + 1) * tk)
        kb_ref[rows, :LANES] = k_ref[0, rows, :].astype(BF16)
        vt_ref[:LANES, rows] = v_ref[0, rows, :].astype(F32).T.astype(BF16)

    @pl.when(bi == 0)
    def _():
        ones_row = lax.broadcasted_iota(jnp.int32, (ATT_V_PAD, tk), 0) == 0
        for t in range(nkv):
            rows = slice(t * tk, (t + 1) * tk)
            pos = (lax.broadcasted_iota(jnp.int32, (tk, LANES), 0) + t * tk).astype(F32)
            pieces = _split(slope2 * pos, ATT_BIAS_PIECES)
            aug = jnp.zeros((tk, LANES), F32)
            for i, piece in enumerate(pieces):
                aug = jnp.where(aug_lane == i, piece.astype(F32), aug)
            kb_ref[rows, LANES:] = aug.astype(BF16)
            vt_ref[LANES:, rows] = jnp.where(ones_row, 1.0, 0.0).astype(BF16)
        key = lax.broadcasted_iota(jnp.int32, (tk, 2 * tq), 0).astype(F32)
        lane2 = lax.broadcasted_iota(jnp.int32, (1, 2 * tq), 1)
        qry = jnp.where(lane2 < tq, lane2, lane2 - tq)
        qlim = ((qry // ATT_CHUNK + 1) * ATT_CHUNK).astype(F32)
        for d in range(band):
            kpos = key + float(d * tk)
            band_ref[d] = jnp.where(
                kpos < qlim, (-2.0 * slope2) * jnp.maximum(kpos - qry.astype(F32), 0.0), MASK_VALUE)

    lane = lax.broadcasted_iota(jnp.int32, (2 * tq, LANES), 1)
    head0 = lane < HEAD_DIM
    first = lax.broadcasted_iota(jnp.int32, (2 * tq, LANES), 0) < tq
    ones_cols = jnp.where(lane < ATT_BIAS_PIECES, 1.0, 0.0)
    lp = lamp_ref[...]
    lam = (jnp.exp(jnp.sum(lp[0:1] * lp[1:2], axis=-1, keepdims=True))
           - jnp.exp(jnp.sum(lp[2:3] * lp[3:4], axis=-1, keepdims=True)) + lam_init)

    def scores(j, qcat, dst_ref):
        dst_ref[...] = _dot_nt(kb_ref[j * tk:(j + 1) * tk, :], qcat)

    def softmax_pv(j, src_ref, m, band_index):
        vt = vt_ref[:, j * tk:(j + 1) * tk]
        s = src_ref[...]
        if band_index is not None:
            s = s + band_ref[band_index]
        if m is None:
            m_new = jnp.max(s, axis=0, keepdims=True)
            acc_ref[...] = _dot(vt, jnp.exp2(s - m_new).astype(BF16))
        else:
            m_new = jnp.maximum(m, jnp.max(s, axis=0, keepdims=True))
            p = jnp.exp2(s - m_new)
            acc_ref[...] = jnp.exp2(m - m_new) * acc_ref[...] + _dot(vt, p.astype(BF16))
        return m_new

    bufs = (sa_ref, sb_ref)
    for n in range(nkv // band):
        qrows = slice(n * tq, (n + 1) * tq)
        q = q_ref[0, qrows, :].astype(F32) * (HEAD_DIM ** -0.5 * log2e)
        q2x = jnp.concatenate([q, q], axis=0)
        qcat = jnp.concatenate([jnp.where(first == head0, q2x, 0.0), ones_cols],
                               axis=1).astype(BF16)
        ntiles = (n + 1) * band
        m = None
        scores(0, qcat, bufs[0])
        for j in range(ntiles):
            if j + 1 < ntiles:
                scores(j + 1, qcat, bufs[(j + 1) % 2])
            m = softmax_pv(j, bufs[j % 2], m, None if j < n * band else j - n * band)
        on = acc_ref[:LANES, :] * (1.0 / acc_ref[LANES:LANES + 1, :])
        o = on[:, :tq] - lam * on[:, tq:]
        ms = jnp.mean(o * o, axis=0, keepdims=True)
        o = o * lax.rsqrt(ms + NORM_EPS) * sub_ref[...] * (1.0 - lam_init)
        o_ref[0, qrows, :] = o.T.astype(o_ref.dtype)


def _diff_attn(q3d, k3d, v3d, slopes, lam_params, subln_col, lam_init, tq):
    b, s, br_w = q3d.shape
    nh = br_w // LANES
    tk = ATT_KV_TILE
    return pl.pallas_call(
        functools.partial(_diff_attn_body, tq=tq, lam_init=lam_init),
        grid=(nh, b),
        in_specs=[
            pl.BlockSpec((1, s, LANES), lambda h, bi: (bi, 0, h)),
            pl.BlockSpec((1, s, LANES), lambda h, bi: (bi, 0, h)),
            pl.BlockSpec((1, s, LANES), lambda h, bi: (bi, 0, h)),
            pl.BlockSpec((1, 1, LANES), lambda h, bi: (h, 0, 0)),
            pl.BlockSpec((4, HEAD_DIM), lambda h, bi: (0, 0)),
            pl.BlockSpec((LANES, 1), lambda h, bi: (0, 0)),
        ],
        out_specs=pl.BlockSpec((1, s, LANES), lambda h, bi: (bi, 0, h)),
        out_shape=jax.ShapeDtypeStruct((b, s, br_w), ACT_DTYPE),
        scratch_shapes=[pltpu.VMEM((s, 2 * LANES), BF16), pltpu.VMEM((LANES + ATT_V_PAD, s), BF16),
                        pltpu.VMEM((tq // tk, tk, 2 * tq), F32),
                        pltpu.VMEM((tk, 2 * tq), F32), pltpu.VMEM((tk, 2 * tq), F32),
                        pltpu.VMEM((LANES + ATT_V_PAD, 2 * tq), F32)],
        compiler_params=pltpu.CompilerParams(
            dimension_semantics=("arbitrary", "arbitrary"),
            vmem_limit_bytes=VMEM_LIMIT_BYTES),
        name="diff_attn",
    )(q3d, k3d, v3d, slopes, lam_params, subln_col)


def _out_body(ybr_ref, gate_ref, qm_ref, gm_ref, x_ref, km_ref, vm_ref, w_ref, pn_ref, o_ref, *, br_w):
    tm = x_ref.shape[0]
    ml = km_ref.shape[1] // MEM_HEADS
    subs = [slice(r * OUT_SUB_ROWS, (r + 1) * OUT_SUB_ROWS) for r in range(tm // OUT_SUB_ROWS)]
    qscale = HEAD_DIM ** -0.5 * math.log2(math.e)
    s = [_dot_nt((qm_ref[rs, :].astype(F32) * qscale).astype(BF16), km_ref[0]) for rs in subs]
    y_br = (ybr_ref[...].astype(F32) * _silu(gate_ref[...].astype(F32))).astype(BF16)
    pcat = []
    for sr in s:
        ps = []
        for hd in range(MEM_HEADS):
            sh = sr[:, hd * ml:(hd + 1) * ml]
            p = jnp.exp2(sh - jnp.max(sh, axis=-1, keepdims=True))
            ps.append((p * (1.0 / jnp.sum(p, axis=-1, keepdims=True))).astype(BF16))
        pcat.append(jnp.concatenate(ps, axis=1))
    y_mem = [_dot(pc, vm_ref[0]) for pc in pcat]
    y_mem = (jnp.concatenate(y_mem, axis=0) * _silu(gm_ref[...].astype(F32))).astype(BF16)
    y = _dot(y_br, w_ref[:br_w, :]) + _dot(y_mem, w_ref[br_w:, :])
    o_ref[...] = x_ref[...] + _rms(y, pn_ref[...])


def _out_proj(ybr, gate, qm, gm, x2d, k_mem, v_mem, w_bf16, post_g, seq, tm):
    m, d = x2d.shape
    br_w = ybr.shape[1]
    ml = k_mem.shape[1]
    per_b = seq // tm
    row = lambda w: pl.BlockSpec((tm, w), lambda i: (i, 0))
    return pl.pallas_call(
        functools.partial(_out_body, br_w=br_w),
        grid=(m // tm,),
        in_specs=[
            row(br_w), row(br_w), row(MEM_W), row(MEM_W), row(d),
            pl.BlockSpec((1, ml, MEM_W), lambda i: (i // per_b, 0, 0)),
            pl.BlockSpec((1, ml, MEM_W), lambda i: (i // per_b, 0, 0)),
            pl.BlockSpec((d, d), lambda i: (0, 0)),
            pl.BlockSpec((1, d), lambda i: (0, 0)),
        ],
        out_specs=row(d),
        out_shape=jax.ShapeDtypeStruct((m, d), F32),
        compiler_params=pltpu.CompilerParams(
            dimension_semantics=("arbitrary",), vmem_limit_bytes=VMEM_LIMIT_BYTES),
        name="out_proj",
    )(ybr, gate, qm, gm, x2d, k_mem, v_mem, w_bf16, post_g)


def _rwkv_params(mu, w0, w2, a0, a2, k_k, k_a, r_k, lnx_w, lnx_b, br_w):
    npairs = br_w // LANES
    rows = [mu[:br_w], mu[br_w:2 * br_w], mu[2 * br_w:3 * br_w], w0, a0, k_k, k_a,
            r_k.reshape(-1), lnx_w, lnx_b]
    pp = jnp.stack([r.reshape(npairs, LANES) for r in rows], axis=1)
    pp = jnp.pad(pp, ((0, 0), (0, _PP_ROWS - len(rows)), (0, 0)))
    mu_wa = mu[3 * br_w:].reshape(1, LANES)
    w2p = w2.reshape(LORA_W, npairs, LANES).transpose(1, 0, 2)
    a2p = a2.reshape(LORA_W, npairs, LANES).transpose(1, 0, 2)
    zeros = jnp.zeros_like(w2p)
    w2a2 = jnp.concatenate(
        [jnp.concatenate([w2p, zeros], axis=2), jnp.concatenate([zeros, a2p], axis=2)], axis=1)
    return pp, mu_wa, w2a2


def kernel(x, mem, pre_norm, post_norm, w_out, mem_norm, w_mem_kv, a_w_in, a_shift_mu, a_w0, a_w2,
           a_a0, a_a2, a_k_k, a_k_a, a_r_k, a_lnx_w, a_lnx_b, kv_norm, w_kv, b_w_in, b_lam_q1,
           b_lam_k1, b_lam_q2, b_lam_k2, b_subln):
    bsz, seq, d = x.shape
    depth = pre_norm.shape[0]
    n_a = a_w_in.shape[0]
    br_w = d - MEM_W
    a_shift = 3 * br_w + 2 * LORA_W
    m = bsz * seq
    tm = 1024
    x2d = x.reshape(m, d)
    slopes = jnp.asarray(
        np.repeat(np.array(_alibi_slopes(br_w // LANES), np.float32)[:, None, None], LANES, axis=2))

    for l in range(depth):
        k_mem, v_mem = _mem_kv(mem, mem_norm[l].reshape(1, d), w_mem_kv[l].astype(BF16))
        if l < n_a:
            i = l
            segs = [(0, 0, a_shift), (0, a_shift, a_shift + br_w),
                    (0, a_shift + br_w, a_shift + br_w + MEM_W),
                    (0, a_shift + br_w + MEM_W, a_shift + br_w + 2 * MEM_W)]
            z, gate, q_mem, g_mem = _norm_proj(
                x2d, pre_norm[l].reshape(1, d), a_w_in[i].astype(BF16), segs, tm)
            pp, mu_wa, w2a2 = _rwkv_params(
                a_shift_mu[i], a_w0[i], a_w2[i], a_a0[i], a_a2[i], a_k_k[i], a_k_a[i], a_r_k[i],
                a_lnx_w[i], a_lnx_b[i], br_w)
            y_br = _rwkv(z.reshape(bsz, seq, a_shift), pp, mu_wa, w2a2).reshape(m, br_w)
        else:
            i = l - n_a
            if l == n_a:
                w_cat = jnp.concatenate([b_w_in[i], w_kv], axis=1).astype(BF16)
                gains = jnp.stack([pre_norm[l], kv_norm], axis=0)
                segs = [(0, 0, br_w), (0, br_w, 2 * br_w), (0, 2 * br_w, 2 * br_w + MEM_W),
                        (0, 2 * br_w + MEM_W, 2 * br_w + 2 * MEM_W),
                        (1, 2 * br_w + 2 * MEM_W, 3 * br_w + 2 * MEM_W),
                        (1, 3 * br_w + 2 * MEM_W, 4 * br_w + 2 * MEM_W)]
                q, gate, q_mem, g_mem, k_sh, v_sh = _norm_proj(x2d, gains, w_cat, segs, tm)
                k_sh = k_sh.reshape(bsz, seq, br_w)
                v_sh = v_sh.reshape(bsz, seq, br_w)
            else:
                segs = [(0, 0, br_w), (0, br_w, 2 * br_w), (0, 2 * br_w, 2 * br_w + MEM_W),
                        (0, 2 * br_w + MEM_W, 2 * br_w + 2 * MEM_W)]
                q, gate, q_mem, g_mem = _norm_proj(
                    x2d, pre_norm[l].reshape(1, d), b_w_in[i].astype(BF16), segs, tm)
            lam_init = 0.8 - 0.6 * math.exp(-0.3 * l)
            lam_params = jnp.stack([b_lam_q1[i], b_lam_k1[i], b_lam_q2[i], b_lam_k2[i]], axis=0)
            y_br = _diff_attn(q.reshape(bsz, seq, br_w), k_sh, v_sh, slopes, lam_params,
                              b_subln[i].reshape(LANES, 1), lam_init, ATT_Q_TILE).reshape(m, br_w)
        x2d = _out_proj(y_br, gate, q_mem, g_mem, x2d, k_mem, v_mem, w_out[l].astype(BF16),
                        post_norm[l].reshape(1, d), seq, OUT_ROWS)
    return x2d.reshape(bsz, seq, d)
```

```python
import functools
import math

import numpy as np
import jax
import jax.numpy as jnp
from jax import lax
from jax.experimental import pallas as pl
from jax.experimental.pallas import tpu as pltpu

F32 = jnp.float32
BF16 = jnp.bfloat16
ACT_DTYPE = BF16

HEAD_DIM = 64
LANES = 128
MEM_HEADS = 4
MEM_W = MEM_HEADS * HEAD_DIM
LORA_W = 64
ATT_CHUNK = 64
ATT_Q_TILE = 512
ATT_BATCH = 2
ATT_BIAS_PIECES = 3
ATT_V_PAD = 16
RWKV_CHUNK = 64
RWKV_GROUP = 16
RWKV_CUMSUM_ROWS = 256
RWKV_BATCH = 2
OUT_ROWS = 1024
OUT_SUB_ROWS = 256
NORM_EPS = 1e-6
LNX_EPS = 64e-5
MASK_VALUE = -1e30
VMEM_LIMIT_BYTES = 56 * 1024 * 1024


def _dot(a, b):
    return jnp.dot(a, b, preferred_element_type=F32)


def _dot_nt(a, b):
    return lax.dot_general(a, b, (((1,), (1,)), ((), ())), preferred_element_type=F32)


def _dot_tn(a, b):
    return lax.dot_general(a, b, (((0,), (0,)), ((), ())), preferred_element_type=F32)


def _split(x, pieces):
    out = []
    rem = x
    for i in range(pieces):
        p = rem.astype(BF16)
        out.append(p)
        if i + 1 < pieces:
            rem = rem - p.astype(F32)
    return out


def _mm(a, b, fn=_dot, pa=1, pb=1):
    aps = _split(a, pa)
    bps = _split(b, pb)
    order = max(pa, pb)
    acc = None
    for i, ap in enumerate(aps):
        for j, bp in enumerate(bps):
            if i + j < order:
                t = fn(ap, bp)
                acc = t if acc is None else acc + t
    return acc


def _rms(x, g):
    ms = jnp.mean(x * x, axis=-1, keepdims=True)
    return (x * lax.rsqrt(ms + NORM_EPS)) * g


def _silu(x):
    h = 0.5 * x
    return h + h * jnp.tanh(h)


def _norm_proj_body(x_ref, g_ref, w_ref, *o_refs, segments):
    x = x_ref[...]
    ms = jnp.mean(x * x, axis=-1, keepdims=True)
    xn = x * lax.rsqrt(ms + NORM_EPS)
    hs = {}
    for o_ref, (gi, lo, hi) in zip(o_refs, segments):
        if gi not in hs:
            hs[gi] = (xn * g_ref[gi:gi + 1, :]).astype(BF16)
        o_ref[...] = _dot(hs[gi], w_ref[:, lo:hi]).astype(o_ref.dtype)


def _norm_proj(x2d, gains, w_bf16, segments, tm):
    m, d = x2d.shape
    n = w_bf16.shape[1]
    ng = gains.shape[0]
    out_shape = [jax.ShapeDtypeStruct((m, hi - lo), ACT_DTYPE) for _, lo, hi in segments]
    out_specs = [pl.BlockSpec((tm, hi - lo), lambda i: (i, 0)) for _, lo, hi in segments]
    return pl.pallas_call(
        functools.partial(_norm_proj_body, segments=tuple(segments)),
        grid=(m // tm,),
        in_specs=[
            pl.BlockSpec((tm, d), lambda i: (i, 0)),
            pl.BlockSpec((ng, d), lambda i: (0, 0)),
            pl.BlockSpec((d, n), lambda i: (0, 0)),
        ],
        out_specs=out_specs,
        out_shape=out_shape,
        compiler_params=pltpu.CompilerParams(
            dimension_semantics=("arbitrary",), vmem_limit_bytes=VMEM_LIMIT_BYTES),
        name="norm_proj",
    )(x2d, gains, w_bf16)


def _mem_kv_body(mem_ref, g_ref, w_ref, k_ref, v_ref):
    ml = mem_ref.shape[1]
    h = _rms(mem_ref[0], g_ref[...]).astype(BF16)
    kv = _dot(h, w_ref[...])
    lane = lax.broadcasted_iota(jnp.int32, (ml, MEM_W), 1)
    for hd in range(MEM_HEADS):
        in_head = (lane >= hd * HEAD_DIM) & (lane < (hd + 1) * HEAD_DIM)
        k_ref[0, hd * ml:(hd + 1) * ml, :] = jnp.where(in_head, kv[:, :MEM_W], 0.0).astype(k_ref.dtype)
        v_ref[0, hd * ml:(hd + 1) * ml, :] = jnp.where(in_head, kv[:, MEM_W:], 0.0).astype(v_ref.dtype)


def _mem_kv(mem, g, w_bf16):
    b, ml, d = mem.shape
    return pl.pallas_call(
        _mem_kv_body,
        grid=(b,),
        in_specs=[
            pl.BlockSpec((1, ml, d), lambda i: (i, 0, 0)),
            pl.BlockSpec((1, d), lambda i: (0, 0)),
            pl.BlockSpec((d, 2 * MEM_W), lambda i: (0, 0)),
        ],
        out_specs=[pl.BlockSpec((1, MEM_HEADS * ml, MEM_W), lambda i: (i, 0, 0))] * 2,
        out_shape=[jax.ShapeDtypeStruct((b, MEM_HEADS * ml, MEM_W), ACT_DTYPE)] * 2,
        compiler_params=pltpu.CompilerParams(
            dimension_semantics=("arbitrary",), vmem_limit_bytes=VMEM_LIMIT_BYTES),
        name="mem_kv",
    )(mem, g, w_bf16)


_PP_MU_R, _PP_MU_K, _PP_MU_V, _PP_W0, _PP_A0, _PP_KK, _PP_KA, _PP_RK, _PP_LNW, _PP_LNB = range(10)
_PP_ROWS = 16


def _rwkv_body(zr_ref, zk_ref, zv_ref, zwa_ref, pp_ref, muwa_ref, w2a2_ref, y_ref,
               s_ref, prev_ref, *, group):
    c = RWKV_CHUNK
    rows = group * c
    lane = lax.broadcasted_iota(jnp.int32, (c, LANES), 1)
    row = lax.broadcasted_iota(jnp.int32, (c, LANES), 0)
    head0 = lane < HEAD_DIM
    scol = jnp.where(head0, lane, lane - HEAD_DIM)
    strict = scol < row
    incl = scol <= row
    diag = scol == row
    r2 = lax.broadcasted_iota(jnp.int32, (LANES, LANES), 0)
    c2 = lax.broadcasted_iota(jnp.int32, (LANES, LANES), 1)
    blockmask = (r2 < HEAD_DIM) == (c2 < HEAD_DIM)
    blockones = jnp.where(blockmask, 1.0, 0.0).astype(BF16)
    cum_rows = min(rows, RWKV_CUMSUM_ROWS)
    tr = lax.broadcasted_iota(jnp.int32, (cum_rows, cum_rows), 0)
    tc = lax.broadcasted_iota(jnp.int32, (cum_rows, cum_rows), 1)
    tril_ones = jnp.where((tc <= tr) & (tc // c == tr // c), 1.0, 0.0).astype(BF16)
    slab_lane = lax.broadcasted_iota(jnp.int32, (rows, LANES), 1)
    slab_head0 = slab_lane < HEAD_DIM
    slab_first = lax.broadcasted_iota(jnp.int32, (rows, LANES), 0) == 0

    def bd(x):
        return jnp.concatenate([jnp.where(head0, x, 0.0), jnp.where(head0, 0.0, x)], axis=0)

    def seg_sum(x):
        return _dot(x.astype(BF16), blockones)

    pp = pp_ref[0]
    prow = lambda i: pp[i:i + 1, :]
    mu_r, mu_k, mu_v = prow(_PP_MU_R), prow(_PP_MU_K), prow(_PP_MU_V)
    w0, a0 = prow(_PP_W0), prow(_PP_A0)
    k_k, k_a, r_k = prow(_PP_KK), prow(_PP_KA), prow(_PP_RK)
    lnx_w, lnx_b = prow(_PP_LNW), prow(_PP_LNB)
    mu_wa = muwa_ref[...]
    w2a2 = w2a2_ref[0]

    s_ref[...] = jnp.zeros_like(s_ref)
    prev_ref[...] = jnp.zeros_like(prev_ref)

    def shift_mix(ref, bj, slot, sl, mu):
        z = ref[bj, sl, :].astype(F32)
        zp = pltpu.roll(z, 1, axis=0)
        zp = jnp.where(slab_first, prev_ref[bj, slot:slot + 1, :], zp)
        prev_ref[bj, slot:slot + 1, :] = z[rows - 1:rows, :]
        return z + (zp - z) * mu


    def prepare(bj, sl):
        r = shift_mix(zr_ref, bj, 0, sl, mu_r)
        k = shift_mix(zk_ref, bj, 1, sl, mu_k)
        yield
        v = shift_mix(zv_ref, bj, 2, sl, mu_v)
        wa = shift_mix(zwa_ref, bj, 3, sl, mu_wa)
        lora = _mm(jnp.where(slab_head0, jnp.tanh(wa), wa), w2a2)
        yield
        wlog = w0 + lora[:, :LANES]
        nw = -wlog
        w = -(jnp.maximum(nw, 0.0) + jnp.log(1.0 + jnp.exp(-jnp.abs(nw)))) - 0.5
        logw = -jnp.exp(w)
        a = 1.0 / (1.0 + jnp.exp(-(a0 + lora[:, LANES:])))
        kk = k * k_k
        kk = kk * lax.rsqrt(jnp.maximum(seg_sum(kk * kk), 1e-12))
        yield
        kmod = k * (1.0 + (a - 1.0) * k_a)
        alpha = -kk
        beta = kk * a
        l2 = jnp.concatenate(_split(logw, 2), axis=1)
        cum2 = jnp.concatenate(
            [_dot(tril_ones, l2[i:i + cum_rows]) for i in range(0, rows, cum_rows)], axis=0)
        cum = cum2[:, :LANES] + cum2[:, LANES:]
        yield
        e_neg = jnp.exp(-cum)
        at = alpha * jnp.exp(cum - logw)
        rt = r * jnp.exp(cum)
        bt = beta * e_neg
        kt = kmod * e_neg
        yield
        chunks = [slice(g * c, (g + 1) * c) for g in range(group)]
        cum_c = [cum[cs.stop - 1:cs.stop, :] for cs in chunks]
        e_end = [jnp.exp(cum_c[g] - cum[cs]) for g, cs in enumerate(chunks)]
        per = lambda x: [x[cs] for cs in chunks]
        return dict(
            at=per(at), rt=per(rt), bt=per(bt), kt=per(kt), v=per(v),
            bt_end=[beta[cs] * e_end[g] for g, cs in enumerate(chunks)],
            kt_end=[kmod[cs] * e_end[g] for g, cs in enumerate(chunks)],
            p_c=[jnp.exp(x) for x in cum_c], v_slab=v, rk=r * kmod * r_k)

    def chains(p):
        at, rt, bt, kt, bt_end, kt_end, v = (p[x] for x in ("at", "rt", "bt", "kt", "bt_end", "kt_end", "v"))
        n = range(group)
        lhs = [jnp.concatenate([at[g], rt[g]], axis=0) for g in n]
        xbk = [_mm(lhs[g], jnp.concatenate([bd(bt[g]), bd(kt[g])], axis=0), _dot_nt) for g in n]
        yield
        a_rb = [jnp.where(incl, xbk[g][c:, :LANES], 0.0) for g in n]
        akrk = [jnp.concatenate([jnp.where(strict, xbk[g][:c, LANES:], 0.0),
                                 jnp.where(incl, xbk[g][c:, LANES:], 0.0)], axis=0) for g in n]
        av = [_mm(akrk[g], bd(v[g])) for g in n]
        yield
        apow = [jnp.where(strict, xbk[g][:c, :LANES], 0.0) for g in n]
        tinv = [jnp.where(diag, 1.0, 0.0) + apow[g] for g in n]
        nfac = int(math.log2(c))
        for i in range(1, nfac):
            rhs = [[bd(apow[g])] + ([bd(tinv[g])] if i > 1 else []) for g in n]
            d = [_mm(apow[g], jnp.concatenate(rhs[g], axis=1)) for g in n]
            if i > 1:
                tinv = [tinv[g] + d[g][:, LANES:] for g in n]
            apow = [d[g][:, :LANES] for g in n]
            yield
        tinv = [tinv[g] + _mm(apow[g], bd(tinv[g])) for g in n]
        yield
        x = [_mm(tinv[g], jnp.concatenate([bd(at[g]), bd(av[g][:c])], axis=1)) for g in n]
        wmat = [x[g][:, :LANES] for g in n]
        u0 = [x[g][:, LANES:] for g in n]
        yield
        d2 = [_mm(a_rb[g], jnp.concatenate([bd(wmat[g]), bd(u0[g])], axis=1)) for g in n]
        rp = [rt[g] + d2[g][:, :LANES] for g in n]
        y0 = [d2[g][:, LANES:] + av[g][c:] for g in n]
        yield
        fold = lambda z: jnp.where(head0, z[:HEAD_DIM], z[HEAD_DIM:])
        gp = [fold(_mm(wmat[g], bt_end[g], _dot_tn)) for g in n]
        npart = [fold(_mm(jnp.concatenate([u0[g], v[g]], axis=0),
                          jnp.concatenate([bt_end[g], kt_end[g]], axis=0), _dot_tn)) for g in n]
        yield
        spans = [[(jnp.where(diag, gp[g] + p["p_c"][g], gp[g]), npart[g]) for g in n]]
        while len(spans[-1]) > 1:
            prev = spans[-1]
            nxt = []
            for i in range(0, len(prev), 2):
                (ma, na), (mb, nb) = prev[i], prev[i + 1]
                prod = _mm(jnp.concatenate([ma, na], axis=0), bd(mb))
                nxt.append((prod[:HEAD_DIM], prod[HEAD_DIM:] + nb))
            spans.append(nxt)
            yield
        return dict(rp=rp, y0=y0, spans=spans, v_slab=p["v_slab"], rk=p["rk"])

    def finish(t, bj, sl):
        spans = t["spans"]
        states = {0: s_ref[bj]}
        top = len(spans) - 1
        m_all, n_all = spans[top][0]
        s_ref[bj] = _mm(states[0], bd(m_all)) + n_all
        yield
        for level in range(top, 0, -1):
            width = 1 << level
            for lo in range(0, group, width):
                ma, na = spans[level - 1][lo >> (level - 1)]
                states[lo + width // 2] = _mm(states[lo], bd(ma)) + na
            yield
        y = jnp.concatenate(
            [_mm(t["rp"][g], bd(states[g]), _dot_nt) + t["y0"][g] for g in range(group)], axis=0)
        yield
        mean = seg_sum(y) * (1.0 / HEAD_DIM)
        yc = y - mean
        var = seg_sum(yc * yc) * (1.0 / HEAD_DIM)
        yield
        yn = yc * lax.rsqrt(var + LNX_EPS) * lnx_w + lnx_b
        bonus = seg_sum(t["rk"]) * t["v_slab"]
        y_ref[bj, sl, :] = (yn + bonus).astype(y_ref.dtype)

    def run(*gens):
        results = [None] * len(gens)
        live = list(range(len(gens)))
        while live:
            for i in list(live):
                try:
                    next(gens[i])
                except StopIteration as stop:
                    results[i] = stop.value
                    live.remove(i)
        return results

    nslabs = zr_ref.shape[1] // rows
    units = [(bj, slice(t * rows, (t + 1) * rows)) for t in range(nslabs) for bj in range(zr_ref.shape[0])]
    prepared, = run(prepare(*units[0]))
    done = None
    for u in range(len(units)):
        gens = [chains(prepared)]
        if u + 1 < len(units):
            gens.append(prepare(*units[u + 1]))
        if done is not None:
            gens.append(finish(done, *units[u - 1]))
        out = run(*gens)
        done = out[0]
        if u + 1 < len(units):
            prepared = out[1]
    run(finish(done, *units[-1]))


def _rwkv(z3d, pp, mu_wa, w2a2):
    b, s, a_shift = z3d.shape
    br_w = (a_shift - 2 * LORA_W) // 3
    npairs = br_w // LANES
    nb = RWKV_BATCH
    zspec = lambda off: pl.BlockSpec((nb, s, LANES), lambda bi, hp: (bi, 0, off + hp))
    return pl.pallas_call(
        functools.partial(_rwkv_body, group=RWKV_GROUP),
        grid=(b // nb, npairs),
        in_specs=[
            zspec(0), zspec(npairs), zspec(2 * npairs),
            pl.BlockSpec((nb, s, LANES), lambda bi, hp: (bi, 0, 3 * npairs)),
            pl.BlockSpec((1, _PP_ROWS, LANES), lambda bi, hp: (hp, 0, 0)),
            pl.BlockSpec((1, LANES), lambda bi, hp: (0, 0)),
            pl.BlockSpec((1, LANES, 2 * LANES), lambda bi, hp: (hp, 0, 0)),
        ],
        out_specs=pl.BlockSpec((nb, s, LANES), lambda bi, hp: (bi, 0, hp)),
        out_shape=jax.ShapeDtypeStruct((b, s, br_w), ACT_DTYPE),
        scratch_shapes=[pltpu.VMEM((nb, HEAD_DIM, LANES), F32), pltpu.VMEM((nb, 8, LANES), F32)],
        compiler_params=pltpu.CompilerParams(
            dimension_semantics=("arbitrary", "arbitrary"), vmem_limit_bytes=VMEM_LIMIT_BYTES),
        name="rwkv7_scan",
    )(z3d, z3d, z3d, z3d, pp, mu_wa, w2a2)


def _alibi_slopes(n):
    def pow2(m):
        start = 2.0 ** (-8.0 / m)
        return [start ** (i + 1) for i in range(m)]
    if math.log2(n).is_integer():
        return pow2(n)
    cl = 2 ** int(math.floor(math.log2(n)))
    return pow2(cl) + pow2(2 * cl)[0::2][: n - cl]


def _diff_attn_body(q_ref, k_ref, v_ref, slope_ref, lamp_ref, sub_ref, o_ref,
                    kb_ref, vt_ref, band_ref, sa_ref, sb_ref, acc_ref, *, tq, lam_init):
    bi = pl.program_id(1)
    nb = q_ref.shape[0]
    tk = tq
    half = tq // 2
    nq = k_ref.shape[1] // tq
    log2e = math.log2(math.e)
    slope2 = slope_ref[0][:, :1] * log2e
    aug_lane = lax.broadcasted_iota(jnp.int32, (tk, LANES), 1)

    for bj in range(nb):
        for t in range(nq):
            rows = slice(t * tk, (t + 1) * tk)
            kb_ref[bj, rows, :LANES] = k_ref[bj, rows, :].astype(BF16)
            vt_ref[bj, :LANES, rows] = v_ref[bj, rows, :].astype(F32).T.astype(BF16)

    @pl.when(bi == 0)
    def _():
        ones_row = lax.broadcasted_iota(jnp.int32, (ATT_V_PAD, tk), 0) == 0
        for t in range(nq):
            rows = slice(t * tk, (t + 1) * tk)
            pos = (lax.broadcasted_iota(jnp.int32, (tk, LANES), 0) + t * tk).astype(F32)
            pieces = _split(slope2 * pos, ATT_BIAS_PIECES)
            aug = jnp.zeros((tk, LANES), F32)
            for i, piece in enumerate(pieces):
                aug = jnp.where(aug_lane == i, piece.astype(F32), aug)
            for bj in range(nb):
                kb_ref[bj, rows, LANES:] = aug.astype(BF16)
                vt_ref[bj, LANES:, rows] = jnp.where(ones_row, 1.0, 0.0).astype(BF16)
        kpos = lax.broadcasted_iota(jnp.int32, (tk, 2 * tq), 0).astype(F32)
        lane2 = lax.broadcasted_iota(jnp.int32, (1, 2 * tq), 1)
        qry = jnp.where(lane2 < tq, lane2, lane2 - tq)
        qlim = ((qry // ATT_CHUNK + 1) * ATT_CHUNK).astype(F32)
        band_ref[...] = jnp.where(
            kpos < qlim, (-2.0 * slope2) * jnp.maximum(kpos - qry.astype(F32), 0.0), MASK_VALUE)

    lane = lax.broadcasted_iota(jnp.int32, (2 * tq, LANES), 1)
    head0 = lane < HEAD_DIM
    first = lax.broadcasted_iota(jnp.int32, (2 * tq, LANES), 0) < tq
    ones_cols = jnp.where(lane < ATT_BIAS_PIECES, 1.0, 0.0)
    lp = lamp_ref[...]
    lam = (jnp.exp(jnp.sum(lp[0:1] * lp[1:2], axis=-1, keepdims=True))
           - jnp.exp(jnp.sum(lp[2:3] * lp[3:4], axis=-1, keepdims=True)) + lam_init)
    late = lambda x: jnp.concatenate([x[..., half:tq], x[..., tq + half:]], axis=-1)

    def update(p_rows, vt, m, m_new, cols):
        pv = _dot(vt, p_rows.astype(BF16))
        off = 0
        for cs in cols:
            w = cs.stop - cs.start
            if m is None:
                acc_ref[:, cs] = pv[:, off:off + w]
            else:
                acc_ref[:, cs] = jnp.exp2(m - m_new)[:, off:off + w] * acc_ref[:, cs] + pv[:, off:off + w]
            off += w

    def tiles_of(n):
        past = [(slice(j * tk, (j + 1) * tk), None, None) for j in range(n)]
        own0 = (slice(n * tq, n * tq + half), None, slice(0, half))
        own1 = (slice(n * tq + half, (n + 1) * tq), "late", slice(half, tq))
        return past + [own0, own1]

    all_cols = [slice(0, 2 * tq)]
    late_cols = [slice(half, tq), slice(tq + half, 2 * tq)]
    bufs = (sa_ref, sb_ref)
    for bj in range(nb):
        for n in range(nq):
            qrows = slice(n * tq, (n + 1) * tq)
            q = q_ref[bj, qrows, :].astype(F32) * (HEAD_DIM ** -0.5 * log2e)
            q2x = jnp.concatenate([q, q], axis=0)
            qcat = jnp.concatenate([jnp.where(first == head0, q2x, 0.0), ones_cols],
                                   axis=1).astype(BF16)
            qlate = jnp.concatenate([qcat[half:tq], qcat[tq + half:]], axis=0)
            tiles = tiles_of(n)

            def scores(i, dst_ref):
                keys, which, _ = tiles[i]
                nk = keys.stop - keys.start
                if which is None:
                    dst_ref[:nk, :] = _dot_nt(kb_ref[bj, keys, :], qcat)
                else:
                    dst_ref[:nk, :tq] = _dot_nt(kb_ref[bj, keys, :], qlate)

            m = None
            scores(0, bufs[0])
            for i, (keys, which, brows) in enumerate(tiles):
                if i + 1 < len(tiles):
                    scores(i + 1, bufs[(i + 1) % 2])
                nk = keys.stop - keys.start
                vt = vt_ref[bj, :, keys]
                if which is None:
                    s = bufs[i % 2][:nk, :]
                    if brows is not None:
                        s = s + band_ref[brows, :]
                    m_new = jnp.max(s, axis=0, keepdims=True)
                    if m is not None:
                        m_new = jnp.maximum(m, m_new)
                    update(jnp.exp2(s - m_new), vt, m, m_new, all_cols)
                    m = m_new
                else:
                    s = bufs[i % 2][:nk, :tq] + late(band_ref[brows, :])
                    ml = late(m)
                    m_new = jnp.maximum(ml, jnp.max(s, axis=0, keepdims=True))
                    update(jnp.exp2(s - m_new), vt, ml, m_new, late_cols)
            on = acc_ref[:LANES, :] * (1.0 / acc_ref[LANES:LANES + 1, :])
            o = on[:, :tq] - lam * on[:, tq:]
            ms = jnp.mean(o * o, axis=0, keepdims=True)
            o = o * lax.rsqrt(ms + NORM_EPS) * sub_ref[...] * (1.0 - lam_init)
            o_ref[bj, qrows, :] = o.T.astype(o_ref.dtype)


def _diff_attn(q3d, k3d, v3d, slopes, lam_params, subln_col, lam_init, tq):
    b, s, br_w = q3d.shape
    nh = br_w // LANES
    nb = ATT_BATCH
    seq_spec = pl.BlockSpec((nb, s, LANES), lambda h, bi: (bi, 0, h))
    return pl.pallas_call(
        functools.partial(_diff_attn_body, tq=tq, lam_init=lam_init),
        grid=(nh, b // nb),
        in_specs=[
            seq_spec, seq_spec, seq_spec,
            pl.BlockSpec((1, 1, LANES), lambda h, bi: (h, 0, 0)),
            pl.BlockSpec((4, HEAD_DIM), lambda h, bi: (0, 0)),
            pl.BlockSpec((LANES, 1), lambda h, bi: (0, 0)),
        ],
        out_specs=seq_spec,
        out_shape=jax.ShapeDtypeStruct((b, s, br_w), ACT_DTYPE),
        scratch_shapes=[pltpu.VMEM((nb, s, 2 * LANES), BF16), pltpu.VMEM((nb, LANES + ATT_V_PAD, s), BF16),
                        pltpu.VMEM((tq, 2 * tq), F32),
                        pltpu.VMEM((tq, 2 * tq), F32), pltpu.VMEM((tq, 2 * tq), F32),
                        pltpu.VMEM((LANES + ATT_V_PAD, 2 * tq), F32)],
        compiler_params=pltpu.CompilerParams(
            dimension_semantics=("arbitrary", "arbitrary"),
            vmem_limit_bytes=VMEM_LIMIT_BYTES),
        name="diff_attn",
    )(q3d, k3d, v3d, slopes, lam_params, subln_col)


def _out_body(ybr_ref, gate_ref, qm_ref, gm_ref, x_ref, km_ref, vm_ref, w_ref, pn_ref, o_ref, *, br_w):
    tm = x_ref.shape[0]
    ml = km_ref.shape[1] // MEM_HEADS
    subs = [slice(r * OUT_SUB_ROWS, (r + 1) * OUT_SUB_ROWS) for r in range(tm // OUT_SUB_ROWS)]
    qscale = HEAD_DIM ** -0.5 * math.log2(math.e)
    s = [_dot_nt((qm_ref[rs, :].astype(F32) * qscale).astype(BF16), km_ref[0]) for rs in subs]
    y_br = (ybr_ref[...].astype(F32) * _silu(gate_ref[...].astype(F32))).astype(BF16)
    pcat = []
    for sr in s:
        ps = []
        for hd in range(MEM_HEADS):
            sh = sr[:, hd * ml:(hd + 1) * ml]
            p = jnp.exp2(sh - jnp.max(sh, axis=-1, keepdims=True))
            ps.append((p * (1.0 / jnp.sum(p, axis=-1, keepdims=True))).astype(BF16))
        pcat.append(jnp.concatenate(ps, axis=1))
    y_mem = [_dot(pc, vm_ref[0]) for pc in pcat]
    y_mem = (jnp.concatenate(y_mem, axis=0) * _silu(gm_ref[...].astype(F32))).astype(BF16)
    y = _dot(y_br, w_ref[:br_w, :]) + _dot(y_mem, w_ref[br_w:, :])
    o_ref[...] = x_ref[...] + _rms(y, pn_ref[...])


def _out_proj(ybr, gate, qm, gm, x2d, k_mem, v_mem, w_bf16, post_g, seq, tm):
    m, d = x2d.shape
    br_w = ybr.shape[1]
    ml = k_mem.shape[1]
    per_b = seq // tm
    row = lambda w: pl.BlockSpec((tm, w), lambda i: (i, 0))
    return pl.pallas_call(
        functools.partial(_out_body, br_w=br_w),
        grid=(m // tm,),
        in_specs=[
            row(br_w), row(br_w), row(MEM_W), row(MEM_W), row(d),
            pl.BlockSpec((1, ml, MEM_W), lambda i: (i // per_b, 0, 0)),
            pl.BlockSpec((1, ml, MEM_W), lambda i: (i // per_b, 0, 0)),
            pl.BlockSpec((d, d), lambda i: (0, 0)),
            pl.BlockSpec((1, d), lambda i: (0, 0)),
        ],
        out_specs=row(d),
        out_shape=jax.ShapeDtypeStruct((m, d), F32),
        compiler_params=pltpu.CompilerParams(
            dimension_semantics=("arbitrary",), vmem_limit_bytes=VMEM_LIMIT_BYTES),
        name="out_proj",
    )(ybr, gate, qm, gm, x2d, k_mem, v_mem, w_bf16, post_g)


def _rwkv_params(mu, w0, w2, a0, a2, k_k, k_a, r_k, lnx_w, lnx_b, br_w):
    npairs = br_w // LANES
    rows = [mu[:br_w], mu[br_w:2 * br_w], mu[2 * br_w:3 * br_w], w0, a0, k_k, k_a,
            r_k.reshape(-1), lnx_w, lnx_b]
    pp = jnp.stack([r.reshape(npairs, LANES) for r in rows], axis=1)
    pp = jnp.pad(pp, ((0, 0), (0, _PP_ROWS - len(rows)), (0, 0)))
    mu_wa = mu[3 * br_w:].reshape(1, LANES)
    w2p = w2.reshape(LORA_W, npairs, LANES).transpose(1, 0, 2)
    a2p = a2.reshape(LORA_W, npairs, LANES).transpose(1, 0, 2)
    zeros = jnp.zeros_like(w2p)
    w2a2 = jnp.concatenate(
        [jnp.concatenate([w2p, zeros], axis=2), jnp.concatenate([zeros, a2p], axis=2)], axis=1)
    return pp, mu_wa, w2a2


def kernel(x, mem, pre_norm, post_norm, w_out, mem_norm, w_mem_kv, a_w_in, a_shift_mu, a_w0, a_w2,
           a_a0, a_a2, a_k_k, a_k_a, a_r_k, a_lnx_w, a_lnx_b, kv_norm, w_kv, b_w_in, b_lam_q1,
           b_lam_k1, b_lam_q2, b_lam_k2, b_subln):
    bsz, seq, d = x.shape
    depth = pre_norm.shape[0]
    n_a = a_w_in.shape[0]
    br_w = d - MEM_W
    a_shift = 3 * br_w + 2 * LORA_W
    m = bsz * seq
    tm = 1024
    x2d = x.reshape(m, d)
    slopes = jnp.asarray(
        np.repeat(np.array(_alibi_slopes(br_w // LANES), np.float32)[:, None, None], LANES, axis=2))

    for l in range(depth):
        k_mem, v_mem = _mem_kv(mem, mem_norm[l].reshape(1, d), w_mem_kv[l].astype(BF16))
        if l < n_a:
            i = l
            segs = [(0, 0, a_shift), (0, a_shift, a_shift + br_w),
                    (0, a_shift + br_w, a_shift + br_w + MEM_W),
                    (0, a_shift + br_w + MEM_W, a_shift + br_w + 2 * MEM_W)]
            z, gate, q_mem, g_mem = _norm_proj(
                x2d, pre_norm[l].reshape(1, d), a_w_in[i].astype(BF16), segs, tm)
            pp, mu_wa, w2a2 = _rwkv_params(
                a_shift_mu[i], a_w0[i], a_w2[i], a_a0[i], a_a2[i], a_k_k[i], a_k_a[i], a_r_k[i],
                a_lnx_w[i], a_lnx_b[i], br_w)
            y_br = _rwkv(z.reshape(bsz, seq, a_shift), pp, mu_wa, w2a2).reshape(m, br_w)
        else:
            i = l - n_a
            if l == n_a:
                w_cat = jnp.concatenate([b_w_in[i], w_kv], axis=1).astype(BF16)
                gains = jnp.stack([pre_norm[l], kv_norm], axis=0)
                segs = [(0, 0, br_w), (0, br_w, 2 * br_w), (0, 2 * br_w, 2 * br_w + MEM_W),
                        (0, 2 * br_w + MEM_W, 2 * br_w + 2 * MEM_W),
                        (1, 2 * br_w + 2 * MEM_W, 3 * br_w + 2 * MEM_W),
                        (1, 3 * br_w + 2 * MEM_W, 4 * br_w + 2 * MEM_W)]
                q, gate, q_mem, g_mem, k_sh, v_sh = _norm_proj(x2d, gains, w_cat, segs, tm)
                k_sh = k_sh.reshape(bsz, seq, br_w)
                v_sh = v_sh.reshape(bsz, seq, br_w)
            else:
                segs = [(0, 0, br_w), (0, br_w, 2 * br_w), (0, 2 * br_w, 2 * br_w + MEM_W),
                        (0, 2 * br_w + MEM_W, 2 * br_w + 2 * MEM_W)]
                q, gate, q_mem, g_mem = _norm_proj(
                    x2d, pre_norm[l].reshape(1, d), b_w_in[i].astype(BF16), segs, tm)
            lam_init = 0.8 - 0.6 * math.exp(-0.3 * l)
            lam_params = jnp.stack([b_lam_q1[i], b_lam_k1[i], b_lam_q2[i], b_lam_k2[i]], axis=0)
            y_br = _diff_attn(q.reshape(bsz, seq, br_w), k_sh, v_sh, slopes, lam_params,
                              b_subln[i].reshape(LANES, 1), lam_init, ATT_Q_TILE).reshape(m, br_w)
        x2d = _out_proj(y_br, gate, q_mem, g_mem, x2d, k_mem, v_mem, w_out[l].astype(BF16),
                        post_norm[l].reshape(1, d), seq, OUT_ROWS)
    return x2d.reshape(bsz, seq, d)
```

```python
import functools
import math

import numpy as np
import jax
import jax.numpy as jnp
from jax import lax
from jax.experimental import pallas as pl
from jax.experimental.pallas import tpu as pltpu

F32 = jnp.float32
BF16 = jnp.bfloat16
ACT_DTYPE = BF16

HEAD_DIM = 64
LANES = 128
SUBLANES = 8
MEM_HEADS = 4
MEM_W = MEM_HEADS * HEAD_DIM
LORA_W = 64
ATT_CHUNK = 64
ATT_Q_TILE = 512
ATT_BATCH = 2
ATT_BIAS_PIECES = 3
ATT_V_PAD = 16
RWKV_CHUNK = 64
RWKV_GROUP = 16
RWKV_CUMSUM_ROWS = 128
RWKV_BATCH = 2
OUT_ROWS = 1024
OUT_SUB_ROWS = 512
NORM_EPS = 1e-6
LNX_EPS = 64e-5
KK_NORM_FLOOR = 1e-12
MASK_VALUE = -1e30
VMEM_LIMIT_BYTES = 56 * 1024 * 1024


def _dot(a, b):
    return jnp.dot(a, b, preferred_element_type=F32)


def _dot_nt(a, b):
    return lax.dot_general(a, b, (((1,), (1,)), ((), ())), preferred_element_type=F32)


def _dot_tn(a, b):
    return lax.dot_general(a, b, (((0,), (0,)), ((), ())), preferred_element_type=F32)


def _split(x, pieces):
    out = []
    rem = x
    for i in range(pieces):
        p = rem.astype(BF16)
        out.append(p)
        if i + 1 < pieces:
            rem = rem - p.astype(F32)
    return out


def _mm(a, b, fn=_dot, pa=1, pb=1):
    aps = _split(a, pa)
    bps = _split(b, pb)
    order = max(pa, pb)
    acc = None
    for i, ap in enumerate(aps):
        for j, bp in enumerate(bps):
            if i + j < order:
                t = fn(ap, bp)
                acc = t if acc is None else acc + t
    return acc


def _rms(x, g):
    ms = jnp.mean(x * x, axis=-1, keepdims=True)
    return (x * lax.rsqrt(ms + NORM_EPS)) * g


def _silu(x):
    h = 0.5 * x
    return h + h * jnp.tanh(h)


def _norm_proj_body(x_ref, g_ref, w_ref, *o_refs, segments):
    x = x_ref[...]
    ms = jnp.mean(x * x, axis=-1, keepdims=True)
    xn = x * lax.rsqrt(ms + NORM_EPS)
    hs = {}
    for o_ref, (gi, lo, hi) in zip(o_refs, segments):
        if gi not in hs:
            hs[gi] = (xn * g_ref[gi:gi + 1, :]).astype(BF16)
        o_ref[...] = _dot(hs[gi], w_ref[:, lo:hi]).astype(o_ref.dtype)


def _norm_proj(x2d, gains, w_bf16, segments, tm):
    m, d = x2d.shape
    n = w_bf16.shape[1]
    ng = gains.shape[0]
    out_shape = [jax.ShapeDtypeStruct((m, hi - lo), ACT_DTYPE) for _, lo, hi in segments]
    out_specs = [pl.BlockSpec((tm, hi - lo), lambda i: (i, 0)) for _, lo, hi in segments]
    return pl.pallas_call(
        functools.partial(_norm_proj_body, segments=tuple(segments)),
        grid=(m // tm,),
        in_specs=[
            pl.BlockSpec((tm, d), lambda i: (i, 0)),
            pl.BlockSpec((ng, d), lambda i: (0, 0)),
            pl.BlockSpec((d, n), lambda i: (0, 0)),
        ],
        out_specs=out_specs,
        out_shape=out_shape,
        compiler_params=pltpu.CompilerParams(
            dimension_semantics=("arbitrary",), vmem_limit_bytes=VMEM_LIMIT_BYTES),
        name="norm_proj",
    )(x2d, gains, w_bf16)


def _mem_kv_body(mem_ref, g_ref, w_ref, k_ref, v_ref):
    ml = mem_ref.shape[1]
    h = _rms(mem_ref[0], g_ref[0]).astype(BF16)
    kv = _dot(h, w_ref[0])
    lane = lax.broadcasted_iota(jnp.int32, (ml, MEM_W), 1)
    for hd in range(MEM_HEADS):
        in_head = (lane >= hd * HEAD_DIM) & (lane < (hd + 1) * HEAD_DIM)
        k_ref[0, 0, hd * ml:(hd + 1) * ml, :] = jnp.where(in_head, kv[:, :MEM_W], 0.0).astype(k_ref.dtype)
        v_ref[0, 0, hd * ml:(hd + 1) * ml, :] = jnp.where(in_head, kv[:, MEM_W:], 0.0).astype(v_ref.dtype)


def _mem_kv(mem, gains, w_bf16):
    b, ml, d = mem.shape
    nl = gains.shape[0]
    out_spec = pl.BlockSpec((1, 1, MEM_HEADS * ml, MEM_W), lambda l, i: (l, i, 0, 0))
    return pl.pallas_call(
        _mem_kv_body,
        grid=(nl, b),
        in_specs=[
            pl.BlockSpec((1, ml, d), lambda l, i: (i, 0, 0)),
            pl.BlockSpec((1, 1, d), lambda l, i: (l, 0, 0)),
            pl.BlockSpec((1, d, 2 * MEM_W), lambda l, i: (l, 0, 0)),
        ],
        out_specs=[out_spec] * 2,
        out_shape=[jax.ShapeDtypeStruct((nl, b, MEM_HEADS * ml, MEM_W), ACT_DTYPE)] * 2,
        compiler_params=pltpu.CompilerParams(
            dimension_semantics=("arbitrary", "arbitrary"), vmem_limit_bytes=VMEM_LIMIT_BYTES),
        name="mem_kv",
    )(mem, gains, w_bf16)


_PP_MU_R, _PP_MU_K, _PP_MU_V, _PP_W0, _PP_A0, _PP_KK, _PP_KA, _PP_RK, _PP_LNW, _PP_LNB = range(10)
_PP_ROWS = 16


def _rwkv_body(zr_ref, zk_ref, zv_ref, zwa_ref, pp_ref, muwa_ref, w2a2_ref, y_ref,
               s_ref, prev_ref, *, group):
    c = RWKV_CHUNK
    rows = group * c
    lane = lax.broadcasted_iota(jnp.int32, (c, LANES), 1)
    row = lax.broadcasted_iota(jnp.int32, (c, LANES), 0)
    head0 = lane < HEAD_DIM
    scol = jnp.where(head0, lane, lane - HEAD_DIM)
    strict = scol < row
    incl = scol <= row
    diag = scol == row
    r2 = lax.broadcasted_iota(jnp.int32, (LANES, LANES), 0)
    c2 = lax.broadcasted_iota(jnp.int32, (LANES, LANES), 1)
    blockmask = (r2 < HEAD_DIM) == (c2 < HEAD_DIM)
    blockones = jnp.where(blockmask, 1.0, 0.0).astype(BF16)
    cum_rows = min(rows, RWKV_CUMSUM_ROWS)
    tr = lax.broadcasted_iota(jnp.int32, (cum_rows, cum_rows), 0)
    tc = lax.broadcasted_iota(jnp.int32, (cum_rows, cum_rows), 1)
    tril_ones = jnp.where((tc <= tr) & (tc // c == tr // c), 1.0, 0.0).astype(BF16)
    slab_lane = lax.broadcasted_iota(jnp.int32, (rows, LANES), 1)
    slab_head0 = slab_lane < HEAD_DIM
    slab_first = lax.broadcasted_iota(jnp.int32, (rows, LANES), 0) == 0

    def bd(x):
        return jnp.concatenate([jnp.where(head0, x, 0.0), jnp.where(head0, 0.0, x)], axis=0)

    def seg_sum(x):
        return _dot(x.astype(BF16), blockones)

    pp = pp_ref[0]
    prow = lambda i: pp[i:i + 1, :]
    mu_r, mu_k, mu_v = prow(_PP_MU_R), prow(_PP_MU_K), prow(_PP_MU_V)
    w0, a0 = prow(_PP_W0), prow(_PP_A0)
    k_k, k_a, r_k = prow(_PP_KK), prow(_PP_KA), prow(_PP_RK)
    lnx_w, lnx_b = prow(_PP_LNW), prow(_PP_LNB)
    mu_wa = muwa_ref[...]
    w2a2 = w2a2_ref[0]

    s_ref[...] = jnp.zeros_like(s_ref)
    prev_ref[...] = jnp.zeros_like(prev_ref)

    def shift_mix(ref, bj, slot, sl, mu):
        z = ref[bj, sl, :].astype(F32)
        zp = pltpu.roll(z, 1, axis=0)
        zp = jnp.where(slab_first, prev_ref[bj, slot:slot + 1, :], zp)
        prev_ref[bj, slot:slot + 1, :] = z[rows - 1:rows, :]
        return z + (zp - z) * mu


    def prepare(bj, sl):
        r = shift_mix(zr_ref, bj, 0, sl, mu_r)
        k = shift_mix(zk_ref, bj, 1, sl, mu_k)
        yield
        v = shift_mix(zv_ref, bj, 2, sl, mu_v)
        wa = shift_mix(zwa_ref, bj, 3, sl, mu_wa)
        lora = _mm(jnp.where(slab_head0, jnp.tanh(wa), wa), w2a2)
        yield
        wlog = w0 + lora[:, :LANES]
        nw = -wlog
        w = -(jnp.maximum(nw, 0.0) + jnp.log(1.0 + jnp.exp(-jnp.abs(nw)))) - 0.5
        logw = -jnp.exp(w)
        a = 1.0 / (1.0 + jnp.exp(-(a0 + lora[:, LANES:])))
        kk = k * k_k
        kk = kk * lax.rsqrt(jnp.maximum(seg_sum(kk * kk), KK_NORM_FLOOR))
        yield
        kmod = k * (1.0 + (a - 1.0) * k_a)
        alpha = -kk
        beta = kk * a
        l2 = jnp.concatenate(_split(logw, 2), axis=1)
        cum2 = jnp.concatenate(
            [_dot(tril_ones, l2[i:i + cum_rows]) for i in range(0, rows, cum_rows)], axis=0)
        cum = cum2[:, :LANES] + cum2[:, LANES:]
        yield
        e_neg = jnp.exp(-cum)
        at = alpha * jnp.exp(cum - logw)
        rt = r * jnp.exp(cum)
        bt = beta * e_neg
        kt = kmod * e_neg
        yield
        chunks = [slice(g * c, (g + 1) * c) for g in range(group)]
        cum_c = [cum[cs.stop - 1:cs.stop, :] for cs in chunks]
        e_end = [jnp.exp(cum_c[g] - cum[cs]) for g, cs in enumerate(chunks)]
        per = lambda x: [x[cs] for cs in chunks]
        return dict(
            at=per(at), rt=per(rt), bt=per(bt), kt=per(kt), v=per(v),
            bt_end=[beta[cs] * e_end[g] for g, cs in enumerate(chunks)],
            kt_end=[kmod[cs] * e_end[g] for g, cs in enumerate(chunks)],
            p_c=[jnp.exp(x) for x in cum_c], v_slab=v, rk=r * kmod * r_k)

    def chains(p):
        at, rt, bt, kt, bt_end, kt_end, v = (p[x] for x in ("at", "rt", "bt", "kt", "bt_end", "kt_end", "v"))
        n = range(group)
        lhs = [jnp.concatenate([at[g], rt[g]], axis=0) for g in n]
        xbk = [_mm(lhs[g], jnp.concatenate([bd(bt[g]), bd(kt[g])], axis=0), _dot_nt) for g in n]
        yield
        a_rb = [jnp.where(incl, xbk[g][c:, :LANES], 0.0) for g in n]
        akrk = [jnp.concatenate([jnp.where(strict, xbk[g][:c, LANES:], 0.0),
                                 jnp.where(incl, xbk[g][c:, LANES:], 0.0)], axis=0) for g in n]
        av = [_mm(akrk[g], bd(v[g])) for g in n]
        yield
        apow = [jnp.where(strict, xbk[g][:c, :LANES], 0.0) for g in n]
        tinv = [jnp.where(diag, 1.0, 0.0) + apow[g] for g in n]
        nfac = int(math.log2(c))
        for i in range(1, nfac):
            rhs = [[bd(apow[g])] + ([bd(tinv[g])] if i > 1 else []) for g in n]
            d = [_mm(apow[g], jnp.concatenate(rhs[g], axis=1)) for g in n]
            if i > 1:
                tinv = [tinv[g] + d[g][:, LANES:] for g in n]
            apow = [d[g][:, :LANES] for g in n]
            yield
        tinv = [tinv[g] + _mm(apow[g], bd(tinv[g])) for g in n]
        yield
        x = [_mm(tinv[g], jnp.concatenate([bd(at[g]), bd(av[g][:c])], axis=1)) for g in n]
        wmat = [x[g][:, :LANES] for g in n]
        u0 = [x[g][:, LANES:] for g in n]
        yield
        d2 = [_mm(a_rb[g], jnp.concatenate([bd(wmat[g]), bd(u0[g])], axis=1)) for g in n]
        rp = [rt[g] + d2[g][:, :LANES] for g in n]
        y0 = [d2[g][:, LANES:] + av[g][c:] for g in n]
        yield
        fold = lambda z: jnp.where(head0, z[:HEAD_DIM], z[HEAD_DIM:])
        gp = [fold(_mm(wmat[g], bt_end[g], _dot_tn)) for g in n]
        npart = [fold(_mm(jnp.concatenate([u0[g], v[g]], axis=0),
                          jnp.concatenate([bt_end[g], kt_end[g]], axis=0), _dot_tn)) for g in n]
        yield
        spans = [[(jnp.where(diag, gp[g] + p["p_c"][g], gp[g]), npart[g]) for g in n]]
        while len(spans[-1]) > 1:
            prev = spans[-1]
            nxt = []
            for i in range(0, len(prev), 2):
                (ma, na), (mb, nb) = prev[i], prev[i + 1]
                prod = _mm(jnp.concatenate([ma, na], axis=0), bd(mb))
                nxt.append((prod[:HEAD_DIM], prod[HEAD_DIM:] + nb))
            spans.append(nxt)
            yield
        return dict(rp=rp, y0=y0, spans=spans, v_slab=p["v_slab"], rk=p["rk"])

    def finish(t, bj, sl):
        spans = t["spans"]
        states = {0: s_ref[bj]}
        top = len(spans) - 1
        m_all, n_all = spans[top][0]
        s_ref[bj] = _mm(states[0], bd(m_all)) + n_all
        yield
        for level in range(top, 0, -1):
            width = 1 << level
            for lo in range(0, group, width):
                ma, na = spans[level - 1][lo >> (level - 1)]
                states[lo + width // 2] = _mm(states[lo], bd(ma)) + na
            yield
        y = jnp.concatenate(
            [_mm(t["rp"][g], bd(states[g]), _dot_nt) + t["y0"][g] for g in range(group)], axis=0)
        yield
        mean = seg_sum(y) * (1.0 / HEAD_DIM)
        yc = y - mean
        var = seg_sum(yc * yc) * (1.0 / HEAD_DIM)
        yield
        yn = yc * lax.rsqrt(var + LNX_EPS) * lnx_w + lnx_b
        bonus = seg_sum(t["rk"]) * t["v_slab"]
        y_ref[bj, sl, :] = (yn + bonus).astype(y_ref.dtype)

    def run(*gens):
        results = [None] * len(gens)
        live = list(range(len(gens)))
        while live:
            for i in list(live):
                try:
                    next(gens[i])
                except StopIteration as stop:
                    results[i] = stop.value
                    live.remove(i)
        return results

    nslabs = zr_ref.shape[1] // rows
    units = [(bj, slice(t * rows, (t + 1) * rows)) for t in range(nslabs) for bj in range(zr_ref.shape[0])]
    prepared, = run(prepare(*units[0]))
    done = None
    for u in range(len(units)):
        gens = [chains(prepared)]
        if u + 1 < len(units):
            gens.append(prepare(*units[u + 1]))
        if done is not None:
            gens.append(finish(done, *units[u - 1]))
        out = run(*gens)
        done = out[0]
        if u + 1 < len(units):
            prepared = out[1]
    run(finish(done, *units[-1]))


def _rwkv(z3d, pp, mu_wa, w2a2):
    b, s, a_shift = z3d.shape
    br_w = (a_shift - 2 * LORA_W) // 3
    npairs = br_w // LANES
    nb = RWKV_BATCH
    zspec = lambda off: pl.BlockSpec((nb, s, LANES), lambda bi, hp: (bi, 0, off + hp))
    return pl.pallas_call(
        functools.partial(_rwkv_body, group=RWKV_GROUP),
        grid=(b // nb, npairs),
        in_specs=[
            zspec(0), zspec(npairs), zspec(2 * npairs),
            pl.BlockSpec((nb, s, LANES), lambda bi, hp: (bi, 0, 3 * npairs)),
            pl.BlockSpec((1, _PP_ROWS, LANES), lambda bi, hp: (hp, 0, 0)),
            pl.BlockSpec((1, LANES), lambda bi, hp: (0, 0)),
            pl.BlockSpec((1, LANES, 2 * LANES), lambda bi, hp: (hp, 0, 0)),
        ],
        out_specs=pl.BlockSpec((nb, s, LANES), lambda bi, hp: (bi, 0, hp)),
        out_shape=jax.ShapeDtypeStruct((b, s, br_w), ACT_DTYPE),
        scratch_shapes=[pltpu.VMEM((nb, HEAD_DIM, LANES), F32), pltpu.VMEM((nb, SUBLANES, LANES), F32)],
        compiler_params=pltpu.CompilerParams(
            dimension_semantics=("arbitrary", "arbitrary"), vmem_limit_bytes=VMEM_LIMIT_BYTES),
        name="rwkv7_scan",
    )(z3d, z3d, z3d, z3d, pp, mu_wa, w2a2)


def _alibi_slopes(n):
    def pow2(m):
        start = 2.0 ** (-8.0 / m)
        return [start ** (i + 1) for i in range(m)]
    if math.log2(n).is_integer():
        return pow2(n)
    cl = 2 ** int(math.floor(math.log2(n)))
    return pow2(cl) + pow2(2 * cl)[0::2][: n - cl]


def _diff_attn_body(q_ref, k_ref, v_ref, slope_ref, lamp_ref, sub_ref, o_ref,
                    kb_ref, vt_ref, band_ref, sa_ref, sb_ref, acc_ref, *, tq, lam_init):
    bi = pl.program_id(1)
    nb = q_ref.shape[0]
    tk = tq
    half = tq // 2
    nq = k_ref.shape[1] // tq
    log2e = math.log2(math.e)
    slope2 = slope_ref[0][:, :1] * log2e
    aug_lane = lax.broadcasted_iota(jnp.int32, (tk, LANES), 1)

    for bj in range(nb):
        for t in range(nq):
            rows = slice(t * tk, (t + 1) * tk)
            kb_ref[bj, rows, :LANES] = k_ref[bj, rows, :].astype(BF16)
            vt_ref[bj, :LANES, rows] = v_ref[bj, rows, :].astype(F32).T.astype(BF16)

    @pl.when(bi == 0)
    def _():
        ones_row = lax.broadcasted_iota(jnp.int32, (ATT_V_PAD, tk), 0) == 0
        for t in range(nq):
            rows = slice(t * tk, (t + 1) * tk)
            pos = (lax.broadcasted_iota(jnp.int32, (tk, LANES), 0) + t * tk).astype(F32)
            pieces = _split(slope2 * pos, ATT_BIAS_PIECES)
            aug = jnp.zeros((tk, LANES), F32)
            for i, piece in enumerate(pieces):
                aug = jnp.where(aug_lane == i, piece.astype(F32), aug)
            for bj in range(nb):
                kb_ref[bj, rows, LANES:] = aug.astype(BF16)
                vt_ref[bj, LANES:, rows] = jnp.where(ones_row, 1.0, 0.0).astype(BF16)
        kpos = lax.broadcasted_iota(jnp.int32, (tk, 2 * tq), 0).astype(F32)
        lane2 = lax.broadcasted_iota(jnp.int32, (1, 2 * tq), 1)
        qry = jnp.where(lane2 < tq, lane2, lane2 - tq)
        qlim = ((qry // ATT_CHUNK + 1) * ATT_CHUNK).astype(F32)
        band_ref[...] = jnp.where(
            kpos < qlim, (-2.0 * slope2) * jnp.maximum(kpos - qry.astype(F32), 0.0), MASK_VALUE)

    lane = lax.broadcasted_iota(jnp.int32, (2 * tq, LANES), 1)
    head0 = lane < HEAD_DIM
    first = lax.broadcasted_iota(jnp.int32, (2 * tq, LANES), 0) < tq
    ones_cols = jnp.where(lane < ATT_BIAS_PIECES, 1.0, 0.0)
    lp = lamp_ref[...]
    lam = (jnp.exp(jnp.sum(lp[0:1] * lp[1:2], axis=-1, keepdims=True))
           - jnp.exp(jnp.sum(lp[2:3] * lp[3:4], axis=-1, keepdims=True)) + lam_init)
    late = lambda x: jnp.concatenate([x[..., half:tq], x[..., tq + half:]], axis=-1)

    def update(p_rows, vt, m, m_new, cols):
        pv = _dot(vt, p_rows.astype(BF16))
        off = 0
        for cs in cols:
            w = cs.stop - cs.start
            if m is None:
                acc_ref[:, cs] = pv[:, off:off + w]
            else:
                acc_ref[:, cs] = jnp.exp2(m - m_new)[:, off:off + w] * acc_ref[:, cs] + pv[:, off:off + w]
            off += w

    def tiles_of(n):
        past = [(slice(j * tk, (j + 1) * tk), None, None) for j in range(n)]
        own0 = (slice(n * tq, n * tq + half), None, slice(0, half))
        own1 = (slice(n * tq + half, (n + 1) * tq), "late", slice(half, tq))
        return past + [own0, own1]

    all_cols = [slice(0, 2 * tq)]
    late_cols = [slice(half, tq), slice(tq + half, 2 * tq)]
    bufs = (sa_ref, sb_ref)
    for bj in range(nb):
        for n in range(nq):
            qrows = slice(n * tq, (n + 1) * tq)
            q = q_ref[bj, qrows, :].astype(F32) * (HEAD_DIM ** -0.5 * log2e)
            q2x = jnp.concatenate([q, q], axis=0)
            qcat = jnp.concatenate([jnp.where(first == head0, q2x, 0.0), ones_cols],
                                   axis=1).astype(BF16)
            qlate = jnp.concatenate([qcat[half:tq], qcat[tq + half:]], axis=0)
            tiles = tiles_of(n)

            def scores(i, dst_ref):
                keys, which, _ = tiles[i]
                nk = keys.stop - keys.start
                if which is None:
                    dst_ref[:nk, :] = _dot_nt(kb_ref[bj, keys, :], qcat)
                else:
                    dst_ref[:nk, :tq] = _dot_nt(kb_ref[bj, keys, :], qlate)

            m = None
            scores(0, bufs[0])
            for i, (keys, which, brows) in enumerate(tiles):
                if i + 1 < len(tiles):
                    scores(i + 1, bufs[(i + 1) % 2])
                nk = keys.stop - keys.start
                vt = vt_ref[bj, :, keys]
                if which is None:
                    s = bufs[i % 2][:nk, :]
                    if brows is not None:
                        s = s + band_ref[brows, :]
                    m_new = jnp.max(s, axis=0, keepdims=True)
                    if m is not None:
                        m_new = jnp.maximum(m, m_new)
                    update(jnp.exp2(s - m_new), vt, m, m_new, all_cols)
                    m = m_new
                else:
                    s = bufs[i % 2][:nk, :tq] + late(band_ref[brows, :])
                    ml = late(m)
                    m_new = jnp.maximum(ml, jnp.max(s, axis=0, keepdims=True))
                    update(jnp.exp2(s - m_new), vt, ml, m_new, late_cols)
            on = acc_ref[:LANES, :] * (1.0 / acc_ref[LANES:LANES + 1, :])
            o = on[:, :tq] - lam * on[:, tq:]
            ms = jnp.mean(o * o, axis=0, keepdims=True)
            o = o * lax.rsqrt(ms + NORM_EPS) * sub_ref[...] * (1.0 - lam_init)
            o_ref[bj, qrows, :] = o.T.astype(o_ref.dtype)


def _diff_attn(q3d, k3d, v3d, slopes, lam_params, subln_col, lam_init, tq):
    b, s, br_w = q3d.shape
    nh = br_w // LANES
    nb = ATT_BATCH
    seq_spec = pl.BlockSpec((nb, s, LANES), lambda h, bi: (bi, 0, h))
    return pl.pallas_call(
        functools.partial(_diff_attn_body, tq=tq, lam_init=lam_init),
        grid=(nh, b // nb),
        in_specs=[
            seq_spec, seq_spec, seq_spec,
            pl.BlockSpec((1, 1, LANES), lambda h, bi: (h, 0, 0)),
            pl.BlockSpec((4, HEAD_DIM), lambda h, bi: (0, 0)),
            pl.BlockSpec((LANES, 1), lambda h, bi: (0, 0)),
        ],
        out_specs=seq_spec,
        out_shape=jax.ShapeDtypeStruct((b, s, br_w), ACT_DTYPE),
        scratch_shapes=[pltpu.VMEM((nb, s, 2 * LANES), BF16), pltpu.VMEM((nb, LANES + ATT_V_PAD, s), BF16),
                        pltpu.VMEM((tq, 2 * tq), F32),
                        pltpu.VMEM((tq, 2 * tq), F32), pltpu.VMEM((tq, 2 * tq), F32),
                        pltpu.VMEM((LANES + ATT_V_PAD, 2 * tq), F32)],
        compiler_params=pltpu.CompilerParams(
            dimension_semantics=("arbitrary", "arbitrary"),
            vmem_limit_bytes=VMEM_LIMIT_BYTES),
        name="diff_attn",
    )(q3d, k3d, v3d, slopes, lam_params, subln_col)


def _out_body(ybr_ref, gate_ref, qm_ref, gm_ref, x_ref, km_ref, vm_ref, w_ref, pn_ref, o_ref, *, br_w):
    tm = x_ref.shape[0]
    ml = km_ref.shape[1] // MEM_HEADS
    subs = [slice(r * OUT_SUB_ROWS, (r + 1) * OUT_SUB_ROWS) for r in range(tm // OUT_SUB_ROWS)]
    qscale = HEAD_DIM ** -0.5 * math.log2(math.e)
    s = [_dot_nt((qm_ref[rs, :].astype(F32) * qscale).astype(BF16), km_ref[0]) for rs in subs]
    y_br = (ybr_ref[...].astype(F32) * _silu(gate_ref[...].astype(F32))).astype(BF16)
    pcat = []
    for sr in s:
        ps = []
        for hd in range(MEM_HEADS):
            sh = sr[:, hd * ml:(hd + 1) * ml]
            p = jnp.exp2(sh - jnp.max(sh, axis=-1, keepdims=True))
            ps.append((p * (1.0 / jnp.sum(p, axis=-1, keepdims=True))).astype(BF16))
        pcat.append(jnp.concatenate(ps, axis=1))
    y_mem = [_dot(pc, vm_ref[0]) for pc in pcat]
    y_mem = (jnp.concatenate(y_mem, axis=0) * _silu(gm_ref[...].astype(F32))).astype(BF16)
    y = _dot(y_br, w_ref[:br_w, :]) + _dot(y_mem, w_ref[br_w:, :])
    o_ref[...] = x_ref[...] + _rms(y, pn_ref[...])


def _out_proj(ybr, gate, qm, gm, x2d, k_mem, v_mem, w_bf16, post_g, seq, tm):
    m, d = x2d.shape
    br_w = ybr.shape[1]
    ml = k_mem.shape[1]
    per_b = seq // tm
    row = lambda w: pl.BlockSpec((tm, w), lambda i: (i, 0))
    return pl.pallas_call(
        functools.partial(_out_body, br_w=br_w),
        grid=(m // tm,),
        in_specs=[
            row(br_w), row(br_w), row(MEM_W), row(MEM_W), row(d),
            pl.BlockSpec((1, ml, MEM_W), lambda i: (i // per_b, 0, 0)),
            pl.BlockSpec((1, ml, MEM_W), lambda i: (i // per_b, 0, 0)),
            pl.BlockSpec((d, d), lambda i: (0, 0)),
            pl.BlockSpec((1, d), lambda i: (0, 0)),
        ],
        out_specs=row(d),
        out_shape=jax.ShapeDtypeStruct((m, d), F32),
        compiler_params=pltpu.CompilerParams(
            dimension_semantics=("arbitrary",), vmem_limit_bytes=VMEM_LIMIT_BYTES),
        name="out_proj",
    )(ybr, gate, qm, gm, x2d, k_mem, v_mem, w_bf16, post_g)


def _rwkv_params(mu, w0, w2, a0, a2, k_k, k_a, r_k, lnx_w, lnx_b, br_w):
    npairs = br_w // LANES
    rows = [mu[:br_w], mu[br_w:2 * br_w], mu[2 * br_w:3 * br_w], w0, a0, k_k, k_a,
            r_k.reshape(-1), lnx_w, lnx_b]
    pp = jnp.stack([r.reshape(npairs, LANES) for r in rows], axis=1)
    pp = jnp.pad(pp, ((0, 0), (0, _PP_ROWS - len(rows)), (0, 0)))
    mu_wa = mu[3 * br_w:].reshape(1, LANES)
    w2p = w2.reshape(LORA_W, npairs, LANES).transpose(1, 0, 2)
    a2p = a2.reshape(LORA_W, npairs, LANES).transpose(1, 0, 2)
    zeros = jnp.zeros_like(w2p)
    w2a2 = jnp.concatenate(
        [jnp.concatenate([w2p, zeros], axis=2), jnp.concatenate([zeros, a2p], axis=2)], axis=1)
    return pp, mu_wa, w2a2


def kernel(x, mem, pre_norm, post_norm, w_out, mem_norm, w_mem_kv, a_w_in, a_shift_mu, a_w0, a_w2,
           a_a0, a_a2, a_k_k, a_k_a, a_r_k, a_lnx_w, a_lnx_b, kv_norm, w_kv, b_w_in, b_lam_q1,
           b_lam_k1, b_lam_q2, b_lam_k2, b_subln):
    bsz, seq, d = x.shape
    depth = pre_norm.shape[0]
    n_a = a_w_in.shape[0]
    br_w = d - MEM_W
    a_shift = 3 * br_w + 2 * LORA_W
    m = bsz * seq
    tm = 1024
    x2d = x.reshape(m, d)
    slopes = jnp.asarray(
        np.repeat(np.array(_alibi_slopes(br_w // LANES), np.float32)[:, None, None], LANES, axis=2))

    k_mem_all, v_mem_all = _mem_kv(mem, mem_norm.reshape(depth, 1, d), w_mem_kv.astype(BF16))
    for l in range(depth):
        k_mem, v_mem = k_mem_all[l], v_mem_all[l]
        if l < n_a:
            i = l
            segs = [(0, 0, a_shift), (0, a_shift, a_shift + br_w),
                    (0, a_shift + br_w, a_shift + br_w + MEM_W),
                    (0, a_shift + br_w + MEM_W, a_shift + br_w + 2 * MEM_W)]
            z, gate, q_mem, g_mem = _norm_proj(
                x2d, pre_norm[l].reshape(1, d), a_w_in[i].astype(BF16), segs, tm)
            pp, mu_wa, w2a2 = _rwkv_params(
                a_shift_mu[i], a_w0[i], a_w2[i], a_a0[i], a_a2[i], a_k_k[i], a_k_a[i], a_r_k[i],
                a_lnx_w[i], a_lnx_b[i], br_w)
            y_br = _rwkv(z.reshape(bsz, seq, a_shift), pp, mu_wa, w2a2).reshape(m, br_w)
        else:
            i = l - n_a
            if l == n_a:
                w_cat = jnp.concatenate([b_w_in[i], w_kv], axis=1).astype(BF16)
                gains = jnp.stack([pre_norm[l], kv_norm], axis=0)
                segs = [(0, 0, br_w), (0, br_w, 2 * br_w), (0, 2 * br_w, 2 * br_w + MEM_W),
                        (0, 2 * br_w + MEM_W, 2 * br_w + 2 * MEM_W),
                        (1, 2 * br_w + 2 * MEM_W, 3 * br_w + 2 * MEM_W),
                        (1, 3 * br_w + 2 * MEM_W, 4 * br_w + 2 * MEM_W)]
                q, gate, q_mem, g_mem, k_sh, v_sh = _norm_proj(x2d, gains, w_cat, segs, tm)
                k_sh = k_sh.reshape(bsz, seq, br_w)
                v_sh = v_sh.reshape(bsz, seq, br_w)
            else:
                segs = [(0, 0, br_w), (0, br_w, 2 * br_w), (0, 2 * br_w, 2 * br_w + MEM_W),
                        (0, 2 * br_w + MEM_W, 2 * br_w + 2 * MEM_W)]
                q, gate, q_mem, g_mem = _norm_proj(
                    x2d, pre_norm[l].reshape(1, d), b_w_in[i].astype(BF16), segs, tm)
            lam_init = 0.8 - 0.6 * math.exp(-0.3 * l)
            lam_params = jnp.stack([b_lam_q1[i], b_lam_k1[i], b_lam_q2[i], b_lam_k2[i]], axis=0)
            y_br = _diff_attn(q.reshape(bsz, seq, br_w), k_sh, v_sh, slopes, lam_params,
                              b_subln[i].reshape(LANES, 1), lam_init, ATT_Q_TILE).reshape(m, br_w)
        x2d = _out_proj(y_br, gate, q_mem, g_mem, x2d, k_mem, v_mem, w_out[l].astype(BF16),
                        post_norm[l].reshape(1, d), seq, OUT_ROWS)
    return x2d.reshape(bsz, seq, d)
```

```python
import functools
import math

import numpy as np
import jax
import jax.numpy as jnp
from jax import lax
from jax.experimental import pallas as pl
from jax.experimental.pallas import tpu as pltpu

F32 = jnp.float32
BF16 = jnp.bfloat16
ACT_DTYPE = BF16

HEAD_DIM = 64
LANES = 128
SUBLANES = 8
MEM_HEADS = 4
MEM_W = MEM_HEADS * HEAD_DIM
LORA_W = 64
ATT_CHUNK = 64
ATT_Q_TILE = 512
ATT_BATCH = 2
ATT_BIAS_PIECES = 3
ATT_V_PAD = 16
RWKV_CHUNK = 64
RWKV_GROUP = 16
RWKV_CUMSUM_ROWS = 128
RWKV_BATCH = 2
OUT_ROWS = 1024
OUT_SUB_ROWS = 512
NORM_EPS = 1e-6
LNX_EPS = 64e-5
KK_NORM_FLOOR = 1e-12
MASK_VALUE = -1e30
VMEM_LIMIT_BYTES = 56 * 1024 * 1024


def _dot(a, b):
    return jnp.dot(a, b, preferred_element_type=F32)


def _dot_nt(a, b):
    return lax.dot_general(a, b, (((1,), (1,)), ((), ())), preferred_element_type=F32)


def _dot_tn(a, b):
    return lax.dot_general(a, b, (((0,), (0,)), ((), ())), preferred_element_type=F32)


def _split(x, pieces):
    out = []
    rem = x
    for i in range(pieces):
        p = rem.astype(BF16)
        out.append(p)
        if i + 1 < pieces:
            rem = rem - p.astype(F32)
    return out


def _mm(a, b, fn=_dot, pa=1, pb=1):
    aps = _split(a, pa)
    bps = _split(b, pb)
    order = max(pa, pb)
    acc = None
    for i, ap in enumerate(aps):
        for j, bp in enumerate(bps):
            if i + j < order:
                t = fn(ap, bp)
                acc = t if acc is None else acc + t
    return acc


def _rms(x, g):
    ms = jnp.mean(x * x, axis=-1, keepdims=True)
    return (x * lax.rsqrt(ms + NORM_EPS)) * g


def _silu(x):
    h = 0.5 * x
    return h + h * jnp.tanh(h)


def _norm_proj_body(x_ref, g_ref, w_ref, *o_refs, segments):
    x = x_ref[...]
    ms = jnp.mean(x * x, axis=-1, keepdims=True)
    xn = x * lax.rsqrt(ms + NORM_EPS)
    hs = {}
    for o_ref, (gi, lo, hi) in zip(o_refs, segments):
        if gi not in hs:
            hs[gi] = (xn * g_ref[gi:gi + 1, :]).astype(BF16)
        o_ref[...] = _dot(hs[gi], w_ref[:, lo:hi]).astype(o_ref.dtype)


def _norm_proj(x2d, gains, w_bf16, segments, tm):
    m, d = x2d.shape
    n = w_bf16.shape[1]
    ng = gains.shape[0]
    out_shape = [jax.ShapeDtypeStruct((m, hi - lo), ACT_DTYPE) for _, lo, hi in segments]
    out_specs = [pl.BlockSpec((tm, hi - lo), lambda i: (i, 0)) for _, lo, hi in segments]
    return pl.pallas_call(
        functools.partial(_norm_proj_body, segments=tuple(segments)),
        grid=(m // tm,),
        in_specs=[
            pl.BlockSpec((tm, d), lambda i: (i, 0)),
            pl.BlockSpec((ng, d), lambda i: (0, 0)),
            pl.BlockSpec((d, n), lambda i: (0, 0)),
        ],
        out_specs=out_specs,
        out_shape=out_shape,
        compiler_params=pltpu.CompilerParams(
            dimension_semantics=("arbitrary",), vmem_limit_bytes=VMEM_LIMIT_BYTES),
        name="norm_proj",
    )(x2d, gains, w_bf16)


def _mem_kv_body(mem_ref, g_ref, w_ref, k_ref, v_ref):
    ml = mem_ref.shape[1]
    h = _rms(mem_ref[0], g_ref[0]).astype(BF16)
    kv = _dot(h, w_ref[0])
    lane = lax.broadcasted_iota(jnp.int32, (ml, MEM_W), 1)
    for hd in range(MEM_HEADS):
        in_head = (lane >= hd * HEAD_DIM) & (lane < (hd + 1) * HEAD_DIM)
        k_ref[0, 0, hd * ml:(hd + 1) * ml, :] = jnp.where(in_head, kv[:, :MEM_W], 0.0).astype(k_ref.dtype)
        v_ref[0, 0, hd * ml:(hd + 1) * ml, :] = jnp.where(in_head, kv[:, MEM_W:], 0.0).astype(v_ref.dtype)


def _mem_kv(mem, gains, w_bf16):
    b, ml, d = mem.shape
    nl = gains.shape[0]
    out_spec = pl.BlockSpec((1, 1, MEM_HEADS * ml, MEM_W), lambda l, i: (l, i, 0, 0))
    return pl.pallas_call(
        _mem_kv_body,
        grid=(nl, b),
        in_specs=[
            pl.BlockSpec((1, ml, d), lambda l, i: (i, 0, 0)),
            pl.BlockSpec((1, 1, d), lambda l, i: (l, 0, 0)),
            pl.BlockSpec((1, d, 2 * MEM_W), lambda l, i: (l, 0, 0)),
        ],
        out_specs=[out_spec] * 2,
        out_shape=[jax.ShapeDtypeStruct((nl, b, MEM_HEADS * ml, MEM_W), ACT_DTYPE)] * 2,
        compiler_params=pltpu.CompilerParams(
            dimension_semantics=("arbitrary", "arbitrary"), vmem_limit_bytes=VMEM_LIMIT_BYTES),
        name="mem_kv",
    )(mem, gains, w_bf16)


_PP_MU_R, _PP_MU_K, _PP_MU_V, _PP_W0, _PP_A0, _PP_KK, _PP_KA, _PP_RK, _PP_LNW, _PP_LNB = range(10)
_PP_ROWS = 16


def _rwkv_body(zr_ref, zk_ref, zv_ref, zwa_ref, pp_ref, muwa_ref, w2a2_ref, y_ref,
               s_ref, prev_ref, *, group):
    c = RWKV_CHUNK
    rows = group * c
    lane = lax.broadcasted_iota(jnp.int32, (c, LANES), 1)
    row = lax.broadcasted_iota(jnp.int32, (c, LANES), 0)
    head0 = lane < HEAD_DIM
    scol = jnp.where(head0, lane, lane - HEAD_DIM)
    strict = scol < row
    incl = scol <= row
    diag = scol == row
    r2 = lax.broadcasted_iota(jnp.int32, (LANES, LANES), 0)
    c2 = lax.broadcasted_iota(jnp.int32, (LANES, LANES), 1)
    blockmask = (r2 < HEAD_DIM) == (c2 < HEAD_DIM)
    blockones = jnp.where(blockmask, 1.0, 0.0).astype(BF16)
    cum_rows = min(rows, RWKV_CUMSUM_ROWS)
    tr = lax.broadcasted_iota(jnp.int32, (cum_rows, cum_rows), 0)
    tc = lax.broadcasted_iota(jnp.int32, (cum_rows, cum_rows), 1)
    tril_ones = jnp.where((tc <= tr) & (tc // c == tr // c), 1.0, 0.0).astype(BF16)
    slab_lane = lax.broadcasted_iota(jnp.int32, (rows, LANES), 1)
    slab_head0 = slab_lane < HEAD_DIM
    slab_first = lax.broadcasted_iota(jnp.int32, (rows, LANES), 0) == 0

    def bd(x):
        return jnp.concatenate([jnp.where(head0, x, 0.0), jnp.where(head0, 0.0, x)], axis=0)

    def seg_sum(x):
        return _dot(x.astype(BF16), blockones)

    pp = pp_ref[0]
    prow = lambda i: pp[i:i + 1, :]
    mu_r, mu_k, mu_v = prow(_PP_MU_R), prow(_PP_MU_K), prow(_PP_MU_V)
    w0, a0 = prow(_PP_W0), prow(_PP_A0)
    k_k, k_a, r_k = prow(_PP_KK), prow(_PP_KA), prow(_PP_RK)
    lnx_w, lnx_b = prow(_PP_LNW), prow(_PP_LNB)
    mu_wa = muwa_ref[...]
    w2a2 = w2a2_ref[0]

    s_ref[...] = jnp.zeros_like(s_ref)
    prev_ref[...] = jnp.zeros_like(prev_ref)

    def shift_mix(ref, bj, slot, sl, mu):
        z = ref[bj, sl, :].astype(F32)
        zp = pltpu.roll(z, 1, axis=0)
        zp = jnp.where(slab_first, prev_ref[bj, slot:slot + 1, :], zp)
        prev_ref[bj, slot:slot + 1, :] = z[rows - 1:rows, :]
        return z + (zp - z) * mu


    def prepare(bj, sl):
        r = shift_mix(zr_ref, bj, 0, sl, mu_r)
        k = shift_mix(zk_ref, bj, 1, sl, mu_k)
        yield
        v = shift_mix(zv_ref, bj, 2, sl, mu_v)
        wa = shift_mix(zwa_ref, bj, 3, sl, mu_wa)
        lora = _mm(jnp.where(slab_head0, jnp.tanh(wa), wa), w2a2)
        yield
        wlog = w0 + lora[:, :LANES]
        nw = -wlog
        w = -(jnp.maximum(nw, 0.0) + jnp.log(1.0 + jnp.exp(-jnp.abs(nw)))) - 0.5
        logw = -jnp.exp(w)
        a = 1.0 / (1.0 + jnp.exp(-(a0 + lora[:, LANES:])))
        kk = k * k_k
        kk = kk * lax.rsqrt(jnp.maximum(seg_sum(kk * kk), KK_NORM_FLOOR))
        yield
        kmod = k * (1.0 + (a - 1.0) * k_a)
        alpha = -kk
        beta = kk * a
        l2 = jnp.concatenate(_split(logw, 2), axis=1)
        cum2 = jnp.concatenate(
            [_dot(tril_ones, l2[i:i + cum_rows]) for i in range(0, rows, cum_rows)], axis=0)
        cum = cum2[:, :LANES] + cum2[:, LANES:]
        yield
        e_neg = jnp.exp(-cum)
        at = alpha * jnp.exp(cum - logw)
        rt = r * jnp.exp(cum)
        bt = beta * e_neg
        kt = kmod * e_neg
        yield
        chunks = [slice(g * c, (g + 1) * c) for g in range(group)]
        cum_c = [cum[cs.stop - 1:cs.stop, :] for cs in chunks]
        e_end = [jnp.exp(cum_c[g] - cum[cs]) for g, cs in enumerate(chunks)]
        per = lambda x: [x[cs] for cs in chunks]
        return dict(
            at=per(at), rt=per(rt), bt=per(bt), kt=per(kt), v=per(v),
            bt_end=[beta[cs] * e_end[g] for g, cs in enumerate(chunks)],
            kt_end=[kmod[cs] * e_end[g] for g, cs in enumerate(chunks)],
            p_c=[jnp.exp(x) for x in cum_c], v_slab=v, rk=r * kmod * r_k)

    def chains(p):
        at, rt, bt, kt, bt_end, kt_end, v = (p[x] for x in ("at", "rt", "bt", "kt", "bt_end", "kt_end", "v"))
        n = range(group)
        lhs = [jnp.concatenate([at[g], rt[g]], axis=0) for g in n]
        xbk = [_mm(lhs[g], jnp.concatenate([bd(bt[g]), bd(kt[g])], axis=0), _dot_nt) for g in n]
        yield
        a_rb = [jnp.where(incl, xbk[g][c:, :LANES], 0.0) for g in n]
        akrk = [jnp.concatenate([jnp.where(strict, xbk[g][:c, LANES:], 0.0),
                                 jnp.where(incl, xbk[g][c:, LANES:], 0.0)], axis=0) for g in n]
        av = [_mm(akrk[g], bd(v[g])) for g in n]
        yield
        apow = [jnp.where(strict, xbk[g][:c, :LANES], 0.0) for g in n]
        tinv = [jnp.where(diag, 1.0, 0.0) + apow[g] for g in n]
        nfac = int(math.log2(c))
        for i in range(1, nfac):
            rhs = [[bd(apow[g])] + ([bd(tinv[g])] if i > 1 else []) for g in n]
            d = [_mm(apow[g], jnp.concatenate(rhs[g], axis=1)) for g in n]
            if i > 1:
                tinv = [tinv[g] + d[g][:, LANES:] for g in n]
            apow = [d[g][:, :LANES] for g in n]
            yield
        tinv = [tinv[g] + _mm(apow[g], bd(tinv[g])) for g in n]
        yield
        x = [_mm(tinv[g], jnp.concatenate([bd(at[g]), bd(av[g][:c])], axis=1)) for g in n]
        wmat = [x[g][:, :LANES] for g in n]
        u0 = [x[g][:, LANES:] for g in n]
        yield
        d2 = [_mm(a_rb[g], jnp.concatenate([bd(wmat[g]), bd(u0[g])], axis=1)) for g in n]
        rp = [rt[g] + d2[g][:, :LANES] for g in n]
        y0 = [d2[g][:, LANES:] + av[g][c:] for g in n]
        yield
        fold = lambda z: jnp.where(head0, z[:HEAD_DIM], z[HEAD_DIM:])
        gp = [fold(_mm(wmat[g], bt_end[g], _dot_tn)) for g in n]
        npart = [fold(_mm(jnp.concatenate([u0[g], v[g]], axis=0),
                          jnp.concatenate([bt_end[g], kt_end[g]], axis=0), _dot_tn)) for g in n]
        yield
        spans = [[(jnp.where(diag, gp[g] + p["p_c"][g], gp[g]), npart[g]) for g in n]]
        while len(spans[-1]) > 1:
            prev = spans[-1]
            nxt = []
            for i in range(0, len(prev), 2):
                (ma, na), (mb, nb) = prev[i], prev[i + 1]
                prod = _mm(jnp.concatenate([ma, na], axis=0), bd(mb))
                nxt.append((prod[:HEAD_DIM], prod[HEAD_DIM:] + nb))
            spans.append(nxt)
            yield
        return dict(rp=rp, y0=y0, spans=spans, v_slab=p["v_slab"], rk=p["rk"])

    def finish(t, bj, sl):
        spans = t["spans"]
        states = {0: s_ref[bj]}
        top = len(spans) - 1
        m_all, n_all = spans[top][0]
        s_ref[bj] = _mm(states[0], bd(m_all)) + n_all
        yield
        for level in range(top, 0, -1):
            width = 1 << level
            for lo in range(0, group, width):
                ma, na = spans[level - 1][lo >> (level - 1)]
                states[lo + width // 2] = _mm(states[lo], bd(ma)) + na
            yield
        y = jnp.concatenate(
            [_mm(t["rp"][g], bd(states[g]), _dot_nt) + t["y0"][g] for g in range(group)], axis=0)
        yield
        mean = seg_sum(y) * (1.0 / HEAD_DIM)
        yc = y - mean
        var = seg_sum(yc * yc) * (1.0 / HEAD_DIM)
        yield
        yn = yc * lax.rsqrt(var + LNX_EPS) * lnx_w + lnx_b
        bonus = seg_sum(t["rk"]) * t["v_slab"]
        y_ref[bj, sl, :] = (yn + bonus).astype(y_ref.dtype)

    def run(*gens):
        results = [None] * len(gens)
        live = list(range(len(gens)))
        while live:
            for i in list(live):
                try:
                    next(gens[i])
                except StopIteration as stop:
                    results[i] = stop.value
                    live.remove(i)
        return results

    nslabs = zr_ref.shape[1] // rows
    units = [(bj, slice(t * rows, (t + 1) * rows)) for t in range(nslabs) for bj in range(zr_ref.shape[0])]
    prepared, = run(prepare(*units[0]))
    done = None
    for u in range(len(units)):
        gens = [chains(prepared)]
        if u + 1 < len(units):
            gens.append(prepare(*units[u + 1]))
        if done is not None:
            gens.append(finish(done, *units[u - 1]))
        out = run(*gens)
        done = out[0]
        if u + 1 < len(units):
            prepared = out[1]
    run(finish(done, *units[-1]))


def _rwkv(z3d, pp, mu_wa, w2a2):
    b, s, a_shift = z3d.shape
    br_w = (a_shift - 2 * LORA_W) // 3
    npairs = br_w // LANES
    nb = RWKV_BATCH
    zspec = lambda off: pl.BlockSpec((nb, s, LANES), lambda bi, hp: (bi, 0, off + hp))
    return pl.pallas_call(
        functools.partial(_rwkv_body, group=RWKV_GROUP),
        grid=(b // nb, npairs),
        in_specs=[
            zspec(0), zspec(npairs), zspec(2 * npairs),
            pl.BlockSpec((nb, s, LANES), lambda bi, hp: (bi, 0, 3 * npairs)),
            pl.BlockSpec((1, _PP_ROWS, LANES), lambda bi, hp: (hp, 0, 0)),
            pl.BlockSpec((1, LANES), lambda bi, hp: (0, 0)),
            pl.BlockSpec((1, LANES, 2 * LANES), lambda bi, hp: (hp, 0, 0)),
        ],
        out_specs=pl.BlockSpec((nb, s, LANES), lambda bi, hp: (bi, 0, hp)),
        out_shape=jax.ShapeDtypeStruct((b, s, br_w), ACT_DTYPE),
        scratch_shapes=[pltpu.VMEM((nb, HEAD_DIM, LANES), F32), pltpu.VMEM((nb, SUBLANES, LANES), F32)],
        compiler_params=pltpu.CompilerParams(
            dimension_semantics=("arbitrary", "arbitrary"), vmem_limit_bytes=VMEM_LIMIT_BYTES),
        name="rwkv7_scan",
    )(z3d, z3d, z3d, z3d, pp, mu_wa, w2a2)


def _alibi_slopes(n):
    def pow2(m):
        start = 2.0 ** (-8.0 / m)
        return [start ** (i + 1) for i in range(m)]
    if math.log2(n).is_integer():
        return pow2(n)
    cl = 2 ** int(math.floor(math.log2(n)))
    return pow2(cl) + pow2(2 * cl)[0::2][: n - cl]


def _diff_attn_body(q_ref, k_ref, v_ref, slope_ref, lamp_ref, sub_ref, o_ref,
                    kb_ref, vt_ref, band_ref, sa_ref, sb_ref, acc_ref, *, tq, lam_init):
    bi = pl.program_id(1)
    nb = q_ref.shape[0]
    tk = tq
    half = tq // 2
    nq = k_ref.shape[1] // tq
    log2e = math.log2(math.e)
    slope2 = slope_ref[0][:, :1] * log2e
    aug_lane = lax.broadcasted_iota(jnp.int32, (tk, LANES), 1)

    for bj in range(nb):
        for t in range(nq):
            rows = slice(t * tk, (t + 1) * tk)
            kb_ref[bj, rows, :LANES] = k_ref[bj, rows, :].astype(BF16)
            vt_ref[bj, :LANES, rows] = v_ref[bj, rows, :].astype(F32).T.astype(BF16)

    @pl.when(bi == 0)
    def _():
        ones_row = lax.broadcasted_iota(jnp.int32, (ATT_V_PAD, tk), 0) == 0
        for t in range(nq):
            rows = slice(t * tk, (t + 1) * tk)
            pos = (lax.broadcasted_iota(jnp.int32, (tk, LANES), 0) + t * tk).astype(F32)
            pieces = _split(slope2 * pos, ATT_BIAS_PIECES)
            aug = jnp.zeros((tk, LANES), F32)
            for i, piece in enumerate(pieces):
                aug = jnp.where(aug_lane == i, piece.astype(F32), aug)
            for bj in range(nb):
                kb_ref[bj, rows, LANES:] = aug.astype(BF16)
                vt_ref[bj, LANES:, rows] = jnp.where(ones_row, 1.0, 0.0).astype(BF16)
        kpos = lax.broadcasted_iota(jnp.int32, (tk, 2 * tq), 0).astype(F32)
        lane2 = lax.broadcasted_iota(jnp.int32, (1, 2 * tq), 1)
        qry = jnp.where(lane2 < tq, lane2, lane2 - tq)
        qlim = ((qry // ATT_CHUNK + 1) * ATT_CHUNK).astype(F32)
        band_ref[...] = jnp.where(
            kpos < qlim, (-2.0 * slope2) * jnp.maximum(kpos - qry.astype(F32), 0.0), MASK_VALUE)

    lane = lax.broadcasted_iota(jnp.int32, (2 * tq, LANES), 1)
    head0 = lane < HEAD_DIM
    first = lax.broadcasted_iota(jnp.int32, (2 * tq, LANES), 0) < tq
    ones_cols = jnp.where(lane < ATT_BIAS_PIECES, 1.0, 0.0)
    lp = lamp_ref[...]
    lam = (jnp.exp(jnp.sum(lp[0:1] * lp[1:2], axis=-1, keepdims=True))
           - jnp.exp(jnp.sum(lp[2:3] * lp[3:4], axis=-1, keepdims=True)) + lam_init)
    late = lambda x: jnp.concatenate([x[..., half:tq], x[..., tq + half:]], axis=-1)

    def update(p_rows, vt, m, m_new, cols):
        pv = _dot(vt, p_rows.astype(BF16))
        off = 0
        for cs in cols:
            w = cs.stop - cs.start
            if m is None:
                acc_ref[:, cs] = pv[:, off:off + w]
            else:
                acc_ref[:, cs] = jnp.exp2(m - m_new)[:, off:off + w] * acc_ref[:, cs] + pv[:, off:off + w]
            off += w

    def tiles_of(n):
        past = [(slice(j * tk, (j + 1) * tk), None, None) for j in range(n)]
        own0 = (slice(n * tq, n * tq + half), None, slice(0, half))
        own1 = (slice(n * tq + half, (n + 1) * tq), "late", slice(half, tq))
        return past + [own0, own1]

    all_cols = [slice(0, 2 * tq)]
    late_cols = [slice(half, tq), slice(tq + half, 2 * tq)]
    bufs = (sa_ref, sb_ref)
    for bj in range(nb):
        for n in range(nq):
            qrows = slice(n * tq, (n + 1) * tq)
            q = q_ref[bj, qrows, :].astype(F32) * (HEAD_DIM ** -0.5 * log2e)
            q2x = jnp.concatenate([q, q], axis=0)
            qcat = jnp.concatenate([jnp.where(first == head0, q2x, 0.0), ones_cols],
                                   axis=1).astype(BF16)
            qlate = jnp.concatenate([qcat[half:tq], qcat[tq + half:]], axis=0)
            tiles = tiles_of(n)

            def scores(i, dst_ref):
                keys, which, _ = tiles[i]
                nk = keys.stop - keys.start
                if which is None:
                    dst_ref[:nk, :] = _dot_nt(kb_ref[bj, keys, :], qcat)
                else:
                    dst_ref[:nk, :tq] = _dot_nt(kb_ref[bj, keys, :], qlate)

            m = None
            scores(0, bufs[0])
            for i, (keys, which, brows) in enumerate(tiles):
                if i + 1 < len(tiles):
                    scores(i + 1, bufs[(i + 1) % 2])
                nk = keys.stop - keys.start
                vt = vt_ref[bj, :, keys]
                if which is None:
                    s = bufs[i % 2][:nk, :]
                    if brows is not None:
                        s = s + band_ref[brows, :]
                    m_new = jnp.max(s, axis=0, keepdims=True)
                    if m is not None:
                        m_new = jnp.maximum(m, m_new)
                    update(jnp.exp2(s - m_new), vt, m, m_new, all_cols)
                    m = m_new
                else:
                    s = bufs[i % 2][:nk, :tq] + late(band_ref[brows, :])
                    ml = late(m)
                    m_new = jnp.maximum(ml, jnp.max(s, axis=0, keepdims=True))
                    update(jnp.exp2(s - m_new), vt, ml, m_new, late_cols)
            on = acc_ref[:LANES, :] * (1.0 / acc_ref[LANES:LANES + 1, :])
            o = on[:, :tq] - lam * on[:, tq:]
            ms = jnp.mean(o * o, axis=0, keepdims=True)
            o = o * lax.rsqrt(ms + NORM_EPS) * sub_ref[...] * (1.0 - lam_init)
            o_ref[bj, qrows, :] = o.T.astype(o_ref.dtype)


def _diff_attn(q3d, k3d, v3d, slopes, lam_params, subln_col, lam_init, tq):
    b, s, br_w = q3d.shape
    nh = br_w // LANES
    nb = ATT_BATCH
    seq_spec = pl.BlockSpec((nb, s, LANES), lambda h, bi: (bi, 0, h))
    return pl.pallas_call(
        functools.partial(_diff_attn_body, tq=tq, lam_init=lam_init),
        grid=(nh, b // nb),
        in_specs=[
            seq_spec, seq_spec, seq_spec,
            pl.BlockSpec((1, 1, LANES), lambda h, bi: (h, 0, 0)),
            pl.BlockSpec((4, HEAD_DIM), lambda h, bi: (0, 0)),
            pl.BlockSpec((LANES, 1), lambda h, bi: (0, 0)),
        ],
        out_specs=seq_spec,
        out_shape=jax.ShapeDtypeStruct((b, s, br_w), ACT_DTYPE),
        scratch_shapes=[pltpu.VMEM((nb, s, 2 * LANES), BF16), pltpu.VMEM((nb, LANES + ATT_V_PAD, s), BF16),
                        pltpu.VMEM((tq, 2 * tq), F32),
                        pltpu.VMEM((tq, 2 * tq), F32), pltpu.VMEM((tq, 2 * tq), F32),
                        pltpu.VMEM((LANES + ATT_V_PAD, 2 * tq), F32)],
        compiler_params=pltpu.CompilerParams(
            dimension_semantics=("arbitrary", "arbitrary"),
            vmem_limit_bytes=VMEM_LIMIT_BYTES),
        name="diff_attn",
    )(q3d, k3d, v3d, slopes, lam_params, subln_col)


def _out_body(ybr_ref, gate_ref, qm_ref, gm_ref, x_ref, km_ref, vm_ref, w_ref, pn_ref, o_ref, *, br_w):
    tm = x_ref.shape[0]
    ml = km_ref.shape[2] // MEM_HEADS
    subs = [slice(r * OUT_SUB_ROWS, (r + 1) * OUT_SUB_ROWS) for r in range(tm // OUT_SUB_ROWS)]
    qscale = HEAD_DIM ** -0.5 * math.log2(math.e)
    s = [_dot_nt((qm_ref[rs, :].astype(F32) * qscale).astype(BF16), km_ref[0, 0]) for rs in subs]
    y_br = (ybr_ref[...].astype(F32) * _silu(gate_ref[...].astype(F32))).astype(BF16)
    pcat = []
    for sr in s:
        ps = []
        for hd in range(MEM_HEADS):
            sh = sr[:, hd * ml:(hd + 1) * ml]
            p = jnp.exp2(sh - jnp.max(sh, axis=-1, keepdims=True))
            ps.append((p * (1.0 / jnp.sum(p, axis=-1, keepdims=True))).astype(BF16))
        pcat.append(jnp.concatenate(ps, axis=1))
    y_mem = [_dot(pc, vm_ref[0, 0]) for pc in pcat]
    y_mem = (jnp.concatenate(y_mem, axis=0) * _silu(gm_ref[...].astype(F32))).astype(BF16)
    y = _dot(y_br, w_ref[:br_w, :]) + _dot(y_mem, w_ref[br_w:, :])
    o_ref[...] = x_ref[...] + _rms(y, pn_ref[...])


def _out_proj(ybr, gate, qm, gm, x2d, k_mem, v_mem, layer, w_bf16, post_g, seq, tm):
    m, d = x2d.shape
    br_w = ybr.shape[1]
    ml = k_mem.shape[2]
    per_b = seq // tm
    mem_spec = pl.BlockSpec((1, 1, ml, MEM_W), lambda i: (layer, i // per_b, 0, 0))
    row = lambda w: pl.BlockSpec((tm, w), lambda i: (i, 0))
    return pl.pallas_call(
        functools.partial(_out_body, br_w=br_w),
        grid=(m // tm,),
        in_specs=[
            row(br_w), row(br_w), row(MEM_W), row(MEM_W), row(d),
            mem_spec, mem_spec,
            pl.BlockSpec((d, d), lambda i: (0, 0)),
            pl.BlockSpec((1, d), lambda i: (0, 0)),
        ],
        out_specs=row(d),
        out_shape=jax.ShapeDtypeStruct((m, d), F32),
        compiler_params=pltpu.CompilerParams(
            dimension_semantics=("arbitrary",), vmem_limit_bytes=VMEM_LIMIT_BYTES),
        name="out_proj",
    )(ybr, gate, qm, gm, x2d, k_mem, v_mem, w_bf16, post_g)


def _rwkv_params(mu, w0, w2, a0, a2, k_k, k_a, r_k, lnx_w, lnx_b, br_w):
    npairs = br_w // LANES
    rows = [mu[:br_w], mu[br_w:2 * br_w], mu[2 * br_w:3 * br_w], w0, a0, k_k, k_a,
            r_k.reshape(-1), lnx_w, lnx_b]
    pp = jnp.stack([r.reshape(npairs, LANES) for r in rows], axis=1)
    pp = jnp.pad(pp, ((0, 0), (0, _PP_ROWS - len(rows)), (0, 0)))
    mu_wa = mu[3 * br_w:].reshape(1, LANES)
    w2p = w2.reshape(LORA_W, npairs, LANES).transpose(1, 0, 2)
    a2p = a2.reshape(LORA_W, npairs, LANES).transpose(1, 0, 2)
    zeros = jnp.zeros_like(w2p)
    w2a2 = jnp.concatenate(
        [jnp.concatenate([w2p, zeros], axis=2), jnp.concatenate([zeros, a2p], axis=2)], axis=1)
    return pp, mu_wa, w2a2


def kernel(x, mem, pre_norm, post_norm, w_out, mem_norm, w_mem_kv, a_w_in, a_shift_mu, a_w0, a_w2,
           a_a0, a_a2, a_k_k, a_k_a, a_r_k, a_lnx_w, a_lnx_b, kv_norm, w_kv, b_w_in, b_lam_q1,
           b_lam_k1, b_lam_q2, b_lam_k2, b_subln):
    bsz, seq, d = x.shape
    depth = pre_norm.shape[0]
    n_a = a_w_in.shape[0]
    br_w = d - MEM_W
    a_shift = 3 * br_w + 2 * LORA_W
    m = bsz * seq
    tm = 1024
    x2d = x.reshape(m, d)
    slopes = jnp.asarray(
        np.repeat(np.array(_alibi_slopes(br_w // LANES), np.float32)[:, None, None], LANES, axis=2))

    k_mem, v_mem = _mem_kv(mem, mem_norm.reshape(depth, 1, d), w_mem_kv.astype(BF16))
    for l in range(depth):
        if l < n_a:
            i = l
            segs = [(0, 0, a_shift), (0, a_shift, a_shift + br_w),
                    (0, a_shift + br_w, a_shift + br_w + MEM_W),
                    (0, a_shift + br_w + MEM_W, a_shift + br_w + 2 * MEM_W)]
            z, gate, q_mem, g_mem = _norm_proj(
                x2d, pre_norm[l].reshape(1, d), a_w_in[i].astype(BF16), segs, tm)
            pp, mu_wa, w2a2 = _rwkv_params(
                a_shift_mu[i], a_w0[i], a_w2[i], a_a0[i], a_a2[i], a_k_k[i], a_k_a[i], a_r_k[i],
                a_lnx_w[i], a_lnx_b[i], br_w)
            y_br = _rwkv(z.reshape(bsz, seq, a_shift), pp, mu_wa, w2a2).reshape(m, br_w)
        else:
            i = l - n_a
            if l == n_a:
                w_cat = jnp.concatenate([b_w_in[i], w_kv], axis=1).astype(BF16)
                gains = jnp.stack([pre_norm[l], kv_norm], axis=0)
                segs = [(0, 0, br_w), (0, br_w, 2 * br_w), (0, 2 * br_w, 2 * br_w + MEM_W),
                        (0, 2 * br_w + MEM_W, 2 * br_w + 2 * MEM_W),
                        (1, 2 * br_w + 2 * MEM_W, 3 * br_w + 2 * MEM_W),
                        (1, 3 * br_w + 2 * MEM_W, 4 * br_w + 2 * MEM_W)]
                q, gate, q_mem, g_mem, k_sh, v_sh = _norm_proj(x2d, gains, w_cat, segs, tm)
                k_sh = k_sh.reshape(bsz, seq, br_w)
                v_sh = v_sh.reshape(bsz, seq, br_w)
            else:
                segs = [(0, 0, br_w), (0, br_w, 2 * br_w), (0, 2 * br_w, 2 * br_w + MEM_W),
                        (0, 2 * br_w + MEM_W, 2 * br_w + 2 * MEM_W)]
                q, gate, q_mem, g_mem = _norm_proj(
                    x2d, pre_norm[l].reshape(1, d), b_w_in[i].astype(BF16), segs, tm)
            lam_init = 0.8 - 0.6 * math.exp(-0.3 * l)
            lam_params = jnp.stack([b_lam_q1[i], b_lam_k1[i], b_lam_q2[i], b_lam_k2[i]], axis=0)
            y_br = _diff_attn(q.reshape(bsz, seq, br_w), k_sh, v_sh, slopes, lam_params,
                              b_subln[i].reshape(LANES, 1), lam_init, ATT_Q_TILE).reshape(m, br_w)
        x2d = _out_proj(y_br, gate, q_mem, g_mem, x2d, k_mem, v_mem, l, w_out[l].astype(BF16),
                        post_norm[l].reshape(1, d), seq, OUT_ROWS)
    return x2d.reshape(bsz, seq, d)
```

```python
import functools
import math

import numpy as np
import jax
import jax.numpy as jnp
from jax import lax
from jax.experimental import pallas as pl
from jax.experimental.pallas import tpu as pltpu

F32 = jnp.float32
BF16 = jnp.bfloat16
ACT_DTYPE = BF16

HEAD_DIM = 64
LANES = 128
SUBLANES = 8
MEM_HEADS = 4
MEM_W = MEM_HEADS * HEAD_DIM
LORA_W = 64
ATT_CHUNK = 64
ATT_Q_TILE = 512
ATT_BATCH = 2
ATT_BIAS_PIECES = 3
ATT_V_PAD = 16
RWKV_CHUNK = 64
RWKV_GROUP = 16
RWKV_CUMSUM_ROWS = 128
RWKV_BATCH = 2
OUT_ROWS = 1024
OUT_SUB_ROWS = 512
NORM_EPS = 1e-6
LNX_EPS = 64e-5
KK_NORM_FLOOR = 1e-12
MASK_VALUE = -1e30
VMEM_LIMIT_BYTES = 56 * 1024 * 1024


def _dot(a, b):
    return jnp.dot(a, b, preferred_element_type=F32)


def _dot_nt(a, b):
    return lax.dot_general(a, b, (((1,), (1,)), ((), ())), preferred_element_type=F32)


def _dot_tn(a, b):
    return lax.dot_general(a, b, (((0,), (0,)), ((), ())), preferred_element_type=F32)


def _split(x, pieces):
    out = []
    rem = x
    for i in range(pieces):
        p = rem.astype(BF16)
        out.append(p)
        if i + 1 < pieces:
            rem = rem - p.astype(F32)
    return out


def _mm(a, b, fn=_dot, pa=1, pb=1):
    aps = _split(a, pa)
    bps = _split(b, pb)
    order = max(pa, pb)
    acc = None
    for i, ap in enumerate(aps):
        for j, bp in enumerate(bps):
            if i + j < order:
                t = fn(ap, bp)
                acc = t if acc is None else acc + t
    return acc


def _rms(x, g):
    ms = jnp.mean(x * x, axis=-1, keepdims=True)
    return (x * lax.rsqrt(ms + NORM_EPS)) * g


def _silu(x):
    h = 0.5 * x
    return h + h * jnp.tanh(h)


def _norm_proj_body(x_ref, g_ref, w_ref, *o_refs, segments):
    x = x_ref[...]
    ms = jnp.mean(x * x, axis=-1, keepdims=True)
    xn = x * lax.rsqrt(ms + NORM_EPS)
    hs = {}
    for o_ref, (gi, lo, hi) in zip(o_refs, segments):
        if gi not in hs:
            hs[gi] = (xn * g_ref[gi:gi + 1, :]).astype(BF16)
        o_ref[...] = _dot(hs[gi], w_ref[:, lo:hi]).astype(o_ref.dtype)


def _norm_proj(x2d, gains, w_bf16, segments, tm):
    m, d = x2d.shape
    n = w_bf16.shape[1]
    ng = gains.shape[0]
    out_shape = [jax.ShapeDtypeStruct((m, hi - lo), ACT_DTYPE) for _, lo, hi in segments]
    out_specs = [pl.BlockSpec((tm, hi - lo), lambda i: (i, 0)) for _, lo, hi in segments]
    return pl.pallas_call(
        functools.partial(_norm_proj_body, segments=tuple(segments)),
        grid=(m // tm,),
        in_specs=[
            pl.BlockSpec((tm, d), lambda i: (i, 0)),
            pl.BlockSpec((ng, d), lambda i: (0, 0)),
            pl.BlockSpec((d, n), lambda i: (0, 0)),
        ],
        out_specs=out_specs,
        out_shape=out_shape,
        compiler_params=pltpu.CompilerParams(
            dimension_semantics=("arbitrary",), vmem_limit_bytes=VMEM_LIMIT_BYTES),
        name="norm_proj",
    )(x2d, gains, w_bf16)


def _mem_kv_body(mem_ref, g_ref, w_ref, k_ref, v_ref):
    ml = mem_ref.shape[1]
    h = _rms(mem_ref[0], g_ref[0]).astype(BF16)
    kv = _dot(h, w_ref[0])
    lane = lax.broadcasted_iota(jnp.int32, (ml, MEM_W), 1)
    for hd in range(MEM_HEADS):
        in_head = (lane >= hd * HEAD_DIM) & (lane < (hd + 1) * HEAD_DIM)
        k_ref[0, 0, hd * ml:(hd + 1) * ml, :] = jnp.where(in_head, kv[:, :MEM_W], 0.0).astype(k_ref.dtype)
        v_ref[0, 0, hd * ml:(hd + 1) * ml, :] = jnp.where(in_head, kv[:, MEM_W:], 0.0).astype(v_ref.dtype)


def _mem_kv(mem, gains, w_bf16):
    b, ml, d = mem.shape
    nl = gains.shape[0]
    out_spec = pl.BlockSpec((1, 1, MEM_HEADS * ml, MEM_W), lambda l, i: (l, i, 0, 0))
    return pl.pallas_call(
        _mem_kv_body,
        grid=(nl, b),
        in_specs=[
            pl.BlockSpec((1, ml, d), lambda l, i: (i, 0, 0)),
            pl.BlockSpec((1, 1, d), lambda l, i: (l, 0, 0)),
            pl.BlockSpec((1, d, 2 * MEM_W), lambda l, i: (l, 0, 0)),
        ],
        out_specs=[out_spec] * 2,
        out_shape=[jax.ShapeDtypeStruct((nl, b, MEM_HEADS * ml, MEM_W), ACT_DTYPE)] * 2,
        compiler_params=pltpu.CompilerParams(
            dimension_semantics=("arbitrary", "arbitrary"), vmem_limit_bytes=VMEM_LIMIT_BYTES),
        name="mem_kv",
    )(mem, gains, w_bf16)


def _rwkv_body(zr_ref, zk_ref, zv_ref, zwa_ref, mur_ref, muk_ref, muv_ref, muwa_ref, w0_ref, a0_ref,
               kk_ref, ka_ref, rk_ref, lnw_ref, lnb_ref, w2_ref, a2_ref, y_ref, s_ref, prev_ref, *, group):
    c = RWKV_CHUNK
    rows = group * c
    lane = lax.broadcasted_iota(jnp.int32, (c, LANES), 1)
    row = lax.broadcasted_iota(jnp.int32, (c, LANES), 0)
    head0 = lane < HEAD_DIM
    scol = jnp.where(head0, lane, lane - HEAD_DIM)
    strict = scol < row
    incl = scol <= row
    diag = scol == row
    r2 = lax.broadcasted_iota(jnp.int32, (LANES, LANES), 0)
    c2 = lax.broadcasted_iota(jnp.int32, (LANES, LANES), 1)
    blockmask = (r2 < HEAD_DIM) == (c2 < HEAD_DIM)
    blockones = jnp.where(blockmask, 1.0, 0.0).astype(BF16)
    cum_rows = min(rows, RWKV_CUMSUM_ROWS)
    tr = lax.broadcasted_iota(jnp.int32, (cum_rows, cum_rows), 0)
    tc = lax.broadcasted_iota(jnp.int32, (cum_rows, cum_rows), 1)
    tril_ones = jnp.where((tc <= tr) & (tc // c == tr // c), 1.0, 0.0).astype(BF16)
    slab_lane = lax.broadcasted_iota(jnp.int32, (rows, LANES), 1)
    slab_head0 = slab_lane < HEAD_DIM
    slab_first = lax.broadcasted_iota(jnp.int32, (rows, LANES), 0) == 0

    def bd(x):
        return jnp.concatenate([jnp.where(head0, x, 0.0), jnp.where(head0, 0.0, x)], axis=0)

    def seg_sum(x):
        return _dot(x.astype(BF16), blockones)

    mu_r, mu_k, mu_v, mu_wa = mur_ref[...], muk_ref[...], muv_ref[...], muwa_ref[...]
    w0, a0 = w0_ref[...], a0_ref[...]
    k_k, k_a, r_k = kk_ref[...], ka_ref[...], rk_ref[...]
    lnx_w, lnx_b = lnw_ref[...], lnb_ref[...]
    zeros = jnp.zeros((LORA_W, LANES), F32)
    w2a2 = jnp.concatenate([jnp.concatenate([w2_ref[...], zeros], axis=1),
                            jnp.concatenate([zeros, a2_ref[...]], axis=1)], axis=0)

    s_ref[...] = jnp.zeros_like(s_ref)
    prev_ref[...] = jnp.zeros_like(prev_ref)

    def shift_mix(ref, bj, slot, sl, mu):
        z = ref[bj, sl, :].astype(F32)
        zp = pltpu.roll(z, 1, axis=0)
        zp = jnp.where(slab_first, prev_ref[bj, slot:slot + 1, :], zp)
        prev_ref[bj, slot:slot + 1, :] = z[rows - 1:rows, :]
        return z + (zp - z) * mu


    def prepare(bj, sl):
        r = shift_mix(zr_ref, bj, 0, sl, mu_r)
        k = shift_mix(zk_ref, bj, 1, sl, mu_k)
        yield
        v = shift_mix(zv_ref, bj, 2, sl, mu_v)
        wa = shift_mix(zwa_ref, bj, 3, sl, mu_wa)
        lora = _mm(jnp.where(slab_head0, jnp.tanh(wa), wa), w2a2)
        yield
        wlog = w0 + lora[:, :LANES]
        nw = -wlog
        w = -(jnp.maximum(nw, 0.0) + jnp.log(1.0 + jnp.exp(-jnp.abs(nw)))) - 0.5
        logw = -jnp.exp(w)
        a = 1.0 / (1.0 + jnp.exp(-(a0 + lora[:, LANES:])))
        kk = k * k_k
        kk = kk * lax.rsqrt(jnp.maximum(seg_sum(kk * kk), KK_NORM_FLOOR))
        yield
        kmod = k * (1.0 + (a - 1.0) * k_a)
        alpha = -kk
        beta = kk * a
        l2 = jnp.concatenate(_split(logw, 2), axis=1)
        cum2 = jnp.concatenate(
            [_dot(tril_ones, l2[i:i + cum_rows]) for i in range(0, rows, cum_rows)], axis=0)
        cum = cum2[:, :LANES] + cum2[:, LANES:]
        yield
        e_neg = jnp.exp(-cum)
        at = alpha * jnp.exp(cum - logw)
        rt = r * jnp.exp(cum)
        bt = beta * e_neg
        kt = kmod * e_neg
        yield
        chunks = [slice(g * c, (g + 1) * c) for g in range(group)]
        cum_c = [cum[cs.stop - 1:cs.stop, :] for cs in chunks]
        e_end = [jnp.exp(cum_c[g] - cum[cs]) for g, cs in enumerate(chunks)]
        per = lambda x: [x[cs] for cs in chunks]
        return dict(
            at=per(at), rt=per(rt), bt=per(bt), kt=per(kt), v=per(v),
            bt_end=[beta[cs] * e_end[g] for g, cs in enumerate(chunks)],
            kt_end=[kmod[cs] * e_end[g] for g, cs in enumerate(chunks)],
            p_c=[jnp.exp(x) for x in cum_c], v_slab=v, rk=r * kmod * r_k)

    def chains(p):
        at, rt, bt, kt, bt_end, kt_end, v = (p[x] for x in ("at", "rt", "bt", "kt", "bt_end", "kt_end", "v"))
        n = range(group)
        lhs = [jnp.concatenate([at[g], rt[g]], axis=0) for g in n]
        xbk = [_mm(lhs[g], jnp.concatenate([bd(bt[g]), bd(kt[g])], axis=0), _dot_nt) for g in n]
        yield
        a_rb = [jnp.where(incl, xbk[g][c:, :LANES], 0.0) for g in n]
        akrk = [jnp.concatenate([jnp.where(strict, xbk[g][:c, LANES:], 0.0),
                                 jnp.where(incl, xbk[g][c:, LANES:], 0.0)], axis=0) for g in n]
        av = [_mm(akrk[g], bd(v[g])) for g in n]
        yield
        apow = [jnp.where(strict, xbk[g][:c, :LANES], 0.0) for g in n]
        tinv = [jnp.where(diag, 1.0, 0.0) + apow[g] for g in n]
        nfac = int(math.log2(c))
        for i in range(1, nfac):
            rhs = [[bd(apow[g])] + ([bd(tinv[g])] if i > 1 else []) for g in n]
            d = [_mm(apow[g], jnp.concatenate(rhs[g], axis=1)) for g in n]
            if i > 1:
                tinv = [tinv[g] + d[g][:, LANES:] for g in n]
            apow = [d[g][:, :LANES] for g in n]
            yield
        tinv = [tinv[g] + _mm(apow[g], bd(tinv[g])) for g in n]
        yield
        x = [_mm(tinv[g], jnp.concatenate([bd(at[g]), bd(av[g][:c])], axis=1)) for g in n]
        wmat = [x[g][:, :LANES] for g in n]
        u0 = [x[g][:, LANES:] for g in n]
        yield
        d2 = [_mm(a_rb[g], jnp.concatenate([bd(wmat[g]), bd(u0[g])], axis=1)) for g in n]
        rp = [rt[g] + d2[g][:, :LANES] for g in n]
        y0 = [d2[g][:, LANES:] + av[g][c:] for g in n]
        yield
        fold = lambda z: jnp.where(head0, z[:HEAD_DIM], z[HEAD_DIM:])
        gp = [fold(_mm(wmat[g], bt_end[g], _dot_tn)) for g in n]
        npart = [fold(_mm(jnp.concatenate([u0[g], v[g]], axis=0),
                          jnp.concatenate([bt_end[g], kt_end[g]], axis=0), _dot_tn)) for g in n]
        yield
        spans = [[(jnp.where(diag, gp[g] + p["p_c"][g], gp[g]), npart[g]) for g in n]]
        while len(spans[-1]) > 1:
            prev = spans[-1]
            nxt = []
            for i in range(0, len(prev), 2):
                (ma, na), (mb, nb) = prev[i], prev[i + 1]
                prod = _mm(jnp.concatenate([ma, na], axis=0), bd(mb))
                nxt.append((prod[:HEAD_DIM], prod[HEAD_DIM:] + nb))
            spans.append(nxt)
            yield
        return dict(rp=rp, y0=y0, spans=spans, v_slab=p["v_slab"], rk=p["rk"])

    def finish(t, bj, sl):
        spans = t["spans"]
        states = {0: s_ref[bj]}
        top = len(spans) - 1
        m_all, n_all = spans[top][0]
        s_ref[bj] = _mm(states[0], bd(m_all)) + n_all
        yield
        for level in range(top, 0, -1):
            width = 1 << level
            for lo in range(0, group, width):
                ma, na = spans[level - 1][lo >> (level - 1)]
                states[lo + width // 2] = _mm(states[lo], bd(ma)) + na
            yield
        y = jnp.concatenate(
            [_mm(t["rp"][g], bd(states[g]), _dot_nt) + t["y0"][g] for g in range(group)], axis=0)
        yield
        mean = seg_sum(y) * (1.0 / HEAD_DIM)
        yc = y - mean
        var = seg_sum(yc * yc) * (1.0 / HEAD_DIM)
        yield
        yn = yc * lax.rsqrt(var + LNX_EPS) * lnx_w + lnx_b
        bonus = seg_sum(t["rk"]) * t["v_slab"]
        y_ref[bj, sl, :] = (yn + bonus).astype(y_ref.dtype)

    def run(*gens):
        results = [None] * len(gens)
        live = list(range(len(gens)))
        while live:
            for i in list(live):
                try:
                    next(gens[i])
                except StopIteration as stop:
                    results[i] = stop.value
                    live.remove(i)
        return results

    nslabs = zr_ref.shape[1] // rows
    units = [(bj, slice(t * rows, (t + 1) * rows)) for t in range(nslabs) for bj in range(zr_ref.shape[0])]
    prepared, = run(prepare(*units[0]))
    done = None
    for u in range(len(units)):
        gens = [chains(prepared)]
        if u + 1 < len(units):
            gens.append(prepare(*units[u + 1]))
        if done is not None:
            gens.append(finish(done, *units[u - 1]))
        out = run(*gens)
        done = out[0]
        if u + 1 < len(units):
            prepared = out[1]
    run(finish(done, *units[-1]))


def _rwkv(z3d, mu, w0, a0, k_k, k_a, r_k, lnx_w, lnx_b, w2, a2):
    b, s, a_shift = z3d.shape
    br_w = (a_shift - 2 * LORA_W) // 3
    npairs = br_w // LANES
    vec = lambda off: pl.BlockSpec((1, LANES), lambda bi, hp: (0, off + hp))
    lora = pl.BlockSpec((LORA_W, LANES), lambda bi, hp: (0, hp))
    nb = RWKV_BATCH
    zspec = lambda off: pl.BlockSpec((nb, s, LANES), lambda bi, hp: (bi, 0, off + hp))
    return pl.pallas_call(
        functools.partial(_rwkv_body, group=RWKV_GROUP),
        grid=(b // nb, npairs),
        in_specs=[
            zspec(0), zspec(npairs), zspec(2 * npairs),
            pl.BlockSpec((nb, s, LANES), lambda bi, hp: (bi, 0, 3 * npairs)),
            vec(0), vec(npairs), vec(2 * npairs),
            pl.BlockSpec((1, LANES), lambda bi, hp: (0, 3 * npairs)),
            vec(0), vec(0), vec(0), vec(0), vec(0), vec(0), vec(0), lora, lora,
        ],
        out_specs=pl.BlockSpec((nb, s, LANES), lambda bi, hp: (bi, 0, hp)),
        out_shape=jax.ShapeDtypeStruct((b, s, br_w), ACT_DTYPE),
        scratch_shapes=[pltpu.VMEM((nb, HEAD_DIM, LANES), F32), pltpu.VMEM((nb, SUBLANES, LANES), F32)],
        compiler_params=pltpu.CompilerParams(
            dimension_semantics=("arbitrary", "arbitrary"), vmem_limit_bytes=VMEM_LIMIT_BYTES),
        name="rwkv7_scan",
    )(z3d, z3d, z3d, z3d, mu, mu, mu, mu, w0, a0, k_k, k_a, r_k, lnx_w, lnx_b, w2, a2)


def _alibi_slopes(n):
    def pow2(m):
        start = 2.0 ** (-8.0 / m)
        return [start ** (i + 1) for i in range(m)]
    if math.log2(n).is_integer():
        return pow2(n)
    cl = 2 ** int(math.floor(math.log2(n)))
    return pow2(cl) + pow2(2 * cl)[0::2][: n - cl]


def _diff_attn_body(q_ref, k_ref, v_ref, slope_ref, lamp_ref, sub_ref, o_ref,
                    kb_ref, vt_ref, band_ref, sa_ref, sb_ref, acc_ref, *, tq, lam_init):
    bi = pl.program_id(1)
    nb = q_ref.shape[0]
    tk = tq
    half = tq // 2
    nq = k_ref.shape[1] // tq
    log2e = math.log2(math.e)
    slope2 = slope_ref[0][:, :1] * log2e
    aug_lane = lax.broadcasted_iota(jnp.int32, (tk, LANES), 1)

    for bj in range(nb):
        for t in range(nq):
            rows = slice(t * tk, (t + 1) * tk)
            kb_ref[bj, rows, :LANES] = k_ref[bj, rows, :].astype(BF16)
            vt_ref[bj, :LANES, rows] = v_ref[bj, rows, :].astype(F32).T.astype(BF16)

    @pl.when(bi == 0)
    def _():
        ones_row = lax.broadcasted_iota(jnp.int32, (ATT_V_PAD, tk), 0) == 0
        for t in range(nq):
            rows = slice(t * tk, (t + 1) * tk)
            pos = (lax.broadcasted_iota(jnp.int32, (tk, LANES), 0) + t * tk).astype(F32)
            pieces = _split(slope2 * pos, ATT_BIAS_PIECES)
            aug = jnp.zeros((tk, LANES), F32)
            for i, piece in enumerate(pieces):
                aug = jnp.where(aug_lane == i, piece.astype(F32), aug)
            for bj in range(nb):
                kb_ref[bj, rows, LANES:] = aug.astype(BF16)
                vt_ref[bj, LANES:, rows] = jnp.where(ones_row, 1.0, 0.0).astype(BF16)
        kpos = lax.broadcasted_iota(jnp.int32, (tk, 2 * tq), 0).astype(F32)
        lane2 = lax.broadcasted_iota(jnp.int32, (1, 2 * tq), 1)
        qry = jnp.where(lane2 < tq, lane2, lane2 - tq)
        qlim = ((qry // ATT_CHUNK + 1) * ATT_CHUNK).astype(F32)
        band_ref[...] = jnp.where(
            kpos < qlim, (-2.0 * slope2) * jnp.maximum(kpos - qry.astype(F32), 0.0), MASK_VALUE)

    lane = lax.broadcasted_iota(jnp.int32, (2 * tq, LANES), 1)
    head0 = lane < HEAD_DIM
    first = lax.broadcasted_iota(jnp.int32, (2 * tq, LANES), 0) < tq
    ones_cols = jnp.where(lane < ATT_BIAS_PIECES, 1.0, 0.0)
    lp = lamp_ref[...]
    lam = (jnp.exp(jnp.sum(lp[0:1] * lp[1:2], axis=-1, keepdims=True))
           - jnp.exp(jnp.sum(lp[2:3] * lp[3:4], axis=-1, keepdims=True)) + lam_init)
    late = lambda x: jnp.concatenate([x[..., half:tq], x[..., tq + half:]], axis=-1)

    def update(p_rows, vt, m, m_new, cols):
        pv = _dot(vt, p_rows.astype(BF16))
        off = 0
        for cs in cols:
            w = cs.stop - cs.start
            if m is None:
                acc_ref[:, cs] = pv[:, off:off + w]
            else:
                acc_ref[:, cs] = jnp.exp2(m - m_new)[:, off:off + w] * acc_ref[:, cs] + pv[:, off:off + w]
            off += w

    def tiles_of(n):
        past = [(slice(j * tk, (j + 1) * tk), None, None) for j in range(n)]
        own0 = (slice(n * tq, n * tq + half), None, slice(0, half))
        own1 = (slice(n * tq + half, (n + 1) * tq), "late", slice(half, tq))
        return past + [own0, own1]

    all_cols = [slice(0, 2 * tq)]
    late_cols = [slice(half, tq), slice(tq + half, 2 * tq)]
    bufs = (sa_ref, sb_ref)
    for bj in range(nb):
        for n in range(nq):
            qrows = slice(n * tq, (n + 1) * tq)
            q = q_ref[bj, qrows, :].astype(F32) * (HEAD_DIM ** -0.5 * log2e)
            q2x = jnp.concatenate([q, q], axis=0)
            qcat = jnp.concatenate([jnp.where(first == head0, q2x, 0.0), ones_cols],
                                   axis=1).astype(BF16)
            qlate = jnp.concatenate([qcat[half:tq], qcat[tq + half:]], axis=0)
            tiles = tiles_of(n)

            def scores(i, dst_ref):
                keys, which, _ = tiles[i]
                nk = keys.stop - keys.start
                if which is None:
                    dst_ref[:nk, :] = _dot_nt(kb_ref[bj, keys, :], qcat)
                else:
                    dst_ref[:nk, :tq] = _dot_nt(kb_ref[bj, keys, :], qlate)

            m = None
            scores(0, bufs[0])
            for i, (keys, which, brows) in enumerate(tiles):
                if i + 1 < len(tiles):
                    scores(i + 1, bufs[(i + 1) % 2])
                nk = keys.stop - keys.start
                vt = vt_ref[bj, :, keys]
                if which is None:
                    s = bufs[i % 2][:nk, :]
                    if brows is not None:
                        s = s + band_ref[brows, :]
                    m_new = jnp.max(s, axis=0, keepdims=True)
                    if m is not None:
                        m_new = jnp.maximum(m, m_new)
                    update(jnp.exp2(s - m_new), vt, m, m_new, all_cols)
                    m = m_new
                else:
                    s = bufs[i % 2][:nk, :tq] + late(band_ref[brows, :])
                    ml = late(m)
                    m_new = jnp.maximum(ml, jnp.max(s, axis=0, keepdims=True))
                    update(jnp.exp2(s - m_new), vt, ml, m_new, late_cols)
            on = acc_ref[:LANES, :] * (1.0 / acc_ref[LANES:LANES + 1, :])
            o = on[:, :tq] - lam * on[:, tq:]
            ms = jnp.mean(o * o, axis=0, keepdims=True)
            o = o * lax.rsqrt(ms + NORM_EPS) * sub_ref[...] * (1.0 - lam_init)
            o_ref[bj, qrows, :] = o.T.astype(o_ref.dtype)


def _diff_attn(q3d, k3d, v3d, slopes, lam_params, subln_col, lam_init, tq):
    b, s, br_w = q3d.shape
    nh = br_w // LANES
    nb = ATT_BATCH
    seq_spec = pl.BlockSpec((nb, s, LANES), lambda h, bi: (bi, 0, h))
    return pl.pallas_call(
        functools.partial(_diff_attn_body, tq=tq, lam_init=lam_init),
        grid=(nh, b // nb),
        in_specs=[
            seq_spec, seq_spec, seq_spec,
            pl.BlockSpec((1, 1, LANES), lambda h, bi: (h, 0, 0)),
            pl.BlockSpec((4, HEAD_DIM), lambda h, bi: (0, 0)),
            pl.BlockSpec((LANES, 1), lambda h, bi: (0, 0)),
        ],
        out_specs=seq_spec,
        out_shape=jax.ShapeDtypeStruct((b, s, br_w), ACT_DTYPE),
        scratch_shapes=[pltpu.VMEM((nb, s, 2 * LANES), BF16), pltpu.VMEM((nb, LANES + ATT_V_PAD, s), BF16),
                        pltpu.VMEM((tq, 2 * tq), F32),
                        pltpu.VMEM((tq, 2 * tq), F32), pltpu.VMEM((tq, 2 * tq), F32),
                        pltpu.VMEM((LANES + ATT_V_PAD, 2 * tq), F32)],
        compiler_params=pltpu.CompilerParams(
            dimension_semantics=("arbitrary", "arbitrary"),
            vmem_limit_bytes=VMEM_LIMIT_BYTES),
        name="diff_attn",
    )(q3d, k3d, v3d, slopes, lam_params, subln_col)


def _out_body(ybr_ref, gate_ref, qm_ref, gm_ref, x_ref, km_ref, vm_ref, w_ref, pn_ref, o_ref, *, br_w):
    tm = x_ref.shape[0]
    ml = km_ref.shape[2] // MEM_HEADS
    subs = [slice(r * OUT_SUB_ROWS, (r + 1) * OUT_SUB_ROWS) for r in range(tm // OUT_SUB_ROWS)]
    qscale = HEAD_DIM ** -0.5 * math.log2(math.e)
    s = [_dot_nt((qm_ref[rs, :].astype(F32) * qscale).astype(BF16), km_ref[0, 0]) for rs in subs]
    y_br = (ybr_ref[...].astype(F32) * _silu(gate_ref[...].astype(F32))).astype(BF16)
    pcat = []
    for sr in s:
        ps = []
        for hd in range(MEM_HEADS):
            sh = sr[:, hd * ml:(hd + 1) * ml]
            p = jnp.exp2(sh - jnp.max(sh, axis=-1, keepdims=True))
            ps.append((p * (1.0 / jnp.sum(p, axis=-1, keepdims=True))).astype(BF16))
        pcat.append(jnp.concatenate(ps, axis=1))
    y_mem = [_dot(pc, vm_ref[0, 0]) for pc in pcat]
    y_mem = (jnp.concatenate(y_mem, axis=0) * _silu(gm_ref[...].astype(F32))).astype(BF16)
    y = _dot(y_br, w_ref[:br_w, :]) + _dot(y_mem, w_ref[br_w:, :])
    o_ref[...] = x_ref[...] + _rms(y, pn_ref[...])


def _out_proj(ybr, gate, qm, gm, x2d, k_mem, v_mem, layer, w_bf16, post_g, seq, tm):
    m, d = x2d.shape
    br_w = ybr.shape[1]
    ml = k_mem.shape[2]
    per_b = seq // tm
    mem_spec = pl.BlockSpec((1, 1, ml, MEM_W), lambda i: (layer, i // per_b, 0, 0))
    row = lambda w: pl.BlockSpec((tm, w), lambda i: (i, 0))
    return pl.pallas_call(
        functools.partial(_out_body, br_w=br_w),
        grid=(m // tm,),
        in_specs=[
            row(br_w), row(br_w), row(MEM_W), row(MEM_W), row(d),
            mem_spec, mem_spec,
            pl.BlockSpec((d, d), lambda i: (0, 0)),
            pl.BlockSpec((1, d), lambda i: (0, 0)),
        ],
        out_specs=row(d),
        out_shape=jax.ShapeDtypeStruct((m, d), F32),
        compiler_params=pltpu.CompilerParams(
            dimension_semantics=("arbitrary",), vmem_limit_bytes=VMEM_LIMIT_BYTES),
        name="out_proj",
    )(ybr, gate, qm, gm, x2d, k_mem, v_mem, w_bf16, post_g)


def kernel(x, mem, pre_norm, post_norm, w_out, mem_norm, w_mem_kv, a_w_in, a_shift_mu, a_w0, a_w2,
           a_a0, a_a2, a_k_k, a_k_a, a_r_k, a_lnx_w, a_lnx_b, kv_norm, w_kv, b_w_in, b_lam_q1,
           b_lam_k1, b_lam_q2, b_lam_k2, b_subln):
    bsz, seq, d = x.shape
    depth = pre_norm.shape[0]
    n_a = a_w_in.shape[0]
    br_w = d - MEM_W
    a_shift = 3 * br_w + 2 * LORA_W
    m = bsz * seq
    tm = 1024
    x2d = x.reshape(m, d)
    slopes = jnp.asarray(
        np.repeat(np.array(_alibi_slopes(br_w // LANES), np.float32)[:, None, None], LANES, axis=2))

    k_mem, v_mem = _mem_kv(mem, mem_norm.reshape(depth, 1, d), w_mem_kv.astype(BF16))
    for l in range(depth):
        if l < n_a:
            i = l
            segs = [(0, 0, a_shift), (0, a_shift, a_shift + br_w),
                    (0, a_shift + br_w, a_shift + br_w + MEM_W),
                    (0, a_shift + br_w + MEM_W, a_shift + br_w + 2 * MEM_W)]
            z, gate, q_mem, g_mem = _norm_proj(
                x2d, pre_norm[l].reshape(1, d), a_w_in[i].astype(BF16), segs, tm)
            row = lambda p: p[i].reshape(1, -1)
            y_br = _rwkv(z.reshape(bsz, seq, a_shift), row(a_shift_mu), row(a_w0), row(a_a0), row(a_k_k),
                         row(a_k_a), row(a_r_k), row(a_lnx_w), row(a_lnx_b), a_w2[i], a_a2[i]).reshape(m, br_w)
        else:
            i = l - n_a
            if l == n_a:
                w_cat = jnp.concatenate([b_w_in[i], w_kv], axis=1).astype(BF16)
                gains = jnp.stack([pre_norm[l], kv_norm], axis=0)
                segs = [(0, 0, br_w), (0, br_w, 2 * br_w), (0, 2 * br_w, 2 * br_w + MEM_W),
                        (0, 2 * br_w + MEM_W, 2 * br_w + 2 * MEM_W),
                        (1, 2 * br_w + 2 * MEM_W, 3 * br_w + 2 * MEM_W),
                        (1, 3 * br_w + 2 * MEM_W, 4 * br_w + 2 * MEM_W)]
                q, gate, q_mem, g_mem, k_sh, v_sh = _norm_proj(x2d, gains, w_cat, segs, tm)
                k_sh = k_sh.reshape(bsz, seq, br_w)
                v_sh = v_sh.reshape(bsz, seq, br_w)
            else:
                segs = [(0, 0, br_w), (0, br_w, 2 * br_w), (0, 2 * br_w, 2 * br_w + MEM_W),
                        (0, 2 * br_w + MEM_W, 2 * br_w + 2 * MEM_W)]
                q, gate, q_mem, g_mem = _norm_proj(
                    x2d, pre_norm[l].reshape(1, d), b_w_in[i].astype(BF16), segs, tm)
            lam_init = 0.8 - 0.6 * math.exp(-0.3 * l)
            lam_params = jnp.stack([b_lam_q1[i], b_lam_k1[i], b_lam_q2[i], b_lam_k2[i]], axis=0)
            y_br = _diff_attn(q.reshape(bsz, seq, br_w), k_sh, v_sh, slopes, lam_params,
                              b_subln[i].reshape(LANES, 1), lam_init, ATT_Q_TILE).reshape(m, br_w)
        x2d = _out_proj(y_br, gate, q_mem, g_mem, x2d, k_mem, v_mem, l, w_out[l].astype(BF16),
                        post_norm[l].reshape(1, d), seq, OUT_ROWS)
    return x2d.reshape(bsz, seq, d)
```

```python
import functools
import math

import numpy as np
import jax
import jax.numpy as jnp
from jax import lax
from jax.experimental import pallas as pl
from jax.experimental.pallas import tpu as pltpu

F32 = jnp.float32
BF16 = jnp.bfloat16
ACT_DTYPE = BF16

HEAD_DIM = 64
LANES = 128
SUBLANES = 8
MEM_HEADS = 4
MEM_W = MEM_HEADS * HEAD_DIM
LORA_W = 64
ATT_CHUNK = 64
ATT_Q_TILE = 512
ATT_BATCH = 2
ATT_BIAS_PIECES = 3
ATT_V_PAD = 16
RWKV_CHUNK = 64
RWKV_GROUP = 16
RWKV_CUMSUM_ROWS = 128
RWKV_BATCH = 2
OUT_ROWS = 1024
OUT_SUB_ROWS = 512
NORM_EPS = 1e-6
LNX_EPS = 64e-5
KK_NORM_FLOOR = 1e-12
MASK_VALUE = -1e30
VMEM_LIMIT_BYTES = 56 * 1024 * 1024


def _dot(a, b):
    return jnp.dot(a, b, preferred_element_type=F32)


def _dot_nt(a, b):
    return lax.dot_general(a, b, (((1,), (1,)), ((), ())), preferred_element_type=F32)


def _dot_tn(a, b):
    return lax.dot_general(a, b, (((0,), (0,)), ((), ())), preferred_element_type=F32)


def _split(x, pieces):
    out = []
    rem = x
    for i in range(pieces):
        p = rem.astype(BF16)
        out.append(p)
        if i + 1 < pieces:
            rem = rem - p.astype(F32)
    return out


def _mm(a, b, fn=_dot, pa=1, pb=1):
    aps = _split(a, pa)
    bps = _split(b, pb)
    order = max(pa, pb)
    acc = None
    for i, ap in enumerate(aps):
        for j, bp in enumerate(bps):
            if i + j < order:
                t = fn(ap, bp)
                acc = t if acc is None else acc + t
    return acc


def _rms(x, g):
    ms = jnp.mean(x * x, axis=-1, keepdims=True)
    return (x * lax.rsqrt(ms + NORM_EPS)) * g


def _silu(x):
    h = 0.5 * x
    return h + h * jnp.tanh(h)


def _norm_proj_body(x_ref, *refs, groups):
    ng = len(groups)
    o_refs = refs[2 * ng:]
    x = x_ref[...]
    ms = jnp.mean(x * x, axis=-1, keepdims=True)
    xn = x * lax.rsqrt(ms + NORM_EPS)
    k = 0
    for gi, cols in enumerate(groups):
        g_ref, w_ref = refs[2 * gi], refs[2 * gi + 1]
        h = (xn * g_ref[...]).astype(BF16)
        for lo, hi in cols:
            o_refs[k][...] = _dot(h, w_ref[:, lo:hi]).astype(o_refs[k].dtype)
            k += 1


def _norm_proj(x2d, groups, tm):
    m, d = x2d.shape
    cols = [c for _, _, cs in groups for c in cs]
    in_specs = [pl.BlockSpec((tm, d), lambda i: (i, 0))]
    operands = [x2d]
    for g, w, _ in groups:
        in_specs += [pl.BlockSpec((1, d), lambda i: (0, 0)), pl.BlockSpec(w.shape, lambda i: (0, 0))]
        operands += [g, w]
    return pl.pallas_call(
        functools.partial(_norm_proj_body, groups=tuple(tuple(cs) for _, _, cs in groups)),
        grid=(m // tm,),
        in_specs=in_specs,
        out_specs=[pl.BlockSpec((tm, hi - lo), lambda i: (i, 0)) for lo, hi in cols],
        out_shape=[jax.ShapeDtypeStruct((m, hi - lo), ACT_DTYPE) for lo, hi in cols],
        compiler_params=pltpu.CompilerParams(
            dimension_semantics=("arbitrary",), vmem_limit_bytes=VMEM_LIMIT_BYTES),
        name="norm_proj",
    )(*operands)


def _mem_kv_body(mem_ref, g_ref, w_ref, k_ref, v_ref):
    ml = mem_ref.shape[1]
    h = _rms(mem_ref[0], g_ref[0]).astype(BF16)
    kv = _dot(h, w_ref[0])
    lane = lax.broadcasted_iota(jnp.int32, (ml, MEM_W), 1)
    for hd in range(MEM_HEADS):
        in_head = (lane >= hd * HEAD_DIM) & (lane < (hd + 1) * HEAD_DIM)
        k_ref[0, 0, hd * ml:(hd + 1) * ml, :] = jnp.where(in_head, kv[:, :MEM_W], 0.0).astype(k_ref.dtype)
        v_ref[0, 0, hd * ml:(hd + 1) * ml, :] = jnp.where(in_head, kv[:, MEM_W:], 0.0).astype(v_ref.dtype)


def _mem_kv(mem, gains, w_bf16):
    b, ml, d = mem.shape
    nl = gains.shape[0]
    out_spec = pl.BlockSpec((1, 1, MEM_HEADS * ml, MEM_W), lambda l, i: (l, i, 0, 0))
    return pl.pallas_call(
        _mem_kv_body,
        grid=(nl, b),
        in_specs=[
            pl.BlockSpec((1, ml, d), lambda l, i: (i, 0, 0)),
            pl.BlockSpec((1, 1, d), lambda l, i: (l, 0, 0)),
            pl.BlockSpec((1, d, 2 * MEM_W), lambda l, i: (l, 0, 0)),
        ],
        out_specs=[out_spec] * 2,
        out_shape=[jax.ShapeDtypeStruct((nl, b, MEM_HEADS * ml, MEM_W), ACT_DTYPE)] * 2,
        compiler_params=pltpu.CompilerParams(
            dimension_semantics=("arbitrary", "arbitrary"), vmem_limit_bytes=VMEM_LIMIT_BYTES),
        name="mem_kv",
    )(mem, gains, w_bf16)


def _rwkv_body(zr_ref, zk_ref, zv_ref, zwa_ref, mur_ref, muk_ref, muv_ref, muwa_ref, w0_ref, a0_ref,
               kk_ref, ka_ref, rk_ref, lnw_ref, lnb_ref, w2_ref, a2_ref, y_ref, s_ref, prev_ref, *, group):
    c = RWKV_CHUNK
    rows = group * c
    lane = lax.broadcasted_iota(jnp.int32, (c, LANES), 1)
    row = lax.broadcasted_iota(jnp.int32, (c, LANES), 0)
    head0 = lane < HEAD_DIM
    scol = jnp.where(head0, lane, lane - HEAD_DIM)
    strict = scol < row
    incl = scol <= row
    diag = scol == row
    r2 = lax.broadcasted_iota(jnp.int32, (LANES, LANES), 0)
    c2 = lax.broadcasted_iota(jnp.int32, (LANES, LANES), 1)
    blockmask = (r2 < HEAD_DIM) == (c2 < HEAD_DIM)
    blockones = jnp.where(blockmask, 1.0, 0.0).astype(BF16)
    cum_rows = min(rows, RWKV_CUMSUM_ROWS)
    tr = lax.broadcasted_iota(jnp.int32, (cum_rows, cum_rows), 0)
    tc = lax.broadcasted_iota(jnp.int32, (cum_rows, cum_rows), 1)
    tril_ones = jnp.where((tc <= tr) & (tc // c == tr // c), 1.0, 0.0).astype(BF16)
    slab_lane = lax.broadcasted_iota(jnp.int32, (rows, LANES), 1)
    slab_head0 = slab_lane < HEAD_DIM
    slab_first = lax.broadcasted_iota(jnp.int32, (rows, LANES), 0) == 0

    def bd(x):
        return jnp.concatenate([jnp.where(head0, x, 0.0), jnp.where(head0, 0.0, x)], axis=0)

    def seg_sum(x):
        return _dot(x.astype(BF16), blockones)

    mu_r, mu_k, mu_v, mu_wa = mur_ref[...], muk_ref[...], muv_ref[...], muwa_ref[...]
    w0, a0 = w0_ref[...], a0_ref[...]
    k_k, k_a, r_k = kk_ref[...], ka_ref[...], rk_ref[...]
    lnx_w, lnx_b = lnw_ref[...], lnb_ref[...]
    zeros = jnp.zeros((LORA_W, LANES), F32)
    w2a2 = jnp.concatenate([jnp.concatenate([w2_ref[...], zeros], axis=1),
                            jnp.concatenate([zeros, a2_ref[...]], axis=1)], axis=0)

    s_ref[...] = jnp.zeros_like(s_ref)
    prev_ref[...] = jnp.zeros_like(prev_ref)

    def shift_mix(ref, bj, slot, sl, mu):
        z = ref[bj, sl, :].astype(F32)
        zp = pltpu.roll(z, 1, axis=0)
        zp = jnp.where(slab_first, prev_ref[bj, slot:slot + 1, :], zp)
        prev_ref[bj, slot:slot + 1, :] = z[rows - 1:rows, :]
        return z + (zp - z) * mu


    def prepare(bj, sl):
        r = shift_mix(zr_ref, bj, 0, sl, mu_r)
        k = shift_mix(zk_ref, bj, 1, sl, mu_k)
        yield
        v = shift_mix(zv_ref, bj, 2, sl, mu_v)
        wa = shift_mix(zwa_ref, bj, 3, sl, mu_wa)
        lora = _mm(jnp.where(slab_head0, jnp.tanh(wa), wa), w2a2)
        yield
        wlog = w0 + lora[:, :LANES]
        nw = -wlog
        w = -(jnp.maximum(nw, 0.0) + jnp.log(1.0 + jnp.exp(-jnp.abs(nw)))) - 0.5
        logw = -jnp.exp(w)
        a = 1.0 / (1.0 + jnp.exp(-(a0 + lora[:, LANES:])))
        kk = k * k_k
        kk = kk * lax.rsqrt(jnp.maximum(seg_sum(kk * kk), KK_NORM_FLOOR))
        yield
        kmod = k * (1.0 + (a - 1.0) * k_a)
        alpha = -kk
        beta = kk * a
        l2 = jnp.concatenate(_split(logw, 2), axis=1)
        cum2 = jnp.concatenate(
            [_dot(tril_ones, l2[i:i + cum_rows]) for i in range(0, rows, cum_rows)], axis=0)
        cum = cum2[:, :LANES] + cum2[:, LANES:]
        yield
        e_neg = jnp.exp(-cum)
        at = alpha * jnp.exp(cum - logw)
        rt = r * jnp.exp(cum)
        bt = beta * e_neg
        kt = kmod * e_neg
        yield
        chunks = [slice(g * c, (g + 1) * c) for g in range(group)]
        cum_c = [cum[cs.stop - 1:cs.stop, :] for cs in chunks]
        e_end = [jnp.exp(cum_c[g] - cum[cs]) for g, cs in enumerate(chunks)]
        per = lambda x: [x[cs] for cs in chunks]
        return dict(
            at=per(at), rt=per(rt), bt=per(bt), kt=per(kt), v=per(v),
            bt_end=[beta[cs] * e_end[g] for g, cs in enumerate(chunks)],
            kt_end=[kmod[cs] * e_end[g] for g, cs in enumerate(chunks)],
            p_c=[jnp.exp(x) for x in cum_c], v_slab=v, rk=r * kmod * r_k)

    def chains(p):
        at, rt, bt, kt, bt_end, kt_end, v = (p[x] for x in ("at", "rt", "bt", "kt", "bt_end", "kt_end", "v"))
        n = range(group)
        lhs = [jnp.concatenate([at[g], rt[g]], axis=0) for g in n]
        xbk = [_mm(lhs[g], jnp.concatenate([bd(bt[g]), bd(kt[g])], axis=0), _dot_nt) for g in n]
        yield
        a_rb = [jnp.where(incl, xbk[g][c:, :LANES], 0.0) for g in n]
        akrk = [jnp.concatenate([jnp.where(strict, xbk[g][:c, LANES:], 0.0),
                                 jnp.where(incl, xbk[g][c:, LANES:], 0.0)], axis=0) for g in n]
        av = [_mm(akrk[g], bd(v[g])) for g in n]
        yield
        apow = [jnp.where(strict, xbk[g][:c, :LANES], 0.0) for g in n]
        tinv = [jnp.where(diag, 1.0, 0.0) + apow[g] for g in n]
        nfac = int(math.log2(c))
        for i in range(1, nfac):
            rhs = [[bd(apow[g])] + ([bd(tinv[g])] if i > 1 else []) for g in n]
            d = [_mm(apow[g], jnp.concatenate(rhs[g], axis=1)) for g in n]
            if i > 1:
                tinv = [tinv[g] + d[g][:, LANES:] for g in n]
            apow = [d[g][:, :LANES] for g in n]
            yield
        tinv = [tinv[g] + _mm(apow[g], bd(tinv[g])) for g in n]
        yield
        x = [_mm(tinv[g], jnp.concatenate([bd(at[g]), bd(av[g][:c])], axis=1)) for g in n]
        wmat = [x[g][:, :LANES] for g in n]
        u0 = [x[g][:, LANES:] for g in n]
        yield
        d2 = [_mm(a_rb[g], jnp.concatenate([bd(wmat[g]), bd(u0[g])], axis=1)) for g in n]
        rp = [rt[g] + d2[g][:, :LANES] for g in n]
        y0 = [d2[g][:, LANES:] + av[g][c:] for g in n]
        yield
        fold = lambda z: jnp.where(head0, z[:HEAD_DIM], z[HEAD_DIM:])
        gp = [fold(_mm(wmat[g], bt_end[g], _dot_tn)) for g in n]
        npart = [fold(_mm(jnp.concatenate([u0[g], v[g]], axis=0),
                          jnp.concatenate([bt_end[g], kt_end[g]], axis=0), _dot_tn)) for g in n]
        yield
        spans = [[(jnp.where(diag, gp[g] + p["p_c"][g], gp[g]), npart[g]) for g in n]]
        while len(spans[-1]) > 1:
            prev = spans[-1]
            nxt = []
            for i in range(0, len(prev), 2):
                (ma, na), (mb, nb) = prev[i], prev[i + 1]
                prod = _mm(jnp.concatenate([ma, na], axis=0), bd(mb))
                nxt.append((prod[:HEAD_DIM], prod[HEAD_DIM:] + nb))
            spans.append(nxt)
            yield
        return dict(rp=rp, y0=y0, spans=spans, v_slab=p["v_slab"], rk=p["rk"])

    def finish(t, bj, sl):
        spans = t["spans"]
        states = {0: s_ref[bj]}
        top = len(spans) - 1
        m_all, n_all = spans[top][0]
        s_ref[bj] = _mm(states[0], bd(m_all)) + n_all
        yield
        for level in range(top, 0, -1):
            width = 1 << level
            for lo in range(0, group, width):
                ma, na = spans[level - 1][lo >> (level - 1)]
                states[lo + width // 2] = _mm(states[lo], bd(ma)) + na
            yield
        y = jnp.concatenate(
            [_mm(t["rp"][g], bd(states[g]), _dot_nt) + t["y0"][g] for g in range(group)], axis=0)
        yield
        mean = seg_sum(y) * (1.0 / HEAD_DIM)
        yc = y - mean
        var = seg_sum(yc * yc) * (1.0 / HEAD_DIM)
        yield
        yn = yc * lax.rsqrt(var + LNX_EPS) * lnx_w + lnx_b
        bonus = seg_sum(t["rk"]) * t["v_slab"]
        y_ref[bj, sl, :] = (yn + bonus).astype(y_ref.dtype)

    def run(*gens):
        results = [None] * len(gens)
        live = list(range(len(gens)))
        while live:
            for i in list(live):
                try:
                    next(gens[i])
                except StopIteration as stop:
                    results[i] = stop.value
                    live.remove(i)
        return results

    nslabs = zr_ref.shape[1] // rows
    units = [(bj, slice(t * rows, (t + 1) * rows)) for t in range(nslabs) for bj in range(zr_ref.shape[0])]
    prepared, = run(prepare(*units[0]))
    done = None
    for u in range(len(units)):
        gens = [chains(prepared)]
        if u + 1 < len(units):
            gens.append(prepare(*units[u + 1]))
        if done is not None:
            gens.append(finish(done, *units[u - 1]))
        out = run(*gens)
        done = out[0]
        if u + 1 < len(units):
            prepared = out[1]
    run(finish(done, *units[-1]))


def _rwkv(z3d, mu, w0, a0, k_k, k_a, r_k, lnx_w, lnx_b, w2, a2):
    b, s, a_shift = z3d.shape
    br_w = (a_shift - 2 * LORA_W) // 3
    npairs = br_w // LANES
    vec = lambda off: pl.BlockSpec((1, LANES), lambda bi, hp: (0, off + hp))
    lora = pl.BlockSpec((LORA_W, LANES), lambda bi, hp: (0, hp))
    nb = RWKV_BATCH
    assert b % nb == 0 and s % (RWKV_CHUNK * RWKV_GROUP) == 0
    zspec = lambda off: pl.BlockSpec((nb, s, LANES), lambda bi, hp: (bi, 0, off + hp))
    return pl.pallas_call(
        functools.partial(_rwkv_body, group=RWKV_GROUP),
        grid=(b // nb, npairs),
        in_specs=[
            zspec(0), zspec(npairs), zspec(2 * npairs),
            pl.BlockSpec((nb, s, LANES), lambda bi, hp: (bi, 0, 3 * npairs)),
            vec(0), vec(npairs), vec(2 * npairs),
            pl.BlockSpec((1, LANES), lambda bi, hp: (0, 3 * npairs)),
            vec(0), vec(0), vec(0), vec(0), vec(0), vec(0), vec(0), lora, lora,
        ],
        out_specs=pl.BlockSpec((nb, s, LANES), lambda bi, hp: (bi, 0, hp)),
        out_shape=jax.ShapeDtypeStruct((b, s, br_w), ACT_DTYPE),
        scratch_shapes=[pltpu.VMEM((nb, HEAD_DIM, LANES), F32), pltpu.VMEM((nb, SUBLANES, LANES), F32)],
        compiler_params=pltpu.CompilerParams(
            dimension_semantics=("arbitrary", "arbitrary"), vmem_limit_bytes=VMEM_LIMIT_BYTES),
        name="rwkv7_scan",
    )(z3d, z3d, z3d, z3d, mu, mu, mu, mu, w0, a0, k_k, k_a, r_k, lnx_w, lnx_b, w2, a2)


def _alibi_slopes(n):
    def pow2(m):
        start = 2.0 ** (-8.0 / m)
        return [start ** (i + 1) for i in range(m)]
    if math.log2(n).is_integer():
        return pow2(n)
    cl = 2 ** int(math.floor(math.log2(n)))
    return pow2(cl) + pow2(2 * cl)[0::2][: n - cl]


def _diff_attn_body(q_ref, k_ref, v_ref, slope_ref, lq1_ref, lk1_ref, lq2_ref, lk2_ref, sub_ref, o_ref,
                    kb_ref, vt_ref, band_ref, sa_ref, sb_ref, acc_ref, *, tq, lam_init):
    bi = pl.program_id(1)
    nb = q_ref.shape[0]
    tk = tq
    half = tq // 2
    nq = k_ref.shape[1] // tq
    log2e = math.log2(math.e)
    slope2 = slope_ref[0][:, :1] * log2e
    aug_lane = lax.broadcasted_iota(jnp.int32, (tk, LANES), 1)

    for bj in range(nb):
        for t in range(nq):
            rows = slice(t * tk, (t + 1) * tk)
            kb_ref[bj, rows, :LANES] = k_ref[bj, rows, :].astype(BF16)
            vt_ref[bj, :LANES, rows] = v_ref[bj, rows, :].astype(F32).T.astype(BF16)

    @pl.when(bi == 0)
    def _():
        ones_row = lax.broadcasted_iota(jnp.int32, (ATT_V_PAD, tk), 0) == 0
        for t in range(nq):
            rows = slice(t * tk, (t + 1) * tk)
            pos = (lax.broadcasted_iota(jnp.int32, (tk, LANES), 0) + t * tk).astype(F32)
            pieces = _split(slope2 * pos, ATT_BIAS_PIECES)
            aug = jnp.zeros((tk, LANES), F32)
            for i, piece in enumerate(pieces):
                aug = jnp.where(aug_lane == i, piece.astype(F32), aug)
            for bj in range(nb):
                kb_ref[bj, rows, LANES:] = aug.astype(BF16)
                vt_ref[bj, LANES:, rows] = jnp.where(ones_row, 1.0, 0.0).astype(BF16)
        kpos = lax.broadcasted_iota(jnp.int32, (tk, 2 * tq), 0).astype(F32)
        lane2 = lax.broadcasted_iota(jnp.int32, (1, 2 * tq), 1)
        qry = jnp.where(lane2 < tq, lane2, lane2 - tq)
        qlim = ((qry // ATT_CHUNK + 1) * ATT_CHUNK).astype(F32)
        band_ref[...] = jnp.where(
            kpos < qlim, (-2.0 * slope2) * jnp.maximum(kpos - qry.astype(F32), 0.0), MASK_VALUE)

    lane = lax.broadcasted_iota(jnp.int32, (2 * tq, LANES), 1)
    head0 = lane < HEAD_DIM
    first = lax.broadcasted_iota(jnp.int32, (2 * tq, LANES), 0) < tq
    ones_cols = jnp.where(lane < ATT_BIAS_PIECES, 1.0, 0.0)
    lam = (jnp.exp(jnp.sum(lq1_ref[...] * lk1_ref[...], axis=-1, keepdims=True))
           - jnp.exp(jnp.sum(lq2_ref[...] * lk2_ref[...], axis=-1, keepdims=True)) + lam_init)
    late = lambda x: jnp.concatenate([x[..., half:tq], x[..., tq + half:]], axis=-1)

    def update(p_rows, vt, m, m_new, cols):
        pv = _dot(vt, p_rows.astype(BF16))
        off = 0
        for cs in cols:
            w = cs.stop - cs.start
            if m is None:
                acc_ref[:, cs] = pv[:, off:off + w]
            else:
                acc_ref[:, cs] = jnp.exp2(m - m_new)[:, off:off + w] * acc_ref[:, cs] + pv[:, off:off + w]
            off += w

    def tiles_of(n):
        past = [(slice(j * tk, (j + 1) * tk), None, None) for j in range(n)]
        own0 = (slice(n * tq, n * tq + half), None, slice(0, half))
        own1 = (slice(n * tq + half, (n + 1) * tq), "late", slice(half, tq))
        return past + [own0, own1]

    all_cols = [slice(0, 2 * tq)]
    late_cols = [slice(half, tq), slice(tq + half, 2 * tq)]
    bufs = (sa_ref, sb_ref)
    for bj in range(nb):
        for n in range(nq):
            qrows = slice(n * tq, (n + 1) * tq)
            q = q_ref[bj, qrows, :].astype(F32) * (HEAD_DIM ** -0.5 * log2e)
            q2x = jnp.concatenate([q, q], axis=0)
            qcat = jnp.concatenate([jnp.where(first == head0, q2x, 0.0), ones_cols],
                                   axis=1).astype(BF16)
            qlate = jnp.concatenate([qcat[half:tq], qcat[tq + half:]], axis=0)
            tiles = tiles_of(n)

            def scores(i, dst_ref):
                keys, which, _ = tiles[i]
                nk = keys.stop - keys.start
                if which is None:
                    dst_ref[:nk, :] = _dot_nt(kb_ref[bj, keys, :], qcat)
                else:
                    dst_ref[:nk, :tq] = _dot_nt(kb_ref[bj, keys, :], qlate)

            m = None
            scores(0, bufs[0])
            for i, (keys, which, brows) in enumerate(tiles):
                if i + 1 < len(tiles):
                    scores(i + 1, bufs[(i + 1) % 2])
                nk = keys.stop - keys.start
                vt = vt_ref[bj, :, keys]
                if which is None:
                    s = bufs[i % 2][:nk, :]
                    if brows is not None:
                        s = s + band_ref[brows, :]
                    m_new = jnp.max(s, axis=0, keepdims=True)
                    if m is not None:
                        m_new = jnp.maximum(m, m_new)
                    update(jnp.exp2(s - m_new), vt, m, m_new, all_cols)
                    m = m_new
                else:
                    s = bufs[i % 2][:nk, :tq] + late(band_ref[brows, :])
                    ml = late(m)
                    m_new = jnp.maximum(ml, jnp.max(s, axis=0, keepdims=True))
                    update(jnp.exp2(s - m_new), vt, ml, m_new, late_cols)
            on = acc_ref[:LANES, :] * (1.0 / acc_ref[LANES:LANES + 1, :])
            o = on[:, :tq] - lam * on[:, tq:]
            ms = jnp.mean(o * o, axis=0, keepdims=True)
            o = o * lax.rsqrt(ms + NORM_EPS) * sub_ref[...] * (1.0 - lam_init)
            o_ref[bj, qrows, :] = o.T.astype(o_ref.dtype)


def _diff_attn(q3d, k3d, v3d, slopes, lam_vectors, subln_col, lam_init, tq):
    b, s, br_w = q3d.shape
    nh = br_w // LANES
    nb = ATT_BATCH
    assert b % nb == 0 and s % tq == 0
    lam_spec = pl.BlockSpec((1, HEAD_DIM), lambda h, bi: (0, 0))
    seq_spec = pl.BlockSpec((nb, s, LANES), lambda h, bi: (bi, 0, h))
    return pl.pallas_call(
        functools.partial(_diff_attn_body, tq=tq, lam_init=lam_init),
        grid=(nh, b // nb),
        in_specs=[
            seq_spec, seq_spec, seq_spec,
            pl.BlockSpec((1, 1, LANES), lambda h, bi: (h, 0, 0)),
            lam_spec, lam_spec, lam_spec, lam_spec,
            pl.BlockSpec((LANES, 1), lambda h, bi: (0, 0)),
        ],
        out_specs=seq_spec,
        out_shape=jax.ShapeDtypeStruct((b, s, br_w), ACT_DTYPE),
        scratch_shapes=[pltpu.VMEM((nb, s, 2 * LANES), BF16), pltpu.VMEM((nb, LANES + ATT_V_PAD, s), BF16),
                        pltpu.VMEM((tq, 2 * tq), F32),
                        pltpu.VMEM((tq, 2 * tq), F32), pltpu.VMEM((tq, 2 * tq), F32),
                        pltpu.VMEM((LANES + ATT_V_PAD, 2 * tq), F32)],
        compiler_params=pltpu.CompilerParams(
            dimension_semantics=("arbitrary", "arbitrary"),
            vmem_limit_bytes=VMEM_LIMIT_BYTES),
        name="diff_attn",
    )(q3d, k3d, v3d, slopes, *lam_vectors, subln_col)


def _out_body(ybr_ref, gate_ref, qm_ref, gm_ref, x_ref, km_ref, vm_ref, w_ref, pn_ref, o_ref, *, br_w):
    tm = x_ref.shape[0]
    ml = km_ref.shape[2] // MEM_HEADS
    subs = [slice(r * OUT_SUB_ROWS, (r + 1) * OUT_SUB_ROWS) for r in range(tm // OUT_SUB_ROWS)]
    qscale = HEAD_DIM ** -0.5 * math.log2(math.e)
    s = [_dot_nt((qm_ref[rs, :].astype(F32) * qscale).astype(BF16), km_ref[0, 0]) for rs in subs]
    y_br = (ybr_ref[...].astype(F32) * _silu(gate_ref[...].astype(F32))).astype(BF16)
    pcat = []
    for sr in s:
        ps = []
        for hd in range(MEM_HEADS):
            sh = sr[:, hd * ml:(hd + 1) * ml]
            p = jnp.exp2(sh - jnp.max(sh, axis=-1, keepdims=True))
            ps.append((p * (1.0 / jnp.sum(p, axis=-1, keepdims=True))).astype(BF16))
        pcat.append(jnp.concatenate(ps, axis=1))
    y_mem = [_dot(pc, vm_ref[0, 0]) for pc in pcat]
    y_mem = (jnp.concatenate(y_mem, axis=0) * _silu(gm_ref[...].astype(F32))).astype(BF16)
    y = _dot(y_br, w_ref[:br_w, :]) + _dot(y_mem, w_ref[br_w:, :])
    o_ref[...] = x_ref[...] + _rms(y, pn_ref[...])


def _out_proj(ybr, gate, qm, gm, x2d, k_mem, v_mem, layer, w_bf16, post_g, seq, tm):
    m, d = x2d.shape
    br_w = ybr.shape[1]
    ml = k_mem.shape[2]
    per_b = seq // tm
    mem_spec = pl.BlockSpec((1, 1, ml, MEM_W), lambda i: (layer, i // per_b, 0, 0))
    row = lambda w: pl.BlockSpec((tm, w), lambda i: (i, 0))
    return pl.pallas_call(
        functools.partial(_out_body, br_w=br_w),
        grid=(m // tm,),
        in_specs=[
            row(br_w), row(br_w), row(MEM_W), row(MEM_W), row(d),
            mem_spec, mem_spec,
            pl.BlockSpec((d, d), lambda i: (0, 0)),
            pl.BlockSpec((1, d), lambda i: (0, 0)),
        ],
        out_specs=row(d),
        out_shape=jax.ShapeDtypeStruct((m, d), F32),
        compiler_params=pltpu.CompilerParams(
            dimension_semantics=("arbitrary",), vmem_limit_bytes=VMEM_LIMIT_BYTES),
        name="out_proj",
    )(ybr, gate, qm, gm, x2d, k_mem, v_mem, w_bf16, post_g)


def kernel(x, mem, pre_norm, post_norm, w_out, mem_norm, w_mem_kv, a_w_in, a_shift_mu, a_w0, a_w2,
           a_a0, a_a2, a_k_k, a_k_a, a_r_k, a_lnx_w, a_lnx_b, kv_norm, w_kv, b_w_in, b_lam_q1,
           b_lam_k1, b_lam_q2, b_lam_k2, b_subln):
    bsz, seq, d = x.shape
    depth = pre_norm.shape[0]
    n_a = a_w_in.shape[0]
    br_w = d - MEM_W
    a_shift = 3 * br_w + 2 * LORA_W
    m = bsz * seq
    tm = 1024
    x2d = x.reshape(m, d)
    slopes = jnp.asarray(
        np.repeat(np.array(_alibi_slopes(br_w // LANES), np.float32)[:, None, None], LANES, axis=2))

    k_mem, v_mem = _mem_kv(mem, mem_norm.reshape(depth, 1, d), w_mem_kv.astype(BF16))
    for l in range(depth):
        if l < n_a:
            i = l
            cols = [(0, a_shift), (a_shift, a_shift + br_w), (a_shift + br_w, a_shift + br_w + MEM_W),
                    (a_shift + br_w + MEM_W, a_shift + br_w + 2 * MEM_W)]
            z, gate, q_mem, g_mem = _norm_proj(
                x2d, [(pre_norm[l].reshape(1, d), a_w_in[i].astype(BF16), cols)], tm)
            row = lambda p: p[i].reshape(1, -1)
            y_br = _rwkv(z.reshape(bsz, seq, a_shift), row(a_shift_mu), row(a_w0), row(a_a0), row(a_k_k),
                         row(a_k_a), row(a_r_k), row(a_lnx_w), row(a_lnx_b), a_w2[i], a_a2[i]).reshape(m, br_w)
        else:
            i = l - n_a
            cols = [(0, br_w), (br_w, 2 * br_w), (2 * br_w, 2 * br_w + MEM_W),
                    (2 * br_w + MEM_W, 2 * br_w + 2 * MEM_W)]
            groups = [(pre_norm[l].reshape(1, d), b_w_in[i].astype(BF16), cols)]
            if l == n_a:
                groups.append((kv_norm.reshape(1, d), w_kv.astype(BF16), [(0, br_w), (br_w, 2 * br_w)]))
                q, gate, q_mem, g_mem, k_sh, v_sh = _norm_proj(x2d, groups, tm)
                k_sh = k_sh.reshape(bsz, seq, br_w)
                v_sh = v_sh.reshape(bsz, seq, br_w)
            else:
                q, gate, q_mem, g_mem = _norm_proj(x2d, groups, tm)
            lam_init = 0.8 - 0.6 * math.exp(-0.3 * l)
            lam_vectors = [p[i].reshape(1, HEAD_DIM) for p in (b_lam_q1, b_lam_k1, b_lam_q2, b_lam_k2)]
            y_br = _diff_attn(q.reshape(bsz, seq, br_w), k_sh, v_sh, slopes, lam_vectors,
                              b_subln[i].reshape(LANES, 1), lam_init, ATT_Q_TILE).reshape(m, br_w)
        x2d = _out_proj(y_br, gate, q_mem, g_mem, x2d, k_mem, v_mem, l, w_out[l].astype(BF16),
                        post_norm[l].reshape(1, d), seq, OUT_ROWS)
    return x2d.reshape(bsz, seq, d)
```

```python
import functools
import math

import numpy as np
import jax
import jax.numpy as jnp
from jax import lax
from jax.experimental import pallas as pl
from jax.experimental.pallas import tpu as pltpu

F32 = jnp.float32
BF16 = jnp.bfloat16
ACT_DTYPE = BF16

HEAD_DIM = 64
LANES = 128
SUBLANES = 8
MEM_HEADS = 4
MEM_W = MEM_HEADS * HEAD_DIM
LORA_W = 64
ATT_CHUNK = 64
ATT_Q_TILE = 512
ATT_BATCH = 2
ATT_BIAS_PIECES = 3
ATT_V_PAD = 16
RWKV_CHUNK = 64
RWKV_GROUP = 16
RWKV_CUMSUM_ROWS = 128
RWKV_BATCH = 2
OUT_ROWS = 1024
OUT_SUB_ROWS = 512
NORM_EPS = 1e-6
LNX_EPS = 64e-5
KK_NORM_FLOOR = 1e-12
MASK_VALUE = -1e30
VMEM_LIMIT_BYTES = 56 * 1024 * 1024


def _dot(a, b):
    return jnp.dot(a, b, preferred_element_type=F32)


def _dot_nt(a, b):
    return lax.dot_general(a, b, (((1,), (1,)), ((), ())), preferred_element_type=F32)


def _dot_tn(a, b):
    return lax.dot_general(a, b, (((0,), (0,)), ((), ())), preferred_element_type=F32)


def _split(x, pieces):
    out = []
    rem = x
    for i in range(pieces):
        p = rem.astype(BF16)
        out.append(p)
        if i + 1 < pieces:
            rem = rem - p.astype(F32)
    return out


def _mm(a, b, fn=_dot, pa=1, pb=1):
    aps = _split(a, pa)
    bps = _split(b, pb)
    order = max(pa, pb)
    acc = None
    for i, ap in enumerate(aps):
        for j, bp in enumerate(bps):
            if i + j < order:
                t = fn(ap, bp)
                acc = t if acc is None else acc + t
    return acc


def _rms(x, g):
    ms = jnp.mean(x * x, axis=-1, keepdims=True)
    return (x * lax.rsqrt(ms + NORM_EPS)) * g


def _silu(x):
    h = 0.5 * x
    return h + h * jnp.tanh(h)


def _norm_proj_body(x_ref, *refs, groups):
    ng = len(groups)
    nout = sum(len(cols) for cols in groups)
    o_refs = refs[2 * ng:2 * ng + nout]
    wb_refs = refs[2 * ng + nout:]

    @pl.when(pl.program_id(0) == 0)
    def _():
        for gi in range(ng):
            wb_refs[gi][...] = refs[2 * gi + 1][...].astype(BF16)

    x = x_ref[...]
    ms = jnp.mean(x * x, axis=-1, keepdims=True)
    xn = x * lax.rsqrt(ms + NORM_EPS)
    k = 0
    for gi, cols in enumerate(groups):
        h = (xn * refs[2 * gi][...]).astype(BF16)
        for lo, hi in cols:
            o_refs[k][...] = _dot(h, wb_refs[gi][:, lo:hi]).astype(o_refs[k].dtype)
            k += 1


def _norm_proj(x2d, groups, tm):
    m, d = x2d.shape
    cols = [c for _, _, cs in groups for c in cs]
    in_specs = [pl.BlockSpec((tm, d), lambda i: (i, 0))]
    operands = [x2d]
    for g, w, _ in groups:
        in_specs += [pl.BlockSpec((1, d), lambda i: (0, 0)), pl.BlockSpec(w.shape, lambda i: (0, 0))]
        operands += [g, w]
    return pl.pallas_call(
        functools.partial(_norm_proj_body, groups=tuple(tuple(cs) for _, _, cs in groups)),
        grid=(m // tm,),
        in_specs=in_specs,
        out_specs=[pl.BlockSpec((tm, hi - lo), lambda i: (i, 0)) for lo, hi in cols],
        out_shape=[jax.ShapeDtypeStruct((m, hi - lo), ACT_DTYPE) for lo, hi in cols],
        scratch_shapes=[pltpu.VMEM(w.shape, BF16) for _, w, _ in groups],
        compiler_params=pltpu.CompilerParams(
            dimension_semantics=("arbitrary",), vmem_limit_bytes=VMEM_LIMIT_BYTES),
        name="norm_proj",
    )(*operands)


def _mem_kv_body(mem_ref, g_ref, w_ref, k_ref, v_ref):
    ml = mem_ref.shape[1]
    h = _rms(mem_ref[0], g_ref[0]).astype(BF16)
    kv = _dot(h, w_ref[0].astype(BF16))
    lane = lax.broadcasted_iota(jnp.int32, (ml, MEM_W), 1)
    for hd in range(MEM_HEADS):
        in_head = (lane >= hd * HEAD_DIM) & (lane < (hd + 1) * HEAD_DIM)
        k_ref[0, 0, hd * ml:(hd + 1) * ml, :] = jnp.where(in_head, kv[:, :MEM_W], 0.0).astype(k_ref.dtype)
        v_ref[0, 0, hd * ml:(hd + 1) * ml, :] = jnp.where(in_head, kv[:, MEM_W:], 0.0).astype(v_ref.dtype)


def _mem_kv(mem, gains, w):
    b, ml, d = mem.shape
    nl = gains.shape[0]
    out_spec = pl.BlockSpec((1, 1, MEM_HEADS * ml, MEM_W), lambda l, i: (l, i, 0, 0))
    return pl.pallas_call(
        _mem_kv_body,
        grid=(nl, b),
        in_specs=[
            pl.BlockSpec((1, ml, d), lambda l, i: (i, 0, 0)),
            pl.BlockSpec((1, 1, d), lambda l, i: (l, 0, 0)),
            pl.BlockSpec((1, d, 2 * MEM_W), lambda l, i: (l, 0, 0)),
        ],
        out_specs=[out_spec] * 2,
        out_shape=[jax.ShapeDtypeStruct((nl, b, MEM_HEADS * ml, MEM_W), ACT_DTYPE)] * 2,
        compiler_params=pltpu.CompilerParams(
            dimension_semantics=("arbitrary", "arbitrary"), vmem_limit_bytes=VMEM_LIMIT_BYTES),
        name="mem_kv",
    )(mem, gains, w)


def _rwkv_body(zr_ref, zk_ref, zv_ref, zwa_ref, mur_ref, muk_ref, muv_ref, muwa_ref, w0_ref, a0_ref,
               kk_ref, ka_ref, rk_ref, lnw_ref, lnb_ref, w2_ref, a2_ref, y_ref, s_ref, prev_ref, *, group):
    c = RWKV_CHUNK
    rows = group * c
    lane = lax.broadcasted_iota(jnp.int32, (c, LANES), 1)
    row = lax.broadcasted_iota(jnp.int32, (c, LANES), 0)
    head0 = lane < HEAD_DIM
    scol = jnp.where(head0, lane, lane - HEAD_DIM)
    strict = scol < row
    incl = scol <= row
    diag = scol == row
    r2 = lax.broadcasted_iota(jnp.int32, (LANES, LANES), 0)
    c2 = lax.broadcasted_iota(jnp.int32, (LANES, LANES), 1)
    blockmask = (r2 < HEAD_DIM) == (c2 < HEAD_DIM)
    blockones = jnp.where(blockmask, 1.0, 0.0).astype(BF16)
    cum_rows = min(rows, RWKV_CUMSUM_ROWS)
    tr = lax.broadcasted_iota(jnp.int32, (cum_rows, cum_rows), 0)
    tc = lax.broadcasted_iota(jnp.int32, (cum_rows, cum_rows), 1)
    tril_ones = jnp.where((tc <= tr) & (tc // c == tr // c), 1.0, 0.0).astype(BF16)
    slab_lane = lax.broadcasted_iota(jnp.int32, (rows, LANES), 1)
    slab_head0 = slab_lane < HEAD_DIM
    slab_first = lax.broadcasted_iota(jnp.int32, (rows, LANES), 0) == 0

    def bd(x):
        return jnp.concatenate([jnp.where(head0, x, 0.0), jnp.where(head0, 0.0, x)], axis=0)

    def seg_sum(x):
        return _dot(x.astype(BF16), blockones)

    mu_r, mu_k, mu_v, mu_wa = mur_ref[...], muk_ref[...], muv_ref[...], muwa_ref[...]
    w0, a0 = w0_ref[...], a0_ref[...]
    k_k, k_a, r_k = kk_ref[...], ka_ref[...], rk_ref[...]
    lnx_w, lnx_b = lnw_ref[...], lnb_ref[...]
    zeros = jnp.zeros((LORA_W, LANES), F32)
    w2a2 = jnp.concatenate([jnp.concatenate([w2_ref[...], zeros], axis=1),
                            jnp.concatenate([zeros, a2_ref[...]], axis=1)], axis=0)

    s_ref[...] = jnp.zeros_like(s_ref)
    prev_ref[...] = jnp.zeros_like(prev_ref)

    def shift_mix(ref, bj, slot, sl, mu):
        z = ref[bj, sl, :].astype(F32)
        zp = pltpu.roll(z, 1, axis=0)
        zp = jnp.where(slab_first, prev_ref[bj, slot:slot + 1, :], zp)
        prev_ref[bj, slot:slot + 1, :] = z[rows - 1:rows, :]
        return z + (zp - z) * mu


    def prepare(bj, sl):
        r = shift_mix(zr_ref, bj, 0, sl, mu_r)
        k = shift_mix(zk_ref, bj, 1, sl, mu_k)
        yield
        v = shift_mix(zv_ref, bj, 2, sl, mu_v)
        wa = shift_mix(zwa_ref, bj, 3, sl, mu_wa)
        lora = _mm(jnp.where(slab_head0, jnp.tanh(wa), wa), w2a2)
        yield
        wlog = w0 + lora[:, :LANES]
        nw = -wlog
        w = -(jnp.maximum(nw, 0.0) + jnp.log(1.0 + jnp.exp(-jnp.abs(nw)))) - 0.5
        logw = -jnp.exp(w)
        a = 1.0 / (1.0 + jnp.exp(-(a0 + lora[:, LANES:])))
        kk = k * k_k
        kk = kk * lax.rsqrt(jnp.maximum(seg_sum(kk * kk), KK_NORM_FLOOR))
        yield
        kmod = k * (1.0 + (a - 1.0) * k_a)
        alpha = -kk
        beta = kk * a
        l2 = jnp.concatenate(_split(logw, 2), axis=1)
        cum2 = jnp.concatenate(
            [_dot(tril_ones, l2[i:i + cum_rows]) for i in range(0, rows, cum_rows)], axis=0)
        cum = cum2[:, :LANES] + cum2[:, LANES:]
        yield
        e_neg = jnp.exp(-cum)
        at = alpha * jnp.exp(cum - logw)
        rt = r * jnp.exp(cum)
        bt = beta * e_neg
        kt = kmod * e_neg
        yield
        chunks = [slice(g * c, (g + 1) * c) for g in range(group)]
        cum_c = [cum[cs.stop - 1:cs.stop, :] for cs in chunks]
        e_end = [jnp.exp(cum_c[g] - cum[cs]) for g, cs in enumerate(chunks)]
        per = lambda x: [x[cs] for cs in chunks]
        return dict(
            at=per(at), rt=per(rt), bt=per(bt), kt=per(kt), v=per(v),
            bt_end=[beta[cs] * e_end[g] for g, cs in enumerate(chunks)],
            kt_end=[kmod[cs] * e_end[g] for g, cs in enumerate(chunks)],
            p_c=[jnp.exp(x) for x in cum_c], v_slab=v, rk=r * kmod * r_k)

    def chains(p):
        at, rt, bt, kt, bt_end, kt_end, v = (p[x] for x in ("at", "rt", "bt", "kt", "bt_end", "kt_end", "v"))
        n = range(group)
        lhs = [jnp.concatenate([at[g], rt[g]], axis=0) for g in n]
        xbk = [_mm(lhs[g], jnp.concatenate([bd(bt[g]), bd(kt[g])], axis=0), _dot_nt) for g in n]
        yield
        a_rb = [jnp.where(incl, xbk[g][c:, :LANES], 0.0) for g in n]
        akrk = [jnp.concatenate([jnp.where(strict, xbk[g][:c, LANES:], 0.0),
                                 jnp.where(incl, xbk[g][c:, LANES:], 0.0)], axis=0) for g in n]
        av = [_mm(akrk[g], bd(v[g])) for g in n]
        yield
        apow = [jnp.where(strict, xbk[g][:c, :LANES], 0.0) for g in n]
        tinv = [jnp.where(diag, 1.0, 0.0) + apow[g] for g in n]
        nfac = int(math.log2(c))
        for i in range(1, nfac):
            rhs = [[bd(apow[g])] + ([bd(tinv[g])] if i > 1 else []) for g in n]
            d = [_mm(apow[g], jnp.concatenate(rhs[g], axis=1)) for g in n]
            if i > 1:
                tinv = [tinv[g] + d[g][:, LANES:] for g in n]
            apow = [d[g][:, :LANES] for g in n]
            yield
        tinv = [tinv[g] + _mm(apow[g], bd(tinv[g])) for g in n]
        yield
        x = [_mm(tinv[g], jnp.concatenate([bd(at[g]), bd(av[g][:c])], axis=1)) for g in n]
        wmat = [x[g][:, :LANES] for g in n]
        u0 = [x[g][:, LANES:] for g in n]
        yield
        d2 = [_mm(a_rb[g], jnp.concatenate([bd(wmat[g]), bd(u0[g])], axis=1)) for g in n]
        rp = [rt[g] + d2[g][:, :LANES] for g in n]
        y0 = [d2[g][:, LANES:] + av[g][c:] for g in n]
        yield
        fold = lambda z: jnp.where(head0, z[:HEAD_DIM], z[HEAD_DIM:])
        gp = [fold(_mm(wmat[g], bt_end[g], _dot_tn)) for g in n]
        npart = [fold(_mm(jnp.concatenate([u0[g], v[g]], axis=0),
                          jnp.concatenate([bt_end[g], kt_end[g]], axis=0), _dot_tn)) for g in n]
        yield
        spans = [[(jnp.where(diag, gp[g] + p["p_c"][g], gp[g]), npart[g]) for g in n]]
        while len(spans[-1]) > 1:
            prev = spans[-1]
            nxt = []
            for i in range(0, len(prev), 2):
                (ma, na), (mb, nb) = prev[i], prev[i + 1]
                prod = _mm(jnp.concatenate([ma, na], axis=0), bd(mb))
                nxt.append((prod[:HEAD_DIM], prod[HEAD_DIM:] + nb))
            spans.append(nxt)
            yield
        return dict(rp=rp, y0=y0, spans=spans, v_slab=p["v_slab"], rk=p["rk"])

    def finish(t, bj, sl):
        spans = t["spans"]
        states = {0: s_ref[bj]}
        top = len(spans) - 1
        m_all, n_all = spans[top][0]
        s_ref[bj] = _mm(states[0], bd(m_all)) + n_all
        yield
        for level in range(top, 0, -1):
            width = 1 << level
            for lo in range(0, group, width):
                ma, na = spans[level - 1][lo >> (level - 1)]
                states[lo + width // 2] = _mm(states[lo], bd(ma)) + na
            yield
        y = jnp.concatenate(
            [_mm(t["rp"][g], bd(states[g]), _dot_nt) + t["y0"][g] for g in range(group)], axis=0)
        yield
        mean = seg_sum(y) * (1.0 / HEAD_DIM)
        yc = y - mean
        var = seg_sum(yc * yc) * (1.0 / HEAD_DIM)
        yield
        yn = yc * lax.rsqrt(var + LNX_EPS) * lnx_w + lnx_b
        bonus = seg_sum(t["rk"]) * t["v_slab"]
        y_ref[bj, sl, :] = (yn + bonus).astype(y_ref.dtype)

    def run(*gens):
        results = [None] * len(gens)
        live = list(range(len(gens)))
        while live:
            for i in list(live):
                try:
                    next(gens[i])
                except StopIteration as stop:
                    results[i] = stop.value
                    live.remove(i)
        return results

    nslabs = zr_ref.shape[1] // rows
    units = [(bj, slice(t * rows, (t + 1) * rows)) for t in range(nslabs) for bj in range(zr_ref.shape[0])]
    prepared, = run(prepare(*units[0]))
    done = None
    for u in range(len(units)):
        gens = [chains(prepared)]
        if u + 1 < len(units):
            gens.append(prepare(*units[u + 1]))
        if done is not None:
            gens.append(finish(done, *units[u - 1]))
        out = run(*gens)
        done = out[0]
        if u + 1 < len(units):
            prepared = out[1]
    run(finish(done, *units[-1]))


def _rwkv(z3d, mu, w0, a0, k_k, k_a, r_k, lnx_w, lnx_b, w2, a2):
    b, s, a_shift = z3d.shape
    br_w = (a_shift - 2 * LORA_W) // 3
    npairs = br_w // LANES
    vec = lambda off: pl.BlockSpec((1, LANES), lambda bi, hp: (0, off + hp))
    lora = pl.BlockSpec((LORA_W, LANES), lambda bi, hp: (0, hp))
    nb = RWKV_BATCH
    assert b % nb == 0 and s % (RWKV_CHUNK * RWKV_GROUP) == 0
    zspec = lambda off: pl.BlockSpec((nb, s, LANES), lambda bi, hp: (bi, 0, off + hp))
    return pl.pallas_call(
        functools.partial(_rwkv_body, group=RWKV_GROUP),
        grid=(b // nb, npairs),
        in_specs=[
            zspec(0), zspec(npairs), zspec(2 * npairs),
            pl.BlockSpec((nb, s, LANES), lambda bi, hp: (bi, 0, 3 * npairs)),
            vec(0), vec(npairs), vec(2 * npairs),
            pl.BlockSpec((1, LANES), lambda bi, hp: (0, 3 * npairs)),
            vec(0), vec(0), vec(0), vec(0), vec(0), vec(0), vec(0), lora, lora,
        ],
        out_specs=pl.BlockSpec((nb, s, LANES), lambda bi, hp: (bi, 0, hp)),
        out_shape=jax.ShapeDtypeStruct((b, s, br_w), ACT_DTYPE),
        scratch_shapes=[pltpu.VMEM((nb, HEAD_DIM, LANES), F32), pltpu.VMEM((nb, SUBLANES, LANES), F32)],
        compiler_params=pltpu.CompilerParams(
            dimension_semantics=("arbitrary", "arbitrary"), vmem_limit_bytes=VMEM_LIMIT_BYTES),
        name="rwkv7_scan",
    )(z3d, z3d, z3d, z3d, mu, mu, mu, mu, w0, a0, k_k, k_a, r_k, lnx_w, lnx_b, w2, a2)


def _alibi_slopes(n):
    def pow2(m):
        start = 2.0 ** (-8.0 / m)
        return [start ** (i + 1) for i in range(m)]
    if math.log2(n).is_integer():
        return pow2(n)
    cl = 2 ** int(math.floor(math.log2(n)))
    return pow2(cl) + pow2(2 * cl)[0::2][: n - cl]


def _diff_attn_body(q_ref, k_ref, v_ref, slope_ref, lq1_ref, lk1_ref, lq2_ref, lk2_ref, sub_ref, o_ref,
                    kb_ref, vt_ref, band_ref, sa_ref, sb_ref, acc_ref, *, tq, lam_init):
    bi = pl.program_id(1)
    nb = q_ref.shape[0]
    tk = tq
    half = tq // 2
    nq = k_ref.shape[1] // tq
    log2e = math.log2(math.e)
    slope2 = slope_ref[0][:, :1] * log2e
    aug_lane = lax.broadcasted_iota(jnp.int32, (tk, LANES), 1)

    for bj in range(nb):
        for t in range(nq):
            rows = slice(t * tk, (t + 1) * tk)
            kb_ref[bj, rows, :LANES] = k_ref[bj, rows, :].astype(BF16)
            vt_ref[bj, :LANES, rows] = v_ref[bj, rows, :].astype(F32).T.astype(BF16)

    @pl.when(bi == 0)
    def _():
        ones_row = lax.broadcasted_iota(jnp.int32, (ATT_V_PAD, tk), 0) == 0
        for t in range(nq):
            rows = slice(t * tk, (t + 1) * tk)
            pos = (lax.broadcasted_iota(jnp.int32, (tk, LANES), 0) + t * tk).astype(F32)
            pieces = _split(slope2 * pos, ATT_BIAS_PIECES)
            aug = jnp.zeros((tk, LANES), F32)
            for i, piece in enumerate(pieces):
                aug = jnp.where(aug_lane == i, piece.astype(F32), aug)
            for bj in range(nb):
                kb_ref[bj, rows, LANES:] = aug.astype(BF16)
                vt_ref[bj, LANES:, rows] = jnp.where(ones_row, 1.0, 0.0).astype(BF16)
        kpos = lax.broadcasted_iota(jnp.int32, (tk, 2 * tq), 0).astype(F32)
        lane2 = lax.broadcasted_iota(jnp.int32, (1, 2 * tq), 1)
        qry = jnp.where(lane2 < tq, lane2, lane2 - tq)
        qlim = ((qry // ATT_CHUNK + 1) * ATT_CHUNK).astype(F32)
        band_ref[...] = jnp.where(
            kpos < qlim, (-2.0 * slope2) * jnp.maximum(kpos - qry.astype(F32), 0.0), MASK_VALUE)

    lane = lax.broadcasted_iota(jnp.int32, (2 * tq, LANES), 1)
    head0 = lane < HEAD_DIM
    first = lax.broadcasted_iota(jnp.int32, (2 * tq, LANES), 0) < tq
    ones_cols = jnp.where(lane < ATT_BIAS_PIECES, 1.0, 0.0)
    lam = (jnp.exp(jnp.sum(lq1_ref[...] * lk1_ref[...], axis=-1, keepdims=True))
           - jnp.exp(jnp.sum(lq2_ref[...] * lk2_ref[...], axis=-1, keepdims=True)) + lam_init)
    late = lambda x: jnp.concatenate([x[..., half:tq], x[..., tq + half:]], axis=-1)

    def update(p_rows, vt, m, m_new, cols):
        pv = _dot(vt, p_rows.astype(BF16))
        off = 0
        for cs in cols:
            w = cs.stop - cs.start
            if m is None:
                acc_ref[:, cs] = pv[:, off:off + w]
            else:
                acc_ref[:, cs] = jnp.exp2(m - m_new)[:, off:off + w] * acc_ref[:, cs] + pv[:, off:off + w]
            off += w

    def tiles_of(n):
        past = [(slice(j * tk, (j + 1) * tk), None, None) for j in range(n)]
        own0 = (slice(n * tq, n * tq + half), None, slice(0, half))
        own1 = (slice(n * tq + half, (n + 1) * tq), "late", slice(half, tq))
        return past + [own0, own1]

    all_cols = [slice(0, 2 * tq)]
    late_cols = [slice(half, tq), slice(tq + half, 2 * tq)]
    bufs = (sa_ref, sb_ref)
    for bj in range(nb):
        for n in range(nq):
            qrows = slice(n * tq, (n + 1) * tq)
            q = q_ref[bj, qrows, :].astype(F32) * (HEAD_DIM ** -0.5 * log2e)
            q2x = jnp.concatenate([q, q], axis=0)
            qcat = jnp.concatenate([jnp.where(first == head0, q2x, 0.0), ones_cols],
                                   axis=1).astype(BF16)
            qlate = jnp.concatenate([qcat[half:tq], qcat[tq + half:]], axis=0)
            tiles = tiles_of(n)

            def scores(i, dst_ref):
                keys, which, _ = tiles[i]
                nk = keys.stop - keys.start
                if which is None:
                    dst_ref[:nk, :] = _dot_nt(kb_ref[bj, keys, :], qcat)
                else:
                    dst_ref[:nk, :tq] = _dot_nt(kb_ref[bj, keys, :], qlate)

            m = None
            scores(0, bufs[0])
            for i, (keys, which, brows) in enumerate(tiles):
                if i + 1 < len(tiles):
                    scores(i + 1, bufs[(i + 1) % 2])
                nk = keys.stop - keys.start
                vt = vt_ref[bj, :, keys]
                if which is None:
                    s = bufs[i % 2][:nk, :]
                    if brows is not None:
                        s = s + band_ref[brows, :]
                    m_new = jnp.max(s, axis=0, keepdims=True)
                    if m is not None:
                        m_new = jnp.maximum(m, m_new)
                    update(jnp.exp2(s - m_new), vt, m, m_new, all_cols)
                    m = m_new
                else:
                    s = bufs[i % 2][:nk, :tq] + late(band_ref[brows, :])
                    ml = late(m)
                    m_new = jnp.maximum(ml, jnp.max(s, axis=0, keepdims=True))
                    update(jnp.exp2(s - m_new), vt, ml, m_new, late_cols)
            on = acc_ref[:LANES, :] * (1.0 / acc_ref[LANES:LANES + 1, :])
            o = on[:, :tq] - lam * on[:, tq:]
            ms = jnp.mean(o * o, axis=0, keepdims=True)
            o = o * lax.rsqrt(ms + NORM_EPS) * sub_ref[...] * (1.0 - lam_init)
            o_ref[bj, qrows, :] = o.T.astype(o_ref.dtype)


def _diff_attn(q3d, k3d, v3d, slopes, lam_vectors, subln_col, lam_init, tq):
    b, s, br_w = q3d.shape
    nh = br_w // LANES
    nb = ATT_BATCH
    assert b % nb == 0 and s % tq == 0
    lam_spec = pl.BlockSpec((1, HEAD_DIM), lambda h, bi: (0, 0))
    seq_spec = pl.BlockSpec((nb, s, LANES), lambda h, bi: (bi, 0, h))
    return pl.pallas_call(
        functools.partial(_diff_attn_body, tq=tq, lam_init=lam_init),
        grid=(nh, b // nb),
        in_specs=[
            seq_spec, seq_spec, seq_spec,
            pl.BlockSpec((1, 1, LANES), lambda h, bi: (h, 0, 0)),
            lam_spec, lam_spec, lam_spec, lam_spec,
            pl.BlockSpec((LANES, 1), lambda h, bi: (0, 0)),
        ],
        out_specs=seq_spec,
        out_shape=jax.ShapeDtypeStruct((b, s, br_w), ACT_DTYPE),
        scratch_shapes=[pltpu.VMEM((nb, s, 2 * LANES), BF16), pltpu.VMEM((nb, LANES + ATT_V_PAD, s), BF16),
                        pltpu.VMEM((tq, 2 * tq), F32),
                        pltpu.VMEM((tq, 2 * tq), F32), pltpu.VMEM((tq, 2 * tq), F32),
                        pltpu.VMEM((LANES + ATT_V_PAD, 2 * tq), F32)],
        compiler_params=pltpu.CompilerParams(
            dimension_semantics=("arbitrary", "arbitrary"),
            vmem_limit_bytes=VMEM_LIMIT_BYTES),
        name="diff_attn",
    )(q3d, k3d, v3d, slopes, *lam_vectors, subln_col)


def _out_body(ybr_ref, gate_ref, qm_ref, gm_ref, x_ref, km_ref, vm_ref, w_ref, pn_ref, o_ref, wb_ref, *, br_w):
    @pl.when(pl.program_id(0) == 0)
    def _():
        wb_ref[...] = w_ref[0].astype(BF16)

    tm = x_ref.shape[0]
    ml = km_ref.shape[2] // MEM_HEADS
    subs = [slice(r * OUT_SUB_ROWS, (r + 1) * OUT_SUB_ROWS) for r in range(tm // OUT_SUB_ROWS)]
    qscale = HEAD_DIM ** -0.5 * math.log2(math.e)
    s = [_dot_nt((qm_ref[rs, :].astype(F32) * qscale).astype(BF16), km_ref[0, 0]) for rs in subs]
    y_br = (ybr_ref[...].astype(F32) * _silu(gate_ref[...].astype(F32))).astype(BF16)
    pcat = []
    for sr in s:
        ps = []
        for hd in range(MEM_HEADS):
            sh = sr[:, hd * ml:(hd + 1) * ml]
            p = jnp.exp2(sh - jnp.max(sh, axis=-1, keepdims=True))
            ps.append((p * (1.0 / jnp.sum(p, axis=-1, keepdims=True))).astype(BF16))
        pcat.append(jnp.concatenate(ps, axis=1))
    y_mem = [_dot(pc, vm_ref[0, 0]) for pc in pcat]
    y_mem = (jnp.concatenate(y_mem, axis=0) * _silu(gm_ref[...].astype(F32))).astype(BF16)
    y = _dot(y_br, wb_ref[:br_w, :]) + _dot(y_mem, wb_ref[br_w:, :])
    o_ref[...] = x_ref[...] + _rms(y, pn_ref[...])


def _out_proj(ybr, gate, qm, gm, x2d, k_mem, v_mem, layer, w, post_g, seq, tm):
    m, d = x2d.shape
    br_w = ybr.shape[1]
    ml = k_mem.shape[2]
    per_b = seq // tm
    mem_spec = pl.BlockSpec((1, 1, ml, MEM_W), lambda i: (layer, i // per_b, 0, 0))
    row = lambda w: pl.BlockSpec((tm, w), lambda i: (i, 0))
    return pl.pallas_call(
        functools.partial(_out_body, br_w=br_w),
        grid=(m // tm,),
        in_specs=[
            row(br_w), row(br_w), row(MEM_W), row(MEM_W), row(d),
            mem_spec, mem_spec,
            pl.BlockSpec((1, d, d), lambda i: (layer, 0, 0)),
            pl.BlockSpec((1, d), lambda i: (0, 0)),
        ],
        out_specs=row(d),
        out_shape=jax.ShapeDtypeStruct((m, d), F32),
        scratch_shapes=[pltpu.VMEM((d, d), BF16)],
        compiler_params=pltpu.CompilerParams(
            dimension_semantics=("arbitrary",), vmem_limit_bytes=VMEM_LIMIT_BYTES),
        name="out_proj",
    )(ybr, gate, qm, gm, x2d, k_mem, v_mem, w, post_g)


def kernel(x, mem, pre_norm, post_norm, w_out, mem_norm, w_mem_kv, a_w_in, a_shift_mu, a_w0, a_w2,
           a_a0, a_a2, a_k_k, a_k_a, a_r_k, a_lnx_w, a_lnx_b, kv_norm, w_kv, b_w_in, b_lam_q1,
           b_lam_k1, b_lam_q2, b_lam_k2, b_subln):
    bsz, seq, d = x.shape
    depth = pre_norm.shape[0]
    n_a = a_w_in.shape[0]
    br_w = d - MEM_W
    a_shift = 3 * br_w + 2 * LORA_W
    m = bsz * seq
    tm = 1024
    x2d = x.reshape(m, d)
    slopes = jnp.asarray(
        np.repeat(np.array(_alibi_slopes(br_w // LANES), np.float32)[:, None, None], LANES, axis=2))

    k_mem, v_mem = _mem_kv(mem, mem_norm.reshape(depth, 1, d), w_mem_kv)
    for l in range(depth):
        if l < n_a:
            i = l
            cols = [(0, a_shift), (a_shift, a_shift + br_w), (a_shift + br_w, a_shift + br_w + MEM_W),
                    (a_shift + br_w + MEM_W, a_shift + br_w + 2 * MEM_W)]
            z, gate, q_mem, g_mem = _norm_proj(
                x2d, [(pre_norm[l].reshape(1, d), a_w_in[i], cols)], tm)
            row = lambda p: p[i].reshape(1, -1)
            y_br = _rwkv(z.reshape(bsz, seq, a_shift), row(a_shift_mu), row(a_w0), row(a_a0), row(a_k_k),
                         row(a_k_a), row(a_r_k), row(a_lnx_w), row(a_lnx_b), a_w2[i], a_a2[i]).reshape(m, br_w)
        else:
            i = l - n_a
            cols = [(0, br_w), (br_w, 2 * br_w), (2 * br_w, 2 * br_w + MEM_W),
                    (2 * br_w + MEM_W, 2 * br_w + 2 * MEM_W)]
            groups = [(pre_norm[l].reshape(1, d), b_w_in[i], cols)]
            if l == n_a:
                groups.append((kv_norm.reshape(1, d), w_kv, [(0, br_w), (br_w, 2 * br_w)]))
                q, gate, q_mem, g_mem, k_sh, v_sh = _norm_proj(x2d, groups, tm)
                k_sh = k_sh.reshape(bsz, seq, br_w)
                v_sh = v_sh.reshape(bsz, seq, br_w)
            else:
                q, gate, q_mem, g_mem = _norm_proj(x2d, groups, tm)
            lam_init = 0.8 - 0.6 * math.exp(-0.3 * l)
            lam_vectors = [p[i].reshape(1, HEAD_DIM) for p in (b_lam_q1, b_lam_k1, b_lam_q2, b_lam_k2)]
            y_br = _diff_attn(q.reshape(bsz, seq, br_w), k_sh, v_sh, slopes, lam_vectors,
                              b_subln[i].reshape(LANES, 1), lam_init, ATT_Q_TILE).reshape(m, br_w)
        x2d = _out_proj(y_br, gate, q_mem, g_mem, x2d, k_mem, v_mem, l, w_out,
                        post_norm[l].reshape(1, d), seq, OUT_ROWS)
    return x2d.reshape(bsz, seq, d)
```

```python
import functools
import math

import numpy as np
import jax
import jax.numpy as jnp
from jax import lax
from jax.experimental import pallas as pl
from jax.experimental.pallas import tpu as pltpu

F32 = jnp.float32
BF16 = jnp.bfloat16
ACT_DTYPE = BF16

HEAD_DIM = 64
LANES = 128
SUBLANES = 8
MEM_HEADS = 4
MEM_W = MEM_HEADS * HEAD_DIM
LORA_W = 64
ATT_CHUNK = 64
ATT_Q_TILE = 512
ATT_BATCH = 2
ATT_BIAS_PIECES = 3
ATT_V_PAD = 16
RWKV_CHUNK = 64
RWKV_GROUP = 16
RWKV_CUMSUM_ROWS = 128
RWKV_BATCH = 2
OUT_ROWS = 1024
OUT_SUB_ROWS = 512
NORM_EPS = 1e-6
LNX_EPS = 64e-5
KK_NORM_FLOOR = 1e-12
MASK_VALUE = -1e30
VMEM_LIMIT_BYTES = 56 * 1024 * 1024


def _dot(a, b):
    return jnp.dot(a, b, preferred_element_type=F32)


def _dot_nt(a, b):
    return lax.dot_general(a, b, (((1,), (1,)), ((), ())), preferred_element_type=F32)


def _dot_tn(a, b):
    return lax.dot_general(a, b, (((0,), (0,)), ((), ())), preferred_element_type=F32)


def _split(x, pieces):
    out = []
    rem = x
    for i in range(pieces):
        p = rem.astype(BF16)
        out.append(p)
        if i + 1 < pieces:
            rem = rem - p.astype(F32)
    return out


def _mm(a, b, fn=_dot, pa=1, pb=1):
    aps = _split(a, pa)
    bps = _split(b, pb)
    order = max(pa, pb)
    acc = None
    for i, ap in enumerate(aps):
        for j, bp in enumerate(bps):
            if i + j < order:
                t = fn(ap, bp)
                acc = t if acc is None else acc + t
    return acc


def _rms(x, g):
    ms = jnp.mean(x * x, axis=-1, keepdims=True)
    return (x * lax.rsqrt(ms + NORM_EPS)) * g


def _silu(x):
    h = 0.5 * x
    return h + h * jnp.tanh(h)


def _norm_proj_body(x_ref, *refs, groups):
    ng = len(groups)
    nout = sum(len(cols) for cols in groups)
    o_refs = refs[2 * ng:2 * ng + nout]
    wb_refs = refs[2 * ng + nout:]

    @pl.when(pl.program_id(0) == 0)
    def _():
        for gi in range(ng):
            wb_refs[gi][...] = refs[2 * gi + 1][...].astype(BF16)

    x = x_ref[...]
    ms = jnp.mean(x * x, axis=-1, keepdims=True)
    xn = x * lax.rsqrt(ms + NORM_EPS)
    k = 0
    for gi, cols in enumerate(groups):
        h = (xn * refs[2 * gi][0]).astype(BF16)
        for lo, hi in cols:
            o_refs[k][...] = _dot(h, wb_refs[gi][:, lo:hi]).astype(o_refs[k].dtype)
            k += 1


def _norm_proj(x2d, groups, tm):
    m, d = x2d.shape
    cols = [c for _, _, cs in groups for c in cs]
    in_specs = [pl.BlockSpec((tm, d), lambda i: (i, 0))]
    operands = [x2d]
    for (g, row), w, _ in groups:
        in_specs += [pl.BlockSpec((1, 1, d), lambda i, row=row: (row, 0, 0)), pl.BlockSpec(w.shape, lambda i: (0, 0))]
        operands += [g, w]
    return pl.pallas_call(
        functools.partial(_norm_proj_body, groups=tuple(tuple(cs) for _, _, cs in groups)),
        grid=(m // tm,),
        in_specs=in_specs,
        out_specs=[pl.BlockSpec((tm, hi - lo), lambda i: (i, 0)) for lo, hi in cols],
        out_shape=[jax.ShapeDtypeStruct((m, hi - lo), ACT_DTYPE) for lo, hi in cols],
        scratch_shapes=[pltpu.VMEM(w.shape, BF16) for _, w, _ in groups],
        compiler_params=pltpu.CompilerParams(
            dimension_semantics=("arbitrary",), vmem_limit_bytes=VMEM_LIMIT_BYTES),
        name="norm_proj",
    )(*operands)


def _mem_kv_body(mem_ref, g_ref, w_ref, k_ref, v_ref):
    nb, ml, d = mem_ref.shape
    h = _rms(mem_ref[...].reshape(nb * ml, d), g_ref[0]).astype(BF16)
    kv = _dot(h, w_ref[0].astype(BF16))
    lane = lax.broadcasted_iota(jnp.int32, (ml, MEM_W), 1)
    for bj in range(nb):
        rows = slice(bj * ml, (bj + 1) * ml)
        for hd in range(MEM_HEADS):
            in_head = (lane >= hd * HEAD_DIM) & (lane < (hd + 1) * HEAD_DIM)
            k_ref[0, bj, hd * ml:(hd + 1) * ml, :] = jnp.where(in_head, kv[rows, :MEM_W], 0.0).astype(k_ref.dtype)
            v_ref[0, bj, hd * ml:(hd + 1) * ml, :] = jnp.where(in_head, kv[rows, MEM_W:], 0.0).astype(v_ref.dtype)


def _mem_kv(mem, gains, w):
    b, ml, d = mem.shape
    nl = gains.shape[0]
    out_spec = pl.BlockSpec((1, b, MEM_HEADS * ml, MEM_W), lambda l: (l, 0, 0, 0))
    return pl.pallas_call(
        _mem_kv_body,
        grid=(nl,),
        in_specs=[
            pl.BlockSpec((b, ml, d), lambda l: (0, 0, 0)),
            pl.BlockSpec((1, 1, d), lambda l: (l, 0, 0)),
            pl.BlockSpec((1, d, 2 * MEM_W), lambda l: (l, 0, 0)),
        ],
        out_specs=[out_spec] * 2,
        out_shape=[jax.ShapeDtypeStruct((nl, b, MEM_HEADS * ml, MEM_W), ACT_DTYPE)] * 2,
        compiler_params=pltpu.CompilerParams(
            dimension_semantics=("arbitrary",), vmem_limit_bytes=VMEM_LIMIT_BYTES),
        name="mem_kv",
    )(mem, gains, w)


def _rwkv_body(zr_ref, zk_ref, zv_ref, zwa_ref, mur_ref, muk_ref, muv_ref, muwa_ref, w0_ref, a0_ref,
               kk_ref, ka_ref, rk_ref, lnw_ref, lnb_ref, w2_ref, a2_ref, y_ref, s_ref, prev_ref, *, group):
    c = RWKV_CHUNK
    rows = group * c
    lane = lax.broadcasted_iota(jnp.int32, (c, LANES), 1)
    row = lax.broadcasted_iota(jnp.int32, (c, LANES), 0)
    head0 = lane < HEAD_DIM
    scol = jnp.where(head0, lane, lane - HEAD_DIM)
    strict = scol < row
    incl = scol <= row
    diag = scol == row
    r2 = lax.broadcasted_iota(jnp.int32, (LANES, LANES), 0)
    c2 = lax.broadcasted_iota(jnp.int32, (LANES, LANES), 1)
    blockmask = (r2 < HEAD_DIM) == (c2 < HEAD_DIM)
    blockones = jnp.where(blockmask, 1.0, 0.0).astype(BF16)
    cum_rows = min(rows, RWKV_CUMSUM_ROWS)
    tr = lax.broadcasted_iota(jnp.int32, (cum_rows, cum_rows), 0)
    tc = lax.broadcasted_iota(jnp.int32, (cum_rows, cum_rows), 1)
    tril_ones = jnp.where((tc <= tr) & (tc // c == tr // c), 1.0, 0.0).astype(BF16)
    slab_lane = lax.broadcasted_iota(jnp.int32, (rows, LANES), 1)
    slab_head0 = slab_lane < HEAD_DIM
    slab_first = lax.broadcasted_iota(jnp.int32, (rows, LANES), 0) == 0

    def bd(x):
        return jnp.concatenate([jnp.where(head0, x, 0.0), jnp.where(head0, 0.0, x)], axis=0)

    def seg_sum(x):
        return _dot(x.astype(BF16), blockones)

    mu_r, mu_k, mu_v, mu_wa = mur_ref[...], muk_ref[...], muv_ref[...], muwa_ref[...]
    w0, a0 = w0_ref[...], a0_ref[...]
    k_k, k_a, r_k = kk_ref[...], ka_ref[...], rk_ref[...]
    lnx_w, lnx_b = lnw_ref[...], lnb_ref[...]
    zeros = jnp.zeros((LORA_W, LANES), F32)
    w2a2 = jnp.concatenate([jnp.concatenate([w2_ref[...], zeros], axis=1),
                            jnp.concatenate([zeros, a2_ref[...]], axis=1)], axis=0)

    s_ref[...] = jnp.zeros_like(s_ref)
    prev_ref[...] = jnp.zeros_like(prev_ref)

    def shift_mix(ref, bj, slot, sl, mu):
        z = ref[bj, sl, :].astype(F32)
        zp = pltpu.roll(z, 1, axis=0)
        zp = jnp.where(slab_first, prev_ref[bj, slot:slot + 1, :], zp)
        prev_ref[bj, slot:slot + 1, :] = z[rows - 1:rows, :]
        return z + (zp - z) * mu


    def prepare(bj, sl):
        r = shift_mix(zr_ref, bj, 0, sl, mu_r)
        k = shift_mix(zk_ref, bj, 1, sl, mu_k)
        yield
        v = shift_mix(zv_ref, bj, 2, sl, mu_v)
        wa = shift_mix(zwa_ref, bj, 3, sl, mu_wa)
        lora = _mm(jnp.where(slab_head0, jnp.tanh(wa), wa), w2a2)
        yield
        wlog = w0 + lora[:, :LANES]
        nw = -wlog
        w = -(jnp.maximum(nw, 0.0) + jnp.log(1.0 + jnp.exp(-jnp.abs(nw)))) - 0.5
        logw = -jnp.exp(w)
        a = 1.0 / (1.0 + jnp.exp(-(a0 + lora[:, LANES:])))
        kk = k * k_k
        kk = kk * lax.rsqrt(jnp.maximum(seg_sum(kk * kk), KK_NORM_FLOOR))
        yield
        kmod = k * (1.0 + (a - 1.0) * k_a)
        alpha = -kk
        beta = kk * a
        l2 = jnp.concatenate(_split(logw, 2), axis=1)
        cum2 = jnp.concatenate(
            [_dot(tril_ones, l2[i:i + cum_rows]) for i in range(0, rows, cum_rows)], axis=0)
        cum = cum2[:, :LANES] + cum2[:, LANES:]
        yield
        e_neg = jnp.exp(-cum)
        at = alpha * jnp.exp(cum - logw)
        rt = r * jnp.exp(cum)
        bt = beta * e_neg
        kt = kmod * e_neg
        yield
        chunks = [slice(g * c, (g + 1) * c) for g in range(group)]
        cum_c = [cum[cs.stop - 1:cs.stop, :] for cs in chunks]
        e_end = [jnp.exp(cum_c[g] - cum[cs]) for g, cs in enumerate(chunks)]
        per = lambda x: [x[cs] for cs in chunks]
        return dict(
            at=per(at), rt=per(rt), bt=per(bt), kt=per(kt), v=per(v),
            bt_end=[beta[cs] * e_end[g] for g, cs in enumerate(chunks)],
            kt_end=[kmod[cs] * e_end[g] for g, cs in enumerate(chunks)],
            p_c=[jnp.exp(x) for x in cum_c], v_slab=v, rk=r * kmod * r_k)

    def chains(p):
        at, rt, bt, kt, bt_end, kt_end, v = (p[x] for x in ("at", "rt", "bt", "kt", "bt_end", "kt_end", "v"))
        n = range(group)
        lhs = [jnp.concatenate([at[g], rt[g]], axis=0) for g in n]
        xbk = [_mm(lhs[g], jnp.concatenate([bd(bt[g]), bd(kt[g])], axis=0), _dot_nt) for g in n]
        yield
        a_rb = [jnp.where(incl, xbk[g][c:, :LANES], 0.0) for g in n]
        akrk = [jnp.concatenate([jnp.where(strict, xbk[g][:c, LANES:], 0.0),
                                 jnp.where(incl, xbk[g][c:, LANES:], 0.0)], axis=0) for g in n]
        av = [_mm(akrk[g], bd(v[g])) for g in n]
        yield
        apow = [jnp.where(strict, xbk[g][:c, :LANES], 0.0) for g in n]
        tinv = [jnp.where(diag, 1.0, 0.0) + apow[g] for g in n]
        nfac = int(math.log2(c))
        for i in range(1, nfac):
            rhs = [[bd(apow[g])] + ([bd(tinv[g])] if i > 1 else []) for g in n]
            d = [_mm(apow[g], jnp.concatenate(rhs[g], axis=1)) for g in n]
            if i > 1:
                tinv = [tinv[g] + d[g][:, LANES:] for g in n]
            apow = [d[g][:, :LANES] for g in n]
            yield
        tinv = [tinv[g] + _mm(apow[g], bd(tinv[g])) for g in n]
        yield
        x = [_mm(tinv[g], jnp.concatenate([bd(at[g]), bd(av[g][:c])], axis=1)) for g in n]
        wmat = [x[g][:, :LANES] for g in n]
        u0 = [x[g][:, LANES:] for g in n]
        yield
        d2 = [_mm(a_rb[g], jnp.concatenate([bd(wmat[g]), bd(u0[g])], axis=1)) for g in n]
        rp = [rt[g] + d2[g][:, :LANES] for g in n]
        y0 = [d2[g][:, LANES:] + av[g][c:] for g in n]
        yield
        fold = lambda z: jnp.where(head0, z[:HEAD_DIM], z[HEAD_DIM:])
        gp = [fold(_mm(wmat[g], bt_end[g], _dot_tn)) for g in n]
        npart = [fold(_mm(jnp.concatenate([u0[g], v[g]], axis=0),
                          jnp.concatenate([bt_end[g], kt_end[g]], axis=0), _dot_tn)) for g in n]
        yield
        spans = [[(jnp.where(diag, gp[g] + p["p_c"][g], gp[g]), npart[g]) for g in n]]
        while len(spans[-1]) > 1:
            prev = spans[-1]
            nxt = []
            for i in range(0, len(prev), 2):
                (ma, na), (mb, nb) = prev[i], prev[i + 1]
                prod = _mm(jnp.concatenate([ma, na], axis=0), bd(mb))
                nxt.append((prod[:HEAD_DIM], prod[HEAD_DIM:] + nb))
            spans.append(nxt)
            yield
        return dict(rp=rp, y0=y0, spans=spans, v_slab=p["v_slab"], rk=p["rk"])

    def finish(t, bj, sl):
        spans = t["spans"]
        states = {0: s_ref[bj]}
        top = len(spans) - 1
        m_all, n_all = spans[top][0]
        s_ref[bj] = _mm(states[0], bd(m_all)) + n_all
        yield
        for level in range(top, 0, -1):
            width = 1 << level
            for lo in range(0, group, width):
                ma, na = spans[level - 1][lo >> (level - 1)]
                states[lo + width // 2] = _mm(states[lo], bd(ma)) + na
            yield
        y = jnp.concatenate(
            [_mm(t["rp"][g], bd(states[g]), _dot_nt) + t["y0"][g] for g in range(group)], axis=0)
        yield
        mean = seg_sum(y) * (1.0 / HEAD_DIM)
        yc = y - mean
        var = seg_sum(yc * yc) * (1.0 / HEAD_DIM)
        yield
        yn = yc * lax.rsqrt(var + LNX_EPS) * lnx_w + lnx_b
        bonus = seg_sum(t["rk"]) * t["v_slab"]
        y_ref[bj, sl, :] = (yn + bonus).astype(y_ref.dtype)

    def run(*gens):
        results = [None] * len(gens)
        live = list(range(len(gens)))
        while live:
            for i in list(live):
                try:
                    next(gens[i])
                except StopIteration as stop:
                    results[i] = stop.value
                    live.remove(i)
        return results

    nslabs = zr_ref.shape[1] // rows
    units = [(bj, slice(t * rows, (t + 1) * rows)) for t in range(nslabs) for bj in range(zr_ref.shape[0])]
    prepared, = run(prepare(*units[0]))
    done = None
    for u in range(len(units)):
        gens = [chains(prepared)]
        if u + 1 < len(units):
            gens.append(prepare(*units[u + 1]))
        if done is not None:
            gens.append(finish(done, *units[u - 1]))
        out = run(*gens)
        done = out[0]
        if u + 1 < len(units):
            prepared = out[1]
    run(finish(done, *units[-1]))


def _rwkv(z3d, mu, w0, a0, k_k, k_a, r_k, lnx_w, lnx_b, w2, a2):
    b, s, a_shift = z3d.shape
    br_w = (a_shift - 2 * LORA_W) // 3
    npairs = br_w // LANES
    vec = lambda off: pl.BlockSpec((1, LANES), lambda bi, hp: (0, off + hp))
    lora = pl.BlockSpec((LORA_W, LANES), lambda bi, hp: (0, hp))
    nb = RWKV_BATCH
    assert b % nb == 0 and s % (RWKV_CHUNK * RWKV_GROUP) == 0
    zspec = lambda off: pl.BlockSpec((nb, s, LANES), lambda bi, hp: (bi, 0, off + hp))
    return pl.pallas_call(
        functools.partial(_rwkv_body, group=RWKV_GROUP),
        grid=(b // nb, npairs),
        in_specs=[
            zspec(0), zspec(npairs), zspec(2 * npairs),
            pl.BlockSpec((nb, s, LANES), lambda bi, hp: (bi, 0, 3 * npairs)),
            vec(0), vec(npairs), vec(2 * npairs),
            pl.BlockSpec((1, LANES), lambda bi, hp: (0, 3 * npairs)),
            vec(0), vec(0), vec(0), vec(0), vec(0), vec(0), vec(0), lora, lora,
        ],
        out_specs=pl.BlockSpec((nb, s, LANES), lambda bi, hp: (bi, 0, hp)),
        out_shape=jax.ShapeDtypeStruct((b, s, br_w), ACT_DTYPE),
        scratch_shapes=[pltpu.VMEM((nb, HEAD_DIM, LANES), F32), pltpu.VMEM((nb, SUBLANES, LANES), F32)],
        compiler_params=pltpu.CompilerParams(
            dimension_semantics=("arbitrary", "arbitrary"), vmem_limit_bytes=VMEM_LIMIT_BYTES),
        name="rwkv7_scan",
    )(z3d, z3d, z3d, z3d, mu, mu, mu, mu, w0, a0, k_k, k_a, r_k, lnx_w, lnx_b, w2, a2)


def _alibi_slopes(n):
    def pow2(m):
        start = 2.0 ** (-8.0 / m)
        return [start ** (i + 1) for i in range(m)]
    if math.log2(n).is_integer():
        return pow2(n)
    cl = 2 ** int(math.floor(math.log2(n)))
    return pow2(cl) + pow2(2 * cl)[0::2][: n - cl]


def _diff_attn_body(q_ref, k_ref, v_ref, slope_ref, lq1_ref, lk1_ref, lq2_ref, lk2_ref, sub_ref, o_ref,
                    kb_ref, vt_ref, band_ref, sa_ref, sb_ref, acc_ref, *, tq, lam_init):
    bi = pl.program_id(1)
    nb = q_ref.shape[0]
    tk = tq
    half = tq // 2
    nq = k_ref.shape[1] // tq
    log2e = math.log2(math.e)
    slope2 = slope_ref[0][:, :1] * log2e
    aug_lane = lax.broadcasted_iota(jnp.int32, (tk, LANES), 1)

    for bj in range(nb):
        for t in range(nq):
            rows = slice(t * tk, (t + 1) * tk)
            kb_ref[bj, rows, :LANES] = k_ref[bj, rows, :].astype(BF16)
            vt_ref[bj, :LANES, rows] = v_ref[bj, rows, :].astype(F32).T.astype(BF16)

    @pl.when(bi == 0)
    def _():
        ones_row = lax.broadcasted_iota(jnp.int32, (ATT_V_PAD, tk), 0) == 0
        for t in range(nq):
            rows = slice(t * tk, (t + 1) * tk)
            pos = (lax.broadcasted_iota(jnp.int32, (tk, LANES), 0) + t * tk).astype(F32)
            pieces = _split(slope2 * pos, ATT_BIAS_PIECES)
            aug = jnp.zeros((tk, LANES), F32)
            for i, piece in enumerate(pieces):
                aug = jnp.where(aug_lane == i, piece.astype(F32), aug)
            for bj in range(nb):
                kb_ref[bj, rows, LANES:] = aug.astype(BF16)
                vt_ref[bj, LANES:, rows] = jnp.where(ones_row, 1.0, 0.0).astype(BF16)
        kpos = lax.broadcasted_iota(jnp.int32, (tk, 2 * tq), 0).astype(F32)
        lane2 = lax.broadcasted_iota(jnp.int32, (1, 2 * tq), 1)
        qry = jnp.where(lane2 < tq, lane2, lane2 - tq)
        qlim = ((qry // ATT_CHUNK + 1) * ATT_CHUNK).astype(F32)
        band_ref[...] = jnp.where(
            kpos < qlim, (-2.0 * slope2) * jnp.maximum(kpos - qry.astype(F32), 0.0), MASK_VALUE)

    lane = lax.broadcasted_iota(jnp.int32, (2 * tq, LANES), 1)
    head0 = lane < HEAD_DIM
    first = lax.broadcasted_iota(jnp.int32, (2 * tq, LANES), 0) < tq
    ones_cols = jnp.where(lane < ATT_BIAS_PIECES, 1.0, 0.0)
    lam = (jnp.exp(jnp.sum(lq1_ref[...] * lk1_ref[...], axis=-1, keepdims=True))
           - jnp.exp(jnp.sum(lq2_ref[...] * lk2_ref[...], axis=-1, keepdims=True)) + lam_init)
    late = lambda x: jnp.concatenate([x[..., half:tq], x[..., tq + half:]], axis=-1)

    def update(p_rows, vt, m, m_new, cols):
        pv = _dot(vt, p_rows.astype(BF16))
        off = 0
        for cs in cols:
            w = cs.stop - cs.start
            if m is None:
                acc_ref[:, cs] = pv[:, off:off + w]
            else:
                acc_ref[:, cs] = jnp.exp2(m - m_new)[:, off:off + w] * acc_ref[:, cs] + pv[:, off:off + w]
            off += w

    def tiles_of(n):
        past = [(slice(j * tk, (j + 1) * tk), None, None) for j in range(n)]
        own0 = (slice(n * tq, n * tq + half), None, slice(0, half))
        own1 = (slice(n * tq + half, (n + 1) * tq), "late", slice(half, tq))
        return past + [own0, own1]

    all_cols = [slice(0, 2 * tq)]
    late_cols = [slice(half, tq), slice(tq + half, 2 * tq)]
    bufs = (sa_ref, sb_ref)
    for bj in range(nb):
        for n in range(nq):
            qrows = slice(n * tq, (n + 1) * tq)
            q = q_ref[bj, qrows, :].astype(F32) * (HEAD_DIM ** -0.5 * log2e)
            q2x = jnp.concatenate([q, q], axis=0)
            qcat = jnp.concatenate([jnp.where(first == head0, q2x, 0.0), ones_cols],
                                   axis=1).astype(BF16)
            qlate = jnp.concatenate([qcat[half:tq], qcat[tq + half:]], axis=0)
            tiles = tiles_of(n)

            def scores(i, dst_ref):
                keys, which, _ = tiles[i]
                nk = keys.stop - keys.start
                if which is None:
                    dst_ref[:nk, :] = _dot_nt(kb_ref[bj, keys, :], qcat)
                else:
                    dst_ref[:nk, :tq] = _dot_nt(kb_ref[bj, keys, :], qlate)

            m = None
            scores(0, bufs[0])
            for i, (keys, which, brows) in enumerate(tiles):
                if i + 1 < len(tiles):
                    scores(i + 1, bufs[(i + 1) % 2])
                nk = keys.stop - keys.start
                vt = vt_ref[bj, :, keys]
                if which is None:
                    s = bufs[i % 2][:nk, :]
                    if brows is not None:
                        s = s + band_ref[brows, :]
                    m_new = jnp.max(s, axis=0, keepdims=True)
                    if m is not None:
                        m_new = jnp.maximum(m, m_new)
                    update(jnp.exp2(s - m_new), vt, m, m_new, all_cols)
                    m = m_new
                else:
                    s = bufs[i % 2][:nk, :tq] + late(band_ref[brows, :])
                    ml = late(m)
                    m_new = jnp.maximum(ml, jnp.max(s, axis=0, keepdims=True))
                    update(jnp.exp2(s - m_new), vt, ml, m_new, late_cols)
            on = acc_ref[:LANES, :] * (1.0 / acc_ref[LANES:LANES + 1, :])
            o = on[:, :tq] - lam * on[:, tq:]
            ms = jnp.mean(o * o, axis=0, keepdims=True)
            o = o * lax.rsqrt(ms + NORM_EPS) * sub_ref[...] * (1.0 - lam_init)
            o_ref[bj, qrows, :] = o.T.astype(o_ref.dtype)


def _diff_attn(q3d, k3d, v3d, slopes, lam_vectors, subln_col, lam_init, tq):
    b, s, br_w = q3d.shape
    nh = br_w // LANES
    nb = ATT_BATCH
    assert b % nb == 0 and s % tq == 0
    lam_spec = pl.BlockSpec((1, HEAD_DIM), lambda h, bi: (0, 0))
    seq_spec = pl.BlockSpec((nb, s, LANES), lambda h, bi: (bi, 0, h))
    return pl.pallas_call(
        functools.partial(_diff_attn_body, tq=tq, lam_init=lam_init),
        grid=(nh, b // nb),
        in_specs=[
            seq_spec, seq_spec, seq_spec,
            pl.BlockSpec((1, 1, LANES), lambda h, bi: (h, 0, 0)),
            lam_spec, lam_spec, lam_spec, lam_spec,
            pl.BlockSpec((LANES, 1), lambda h, bi: (0, 0)),
        ],
        out_specs=seq_spec,
        out_shape=jax.ShapeDtypeStruct((b, s, br_w), ACT_DTYPE),
        scratch_shapes=[pltpu.VMEM((nb, s, 2 * LANES), BF16), pltpu.VMEM((nb, LANES + ATT_V_PAD, s), BF16),
                        pltpu.VMEM((tq, 2 * tq), F32),
                        pltpu.VMEM((tq, 2 * tq), F32), pltpu.VMEM((tq, 2 * tq), F32),
                        pltpu.VMEM((LANES + ATT_V_PAD, 2 * tq), F32)],
        compiler_params=pltpu.CompilerParams(
            dimension_semantics=("arbitrary", "arbitrary"),
            vmem_limit_bytes=VMEM_LIMIT_BYTES),
        name="diff_attn",
    )(q3d, k3d, v3d, slopes, *lam_vectors, subln_col)


def _out_body(ybr_ref, gate_ref, qm_ref, gm_ref, x_ref, km_ref, vm_ref, w_ref, pn_ref, o_ref, wb_ref, *, br_w):
    @pl.when(pl.program_id(0) == 0)
    def _():
        wb_ref[...] = w_ref[0].astype(BF16)

    tm = x_ref.shape[0]
    ml = km_ref.shape[2] // MEM_HEADS
    subs = [slice(r * OUT_SUB_ROWS, (r + 1) * OUT_SUB_ROWS) for r in range(tm // OUT_SUB_ROWS)]
    qscale = HEAD_DIM ** -0.5 * math.log2(math.e)
    s = [_dot_nt((qm_ref[rs, :].astype(F32) * qscale).astype(BF16), km_ref[0, 0]) for rs in subs]
    y_br = (ybr_ref[...].astype(F32) * _silu(gate_ref[...].astype(F32))).astype(BF16)
    pcat = []
    for sr in s:
        ps = []
        for hd in range(MEM_HEADS):
            sh = sr[:, hd * ml:(hd + 1) * ml]
            p = jnp.exp2(sh - jnp.max(sh, axis=-1, keepdims=True))
            ps.append((p * (1.0 / jnp.sum(p, axis=-1, keepdims=True))).astype(BF16))
        pcat.append(jnp.concatenate(ps, axis=1))
    y_mem = [_dot(pc, vm_ref[0, 0]) for pc in pcat]
    y_mem = (jnp.concatenate(y_mem, axis=0) * _silu(gm_ref[...].astype(F32))).astype(BF16)
    y = _dot(y_br, wb_ref[:br_w, :]) + _dot(y_mem, wb_ref[br_w:, :])
    o_ref[...] = x_ref[...] + _rms(y, pn_ref[0])


def _out_proj(ybr, gate, qm, gm, x2d, k_mem, v_mem, layer, w, post_g, seq, tm):
    m, d = x2d.shape
    br_w = ybr.shape[1]
    ml = k_mem.shape[2]
    per_b = seq // tm
    mem_spec = pl.BlockSpec((1, 1, ml, MEM_W), lambda i: (layer, i // per_b, 0, 0))
    row = lambda w: pl.BlockSpec((tm, w), lambda i: (i, 0))
    return pl.pallas_call(
        functools.partial(_out_body, br_w=br_w),
        grid=(m // tm,),
        in_specs=[
            row(br_w), row(br_w), row(MEM_W), row(MEM_W), row(d),
            mem_spec, mem_spec,
            pl.BlockSpec((1, d, d), lambda i: (layer, 0, 0)),
            pl.BlockSpec((1, 1, d), lambda i: (layer, 0, 0)),
        ],
        out_specs=row(d),
        out_shape=jax.ShapeDtypeStruct((m, d), F32),
        scratch_shapes=[pltpu.VMEM((d, d), BF16)],
        compiler_params=pltpu.CompilerParams(
            dimension_semantics=("arbitrary",), vmem_limit_bytes=VMEM_LIMIT_BYTES),
        name="out_proj",
    )(ybr, gate, qm, gm, x2d, k_mem, v_mem, w, post_g)


def kernel(x, mem, pre_norm, post_norm, w_out, mem_norm, w_mem_kv, a_w_in, a_shift_mu, a_w0, a_w2,
           a_a0, a_a2, a_k_k, a_k_a, a_r_k, a_lnx_w, a_lnx_b, kv_norm, w_kv, b_w_in, b_lam_q1,
           b_lam_k1, b_lam_q2, b_lam_k2, b_subln):
    bsz, seq, d = x.shape
    depth = pre_norm.shape[0]
    n_a = a_w_in.shape[0]
    br_w = d - MEM_W
    a_shift = 3 * br_w + 2 * LORA_W
    m = bsz * seq
    tm = 1024
    x2d = x.reshape(m, d)
    slopes = jnp.asarray(
        np.repeat(np.array(_alibi_slopes(br_w // LANES), np.float32)[:, None, None], LANES, axis=2))

    k_mem, v_mem = _mem_kv(mem, mem_norm.reshape(depth, 1, d), w_mem_kv)
    for l in range(depth):
        if l < n_a:
            i = l
            cols = [(0, a_shift), (a_shift, a_shift + br_w), (a_shift + br_w, a_shift + br_w + MEM_W),
                    (a_shift + br_w + MEM_W, a_shift + br_w + 2 * MEM_W)]
            z, gate, q_mem, g_mem = _norm_proj(
                x2d, [((pre_norm.reshape(depth, 1, d), l), a_w_in[i], cols)], tm)
            row = lambda p: p[i].reshape(1, -1)
            y_br = _rwkv(z.reshape(bsz, seq, a_shift), row(a_shift_mu), row(a_w0), row(a_a0), row(a_k_k),
                         row(a_k_a), row(a_r_k), row(a_lnx_w), row(a_lnx_b), a_w2[i], a_a2[i]).reshape(m, br_w)
        else:
            i = l - n_a
            cols = [(0, br_w), (br_w, 2 * br_w), (2 * br_w, 2 * br_w + MEM_W),
                    (2 * br_w + MEM_W, 2 * br_w + 2 * MEM_W)]
            groups = [((pre_norm.reshape(depth, 1, d), l), b_w_in[i], cols)]
            if l == n_a:
                groups.append(((kv_norm.reshape(1, 1, d), 0), w_kv, [(0, br_w), (br_w, 2 * br_w)]))
                q, gate, q_mem, g_mem, k_sh, v_sh = _norm_proj(x2d, groups, tm)
                k_sh = k_sh.reshape(bsz, seq, br_w)
                v_sh = v_sh.reshape(bsz, seq, br_w)
            else:
                q, gate, q_mem, g_mem = _norm_proj(x2d, groups, tm)
            lam_init = 0.8 - 0.6 * math.exp(-0.3 * l)
            lam_vectors = [p[i].reshape(1, HEAD_DIM) for p in (b_lam_q1, b_lam_k1, b_lam_q2, b_lam_k2)]
            y_br = _diff_attn(q.reshape(bsz, seq, br_w), k_sh, v_sh, slopes, lam_vectors,
                              b_subln[i].reshape(LANES, 1), lam_init, ATT_Q_TILE).reshape(m, br_w)
        x2d = _out_proj(y_br, gate, q_mem, g_mem, x2d, k_mem, v_mem, l, w_out,
                        post_norm.reshape(depth, 1, d), seq, OUT_ROWS)
    return x2d.reshape(bsz, seq, d)
```

```python
import functools
import math

import numpy as np
import jax
import jax.numpy as jnp
from jax import lax
from jax.experimental import pallas as pl
from jax.experimental.pallas import tpu as pltpu

F32 = jnp.float32
BF16 = jnp.bfloat16
ACT_DTYPE = BF16

HEAD_DIM = 64
LANES = 128
SUBLANES = 8
MEM_HEADS = 4
MEM_W = MEM_HEADS * HEAD_DIM
LORA_W = 64
ATT_CHUNK = 64
ATT_Q_TILE = 512
ATT_BATCH = 2
ATT_BIAS_PIECES = 3
ATT_V_PAD = 16
RWKV_CHUNK = 64
RWKV_GROUP = 16
RWKV_CUMSUM_ROWS = 128
RWKV_BATCH = 2
OUT_ROWS = 1024
OUT_SUB_ROWS = 512
NORM_EPS = 1e-6
LNX_EPS = 64e-5
KK_NORM_FLOOR = 1e-12
MASK_VALUE = -1e30
VMEM_LIMIT_BYTES = 56 * 1024 * 1024


def _dot(a, b):
    return jnp.dot(a, b, preferred_element_type=F32)


def _dot_nt(a, b):
    return lax.dot_general(a, b, (((1,), (1,)), ((), ())), preferred_element_type=F32)


def _dot_tn(a, b):
    return lax.dot_general(a, b, (((0,), (0,)), ((), ())), preferred_element_type=F32)


def _split(x, pieces):
    out = []
    rem = x
    for i in range(pieces):
        p = rem.astype(BF16)
        out.append(p)
        if i + 1 < pieces:
            rem = rem - p.astype(F32)
    return out


def _mm(a, b, fn=_dot, pa=1, pb=1):
    aps = _split(a, pa)
    bps = _split(b, pb)
    order = max(pa, pb)
    acc = None
    for i, ap in enumerate(aps):
        for j, bp in enumerate(bps):
            if i + j < order:
                t = fn(ap, bp)
                acc = t if acc is None else acc + t
    return acc


def _rms(x, g):
    ms = jnp.mean(x * x, axis=-1, keepdims=True)
    return (x * lax.rsqrt(ms + NORM_EPS)) * g


def _silu(x):
    h = 0.5 * x
    return h + h * jnp.tanh(h)


def _norm_proj_body(x_ref, *refs, groups):
    ng = len(groups)
    nout = sum(len(cols) for cols in groups)
    o_refs = refs[2 * ng:2 * ng + nout]
    wb_refs = refs[2 * ng + nout:]

    @pl.when(pl.program_id(0) == 0)
    def _():
        for gi in range(ng):
            wb_refs[gi][...] = refs[2 * gi + 1][...].astype(BF16)

    x = x_ref[...]
    ms = jnp.mean(x * x, axis=-1, keepdims=True)
    xn = x * lax.rsqrt(ms + NORM_EPS)
    k = 0
    for gi, cols in enumerate(groups):
        h = (xn * refs[2 * gi][0]).astype(BF16)
        for lo, hi in cols:
            o_refs[k][...] = _dot(h, wb_refs[gi][:, lo:hi]).astype(o_refs[k].dtype)
            k += 1


def _norm_proj(x2d, groups, tm):
    m, d = x2d.shape
    cols = [c for _, _, cs in groups for c in cs]
    in_specs = [pl.BlockSpec((tm, d), lambda i: (i, 0))]
    operands = [x2d]
    for (g, row), w, _ in groups:
        in_specs += [pl.BlockSpec((1, 1, d), lambda i, row=row: (row, 0, 0)), pl.BlockSpec(w.shape, lambda i: (0, 0))]
        operands += [g, w]
    return pl.pallas_call(
        functools.partial(_norm_proj_body, groups=tuple(tuple(cs) for _, _, cs in groups)),
        grid=(m // tm,),
        in_specs=in_specs,
        out_specs=[pl.BlockSpec((tm, hi - lo), lambda i: (i, 0)) for lo, hi in cols],
        out_shape=[jax.ShapeDtypeStruct((m, hi - lo), ACT_DTYPE) for lo, hi in cols],
        scratch_shapes=[pltpu.VMEM(w.shape, BF16) for _, w, _ in groups],
        compiler_params=pltpu.CompilerParams(
            dimension_semantics=("arbitrary",), vmem_limit_bytes=VMEM_LIMIT_BYTES),
        name="norm_proj",
    )(*operands)


def _mem_kv_body(mem_ref, g_ref, w_ref, k_ref, v_ref):
    nb, ml, d = mem_ref.shape
    h = _rms(mem_ref[...].reshape(nb * ml, d), g_ref[0]).astype(BF16)
    kv = _dot(h, w_ref[0].astype(BF16))
    lane = lax.broadcasted_iota(jnp.int32, (ml, MEM_W), 1)
    for bj in range(nb):
        rows = slice(bj * ml, (bj + 1) * ml)
        for hd in range(MEM_HEADS):
            in_head = (lane >= hd * HEAD_DIM) & (lane < (hd + 1) * HEAD_DIM)
            k_ref[0, bj, hd * ml:(hd + 1) * ml, :] = jnp.where(in_head, kv[rows, :MEM_W], 0.0).astype(k_ref.dtype)
            v_ref[0, bj, hd * ml:(hd + 1) * ml, :] = jnp.where(in_head, kv[rows, MEM_W:], 0.0).astype(v_ref.dtype)


def _mem_kv(mem, gains, w):
    b, ml, d = mem.shape
    nl = gains.shape[0]
    out_spec = pl.BlockSpec((1, b, MEM_HEADS * ml, MEM_W), lambda l: (l, 0, 0, 0))
    return pl.pallas_call(
        _mem_kv_body,
        grid=(nl,),
        in_specs=[
            pl.BlockSpec((b, ml, d), lambda l: (0, 0, 0)),
            pl.BlockSpec((1, 1, d), lambda l: (l, 0, 0)),
            pl.BlockSpec((1, d, 2 * MEM_W), lambda l: (l, 0, 0)),
        ],
        out_specs=[out_spec] * 2,
        out_shape=[jax.ShapeDtypeStruct((nl, b, MEM_HEADS * ml, MEM_W), ACT_DTYPE)] * 2,
        compiler_params=pltpu.CompilerParams(
            dimension_semantics=("arbitrary",), vmem_limit_bytes=VMEM_LIMIT_BYTES),
        name="mem_kv",
    )(mem, gains, w)


def _rwkv_body(zr_ref, zk_ref, zv_ref, zwa_ref, mur_ref, muk_ref, muv_ref, muwa_ref, w0_ref, a0_ref,
               kk_ref, ka_ref, rk_ref, lnw_ref, lnb_ref, w2_ref, a2_ref, y_ref, s_ref, prev_ref, *, group):
    c = RWKV_CHUNK
    rows = group * c
    lane = lax.broadcasted_iota(jnp.int32, (c, LANES), 1)
    row = lax.broadcasted_iota(jnp.int32, (c, LANES), 0)
    head0 = lane < HEAD_DIM
    scol = jnp.where(head0, lane, lane - HEAD_DIM)
    strict = scol < row
    incl = scol <= row
    diag = scol == row
    r2 = lax.broadcasted_iota(jnp.int32, (LANES, LANES), 0)
    c2 = lax.broadcasted_iota(jnp.int32, (LANES, LANES), 1)
    blockmask = (r2 < HEAD_DIM) == (c2 < HEAD_DIM)
    blockones = jnp.where(blockmask, 1.0, 0.0).astype(BF16)
    cum_rows = min(rows, RWKV_CUMSUM_ROWS)
    tr = lax.broadcasted_iota(jnp.int32, (cum_rows, cum_rows), 0)
    tc = lax.broadcasted_iota(jnp.int32, (cum_rows, cum_rows), 1)
    tril_ones = jnp.where((tc <= tr) & (tc // c == tr // c), 1.0, 0.0).astype(BF16)
    slab_lane = lax.broadcasted_iota(jnp.int32, (rows, LANES), 1)
    slab_head0 = slab_lane < HEAD_DIM
    slab_first = lax.broadcasted_iota(jnp.int32, (rows, LANES), 0) == 0

    def bd(x):
        return jnp.concatenate([jnp.where(head0, x, 0.0), jnp.where(head0, 0.0, x)], axis=0)

    def seg_sum(x):
        return _dot(x.astype(BF16), blockones)

    mu_r, mu_k, mu_v, mu_wa = mur_ref[...], muk_ref[...], muv_ref[...], muwa_ref[...]
    w0, a0 = w0_ref[...], a0_ref[...]
    k_k, k_a, r_k = kk_ref[...], ka_ref[...], rk_ref[...]
    lnx_w, lnx_b = lnw_ref[...], lnb_ref[...]
    zeros = jnp.zeros((LORA_W, LANES), F32)
    w2a2 = jnp.concatenate([jnp.concatenate([w2_ref[...], zeros], axis=1),
                            jnp.concatenate([zeros, a2_ref[...]], axis=1)], axis=0)

    s_ref[...] = jnp.zeros_like(s_ref)
    prev_ref[...] = jnp.zeros_like(prev_ref)

    def shift_mix(ref, bj, slot, sl, mu):
        z = ref[bj, sl, :].astype(F32)
        zp = pltpu.roll(z, 1, axis=0)
        zp = jnp.where(slab_first, prev_ref[bj, slot:slot + 1, :], zp)
        prev_ref[bj, slot:slot + 1, :] = z[rows - 1:rows, :]
        return z + (zp - z) * mu


    def prepare(bj, sl):
        r = shift_mix(zr_ref, bj, 0, sl, mu_r)
        k = shift_mix(zk_ref, bj, 1, sl, mu_k)
        yield
        v = shift_mix(zv_ref, bj, 2, sl, mu_v)
        wa = shift_mix(zwa_ref, bj, 3, sl, mu_wa)
        lora = _mm(jnp.where(slab_head0, jnp.tanh(wa), wa), w2a2)
        yield
        wlog = w0 + lora[:, :LANES]
        nw = -wlog
        w = -(jnp.maximum(nw, 0.0) + jnp.log(1.0 + jnp.exp(-jnp.abs(nw)))) - 0.5
        logw = -jnp.exp(w)
        a = 1.0 / (1.0 + jnp.exp(-(a0 + lora[:, LANES:])))
        kk = k * k_k
        kk = kk * lax.rsqrt(jnp.maximum(seg_sum(kk * kk), KK_NORM_FLOOR))
        yield
        kmod = k * (1.0 + (a - 1.0) * k_a)
        alpha = -kk
        beta = kk * a
        l2 = jnp.concatenate(_split(logw, 2), axis=1)
        cum2 = jnp.concatenate(
            [_dot(tril_ones, l2[i:i + cum_rows]) for i in range(0, rows, cum_rows)], axis=0)
        cum = cum2[:, :LANES] + cum2[:, LANES:]
        yield
        e_neg = jnp.exp(-cum)
        at = alpha * jnp.exp(cum - logw)
        rt = r * jnp.exp(cum)
        bt = beta * e_neg
        kt = kmod * e_neg
        yield
        chunks = [slice(g * c, (g + 1) * c) for g in range(group)]
        cum_c = [cum[cs.stop - 1:cs.stop, :] for cs in chunks]
        e_end = [jnp.exp(cum_c[g] - cum[cs]) for g, cs in enumerate(chunks)]
        per = lambda x: [x[cs] for cs in chunks]
        return dict(
            at=per(at), rt=per(rt), bt=per(bt), kt=per(kt), v=per(v),
            bt_end=[beta[cs] * e_end[g] for g, cs in enumerate(chunks)],
            kt_end=[kmod[cs] * e_end[g] for g, cs in enumerate(chunks)],
            p_c=[jnp.exp(x) for x in cum_c], v_slab=v, rk=r * kmod * r_k)

    def chains(p):
        at, rt, bt, kt, bt_end, kt_end, v = (p[x] for x in ("at", "rt", "bt", "kt", "bt_end", "kt_end", "v"))
        n = range(group)
        lhs = [jnp.concatenate([at[g], rt[g]], axis=0) for g in n]
        xbk = [_mm(lhs[g], jnp.concatenate([bd(bt[g]), bd(kt[g])], axis=0), _dot_nt) for g in n]
        yield
        a_rb = [jnp.where(incl, xbk[g][c:, :LANES], 0.0) for g in n]
        akrk = [jnp.concatenate([jnp.where(strict, xbk[g][:c, LANES:], 0.0),
                                 jnp.where(incl, xbk[g][c:, LANES:], 0.0)], axis=0) for g in n]
        av = [_mm(akrk[g], bd(v[g])) for g in n]
        yield
        apow = [jnp.where(strict, xbk[g][:c, :LANES], 0.0) for g in n]
        tinv = [jnp.where(diag, 1.0, 0.0) + apow[g] for g in n]
        nfac = int(math.log2(c))
        for i in range(1, nfac):
            rhs = [[bd(apow[g])] + ([bd(tinv[g])] if i > 1 else []) for g in n]
            d = [_mm(apow[g], jnp.concatenate(rhs[g], axis=1)) for g in n]
            if i > 1:
                tinv = [tinv[g] + d[g][:, LANES:] for g in n]
            apow = [d[g][:, :LANES] for g in n]
            yield
        tinv = [tinv[g] + _mm(apow[g], bd(tinv[g])) for g in n]
        yield
        x = [_mm(tinv[g], jnp.concatenate([bd(at[g]), bd(av[g][:c])], axis=1)) for g in n]
        wmat = [x[g][:, :LANES] for g in n]
        u0 = [x[g][:, LANES:] for g in n]
        yield
        d2 = [_mm(a_rb[g], jnp.concatenate([bd(wmat[g]), bd(u0[g])], axis=1)) for g in n]
        rp = [rt[g] + d2[g][:, :LANES] for g in n]
        y0 = [d2[g][:, LANES:] + av[g][c:] for g in n]
        yield
        fold = lambda z: jnp.where(head0, z[:HEAD_DIM], z[HEAD_DIM:])
        gp = [fold(_mm(wmat[g], bt_end[g], _dot_tn)) for g in n]
        npart = [fold(_mm(jnp.concatenate([u0[g], v[g]], axis=0),
                          jnp.concatenate([bt_end[g], kt_end[g]], axis=0), _dot_tn)) for g in n]
        yield
        spans = [[(jnp.where(diag, gp[g] + p["p_c"][g], gp[g]), npart[g]) for g in n]]
        while len(spans[-1]) > 1:
            prev = spans[-1]
            nxt = []
            for i in range(0, len(prev), 2):
                (ma, na), (mb, nb) = prev[i], prev[i + 1]
                prod = _mm(jnp.concatenate([ma, na], axis=0), bd(mb))
                nxt.append((prod[:HEAD_DIM], prod[HEAD_DIM:] + nb))
            spans.append(nxt)
            yield
        return dict(rp=rp, y0=y0, spans=spans, v_slab=p["v_slab"], rk=p["rk"])

    def finish(t, bj, sl):
        spans = t["spans"]
        states = {0: s_ref[bj]}
        top = len(spans) - 1
        m_all, n_all = spans[top][0]
        s_ref[bj] = _mm(states[0], bd(m_all)) + n_all
        yield
        for level in range(top, 0, -1):
            width = 1 << level
            for lo in range(0, group, width):
                ma, na = spans[level - 1][lo >> (level - 1)]
                states[lo + width // 2] = _mm(states[lo], bd(ma)) + na
            yield
        y = jnp.concatenate(
            [_mm(t["rp"][g], bd(states[g]), _dot_nt) + t["y0"][g] for g in range(group)], axis=0)
        yield
        mean = seg_sum(y) * (1.0 / HEAD_DIM)
        yc = y - mean
        var = seg_sum(yc * yc) * (1.0 / HEAD_DIM)
        yield
        yn = yc * lax.rsqrt(var + LNX_EPS) * lnx_w + lnx_b
        bonus = seg_sum(t["rk"]) * t["v_slab"]
        y_ref[bj, sl, :] = (yn + bonus).astype(y_ref.dtype)

    def run(*gens):
        results = [None] * len(gens)
        live = list(range(len(gens)))
        while live:
            for i in list(live):
                try:
                    next(gens[i])
                except StopIteration as stop:
                    results[i] = stop.value
                    live.remove(i)
        return results

    nslabs = zr_ref.shape[1] // rows
    units = [(bj, slice(t * rows, (t + 1) * rows)) for t in range(nslabs) for bj in range(zr_ref.shape[0])]
    prepared, = run(prepare(*units[0]))
    done = None
    for u in range(len(units)):
        gens = [chains(prepared)]
        if u + 1 < len(units):
            gens.append(prepare(*units[u + 1]))
        if done is not None:
            gens.append(finish(done, *units[u - 1]))
        out = run(*gens)
        done = out[0]
        if u + 1 < len(units):
            prepared = out[1]
    run(finish(done, *units[-1]))


def _rwkv(z3d, mu, w0, a0, k_k, k_a, r_k, lnx_w, lnx_b, w2, a2):
    b, s, a_shift = z3d.shape
    br_w = (a_shift - 2 * LORA_W) // 3
    npairs = br_w // LANES
    vec = lambda off: pl.BlockSpec((1, LANES), lambda bi, hp: (0, off + hp))
    lora = pl.BlockSpec((LORA_W, LANES), lambda bi, hp: (0, hp))
    nb = RWKV_BATCH
    assert b % nb == 0 and s % (RWKV_CHUNK * RWKV_GROUP) == 0
    zspec = lambda off: pl.BlockSpec((nb, s, LANES), lambda bi, hp: (bi, 0, off + hp))
    return pl.pallas_call(
        functools.partial(_rwkv_body, group=RWKV_GROUP),
        grid=(b // nb, npairs),
        in_specs=[
            zspec(0), zspec(npairs), zspec(2 * npairs),
            pl.BlockSpec((nb, s, LANES), lambda bi, hp: (bi, 0, 3 * npairs)),
            vec(0), vec(npairs), vec(2 * npairs),
            pl.BlockSpec((1, LANES), lambda bi, hp: (0, 3 * npairs)),
            vec(0), vec(0), vec(0), vec(0), vec(0), vec(0), vec(0), lora, lora,
        ],
        out_specs=pl.BlockSpec((nb, s, LANES), lambda bi, hp: (bi, 0, hp)),
        out_shape=jax.ShapeDtypeStruct((b, s, br_w), ACT_DTYPE),
        scratch_shapes=[pltpu.VMEM((nb, HEAD_DIM, LANES), F32), pltpu.VMEM((nb, SUBLANES, LANES), F32)],
        compiler_params=pltpu.CompilerParams(
            dimension_semantics=("arbitrary", "arbitrary"), vmem_limit_bytes=VMEM_LIMIT_BYTES),
        name="rwkv7_scan",
    )(z3d, z3d, z3d, z3d, mu, mu, mu, mu, w0, a0, k_k, k_a, r_k, lnx_w, lnx_b, w2, a2)


def _alibi_slopes(n):
    def pow2(m):
        start = 2.0 ** (-8.0 / m)
        return [start ** (i + 1) for i in range(m)]
    if math.log2(n).is_integer():
        return pow2(n)
    cl = 2 ** int(math.floor(math.log2(n)))
    return pow2(cl) + pow2(2 * cl)[0::2][: n - cl]


def _diff_attn_body(q_ref, k_ref, v_ref, slope_ref, lq1_ref, lk1_ref, lq2_ref, lk2_ref, sub_ref, o_ref,
                    kb_ref, vt_ref, band_ref, sa_ref, sb_ref, acc_ref, *, tq, lam_init):
    bi = pl.program_id(1)
    nb = q_ref.shape[0]
    tk = tq
    half = tq // 2
    nq = k_ref.shape[1] // tq
    log2e = math.log2(math.e)
    slope2 = slope_ref[0][:, :1] * log2e
    aug_lane = lax.broadcasted_iota(jnp.int32, (tk, LANES), 1)

    for bj in range(nb):
        for t in range(nq):
            rows = slice(t * tk, (t + 1) * tk)
            vt_ref[bj, :LANES, rows] = v_ref[bj, rows, :].astype(F32).T.astype(BF16)

    @pl.when(bi == 0)
    def _():
        ones_row = lax.broadcasted_iota(jnp.int32, (ATT_V_PAD, tk), 0) == 0
        for t in range(nq):
            rows = slice(t * tk, (t + 1) * tk)
            pos = (lax.broadcasted_iota(jnp.int32, (tk, LANES), 0) + t * tk).astype(F32)
            pieces = _split(slope2 * pos, ATT_BIAS_PIECES)
            aug = jnp.zeros((tk, LANES), F32)
            for i, piece in enumerate(pieces):
                aug = jnp.where(aug_lane == i, piece.astype(F32), aug)
            kb_ref[rows, :] = aug.astype(BF16)
            for bj in range(nb):
                vt_ref[bj, LANES:, rows] = jnp.where(ones_row, 1.0, 0.0).astype(BF16)
        kpos = lax.broadcasted_iota(jnp.int32, (tk, 2 * tq), 0).astype(F32)
        lane2 = lax.broadcasted_iota(jnp.int32, (1, 2 * tq), 1)
        qry = jnp.where(lane2 < tq, lane2, lane2 - tq)
        qlim = ((qry // ATT_CHUNK + 1) * ATT_CHUNK).astype(F32)
        band_ref[...] = jnp.where(
            kpos < qlim, (-2.0 * slope2) * jnp.maximum(kpos - qry.astype(F32), 0.0), MASK_VALUE)

    lane = lax.broadcasted_iota(jnp.int32, (2 * tq, LANES), 1)
    head0 = lane < HEAD_DIM
    first = lax.broadcasted_iota(jnp.int32, (2 * tq, LANES), 0) < tq
    ones_cols = jnp.where(lane < ATT_BIAS_PIECES, 1.0, 0.0)
    lam = (jnp.exp(jnp.sum(lq1_ref[...] * lk1_ref[...], axis=-1, keepdims=True))
           - jnp.exp(jnp.sum(lq2_ref[...] * lk2_ref[...], axis=-1, keepdims=True)) + lam_init)
    late = lambda x: jnp.concatenate([x[..., half:tq], x[..., tq + half:]], axis=-1)

    def update(p_rows, vt, m, m_new, cols):
        pv = _dot(vt, p_rows.astype(BF16))
        off = 0
        for cs in cols:
            w = cs.stop - cs.start
            if m is None:
                acc_ref[:, cs] = pv[:, off:off + w]
            else:
                acc_ref[:, cs] = jnp.exp2(m - m_new)[:, off:off + w] * acc_ref[:, cs] + pv[:, off:off + w]
            off += w

    def keys_aug(bj, keys):
        return jnp.concatenate([k_ref[bj, keys, :].astype(BF16), kb_ref[keys, :]], axis=1)

    def tiles_of(n):
        past = [(slice(j * tk, (j + 1) * tk), None, None) for j in range(n)]
        own0 = (slice(n * tq, n * tq + half), None, slice(0, half))
        own1 = (slice(n * tq + half, (n + 1) * tq), "late", slice(half, tq))
        return past + [own0, own1]

    all_cols = [slice(0, 2 * tq)]
    late_cols = [slice(half, tq), slice(tq + half, 2 * tq)]
    bufs = (sa_ref, sb_ref)
    for bj in range(nb):
        for n in range(nq):
            qrows = slice(n * tq, (n + 1) * tq)
            q = q_ref[bj, qrows, :].astype(F32) * (HEAD_DIM ** -0.5 * log2e)
            q2x = jnp.concatenate([q, q], axis=0)
            qcat = jnp.concatenate([jnp.where(first == head0, q2x, 0.0), ones_cols],
                                   axis=1).astype(BF16)
            qlate = jnp.concatenate([qcat[half:tq], qcat[tq + half:]], axis=0)
            tiles = tiles_of(n)

            def scores(i, dst_ref):
                keys, which, _ = tiles[i]
                nk = keys.stop - keys.start
                if which is None:
                    dst_ref[:nk, :] = _dot_nt(keys_aug(bj, keys), qcat)
                else:
                    dst_ref[:nk, :tq] = _dot_nt(keys_aug(bj, keys), qlate)

            m = None
            scores(0, bufs[0])
            for i, (keys, which, brows) in enumerate(tiles):
                if i + 1 < len(tiles):
                    scores(i + 1, bufs[(i + 1) % 2])
                nk = keys.stop - keys.start
                vt = vt_ref[bj, :, keys]
                if which is None:
                    s = bufs[i % 2][:nk, :]
                    if brows is not None:
                        s = s + band_ref[brows, :]
                    m_new = jnp.max(s, axis=0, keepdims=True)
                    if m is not None:
                        m_new = jnp.maximum(m, m_new)
                    update(jnp.exp2(s - m_new), vt, m, m_new, all_cols)
                    m = m_new
                else:
                    s = bufs[i % 2][:nk, :tq] + late(band_ref[brows, :])
                    ml = late(m)
                    m_new = jnp.maximum(ml, jnp.max(s, axis=0, keepdims=True))
                    update(jnp.exp2(s - m_new), vt, ml, m_new, late_cols)
            on = acc_ref[:LANES, :] * (1.0 / acc_ref[LANES:LANES + 1, :])
            o = on[:, :tq] - lam * on[:, tq:]
            ms = jnp.mean(o * o, axis=0, keepdims=True)
            o = o * lax.rsqrt(ms + NORM_EPS) * sub_ref[...] * (1.0 - lam_init)
            o_ref[bj, qrows, :] = o.T.astype(o_ref.dtype)


def _diff_attn(q3d, k3d, v3d, slopes, lam_vectors, subln_col, lam_init, tq):
    b, s, br_w = q3d.shape
    nh = br_w // LANES
    nb = ATT_BATCH
    assert b % nb == 0 and s % tq == 0
    lam_spec = pl.BlockSpec((1, HEAD_DIM), lambda h, bi: (0, 0))
    seq_spec = pl.BlockSpec((nb, s, LANES), lambda h, bi: (bi, 0, h))
    return pl.pallas_call(
        functools.partial(_diff_attn_body, tq=tq, lam_init=lam_init),
        grid=(nh, b // nb),
        in_specs=[
            seq_spec, seq_spec, seq_spec,
            pl.BlockSpec((1, 1, LANES), lambda h, bi: (h, 0, 0)),
            lam_spec, lam_spec, lam_spec, lam_spec,
            pl.BlockSpec((LANES, 1), lambda h, bi: (0, 0)),
        ],
        out_specs=seq_spec,
        out_shape=jax.ShapeDtypeStruct((b, s, br_w), ACT_DTYPE),
        scratch_shapes=[pltpu.VMEM((s, LANES), BF16), pltpu.VMEM((nb, LANES + ATT_V_PAD, s), BF16),
                        pltpu.VMEM((tq, 2 * tq), F32),
                        pltpu.VMEM((tq, 2 * tq), F32), pltpu.VMEM((tq, 2 * tq), F32),
                        pltpu.VMEM((LANES + ATT_V_PAD, 2 * tq), F32)],
        compiler_params=pltpu.CompilerParams(
            dimension_semantics=("arbitrary", "arbitrary"),
            vmem_limit_bytes=VMEM_LIMIT_BYTES),
        name="diff_attn",
    )(q3d, k3d, v3d, slopes, *lam_vectors, subln_col)


def _out_body(ybr_ref, gate_ref, qm_ref, gm_ref, x_ref, km_ref, vm_ref, w_ref, pn_ref, o_ref, wb_ref, *, br_w):
    @pl.when(pl.program_id(0) == 0)
    def _():
        wb_ref[...] = w_ref[0].astype(BF16)

    tm = x_ref.shape[0]
    ml = km_ref.shape[2] // MEM_HEADS
    subs = [slice(r * OUT_SUB_ROWS, (r + 1) * OUT_SUB_ROWS) for r in range(tm // OUT_SUB_ROWS)]
    qscale = HEAD_DIM ** -0.5 * math.log2(math.e)
    s = [_dot_nt((qm_ref[rs, :].astype(F32) * qscale).astype(BF16), km_ref[0, 0]) for rs in subs]
    y_br = (ybr_ref[...].astype(F32) * _silu(gate_ref[...].astype(F32))).astype(BF16)
    pcat = []
    for sr in s:
        ps = []
        for hd in range(MEM_HEADS):
            sh = sr[:, hd * ml:(hd + 1) * ml]
            p = jnp.exp2(sh - jnp.max(sh, axis=-1, keepdims=True))
            ps.append((p * (1.0 / jnp.sum(p, axis=-1, keepdims=True))).astype(BF16))
        pcat.append(jnp.concatenate(ps, axis=1))
    y_mem = [_dot(pc, vm_ref[0, 0]) for pc in pcat]
    y_mem = (jnp.concatenate(y_mem, axis=0) * _silu(gm_ref[...].astype(F32))).astype(BF16)
    y = _dot(y_br, wb_ref[:br_w, :]) + _dot(y_mem, wb_ref[br_w:, :])
    o_ref[...] = x_ref[...] + _rms(y, pn_ref[0])


def _out_proj(ybr, gate, qm, gm, x2d, k_mem, v_mem, layer, w, post_g, seq, tm):
    m, d = x2d.shape
    br_w = ybr.shape[1]
    ml = k_mem.shape[2]
    per_b = seq // tm
    mem_spec = pl.BlockSpec((1, 1, ml, MEM_W), lambda i: (layer, i // per_b, 0, 0))
    row = lambda w: pl.BlockSpec((tm, w), lambda i: (i, 0))
    return pl.pallas_call(
        functools.partial(_out_body, br_w=br_w),
        grid=(m // tm,),
        in_specs=[
            row(br_w), row(br_w), row(MEM_W), row(MEM_W), row(d),
            mem_spec, mem_spec,
            pl.BlockSpec((1, d, d), lambda i: (layer, 0, 0)),
            pl.BlockSpec((1, 1, d), lambda i: (layer, 0, 0)),
        ],
        out_specs=row(d),
        out_shape=jax.ShapeDtypeStruct((m, d), F32),
        scratch_shapes=[pltpu.VMEM((d, d), BF16)],
        compiler_params=pltpu.CompilerParams(
            dimension_semantics=("arbitrary",), vmem_limit_bytes=VMEM_LIMIT_BYTES),
        name="out_proj",
    )(ybr, gate, qm, gm, x2d, k_mem, v_mem, w, post_g)


def kernel(x, mem, pre_norm, post_norm, w_out, mem_norm, w_mem_kv, a_w_in, a_shift_mu, a_w0, a_w2,
           a_a0, a_a2, a_k_k, a_k_a, a_r_k, a_lnx_w, a_lnx_b, kv_norm, w_kv, b_w_in, b_lam_q1,
           b_lam_k1, b_lam_q2, b_lam_k2, b_subln):
    bsz, seq, d = x.shape
    depth = pre_norm.shape[0]
    n_a = a_w_in.shape[0]
    br_w = d - MEM_W
    a_shift = 3 * br_w + 2 * LORA_W
    m = bsz * seq
    tm = 1024
    x2d = x.reshape(m, d)
    slopes = jnp.asarray(
        np.repeat(np.array(_alibi_slopes(br_w // LANES), np.float32)[:, None, None], LANES, axis=2))

    k_mem, v_mem = _mem_kv(mem, mem_norm.reshape(depth, 1, d), w_mem_kv)
    for l in range(depth):
        if l < n_a:
            i = l
            cols = [(0, a_shift), (a_shift, a_shift + br_w), (a_shift + br_w, a_shift + br_w + MEM_W),
                    (a_shift + br_w + MEM_W, a_shift + br_w + 2 * MEM_W)]
            z, gate, q_mem, g_mem = _norm_proj(
                x2d, [((pre_norm.reshape(depth, 1, d), l), a_w_in[i], cols)], tm)
            row = lambda p: p[i].reshape(1, -1)
            y_br = _rwkv(z.reshape(bsz, seq, a_shift), row(a_shift_mu), row(a_w0), row(a_a0), row(a_k_k),
                         row(a_k_a), row(a_r_k), row(a_lnx_w), row(a_lnx_b), a_w2[i], a_a2[i]).reshape(m, br_w)
        else:
            i = l - n_a
            cols = [(0, br_w), (br_w, 2 * br_w), (2 * br_w, 2 * br_w + MEM_W),
                    (2 * br_w + MEM_W, 2 * br_w + 2 * MEM_W)]
            groups = [((pre_norm.reshape(depth, 1, d), l), b_w_in[i], cols)]
            if l == n_a:
                groups.append(((kv_norm.reshape(1, 1, d), 0), w_kv, [(0, br_w), (br_w, 2 * br_w)]))
                q, gate, q_mem, g_mem, k_sh, v_sh = _norm_proj(x2d, groups, tm)
                k_sh = k_sh.reshape(bsz, seq, br_w)
                v_sh = v_sh.reshape(bsz, seq, br_w)
            else:
                q, gate, q_mem, g_mem = _norm_proj(x2d, groups, tm)
            lam_init = 0.8 - 0.6 * math.exp(-0.3 * l)
            lam_vectors = [p[i].reshape(1, HEAD_DIM) for p in (b_lam_q1, b_lam_k1, b_lam_q2, b_lam_k2)]
            y_br = _diff_attn(q.reshape(bsz, seq, br_w), k_sh, v_sh, slopes, lam_vectors,
                              b_subln[i].reshape(LANES, 1), lam_init, ATT_Q_TILE).reshape(m, br_w)
        x2d = _out_proj(y_br, gate, q_mem, g_mem, x2d, k_mem, v_mem, l, w_out,
                        post_norm.reshape(depth, 1, d), seq, OUT_ROWS)
    return x2d.reshape(bsz, seq, d)
```

```python
import functools
import math

import numpy as np
import jax
import jax.numpy as jnp
from jax import lax
from jax.experimental import pallas as pl
from jax.experimental.pallas import tpu as pltpu

F32 = jnp.float32
BF16 = jnp.bfloat16
ACT_DTYPE = BF16

HEAD_DIM = 64
LANES = 128
SUBLANES = 8
MEM_HEADS = 4
MEM_W = MEM_HEADS * HEAD_DIM
LORA_W = 64
ATT_CHUNK = 64
ATT_Q_TILE = 512
ATT_BATCH = 2
ATT_BIAS_PIECES = 3
ATT_V_PAD = 16
RWKV_CHUNK = 64
RWKV_GROUP = 16
RWKV_CUMSUM_ROWS = 128
RWKV_BATCH = 2
RWKV_CHAIN_LEAD_STAGES = 2
OUT_ROWS = 1024
OUT_SUB_ROWS = 512
NORM_EPS = 1e-6
LNX_EPS = 64e-5
KK_NORM_FLOOR = 1e-12
MASK_VALUE = -1e30
VMEM_LIMIT_BYTES = 56 * 1024 * 1024


def _dot(a, b):
    return jnp.dot(a, b, preferred_element_type=F32)


def _dot_nt(a, b):
    return lax.dot_general(a, b, (((1,), (1,)), ((), ())), preferred_element_type=F32)


def _dot_tn(a, b):
    return lax.dot_general(a, b, (((0,), (0,)), ((), ())), preferred_element_type=F32)


def _split(x, pieces):
    out = []
    rem = x
    for i in range(pieces):
        p = rem.astype(BF16)
        out.append(p)
        if i + 1 < pieces:
            rem = rem - p.astype(F32)
    return out


def _mm(a, b, fn=_dot, pa=1, pb=1):
    aps = _split(a, pa)
    bps = _split(b, pb)
    order = max(pa, pb)
    acc = None
    for i, ap in enumerate(aps):
        for j, bp in enumerate(bps):
            if i + j < order:
                t = fn(ap, bp)
                acc = t if acc is None else acc + t
    return acc


def _rms(x, g):
    ms = jnp.mean(x * x, axis=-1, keepdims=True)
    return (x * lax.rsqrt(ms + NORM_EPS)) * g


def _silu(x):
    h = 0.5 * x
    return h + h * jnp.tanh(h)


def _norm_proj_body(x_ref, *refs, groups):
    ng = len(groups)
    nout = sum(len(cols) for cols in groups)
    o_refs = refs[2 * ng:2 * ng + nout]
    wb_refs = refs[2 * ng + nout:]

    @pl.when(pl.program_id(0) == 0)
    def _():
        for gi in range(ng):
            wb_refs[gi][...] = refs[2 * gi + 1][...].astype(BF16)

    x = x_ref[...]
    ms = jnp.mean(x * x, axis=-1, keepdims=True)
    xn = x * lax.rsqrt(ms + NORM_EPS)
    k = 0
    for gi, cols in enumerate(groups):
        h = (xn * refs[2 * gi][0]).astype(BF16)
        for lo, hi in cols:
            o_refs[k][...] = _dot(h, wb_refs[gi][:, lo:hi]).astype(o_refs[k].dtype)
            k += 1


def _norm_proj(x2d, groups, tm):
    m, d = x2d.shape
    cols = [c for _, _, cs in groups for c in cs]
    in_specs = [pl.BlockSpec((tm, d), lambda i: (i, 0))]
    operands = [x2d]
    for (g, row), w, _ in groups:
        in_specs += [pl.BlockSpec((1, 1, d), lambda i, row=row: (row, 0, 0)), pl.BlockSpec(w.shape, lambda i: (0, 0))]
        operands += [g, w]
    return pl.pallas_call(
        functools.partial(_norm_proj_body, groups=tuple(tuple(cs) for _, _, cs in groups)),
        grid=(m // tm,),
        in_specs=in_specs,
        out_specs=[pl.BlockSpec((tm, hi - lo), lambda i: (i, 0)) for lo, hi in cols],
        out_shape=[jax.ShapeDtypeStruct((m, hi - lo), ACT_DTYPE) for lo, hi in cols],
        scratch_shapes=[pltpu.VMEM(w.shape, BF16) for _, w, _ in groups],
        compiler_params=pltpu.CompilerParams(
            dimension_semantics=("arbitrary",), vmem_limit_bytes=VMEM_LIMIT_BYTES),
        name="norm_proj",
    )(*operands)


def _mem_kv_body(mem_ref, g_ref, w_ref, k_ref, v_ref):
    nb, ml, d = mem_ref.shape
    h = _rms(mem_ref[...].reshape(nb * ml, d), g_ref[0]).astype(BF16)
    kv = _dot(h, w_ref[0].astype(BF16))
    lane = lax.broadcasted_iota(jnp.int32, (ml, MEM_W), 1)
    for bj in range(nb):
        rows = slice(bj * ml, (bj + 1) * ml)
        for hd in range(MEM_HEADS):
            in_head = (lane >= hd * HEAD_DIM) & (lane < (hd + 1) * HEAD_DIM)
            k_ref[0, bj, hd * ml:(hd + 1) * ml, :] = jnp.where(in_head, kv[rows, :MEM_W], 0.0).astype(k_ref.dtype)
            v_ref[0, bj, hd * ml:(hd + 1) * ml, :] = jnp.where(in_head, kv[rows, MEM_W:], 0.0).astype(v_ref.dtype)


def _mem_kv(mem, gains, w):
    b, ml, d = mem.shape
    nl = gains.shape[0]
    out_spec = pl.BlockSpec((1, b, MEM_HEADS * ml, MEM_W), lambda l: (l, 0, 0, 0))
    return pl.pallas_call(
        _mem_kv_body,
        grid=(nl,),
        in_specs=[
            pl.BlockSpec((b, ml, d), lambda l: (0, 0, 0)),
            pl.BlockSpec((1, 1, d), lambda l: (l, 0, 0)),
            pl.BlockSpec((1, d, 2 * MEM_W), lambda l: (l, 0, 0)),
        ],
        out_specs=[out_spec] * 2,
        out_shape=[jax.ShapeDtypeStruct((nl, b, MEM_HEADS * ml, MEM_W), ACT_DTYPE)] * 2,
        compiler_params=pltpu.CompilerParams(
            dimension_semantics=("arbitrary",), vmem_limit_bytes=VMEM_LIMIT_BYTES),
        name="mem_kv",
    )(mem, gains, w)


def _rwkv_body(zr_ref, zk_ref, zv_ref, zwa_ref, mur_ref, muk_ref, muv_ref, muwa_ref, w0_ref, a0_ref,
               kk_ref, ka_ref, rk_ref, lnw_ref, lnb_ref, w2_ref, a2_ref, y_ref, s_ref, prev_ref, *, group):
    c = RWKV_CHUNK
    rows = group * c
    lane = lax.broadcasted_iota(jnp.int32, (c, LANES), 1)
    row = lax.broadcasted_iota(jnp.int32, (c, LANES), 0)
    head0 = lane < HEAD_DIM
    scol = jnp.where(head0, lane, lane - HEAD_DIM)
    strict = scol < row
    incl = scol <= row
    diag = scol == row
    r2 = lax.broadcasted_iota(jnp.int32, (LANES, LANES), 0)
    c2 = lax.broadcasted_iota(jnp.int32, (LANES, LANES), 1)
    blockmask = (r2 < HEAD_DIM) == (c2 < HEAD_DIM)
    blockones = jnp.where(blockmask, 1.0, 0.0).astype(BF16)
    cum_rows = min(rows, RWKV_CUMSUM_ROWS)
    tr = lax.broadcasted_iota(jnp.int32, (cum_rows, cum_rows), 0)
    tc = lax.broadcasted_iota(jnp.int32, (cum_rows, cum_rows), 1)
    tril_ones = jnp.where((tc <= tr) & (tc // c == tr // c), 1.0, 0.0).astype(BF16)
    slab_lane = lax.broadcasted_iota(jnp.int32, (rows, LANES), 1)
    slab_head0 = slab_lane < HEAD_DIM
    slab_first = lax.broadcasted_iota(jnp.int32, (rows, LANES), 0) == 0

    def bd(x):
        return jnp.concatenate([jnp.where(head0, x, 0.0), jnp.where(head0, 0.0, x)], axis=0)

    def seg_sum(x):
        return _dot(x.astype(BF16), blockones)

    mu_r, mu_k, mu_v, mu_wa = mur_ref[...], muk_ref[...], muv_ref[...], muwa_ref[...]
    w0, a0 = w0_ref[...], a0_ref[...]
    k_k, k_a, r_k = kk_ref[...], ka_ref[...], rk_ref[...]
    lnx_w, lnx_b = lnw_ref[...], lnb_ref[...]
    zeros = jnp.zeros((LORA_W, LANES), F32)
    w2a2 = jnp.concatenate([jnp.concatenate([w2_ref[...], zeros], axis=1),
                            jnp.concatenate([zeros, a2_ref[...]], axis=1)], axis=0)

    s_ref[...] = jnp.zeros_like(s_ref)
    prev_ref[...] = jnp.zeros_like(prev_ref)

    def shift_mix(ref, bj, slot, sl, mu):
        z = ref[bj, sl, :].astype(F32)
        zp = pltpu.roll(z, 1, axis=0)
        zp = jnp.where(slab_first, prev_ref[bj, slot:slot + 1, :], zp)
        prev_ref[bj, slot:slot + 1, :] = z[rows - 1:rows, :]
        return z + (zp - z) * mu


    def prepare(bj, sl):
        r = shift_mix(zr_ref, bj, 0, sl, mu_r)
        k = shift_mix(zk_ref, bj, 1, sl, mu_k)
        yield
        v = shift_mix(zv_ref, bj, 2, sl, mu_v)
        wa = shift_mix(zwa_ref, bj, 3, sl, mu_wa)
        lora = _mm(jnp.where(slab_head0, jnp.tanh(wa), wa), w2a2)
        yield
        wlog = w0 + lora[:, :LANES]
        nw = -wlog
        w = -(jnp.maximum(nw, 0.0) + jnp.log(1.0 + jnp.exp(-jnp.abs(nw)))) - 0.5
        logw = -jnp.exp(w)
        a = 1.0 / (1.0 + jnp.exp(-(a0 + lora[:, LANES:])))
        kk = k * k_k
        kk = kk * lax.rsqrt(jnp.maximum(seg_sum(kk * kk), KK_NORM_FLOOR))
        yield
        kmod = k * (1.0 + (a - 1.0) * k_a)
        alpha = -kk
        beta = kk * a
        l2 = jnp.concatenate(_split(logw, 2), axis=1)
        cum2 = jnp.concatenate(
            [_dot(tril_ones, l2[i:i + cum_rows]) for i in range(0, rows, cum_rows)], axis=0)
        cum = cum2[:, :LANES] + cum2[:, LANES:]
        yield
        e_neg = jnp.exp(-cum)
        at = alpha * jnp.exp(cum - logw)
        rt = r * jnp.exp(cum)
        bt = beta * e_neg
        kt = kmod * e_neg
        yield
        chunks = [slice(g * c, (g + 1) * c) for g in range(group)]
        cum_c = [cum[cs.stop - 1:cs.stop, :] for cs in chunks]
        e_end = [jnp.exp(cum_c[g] - cum[cs]) for g, cs in enumerate(chunks)]
        per = lambda x: [x[cs] for cs in chunks]
        return dict(
            at=per(at), rt=per(rt), bt=per(bt), kt=per(kt), v=per(v),
            bt_end=[beta[cs] * e_end[g] for g, cs in enumerate(chunks)],
            kt_end=[kmod[cs] * e_end[g] for g, cs in enumerate(chunks)],
            p_c=[jnp.exp(x) for x in cum_c], v_slab=v, rk=r * kmod * r_k)

    def chains(p):
        at, rt, bt, kt, bt_end, kt_end, v = (p[x] for x in ("at", "rt", "bt", "kt", "bt_end", "kt_end", "v"))
        n = range(group)
        lhs = [jnp.concatenate([at[g], rt[g]], axis=0) for g in n]
        xbk = [_mm(lhs[g], jnp.concatenate([bd(bt[g]), bd(kt[g])], axis=0), _dot_nt) for g in n]
        yield
        a_rb = [jnp.where(incl, xbk[g][c:, :LANES], 0.0) for g in n]
        akrk = [jnp.concatenate([jnp.where(strict, xbk[g][:c, LANES:], 0.0),
                                 jnp.where(incl, xbk[g][c:, LANES:], 0.0)], axis=0) for g in n]
        av = [_mm(akrk[g], bd(v[g])) for g in n]
        yield
        apow = [jnp.where(strict, xbk[g][:c, :LANES], 0.0) for g in n]
        tinv = [jnp.where(diag, 1.0, 0.0) + apow[g] for g in n]
        nfac = int(math.log2(c))
        for i in range(1, nfac):
            rhs = [[bd(apow[g])] + ([bd(tinv[g])] if i > 1 else []) for g in n]
            d = [_mm(apow[g], jnp.concatenate(rhs[g], axis=1)) for g in n]
            if i > 1:
                tinv = [tinv[g] + d[g][:, LANES:] for g in n]
            apow = [d[g][:, :LANES] for g in n]
            yield
        tinv = [tinv[g] + _mm(apow[g], bd(tinv[g])) for g in n]
        yield
        x = [_mm(tinv[g], jnp.concatenate([bd(at[g]), bd(av[g][:c])], axis=1)) for g in n]
        wmat = [x[g][:, :LANES] for g in n]
        u0 = [x[g][:, LANES:] for g in n]
        yield
        d2 = [_mm(a_rb[g], jnp.concatenate([bd(wmat[g]), bd(u0[g])], axis=1)) for g in n]
        rp = [rt[g] + d2[g][:, :LANES] for g in n]
        y0 = [d2[g][:, LANES:] + av[g][c:] for g in n]
        yield
        fold = lambda z: jnp.where(head0, z[:HEAD_DIM], z[HEAD_DIM:])
        gp = [fold(_mm(wmat[g], bt_end[g], _dot_tn)) for g in n]
        npart = [fold(_mm(jnp.concatenate([u0[g], v[g]], axis=0),
                          jnp.concatenate([bt_end[g], kt_end[g]], axis=0), _dot_tn)) for g in n]
        yield
        spans = [[(jnp.where(diag, gp[g] + p["p_c"][g], gp[g]), npart[g]) for g in n]]
        while len(spans[-1]) > 1:
            prev = spans[-1]
            nxt = []
            for i in range(0, len(prev), 2):
                (ma, na), (mb, nb) = prev[i], prev[i + 1]
                prod = _mm(jnp.concatenate([ma, na], axis=0), bd(mb))
                nxt.append((prod[:HEAD_DIM], prod[HEAD_DIM:] + nb))
            spans.append(nxt)
            yield
        return dict(rp=rp, y0=y0, spans=spans, v_slab=p["v_slab"], rk=p["rk"])

    def finish(t, bj, sl):
        spans = t["spans"]
        states = {0: s_ref[bj]}
        top = len(spans) - 1
        m_all, n_all = spans[top][0]
        s_ref[bj] = _mm(states[0], bd(m_all)) + n_all
        yield
        for level in range(top, 0, -1):
            width = 1 << level
            for lo in range(0, group, width):
                ma, na = spans[level - 1][lo >> (level - 1)]
                states[lo + width // 2] = _mm(states[lo], bd(ma)) + na
            yield
        y = jnp.concatenate(
            [_mm(t["rp"][g], bd(states[g]), _dot_nt) + t["y0"][g] for g in range(group)], axis=0)
        yield
        mean = seg_sum(y) * (1.0 / HEAD_DIM)
        yc = y - mean
        var = seg_sum(yc * yc) * (1.0 / HEAD_DIM)
        yield
        yn = yc * lax.rsqrt(var + LNX_EPS) * lnx_w + lnx_b
        bonus = seg_sum(t["rk"]) * t["v_slab"]
        y_ref[bj, sl, :] = (yn + bonus).astype(y_ref.dtype)

    def run(*gens, lead_stages=1):
        results = [None] * len(gens)
        live = list(range(len(gens)))
        while live:
            for i in list(live):
                for _ in range(lead_stages if i == 0 else 1):
                    if i not in live:
                        break
                    try:
                        next(gens[i])
                    except StopIteration as stop:
                        results[i] = stop.value
                        live.remove(i)
        return results

    nslabs = zr_ref.shape[1] // rows
    units = [(bj, slice(t * rows, (t + 1) * rows)) for t in range(nslabs) for bj in range(zr_ref.shape[0])]
    prepared, = run(prepare(*units[0]))
    done = None
    for u in range(len(units)):
        gens = [chains(prepared)]
        if u + 1 < len(units):
            gens.append(prepare(*units[u + 1]))
        if done is not None:
            gens.append(finish(done, *units[u - 1]))
        out = run(*gens, lead_stages=RWKV_CHAIN_LEAD_STAGES)
        done = out[0]
        if u + 1 < len(units):
            prepared = out[1]
    run(finish(done, *units[-1]))


def _rwkv(z3d, mu, w0, a0, k_k, k_a, r_k, lnx_w, lnx_b, w2, a2):
    b, s, a_shift = z3d.shape
    br_w = (a_shift - 2 * LORA_W) // 3
    npairs = br_w // LANES
    vec = lambda off: pl.BlockSpec((1, LANES), lambda bi, hp: (0, off + hp))
    lora = pl.BlockSpec((LORA_W, LANES), lambda bi, hp: (0, hp))
    nb = RWKV_BATCH
    assert b % nb == 0 and s % (RWKV_CHUNK * RWKV_GROUP) == 0
    zspec = lambda off: pl.BlockSpec((nb, s, LANES), lambda bi, hp: (bi, 0, off + hp))
    return pl.pallas_call(
        functools.partial(_rwkv_body, group=RWKV_GROUP),
        grid=(b // nb, npairs),
        in_specs=[
            zspec(0), zspec(npairs), zspec(2 * npairs),
            pl.BlockSpec((nb, s, LANES), lambda bi, hp: (bi, 0, 3 * npairs)),
            vec(0), vec(npairs), vec(2 * npairs),
            pl.BlockSpec((1, LANES), lambda bi, hp: (0, 3 * npairs)),
            vec(0), vec(0), vec(0), vec(0), vec(0), vec(0), vec(0), lora, lora,
        ],
        out_specs=pl.BlockSpec((nb, s, LANES), lambda bi, hp: (bi, 0, hp)),
        out_shape=jax.ShapeDtypeStruct((b, s, br_w), ACT_DTYPE),
        scratch_shapes=[pltpu.VMEM((nb, HEAD_DIM, LANES), F32), pltpu.VMEM((nb, SUBLANES, LANES), F32)],
        compiler_params=pltpu.CompilerParams(
            dimension_semantics=("arbitrary", "arbitrary"), vmem_limit_bytes=VMEM_LIMIT_BYTES),
        name="rwkv7_scan",
    )(z3d, z3d, z3d, z3d, mu, mu, mu, mu, w0, a0, k_k, k_a, r_k, lnx_w, lnx_b, w2, a2)


def _alibi_slopes(n):
    def pow2(m):
        start = 2.0 ** (-8.0 / m)
        return [start ** (i + 1) for i in range(m)]
    if math.log2(n).is_integer():
        return pow2(n)
    cl = 2 ** int(math.floor(math.log2(n)))
    return pow2(cl) + pow2(2 * cl)[0::2][: n - cl]


def _diff_attn_body(q_ref, k_ref, v_ref, slope_ref, lq1_ref, lk1_ref, lq2_ref, lk2_ref, sub_ref, o_ref,
                    kb_ref, vt_ref, band_ref, sa_ref, sb_ref, acc_ref, *, tq, lam_init):
    bi = pl.program_id(1)
    nb = q_ref.shape[0]
    tk = tq
    half = tq // 2
    nq = k_ref.shape[1] // tq
    log2e = math.log2(math.e)
    slope2 = slope_ref[0][:, :1] * log2e
    aug_lane = lax.broadcasted_iota(jnp.int32, (tk, LANES), 1)

    for bj in range(nb):
        for t in range(nq):
            rows = slice(t * tk, (t + 1) * tk)
            kb_ref[bj, rows, :LANES] = k_ref[bj, rows, :].astype(BF16)
            vt_ref[bj, :LANES, rows] = v_ref[bj, rows, :].astype(F32).T.astype(BF16)

    @pl.when(bi == 0)
    def _():
        ones_row = lax.broadcasted_iota(jnp.int32, (ATT_V_PAD, tk), 0) == 0
        for t in range(nq):
            rows = slice(t * tk, (t + 1) * tk)
            pos = (lax.broadcasted_iota(jnp.int32, (tk, LANES), 0) + t * tk).astype(F32)
            pieces = _split(slope2 * pos, ATT_BIAS_PIECES)
            aug = jnp.zeros((tk, LANES), F32)
            for i, piece in enumerate(pieces):
                aug = jnp.where(aug_lane == i, piece.astype(F32), aug)
            for bj in range(nb):
                kb_ref[bj, rows, LANES:] = aug.astype(BF16)
                vt_ref[bj, LANES:, rows] = jnp.where(ones_row, 1.0, 0.0).astype(BF16)
        kpos = lax.broadcasted_iota(jnp.int32, (tk, 2 * tq), 0).astype(F32)
        lane2 = lax.broadcasted_iota(jnp.int32, (1, 2 * tq), 1)
        qry = jnp.where(lane2 < tq, lane2, lane2 - tq)
        qlim = ((qry // ATT_CHUNK + 1) * ATT_CHUNK).astype(F32)
        band_ref[...] = jnp.where(
            kpos < qlim, (-2.0 * slope2) * jnp.maximum(kpos - qry.astype(F32), 0.0), MASK_VALUE)

    lane = lax.broadcasted_iota(jnp.int32, (2 * tq, LANES), 1)
    head0 = lane < HEAD_DIM
    first = lax.broadcasted_iota(jnp.int32, (2 * tq, LANES), 0) < tq
    ones_cols = jnp.where(lane < ATT_BIAS_PIECES, 1.0, 0.0)
    lam = (jnp.exp(jnp.sum(lq1_ref[...] * lk1_ref[...], axis=-1, keepdims=True))
           - jnp.exp(jnp.sum(lq2_ref[...] * lk2_ref[...], axis=-1, keepdims=True)) + lam_init)
    late = lambda x: jnp.concatenate([x[..., half:tq], x[..., tq + half:]], axis=-1)

    def update(p_rows, vt, m, m_new, cols):
        pv = _dot(vt, p_rows.astype(BF16))
        off = 0
        for cs in cols:
            w = cs.stop - cs.start
            if m is None:
                acc_ref[:, cs] = pv[:, off:off + w]
            else:
                acc_ref[:, cs] = jnp.exp2(m - m_new)[:, off:off + w] * acc_ref[:, cs] + pv[:, off:off + w]
            off += w

    def tiles_of(n):
        past = [(slice(j * tk, (j + 1) * tk), None, None) for j in range(n)]
        own0 = (slice(n * tq, n * tq + half), None, slice(0, half))
        own1 = (slice(n * tq + half, (n + 1) * tq), "late", slice(half, tq))
        return past + [own0, own1]

    all_cols = [slice(0, 2 * tq)]
    late_cols = [slice(half, tq), slice(tq + half, 2 * tq)]
    bufs = (sa_ref, sb_ref)
    for bj in range(nb):
        for n in range(nq):
            qrows = slice(n * tq, (n + 1) * tq)
            q = q_ref[bj, qrows, :].astype(F32) * (HEAD_DIM ** -0.5 * log2e)
            q2x = jnp.concatenate([q, q], axis=0)
            qcat = jnp.concatenate([jnp.where(first == head0, q2x, 0.0), ones_cols],
                                   axis=1).astype(BF16)
            qlate = jnp.concatenate([qcat[half:tq], qcat[tq + half:]], axis=0)
            tiles = tiles_of(n)

            def scores(i, dst_ref):
                keys, which, _ = tiles[i]
                nk = keys.stop - keys.start
                if which is None:
                    dst_ref[:nk, :] = _dot_nt(kb_ref[bj, keys, :], qcat)
                else:
                    dst_ref[:nk, :tq] = _dot_nt(kb_ref[bj, keys, :], qlate)

            m = None
            scores(0, bufs[0])
            for i, (keys, which, brows) in enumerate(tiles):
                if i + 1 < len(tiles):
                    scores(i + 1, bufs[(i + 1) % 2])
                nk = keys.stop - keys.start
                vt = vt_ref[bj, :, keys]
                if which is None:
                    s = bufs[i % 2][:nk, :]
                    if brows is not None:
                        s = s + band_ref[brows, :]
                    m_new = jnp.max(s, axis=0, keepdims=True)
                    if m is not None:
                        m_new = jnp.maximum(m, m_new)
                    update(jnp.exp2(s - m_new), vt, m, m_new, all_cols)
                    m = m_new
                else:
                    s = bufs[i % 2][:nk, :tq] + late(band_ref[brows, :])
                    ml = late(m)
                    m_new = jnp.maximum(ml, jnp.max(s, axis=0, keepdims=True))
                    update(jnp.exp2(s - m_new), vt, ml, m_new, late_cols)
            on = acc_ref[:LANES, :] * (1.0 / acc_ref[LANES:LANES + 1, :])
            o = on[:, :tq] - lam * on[:, tq:]
            ms = jnp.mean(o * o, axis=0, keepdims=True)
            o = o * lax.rsqrt(ms + NORM_EPS) * sub_ref[...] * (1.0 - lam_init)
            o_ref[bj, qrows, :] = o.T.astype(o_ref.dtype)


def _diff_attn(q3d, k3d, v3d, slopes, lam_vectors, subln_col, lam_init, tq):
    b, s, br_w = q3d.shape
    nh = br_w // LANES
    nb = ATT_BATCH
    assert b % nb == 0 and s % tq == 0
    lam_spec = pl.BlockSpec((1, HEAD_DIM), lambda h, bi: (0, 0))
    seq_spec = pl.BlockSpec((nb, s, LANES), lambda h, bi: (bi, 0, h))
    return pl.pallas_call(
        functools.partial(_diff_attn_body, tq=tq, lam_init=lam_init),
        grid=(nh, b // nb),
        in_specs=[
            seq_spec, seq_spec, seq_spec,
            pl.BlockSpec((1, 1, LANES), lambda h, bi: (h, 0, 0)),
            lam_spec, lam_spec, lam_spec, lam_spec,
            pl.BlockSpec((LANES, 1), lambda h, bi: (0, 0)),
        ],
        out_specs=seq_spec,
        out_shape=jax.ShapeDtypeStruct((b, s, br_w), ACT_DTYPE),
        scratch_shapes=[pltpu.VMEM((nb, s, 2 * LANES), BF16), pltpu.VMEM((nb, LANES + ATT_V_PAD, s), BF16),
                        pltpu.VMEM((tq, 2 * tq), F32),
                        pltpu.VMEM((tq, 2 * tq), F32), pltpu.VMEM((tq, 2 * tq), F32),
                        pltpu.VMEM((LANES + ATT_V_PAD, 2 * tq), F32)],
        compiler_params=pltpu.CompilerParams(
            dimension_semantics=("arbitrary", "arbitrary"),
            vmem_limit_bytes=VMEM_LIMIT_BYTES),
        name="diff_attn",
    )(q3d, k3d, v3d, slopes, *lam_vectors, subln_col)


def _out_body(ybr_ref, gate_ref, qm_ref, gm_ref, x_ref, km_ref, vm_ref, w_ref, pn_ref, o_ref, wb_ref, *, br_w):
    @pl.when(pl.program_id(0) == 0)
    def _():
        wb_ref[...] = w_ref[0].astype(BF16)

    tm = x_ref.shape[0]
    ml = km_ref.shape[2] // MEM_HEADS
    subs = [slice(r * OUT_SUB_ROWS, (r + 1) * OUT_SUB_ROWS) for r in range(tm // OUT_SUB_ROWS)]
    qscale = HEAD_DIM ** -0.5 * math.log2(math.e)
    s = [_dot_nt((qm_ref[rs, :].astype(F32) * qscale).astype(BF16), km_ref[0, 0]) for rs in subs]
    y_br = (ybr_ref[...].astype(F32) * _silu(gate_ref[...].astype(F32))).astype(BF16)
    pcat = []
    for sr in s:
        ps = []
        for hd in range(MEM_HEADS):
            sh = sr[:, hd * ml:(hd + 1) * ml]
            p = jnp.exp2(sh - jnp.max(sh, axis=-1, keepdims=True))
            ps.append((p * (1.0 / jnp.sum(p, axis=-1, keepdims=True))).astype(BF16))
        pcat.append(jnp.concatenate(ps, axis=1))
    y_mem = [_dot(pc, vm_ref[0, 0]) for pc in pcat]
    y_mem = (jnp.concatenate(y_mem, axis=0) * _silu(gm_ref[...].astype(F32))).astype(BF16)
    y = _dot(y_br, wb_ref[:br_w, :]) + _dot(y_mem, wb_ref[br_w:, :])
    o_ref[...] = x_ref[...] + _rms(y, pn_ref[0])


def _out_proj(ybr, gate, qm, gm, x2d, k_mem, v_mem, layer, w, post_g, seq, tm):
    m, d = x2d.shape
    br_w = ybr.shape[1]
    ml = k_mem.shape[2]
    per_b = seq // tm
    mem_spec = pl.BlockSpec((1, 1, ml, MEM_W), lambda i: (layer, i // per_b, 0, 0))
    row = lambda w: pl.BlockSpec((tm, w), lambda i: (i, 0))
    return pl.pallas_call(
        functools.partial(_out_body, br_w=br_w),
        grid=(m // tm,),
        in_specs=[
            row(br_w), row(br_w), row(MEM_W), row(MEM_W), row(d),
            mem_spec, mem_spec,
            pl.BlockSpec((1, d, d), lambda i: (layer, 0, 0)),
            pl.BlockSpec((1, 1, d), lambda i: (layer, 0, 0)),
        ],
        out_specs=row(d),
        out_shape=jax.ShapeDtypeStruct((m, d), F32),
        scratch_shapes=[pltpu.VMEM((d, d), BF16)],
        compiler_params=pltpu.CompilerParams(
            dimension_semantics=("arbitrary",), vmem_limit_bytes=VMEM_LIMIT_BYTES),
        name="out_proj",
    )(ybr, gate, qm, gm, x2d, k_mem, v_mem, w, post_g)


def kernel(x, mem, pre_norm, post_norm, w_out, mem_norm, w_mem_kv, a_w_in, a_shift_mu, a_w0, a_w2,
           a_a0, a_a2, a_k_k, a_k_a, a_r_k, a_lnx_w, a_lnx_b, kv_norm, w_kv, b_w_in, b_lam_q1,
           b_lam_k1, b_lam_q2, b_lam_k2, b_subln):
    bsz, seq, d = x.shape
    depth = pre_norm.shape[0]
    n_a = a_w_in.shape[0]
    br_w = d - MEM_W
    a_shift = 3 * br_w + 2 * LORA_W
    m = bsz * seq
    tm = 1024
    x2d = x.reshape(m, d)
    slopes = jnp.asarray(
        np.repeat(np.array(_alibi_slopes(br_w // LANES), np.float32)[:, None, None], LANES, axis=2))

    k_mem, v_mem = _mem_kv(mem, mem_norm.reshape(depth, 1, d), w_mem_kv)
    for l in range(depth):
        if l < n_a:
            i = l
            cols = [(0, a_shift), (a_shift, a_shift + br_w), (a_shift + br_w, a_shift + br_w + MEM_W),
                    (a_shift + br_w + MEM_W, a_shift + br_w + 2 * MEM_W)]
            z, gate, q_mem, g_mem = _norm_proj(
                x2d, [((pre_norm.reshape(depth, 1, d), l), a_w_in[i], cols)], tm)
            row = lambda p: p[i].reshape(1, -1)
            y_br = _rwkv(z.reshape(bsz, seq, a_shift), row(a_shift_mu), row(a_w0), row(a_a0), row(a_k_k),
                         row(a_k_a), row(a_r_k), row(a_lnx_w), row(a_lnx_b), a_w2[i], a_a2[i]).reshape(m, br_w)
        else:
            i = l - n_a
            cols = [(0, br_w), (br_w, 2 * br_w), (2 * br_w, 2 * br_w + MEM_W),
                    (2 * br_w + MEM_W, 2 * br_w + 2 * MEM_W)]
            groups = [((pre_norm.reshape(depth, 1, d), l), b_w_in[i], cols)]
            if l == n_a:
                groups.append(((kv_norm.reshape(1, 1, d), 0), w_kv, [(0, br_w), (br_w, 2 * br_w)]))
                q, gate, q_mem, g_mem, k_sh, v_sh = _norm_proj(x2d, groups, tm)
                k_sh = k_sh.reshape(bsz, seq, br_w)
                v_sh = v_sh.reshape(bsz, seq, br_w)
            else:
                q, gate, q_mem, g_mem = _norm_proj(x2d, groups, tm)
            lam_init = 0.8 - 0.6 * math.exp(-0.3 * l)
            lam_vectors = [p[i].reshape(1, HEAD_DIM) for p in (b_lam_q1, b_lam_k1, b_lam_q2, b_lam_k2)]
            y_br = _diff_attn(q.reshape(bsz, seq, br_w), k_sh, v_sh, slopes, lam_vectors,
                              b_subln[i].reshape(LANES, 1), lam_init, ATT_Q_TILE).reshape(m, br_w)
        x2d = _out_proj(y_br, gate, q_mem, g_mem, x2d, k_mem, v_mem, l, w_out,
                        post_norm.reshape(depth, 1, d), seq, OUT_ROWS)
    return x2d.reshape(bsz, seq, d)
```

```python
import functools
import math

import numpy as np
import jax
import jax.numpy as jnp
from jax import lax
from jax.experimental import pallas as pl
from jax.experimental.pallas import tpu as pltpu

F32 = jnp.float32
BF16 = jnp.bfloat16
ACT_DTYPE = BF16

HEAD_DIM = 64
LANES = 128
SUBLANES = 8
MEM_HEADS = 4
MEM_W = MEM_HEADS * HEAD_DIM
LORA_W = 64
ATT_CHUNK = 64
ATT_Q_TILE = 512
ATT_BATCH = 2
ATT_BIAS_PIECES = 3
ATT_V_PAD = 16
RWKV_CHUNK = 64
RWKV_GROUP = 16
RWKV_CUMSUM_ROWS = 128
RWKV_BATCH = 4
RWKV_CHAIN_LEAD_STAGES = 2
OUT_ROWS = 1024
OUT_SUB_ROWS = 512
NORM_EPS = 1e-6
LNX_EPS = 64e-5
KK_NORM_FLOOR = 1e-12
MASK_VALUE = -1e30
VMEM_LIMIT_BYTES = 56 * 1024 * 1024


def _dot(a, b):
    return jnp.dot(a, b, preferred_element_type=F32)


def _dot_nt(a, b):
    return lax.dot_general(a, b, (((1,), (1,)), ((), ())), preferred_element_type=F32)


def _dot_tn(a, b):
    return lax.dot_general(a, b, (((0,), (0,)), ((), ())), preferred_element_type=F32)


def _split(x, pieces):
    out = []
    rem = x
    for i in range(pieces):
        p = rem.astype(BF16)
        out.append(p)
        if i + 1 < pieces:
            rem = rem - p.astype(F32)
    return out


def _mm(a, b, fn=_dot, pa=1, pb=1):
    aps = _split(a, pa)
    bps = _split(b, pb)
    order = max(pa, pb)
    acc = None
    for i, ap in enumerate(aps):
        for j, bp in enumerate(bps):
            if i + j < order:
                t = fn(ap, bp)
                acc = t if acc is None else acc + t
    return acc


def _rms(x, g):
    ms = jnp.mean(x * x, axis=-1, keepdims=True)
    return (x * lax.rsqrt(ms + NORM_EPS)) * g


def _silu(x):
    h = 0.5 * x
    return h + h * jnp.tanh(h)


def _norm_proj_body(x_ref, *refs, groups):
    ng = len(groups)
    nout = sum(len(cols) for cols in groups)
    o_refs = refs[2 * ng:2 * ng + nout]
    wb_refs = refs[2 * ng + nout:]

    @pl.when(pl.program_id(0) == 0)
    def _():
        for gi in range(ng):
            wb_refs[gi][...] = refs[2 * gi + 1][...].astype(BF16)

    x = x_ref[...]
    ms = jnp.mean(x * x, axis=-1, keepdims=True)
    xn = x * lax.rsqrt(ms + NORM_EPS)
    k = 0
    for gi, cols in enumerate(groups):
        h = (xn * refs[2 * gi][0]).astype(BF16)
        for lo, hi in cols:
            o_refs[k][...] = _dot(h, wb_refs[gi][:, lo:hi]).astype(o_refs[k].dtype)
            k += 1


def _norm_proj(x2d, groups, tm):
    m, d = x2d.shape
    cols = [c for _, _, cs in groups for c in cs]
    in_specs = [pl.BlockSpec((tm, d), lambda i: (i, 0))]
    operands = [x2d]
    for (g, row), w, _ in groups:
        in_specs += [pl.BlockSpec((1, 1, d), lambda i, row=row: (row, 0, 0)), pl.BlockSpec(w.shape, lambda i: (0, 0))]
        operands += [g, w]
    return pl.pallas_call(
        functools.partial(_norm_proj_body, groups=tuple(tuple(cs) for _, _, cs in groups)),
        grid=(m // tm,),
        in_specs=in_specs,
        out_specs=[pl.BlockSpec((tm, hi - lo), lambda i: (i, 0)) for lo, hi in cols],
        out_shape=[jax.ShapeDtypeStruct((m, hi - lo), ACT_DTYPE) for lo, hi in cols],
        scratch_shapes=[pltpu.VMEM(w.shape, BF16) for _, w, _ in groups],
        compiler_params=pltpu.CompilerParams(
            dimension_semantics=("arbitrary",), vmem_limit_bytes=VMEM_LIMIT_BYTES),
        name="norm_proj",
    )(*operands)


def _mem_kv_body(mem_ref, g_ref, w_ref, k_ref, v_ref):
    nb, ml, d = mem_ref.shape
    h = _rms(mem_ref[...].reshape(nb * ml, d), g_ref[0]).astype(BF16)
    kv = _dot(h, w_ref[0].astype(BF16))
    lane = lax.broadcasted_iota(jnp.int32, (ml, MEM_W), 1)
    for bj in range(nb):
        rows = slice(bj * ml, (bj + 1) * ml)
        for hd in range(MEM_HEADS):
            in_head = (lane >= hd * HEAD_DIM) & (lane < (hd + 1) * HEAD_DIM)
            k_ref[0, bj, hd * ml:(hd + 1) * ml, :] = jnp.where(in_head, kv[rows, :MEM_W], 0.0).astype(k_ref.dtype)
            v_ref[0, bj, hd * ml:(hd + 1) * ml, :] = jnp.where(in_head, kv[rows, MEM_W:], 0.0).astype(v_ref.dtype)


def _mem_kv(mem, gains, w):
    b, ml, d = mem.shape
    nl = gains.shape[0]
    out_spec = pl.BlockSpec((1, b, MEM_HEADS * ml, MEM_W), lambda l: (l, 0, 0, 0))
    return pl.pallas_call(
        _mem_kv_body,
        grid=(nl,),
        in_specs=[
            pl.BlockSpec((b, ml, d), lambda l: (0, 0, 0)),
            pl.BlockSpec((1, 1, d), lambda l: (l, 0, 0)),
            pl.BlockSpec((1, d, 2 * MEM_W), lambda l: (l, 0, 0)),
        ],
        out_specs=[out_spec] * 2,
        out_shape=[jax.ShapeDtypeStruct((nl, b, MEM_HEADS * ml, MEM_W), ACT_DTYPE)] * 2,
        compiler_params=pltpu.CompilerParams(
            dimension_semantics=("arbitrary",), vmem_limit_bytes=VMEM_LIMIT_BYTES),
        name="mem_kv",
    )(mem, gains, w)


def _rwkv_body(zr_ref, zk_ref, zv_ref, zwa_ref, mur_ref, muk_ref, muv_ref, muwa_ref, w0_ref, a0_ref,
               kk_ref, ka_ref, rk_ref, lnw_ref, lnb_ref, w2_ref, a2_ref, y_ref, s_ref, prev_ref, *, group):
    c = RWKV_CHUNK
    rows = group * c
    lane = lax.broadcasted_iota(jnp.int32, (c, LANES), 1)
    row = lax.broadcasted_iota(jnp.int32, (c, LANES), 0)
    head0 = lane < HEAD_DIM
    scol = jnp.where(head0, lane, lane - HEAD_DIM)
    strict = scol < row
    incl = scol <= row
    diag = scol == row
    r2 = lax.broadcasted_iota(jnp.int32, (LANES, LANES), 0)
    c2 = lax.broadcasted_iota(jnp.int32, (LANES, LANES), 1)
    blockmask = (r2 < HEAD_DIM) == (c2 < HEAD_DIM)
    blockones = jnp.where(blockmask, 1.0, 0.0).astype(BF16)
    cum_rows = min(rows, RWKV_CUMSUM_ROWS)
    tr = lax.broadcasted_iota(jnp.int32, (cum_rows, cum_rows), 0)
    tc = lax.broadcasted_iota(jnp.int32, (cum_rows, cum_rows), 1)
    tril_ones = jnp.where((tc <= tr) & (tc // c == tr // c), 1.0, 0.0).astype(BF16)
    slab_lane = lax.broadcasted_iota(jnp.int32, (rows, LANES), 1)
    slab_head0 = slab_lane < HEAD_DIM
    slab_first = lax.broadcasted_iota(jnp.int32, (rows, LANES), 0) == 0

    def bd(x):
        return jnp.concatenate([jnp.where(head0, x, 0.0), jnp.where(head0, 0.0, x)], axis=0)

    def seg_sum(x):
        return _dot(x.astype(BF16), blockones)

    mu_r, mu_k, mu_v, mu_wa = mur_ref[...], muk_ref[...], muv_ref[...], muwa_ref[...]
    w0, a0 = w0_ref[...], a0_ref[...]
    k_k, k_a, r_k = kk_ref[...], ka_ref[...], rk_ref[...]
    lnx_w, lnx_b = lnw_ref[...], lnb_ref[...]
    zeros = jnp.zeros((LORA_W, LANES), F32)
    w2a2 = jnp.concatenate([jnp.concatenate([w2_ref[...], zeros], axis=1),
                            jnp.concatenate([zeros, a2_ref[...]], axis=1)], axis=0)

    s_ref[...] = jnp.zeros_like(s_ref)
    prev_ref[...] = jnp.zeros_like(prev_ref)

    def shift_mix(ref, bj, slot, sl, mu):
        z = ref[bj, sl, :].astype(F32)
        zp = pltpu.roll(z, 1, axis=0)
        zp = jnp.where(slab_first, prev_ref[bj, slot:slot + 1, :], zp)
        prev_ref[bj, slot:slot + 1, :] = z[rows - 1:rows, :]
        return z + (zp - z) * mu


    def prepare(bj, sl):
        r = shift_mix(zr_ref, bj, 0, sl, mu_r)
        k = shift_mix(zk_ref, bj, 1, sl, mu_k)
        yield
        v = shift_mix(zv_ref, bj, 2, sl, mu_v)
        wa = shift_mix(zwa_ref, bj, 3, sl, mu_wa)
        lora = _mm(jnp.where(slab_head0, jnp.tanh(wa), wa), w2a2)
        yield
        wlog = w0 + lora[:, :LANES]
        nw = -wlog
        w = -(jnp.maximum(nw, 0.0) + jnp.log(1.0 + jnp.exp(-jnp.abs(nw)))) - 0.5
        logw = -jnp.exp(w)
        a = 1.0 / (1.0 + jnp.exp(-(a0 + lora[:, LANES:])))
        kk = k * k_k
        kk = kk * lax.rsqrt(jnp.maximum(seg_sum(kk * kk), KK_NORM_FLOOR))
        yield
        kmod = k * (1.0 + (a - 1.0) * k_a)
        alpha = -kk
        beta = kk * a
        l2 = jnp.concatenate(_split(logw, 2), axis=1)
        cum2 = jnp.concatenate(
            [_dot(tril_ones, l2[i:i + cum_rows]) for i in range(0, rows, cum_rows)], axis=0)
        cum = cum2[:, :LANES] + cum2[:, LANES:]
        yield
        e_neg = jnp.exp(-cum)
        at = alpha * jnp.exp(cum - logw)
        rt = r * jnp.exp(cum)
        bt = beta * e_neg
        kt = kmod * e_neg
        yield
        chunks = [slice(g * c, (g + 1) * c) for g in range(group)]
        cum_c = [cum[cs.stop - 1:cs.stop, :] for cs in chunks]
        e_end = [jnp.exp(cum_c[g] - cum[cs]) for g, cs in enumerate(chunks)]
        per = lambda x: [x[cs] for cs in chunks]
        return dict(
            at=per(at), rt=per(rt), bt=per(bt), kt=per(kt), v=per(v),
            bt_end=[beta[cs] * e_end[g] for g, cs in enumerate(chunks)],
            kt_end=[kmod[cs] * e_end[g] for g, cs in enumerate(chunks)],
            p_c=[jnp.exp(x) for x in cum_c], v_slab=v, rk=r * kmod * r_k)

    def chains(p):
        at, rt, bt, kt, bt_end, kt_end, v = (p[x] for x in ("at", "rt", "bt", "kt", "bt_end", "kt_end", "v"))
        n = range(group)
        lhs = [jnp.concatenate([at[g], rt[g]], axis=0) for g in n]
        xbk = [_mm(lhs[g], jnp.concatenate([bd(bt[g]), bd(kt[g])], axis=0), _dot_nt) for g in n]
        yield
        a_rb = [jnp.where(incl, xbk[g][c:, :LANES], 0.0) for g in n]
        akrk = [jnp.concatenate([jnp.where(strict, xbk[g][:c, LANES:], 0.0),
                                 jnp.where(incl, xbk[g][c:, LANES:], 0.0)], axis=0) for g in n]
        av = [_mm(akrk[g], bd(v[g])) for g in n]
        yield
        apow = [jnp.where(strict, xbk[g][:c, :LANES], 0.0) for g in n]
        tinv = [jnp.where(diag, 1.0, 0.0) + apow[g] for g in n]
        nfac = int(math.log2(c))
        for i in range(1, nfac):
            rhs = [[bd(apow[g])] + ([bd(tinv[g])] if i > 1 else []) for g in n]
            d = [_mm(apow[g], jnp.concatenate(rhs[g], axis=1)) for g in n]
            if i > 1:
                tinv = [tinv[g] + d[g][:, LANES:] for g in n]
            apow = [d[g][:, :LANES] for g in n]
            yield
        tinv = [tinv[g] + _mm(apow[g], bd(tinv[g])) for g in n]
        yield
        x = [_mm(tinv[g], jnp.concatenate([bd(at[g]), bd(av[g][:c])], axis=1)) for g in n]
        wmat = [x[g][:, :LANES] for g in n]
        u0 = [x[g][:, LANES:] for g in n]
        yield
        d2 = [_mm(a_rb[g], jnp.concatenate([bd(wmat[g]), bd(u0[g])], axis=1)) for g in n]
        rp = [rt[g] + d2[g][:, :LANES] for g in n]
        y0 = [d2[g][:, LANES:] + av[g][c:] for g in n]
        yield
        fold = lambda z: jnp.where(head0, z[:HEAD_DIM], z[HEAD_DIM:])
        gp = [fold(_mm(wmat[g], bt_end[g], _dot_tn)) for g in n]
        npart = [fold(_mm(jnp.concatenate([u0[g], v[g]], axis=0),
                          jnp.concatenate([bt_end[g], kt_end[g]], axis=0), _dot_tn)) for g in n]
        yield
        spans = [[(jnp.where(diag, gp[g] + p["p_c"][g], gp[g]), npart[g]) for g in n]]
        while len(spans[-1]) > 1:
            prev = spans[-1]
            nxt = []
            for i in range(0, len(prev), 2):
                (ma, na), (mb, nb) = prev[i], prev[i + 1]
                prod = _mm(jnp.concatenate([ma, na], axis=0), bd(mb))
                nxt.append((prod[:HEAD_DIM], prod[HEAD_DIM:] + nb))
            spans.append(nxt)
            yield
        return dict(rp=rp, y0=y0, spans=spans, v_slab=p["v_slab"], rk=p["rk"])

    def finish(t, bj, sl):
        spans = t["spans"]
        states = {0: s_ref[bj]}
        top = len(spans) - 1
        m_all, n_all = spans[top][0]
        s_ref[bj] = _mm(states[0], bd(m_all)) + n_all
        yield
        for level in range(top, 0, -1):
            width = 1 << level
            for lo in range(0, group, width):
                ma, na = spans[level - 1][lo >> (level - 1)]
                states[lo + width // 2] = _mm(states[lo], bd(ma)) + na
            yield
        y = jnp.concatenate(
            [_mm(t["rp"][g], bd(states[g]), _dot_nt) + t["y0"][g] for g in range(group)], axis=0)
        yield
        mean = seg_sum(y) * (1.0 / HEAD_DIM)
        yc = y - mean
        var = seg_sum(yc * yc) * (1.0 / HEAD_DIM)
        yield
        yn = yc * lax.rsqrt(var + LNX_EPS) * lnx_w + lnx_b
        bonus = seg_sum(t["rk"]) * t["v_slab"]
        y_ref[bj, sl, :] = (yn + bonus).astype(y_ref.dtype)

    def run(*gens, lead_stages=1):
        results = [None] * len(gens)
        live = list(range(len(gens)))
        while live:
            for i in list(live):
                for _ in range(lead_stages if i == 0 else 1):
                    if i not in live:
                        break
                    try:
                        next(gens[i])
                    except StopIteration as stop:
                        results[i] = stop.value
                        live.remove(i)
        return results

    nslabs = zr_ref.shape[1] // rows
    units = [(bj, slice(t * rows, (t + 1) * rows)) for t in range(nslabs) for bj in range(zr_ref.shape[0])]
    prepared, = run(prepare(*units[0]))
    done = None
    for u in range(len(units)):
        gens = [chains(prepared)]
        if u + 1 < len(units):
            gens.append(prepare(*units[u + 1]))
        if done is not None:
            gens.append(finish(done, *units[u - 1]))
        out = run(*gens, lead_stages=RWKV_CHAIN_LEAD_STAGES)
        done = out[0]
        if u + 1 < len(units):
            prepared = out[1]
    run(finish(done, *units[-1]))


def _rwkv(z3d, mu, w0, a0, k_k, k_a, r_k, lnx_w, lnx_b, w2, a2):
    b, s, a_shift = z3d.shape
    br_w = (a_shift - 2 * LORA_W) // 3
    npairs = br_w // LANES
    vec = lambda off: pl.BlockSpec((1, LANES), lambda bi, hp: (0, off + hp))
    lora = pl.BlockSpec((LORA_W, LANES), lambda bi, hp: (0, hp))
    nb = RWKV_BATCH
    assert b % nb == 0 and s % (RWKV_CHUNK * RWKV_GROUP) == 0
    zspec = lambda off: pl.BlockSpec((nb, s, LANES), lambda bi, hp: (bi, 0, off + hp))
    return pl.pallas_call(
        functools.partial(_rwkv_body, group=RWKV_GROUP),
        grid=(b // nb, npairs),
        in_specs=[
            zspec(0), zspec(npairs), zspec(2 * npairs),
            pl.BlockSpec((nb, s, LANES), lambda bi, hp: (bi, 0, 3 * npairs)),
            vec(0), vec(npairs), vec(2 * npairs),
            pl.BlockSpec((1, LANES), lambda bi, hp: (0, 3 * npairs)),
            vec(0), vec(0), vec(0), vec(0), vec(0), vec(0), vec(0), lora, lora,
        ],
        out_specs=pl.BlockSpec((nb, s, LANES), lambda bi, hp: (bi, 0, hp)),
        out_shape=jax.ShapeDtypeStruct((b, s, br_w), ACT_DTYPE),
        scratch_shapes=[pltpu.VMEM((nb, HEAD_DIM, LANES), F32), pltpu.VMEM((nb, SUBLANES, LANES), F32)],
        compiler_params=pltpu.CompilerParams(
            dimension_semantics=("arbitrary", "arbitrary"), vmem_limit_bytes=VMEM_LIMIT_BYTES),
        name="rwkv7_scan",
    )(z3d, z3d, z3d, z3d, mu, mu, mu, mu, w0, a0, k_k, k_a, r_k, lnx_w, lnx_b, w2, a2)


def _alibi_slopes(n):
    def pow2(m):
        start = 2.0 ** (-8.0 / m)
        return [start ** (i + 1) for i in range(m)]
    if math.log2(n).is_integer():
        return pow2(n)
    cl = 2 ** int(math.floor(math.log2(n)))
    return pow2(cl) + pow2(2 * cl)[0::2][: n - cl]


def _diff_attn_body(q_ref, k_ref, v_ref, slope_ref, lq1_ref, lk1_ref, lq2_ref, lk2_ref, sub_ref, o_ref,
                    kb_ref, vt_ref, band_ref, sa_ref, sb_ref, acc_ref, *, tq, lam_init):
    bi = pl.program_id(1)
    nb = q_ref.shape[0]
    tk = tq
    half = tq // 2
    nq = k_ref.shape[1] // tq
    log2e = math.log2(math.e)
    slope2 = slope_ref[0][:, :1] * log2e
    aug_lane = lax.broadcasted_iota(jnp.int32, (tk, LANES), 1)

    for bj in range(nb):
        for t in range(nq):
            rows = slice(t * tk, (t + 1) * tk)
            kb_ref[bj, rows, :LANES] = k_ref[bj, rows, :].astype(BF16)
            vt_ref[bj, :LANES, rows] = v_ref[bj, rows, :].astype(F32).T.astype(BF16)

    @pl.when(bi == 0)
    def _():
        ones_row = lax.broadcasted_iota(jnp.int32, (ATT_V_PAD, tk), 0) == 0
        for t in range(nq):
            rows = slice(t * tk, (t + 1) * tk)
            pos = (lax.broadcasted_iota(jnp.int32, (tk, LANES), 0) + t * tk).astype(F32)
            pieces = _split(slope2 * pos, ATT_BIAS_PIECES)
            aug = jnp.zeros((tk, LANES), F32)
            for i, piece in enumerate(pieces):
                aug = jnp.where(aug_lane == i, piece.astype(F32), aug)
            for bj in range(nb):
                kb_ref[bj, rows, LANES:] = aug.astype(BF16)
                vt_ref[bj, LANES:, rows] = jnp.where(ones_row, 1.0, 0.0).astype(BF16)
        kpos = lax.broadcasted_iota(jnp.int32, (tk, 2 * tq), 0).astype(F32)
        lane2 = lax.broadcasted_iota(jnp.int32, (1, 2 * tq), 1)
        qry = jnp.where(lane2 < tq, lane2, lane2 - tq)
        qlim = ((qry // ATT_CHUNK + 1) * ATT_CHUNK).astype(F32)
        band_ref[...] = jnp.where(
            kpos < qlim, (-2.0 * slope2) * jnp.maximum(kpos - qry.astype(F32), 0.0), MASK_VALUE)

    lane = lax.broadcasted_iota(jnp.int32, (2 * tq, LANES), 1)
    head0 = lane < HEAD_DIM
    first = lax.broadcasted_iota(jnp.int32, (2 * tq, LANES), 0) < tq
    ones_cols = jnp.where(lane < ATT_BIAS_PIECES, 1.0, 0.0)
    lam = (jnp.exp(jnp.sum(lq1_ref[...] * lk1_ref[...], axis=-1, keepdims=True))
           - jnp.exp(jnp.sum(lq2_ref[...] * lk2_ref[...], axis=-1, keepdims=True)) + lam_init)
    late = lambda x: jnp.concatenate([x[..., half:tq], x[..., tq + half:]], axis=-1)

    def update(p_rows, vt, m, m_new, cols):
        pv = _dot(vt, p_rows.astype(BF16))
        off = 0
        for cs in cols:
            w = cs.stop - cs.start
            if m is None:
                acc_ref[:, cs] = pv[:, off:off + w]
            else:
                acc_ref[:, cs] = jnp.exp2(m - m_new)[:, off:off + w] * acc_ref[:, cs] + pv[:, off:off + w]
            off += w

    def tiles_of(n):
        past = [(slice(j * tk, (j + 1) * tk), None, None) for j in range(n)]
        own0 = (slice(n * tq, n * tq + half), None, slice(0, half))
        own1 = (slice(n * tq + half, (n + 1) * tq), "late", slice(half, tq))
        return past + [own0, own1]

    all_cols = [slice(0, 2 * tq)]
    late_cols = [slice(half, tq), slice(tq + half, 2 * tq)]
    bufs = (sa_ref, sb_ref)
    for bj in range(nb):
        for n in range(nq):
            qrows = slice(n * tq, (n + 1) * tq)
            q = q_ref[bj, qrows, :].astype(F32) * (HEAD_DIM ** -0.5 * log2e)
            q2x = jnp.concatenate([q, q], axis=0)
            qcat = jnp.concatenate([jnp.where(first == head0, q2x, 0.0), ones_cols],
                                   axis=1).astype(BF16)
            qlate = jnp.concatenate([qcat[half:tq], qcat[tq + half:]], axis=0)
            tiles = tiles_of(n)

            def scores(i, dst_ref):
                keys, which, _ = tiles[i]
                nk = keys.stop - keys.start
                if which is None:
                    dst_ref[:nk, :] = _dot_nt(kb_ref[bj, keys, :], qcat)
                else:
                    dst_ref[:nk, :tq] = _dot_nt(kb_ref[bj, keys, :], qlate)

            m = None
            scores(0, bufs[0])
            for i, (keys, which, brows) in enumerate(tiles):
                if i + 1 < len(tiles):
                    scores(i + 1, bufs[(i + 1) % 2])
                nk = keys.stop - keys.start
                vt = vt_ref[bj, :, keys]
                if which is None:
                    s = bufs[i % 2][:nk, :]
                    if brows is not None:
                        s = s + band_ref[brows, :]
                    m_new = jnp.max(s, axis=0, keepdims=True)
                    if m is not None:
                        m_new = jnp.maximum(m, m_new)
                    update(jnp.exp2(s - m_new), vt, m, m_new, all_cols)
                    m = m_new
                else:
                    s = bufs[i % 2][:nk, :tq] + late(band_ref[brows, :])
                    ml = late(m)
                    m_new = jnp.maximum(ml, jnp.max(s, axis=0, keepdims=True))
                    update(jnp.exp2(s - m_new), vt, ml, m_new, late_cols)
            on = acc_ref[:LANES, :] * (1.0 / acc_ref[LANES:LANES + 1, :])
            o = on[:, :tq] - lam * on[:, tq:]
            ms = jnp.mean(o * o, axis=0, keepdims=True)
            o = o * lax.rsqrt(ms + NORM_EPS) * sub_ref[...] * (1.0 - lam_init)
            o_ref[bj, qrows, :] = o.T.astype(o_ref.dtype)


def _diff_attn(q3d, k3d, v3d, slopes, lam_vectors, subln_col, lam_init, tq):
    b, s, br_w = q3d.shape
    nh = br_w // LANES
    nb = ATT_BATCH
    assert b % nb == 0 and s % tq == 0
    lam_spec = pl.BlockSpec((1, HEAD_DIM), lambda h, bi: (0, 0))
    seq_spec = pl.BlockSpec((nb, s, LANES), lambda h, bi: (bi, 0, h))
    return pl.pallas_call(
        functools.partial(_diff_attn_body, tq=tq, lam_init=lam_init),
        grid=(nh, b // nb),
        in_specs=[
            seq_spec, seq_spec, seq_spec,
            pl.BlockSpec((1, 1, LANES), lambda h, bi: (h, 0, 0)),
            lam_spec, lam_spec, lam_spec, lam_spec,
            pl.BlockSpec((LANES, 1), lambda h, bi: (0, 0)),
        ],
        out_specs=seq_spec,
        out_shape=jax.ShapeDtypeStruct((b, s, br_w), ACT_DTYPE),
        scratch_shapes=[pltpu.VMEM((nb, s, 2 * LANES), BF16), pltpu.VMEM((nb, LANES + ATT_V_PAD, s), BF16),
                        pltpu.VMEM((tq, 2 * tq), F32),
                        pltpu.VMEM((tq, 2 * tq), F32), pltpu.VMEM((tq, 2 * tq), F32),
                        pltpu.VMEM((LANES + ATT_V_PAD, 2 * tq), F32)],
        compiler_params=pltpu.CompilerParams(
            dimension_semantics=("arbitrary", "arbitrary"),
            vmem_limit_bytes=VMEM_LIMIT_BYTES),
        name="diff_attn",
    )(q3d, k3d, v3d, slopes, *lam_vectors, subln_col)


def _out_body(ybr_ref, gate_ref, qm_ref, gm_ref, x_ref, km_ref, vm_ref, w_ref, pn_ref, o_ref, wb_ref, *, br_w):
    @pl.when(pl.program_id(0) == 0)
    def _():
        wb_ref[...] = w_ref[0].astype(BF16)

    tm = x_ref.shape[0]
    ml = km_ref.shape[2] // MEM_HEADS
    subs = [slice(r * OUT_SUB_ROWS, (r + 1) * OUT_SUB_ROWS) for r in range(tm // OUT_SUB_ROWS)]
    qscale = HEAD_DIM ** -0.5 * math.log2(math.e)
    s = [_dot_nt((qm_ref[rs, :].astype(F32) * qscale).astype(BF16), km_ref[0, 0]) for rs in subs]
    y_br = (ybr_ref[...].astype(F32) * _silu(gate_ref[...].astype(F32))).astype(BF16)
    pcat = []
    for sr in s:
        ps = []
        for hd in range(MEM_HEADS):
            sh = sr[:, hd * ml:(hd + 1) * ml]
            p = jnp.exp2(sh - jnp.max(sh, axis=-1, keepdims=True))
            ps.append((p * (1.0 / jnp.sum(p, axis=-1, keepdims=True))).astype(BF16))
        pcat.append(jnp.concatenate(ps, axis=1))
    y_mem = [_dot(pc, vm_ref[0, 0]) for pc in pcat]
    y_mem = (jnp.concatenate(y_mem, axis=0) * _silu(gm_ref[...].astype(F32))).astype(BF16)
    y = _dot(y_br, wb_ref[:br_w, :]) + _dot(y_mem, wb_ref[br_w:, :])
    o_ref[...] = x_ref[...] + _rms(y, pn_ref[0])


def _out_proj(ybr, gate, qm, gm, x2d, k_mem, v_mem, layer, w, post_g, seq, tm):
    m, d = x2d.shape
    br_w = ybr.shape[1]
    ml = k_mem.shape[2]
    per_b = seq // tm
    mem_spec = pl.BlockSpec((1, 1, ml, MEM_W), lambda i: (layer, i // per_b, 0, 0))
    row = lambda w: pl.BlockSpec((tm, w), lambda i: (i, 0))
    return pl.pallas_call(
        functools.partial(_out_body, br_w=br_w),
        grid=(m // tm,),
        in_specs=[
            row(br_w), row(br_w), row(MEM_W), row(MEM_W), row(d),
            mem_spec, mem_spec,
            pl.BlockSpec((1, d, d), lambda i: (layer, 0, 0)),
            pl.BlockSpec((1, 1, d), lambda i: (layer, 0, 0)),
        ],
        out_specs=row(d),
        out_shape=jax.ShapeDtypeStruct((m, d), F32),
        scratch_shapes=[pltpu.VMEM((d, d), BF16)],
        compiler_params=pltpu.CompilerParams(
            dimension_semantics=("arbitrary",), vmem_limit_bytes=VMEM_LIMIT_BYTES),
        name="out_proj",
    )(ybr, gate, qm, gm, x2d, k_mem, v_mem, w, post_g)


def kernel(x, mem, pre_norm, post_norm, w_out, mem_norm, w_mem_kv, a_w_in, a_shift_mu, a_w0, a_w2,
           a_a0, a_a2, a_k_k, a_k_a, a_r_k, a_lnx_w, a_lnx_b, kv_norm, w_kv, b_w_in, b_lam_q1,
           b_lam_k1, b_lam_q2, b_lam_k2, b_subln):
    bsz, seq, d = x.shape
    depth = pre_norm.shape[0]
    n_a = a_w_in.shape[0]
    br_w = d - MEM_W
    a_shift = 3 * br_w + 2 * LORA_W
    m = bsz * seq
    tm = 1024
    x2d = x.reshape(m, d)
    slopes = jnp.asarray(
        np.repeat(np.array(_alibi_slopes(br_w // LANES), np.float32)[:, None, None], LANES, axis=2))

    k_mem, v_mem = _mem_kv(mem, mem_norm.reshape(depth, 1, d), w_mem_kv)
    for l in range(depth):
        if l < n_a:
            i = l
            cols = [(0, a_shift), (a_shift, a_shift + br_w), (a_shift + br_w, a_shift + br_w + MEM_W),
                    (a_shift + br_w + MEM_W, a_shift + br_w + 2 * MEM_W)]
            z, gate, q_mem, g_mem = _norm_proj(
                x2d, [((pre_norm.reshape(depth, 1, d), l), a_w_in[i], cols)], tm)
            row = lambda p: p[i].reshape(1, -1)
            y_br = _rwkv(z.reshape(bsz, seq, a_shift), row(a_shift_mu), row(a_w0), row(a_a0), row(a_k_k),
                         row(a_k_a), row(a_r_k), row(a_lnx_w), row(a_lnx_b), a_w2[i], a_a2[i]).reshape(m, br_w)
        else:
            i = l - n_a
            cols = [(0, br_w), (br_w, 2 * br_w), (2 * br_w, 2 * br_w + MEM_W),
                    (2 * br_w + MEM_W, 2 * br_w + 2 * MEM_W)]
            groups = [((pre_norm.reshape(depth, 1, d), l), b_w_in[i], cols)]
            if l == n_a:
                groups.append(((kv_norm.reshape(1, 1, d), 0), w_kv, [(0, br_w), (br_w, 2 * br_w)]))
                q, gate, q_mem, g_mem, k_sh, v_sh = _norm_proj(x2d, groups, tm)
                k_sh = k_sh.reshape(bsz, seq, br_w)
                v_sh = v_sh.reshape(bsz, seq, br_w)
            else:
                q, gate, q_mem, g_mem = _norm_proj(x2d, groups, tm)
            lam_init = 0.8 - 0.6 * math.exp(-0.3 * l)
            lam_vectors = [p[i].reshape(1, HEAD_DIM) for p in (b_lam_q1, b_lam_k1, b_lam_q2, b_lam_k2)]
            y_br = _diff_attn(q.reshape(bsz, seq, br_w), k_sh, v_sh, slopes, lam_vectors,
                              b_subln[i].reshape(LANES, 1), lam_init, ATT_Q_TILE).reshape(m, br_w)
        x2d = _out_proj(y_br, gate, q_mem, g_mem, x2d, k_mem, v_mem, l, w_out,
                        post_norm.reshape(depth, 1, d), seq, OUT_ROWS)
    return x2d.reshape(bsz, seq, d)
```

```python
import functools
import math

import numpy as np
import jax
import jax.numpy as jnp
from jax import lax
from jax.experimental import pallas as pl
from jax.experimental.pallas import tpu as pltpu

F32 = jnp.float32
BF16 = jnp.bfloat16
ACT_DTYPE = BF16

HEAD_DIM = 64
LANES = 128
SUBLANES = 8
MEM_HEADS = 4
MEM_W = MEM_HEADS * HEAD_DIM
LORA_W = 64
ATT_CHUNK = 64
ATT_Q_TILE = 512
ATT_BATCH = 2
ATT_BIAS_PIECES = 3
ATT_V_PAD = 16
RWKV_CHUNK = 64
RWKV_GROUP = 16
RWKV_CUMSUM_ROWS = 128
RWKV_BATCH = 2
RWKV_CHAIN_LEAD_STAGES = 2
OUT_ROWS = 1024
OUT_SUB_ROWS = 512
NORM_EPS = 1e-6
LNX_EPS = 64e-5
KK_NORM_FLOOR = 1e-12
MASK_VALUE = -1e30
VMEM_LIMIT_BYTES = 56 * 1024 * 1024


def _dot(a, b):
    return jnp.dot(a, b, preferred_element_type=F32)


def _dot_nt(a, b):
    return lax.dot_general(a, b, (((1,), (1,)), ((), ())), preferred_element_type=F32)


def _dot_tn(a, b):
    return lax.dot_general(a, b, (((0,), (0,)), ((), ())), preferred_element_type=F32)


def _split(x, pieces):
    out = []
    rem = x
    for i in range(pieces):
        p = rem.astype(BF16)
        out.append(p)
        if i + 1 < pieces:
            rem = rem - p.astype(F32)
    return out


def _mm(a, b, fn=_dot, pa=1, pb=1):
    aps = _split(a, pa)
    bps = _split(b, pb)
    order = max(pa, pb)
    acc = None
    for i, ap in enumerate(aps):
        for j, bp in enumerate(bps):
            if i + j < order:
                t = fn(ap, bp)
                acc = t if acc is None else acc + t
    return acc


def _rms(x, g):
    ms = jnp.mean(x * x, axis=-1, keepdims=True)
    return (x * lax.rsqrt(ms + NORM_EPS)) * g


def _silu(x):
    h = 0.5 * x
    return h + h * jnp.tanh(h)


def _norm_proj_body(x_ref, *refs, groups):
    ng = len(groups)
    nout = sum(len(cols) for cols in groups)
    o_refs = refs[2 * ng:2 * ng + nout]
    wb_refs = refs[2 * ng + nout:]

    @pl.when(pl.program_id(0) == 0)
    def _():
        for gi in range(ng):
            wb_refs[gi][...] = refs[2 * gi + 1][...].astype(BF16)

    x = x_ref[...]
    ms = jnp.mean(x * x, axis=-1, keepdims=True)
    xn = x * lax.rsqrt(ms + NORM_EPS)
    k = 0
    for gi, cols in enumerate(groups):
        h = (xn * refs[2 * gi][0]).astype(BF16)
        for lo, hi in cols:
            o_refs[k][...] = _dot(h, wb_refs[gi][:, lo:hi]).astype(o_refs[k].dtype)
            k += 1


def _norm_proj(x2d, groups, tm):
    m, d = x2d.shape
    cols = [c for _, _, cs in groups for c in cs]
    in_specs = [pl.BlockSpec((tm, d), lambda i: (i, 0))]
    operands = [x2d]
    for (g, row), w, _ in groups:
        in_specs += [pl.BlockSpec((1, 1, d), lambda i, row=row: (row, 0, 0)), pl.BlockSpec(w.shape, lambda i: (0, 0))]
        operands += [g, w]
    return pl.pallas_call(
        functools.partial(_norm_proj_body, groups=tuple(tuple(cs) for _, _, cs in groups)),
        grid=(m // tm,),
        in_specs=in_specs,
        out_specs=[pl.BlockSpec((tm, hi - lo), lambda i: (i, 0)) for lo, hi in cols],
        out_shape=[jax.ShapeDtypeStruct((m, hi - lo), ACT_DTYPE) for lo, hi in cols],
        scratch_shapes=[pltpu.VMEM(w.shape, BF16) for _, w, _ in groups],
        compiler_params=pltpu.CompilerParams(
            dimension_semantics=("arbitrary",), vmem_limit_bytes=VMEM_LIMIT_BYTES),
        name="norm_proj",
    )(*operands)


def _mem_kv_body(mem_ref, g_ref, w_ref, k_ref, v_ref):
    nb, ml, d = mem_ref.shape
    h = _rms(mem_ref[...].reshape(nb * ml, d), g_ref[0]).astype(BF16)
    kv = _dot(h, w_ref[0].astype(BF16))
    lane = lax.broadcasted_iota(jnp.int32, (ml, MEM_W), 1)
    for bj in range(nb):
        rows = slice(bj * ml, (bj + 1) * ml)
        for hd in range(MEM_HEADS):
            in_head = (lane >= hd * HEAD_DIM) & (lane < (hd + 1) * HEAD_DIM)
            k_ref[0, bj, hd * ml:(hd + 1) * ml, :] = jnp.where(in_head, kv[rows, :MEM_W], 0.0).astype(k_ref.dtype)
            v_ref[0, bj, hd * ml:(hd + 1) * ml, :] = jnp.where(in_head, kv[rows, MEM_W:], 0.0).astype(v_ref.dtype)


def _mem_kv(mem, gains, w):
    b, ml, d = mem.shape
    nl = gains.shape[0]
    out_spec = pl.BlockSpec((1, b, MEM_HEADS * ml, MEM_W), lambda l: (l, 0, 0, 0))
    return pl.pallas_call(
        _mem_kv_body,
        grid=(nl,),
        in_specs=[
            pl.BlockSpec((b, ml, d), lambda l: (0, 0, 0)),
            pl.BlockSpec((1, 1, d), lambda l: (l, 0, 0)),
            pl.BlockSpec((1, d, 2 * MEM_W), lambda l: (l, 0, 0)),
        ],
        out_specs=[out_spec] * 2,
        out_shape=[jax.ShapeDtypeStruct((nl, b, MEM_HEADS * ml, MEM_W), ACT_DTYPE)] * 2,
        compiler_params=pltpu.CompilerParams(
            dimension_semantics=("arbitrary",), vmem_limit_bytes=VMEM_LIMIT_BYTES),
        name="mem_kv",
    )(mem, gains, w)


def _rwkv_body(zr_ref, zk_ref, zv_ref, zwa_ref, mur_ref, muk_ref, muv_ref, muwa_ref, w0_ref, a0_ref,
               kk_ref, ka_ref, rk_ref, lnw_ref, lnb_ref, w2_ref, a2_ref, y_ref, s_ref, prev_ref, *, group):
    c = RWKV_CHUNK
    rows = group * c
    lane = lax.broadcasted_iota(jnp.int32, (c, LANES), 1)
    row = lax.broadcasted_iota(jnp.int32, (c, LANES), 0)
    head0 = lane < HEAD_DIM
    scol = jnp.where(head0, lane, lane - HEAD_DIM)
    strict = scol < row
    incl = scol <= row
    diag = scol == row
    r2 = lax.broadcasted_iota(jnp.int32, (LANES, LANES), 0)
    c2 = lax.broadcasted_iota(jnp.int32, (LANES, LANES), 1)
    blockmask = (r2 < HEAD_DIM) == (c2 < HEAD_DIM)
    blockones = jnp.where(blockmask, 1.0, 0.0).astype(BF16)
    cum_rows = min(rows, RWKV_CUMSUM_ROWS)
    tr = lax.broadcasted_iota(jnp.int32, (cum_rows, cum_rows), 0)
    tc = lax.broadcasted_iota(jnp.int32, (cum_rows, cum_rows), 1)
    tril_ones = jnp.where((tc <= tr) & (tc // c == tr // c), 1.0, 0.0).astype(BF16)
    slab_lane = lax.broadcasted_iota(jnp.int32, (rows, LANES), 1)
    slab_head0 = slab_lane < HEAD_DIM
    slab_first = lax.broadcasted_iota(jnp.int32, (rows, LANES), 0) == 0

    def bd(x):
        return jnp.concatenate([jnp.where(head0, x, 0.0), jnp.where(head0, 0.0, x)], axis=0)

    def seg_sum(x):
        return _dot(x.astype(BF16), blockones)

    mu_r, mu_k, mu_v, mu_wa = mur_ref[...], muk_ref[...], muv_ref[...], muwa_ref[...]
    w0, a0 = w0_ref[...], a0_ref[...]
    k_k, k_a, r_k = kk_ref[...], ka_ref[...], rk_ref[...]
    lnx_w, lnx_b = lnw_ref[...], lnb_ref[...]
    zeros = jnp.zeros((LORA_W, LANES), F32)
    w2a2 = jnp.concatenate([jnp.concatenate([w2_ref[...], zeros], axis=1),
                            jnp.concatenate([zeros, a2_ref[...]], axis=1)], axis=0)

    s_ref[...] = jnp.zeros_like(s_ref)
    prev_ref[...] = jnp.zeros_like(prev_ref)

    def shift_mix(ref, bj, slot, sl, mu):
        z = ref[bj, sl, :].astype(F32)
        zp = pltpu.roll(z, 1, axis=0)
        zp = jnp.where(slab_first, prev_ref[bj, slot:slot + 1, :], zp)
        prev_ref[bj, slot:slot + 1, :] = z[rows - 1:rows, :]
        return z + (zp - z) * mu


    def prepare(bj, sl):
        r = shift_mix(zr_ref, bj, 0, sl, mu_r)
        k = shift_mix(zk_ref, bj, 1, sl, mu_k)
        yield
        v = shift_mix(zv_ref, bj, 2, sl, mu_v)
        wa = shift_mix(zwa_ref, bj, 3, sl, mu_wa)
        lora = _mm(jnp.where(slab_head0, jnp.tanh(wa), wa), w2a2)
        yield
        wlog = w0 + lora[:, :LANES]
        nw = -wlog
        w = -(jnp.maximum(nw, 0.0) + jnp.log(1.0 + jnp.exp(-jnp.abs(nw)))) - 0.5
        logw = -jnp.exp(w)
        a = 1.0 / (1.0 + jnp.exp(-(a0 + lora[:, LANES:])))
        kk = k * k_k
        kk = kk * lax.rsqrt(jnp.maximum(seg_sum(kk * kk), KK_NORM_FLOOR))
        yield
        kmod = k * (1.0 + (a - 1.0) * k_a)
        alpha = -kk
        beta = kk * a
        l2 = jnp.concatenate(_split(logw, 2), axis=1)
        cum2 = jnp.concatenate(
            [_dot(tril_ones, l2[i:i + cum_rows]) for i in range(0, rows, cum_rows)], axis=0)
        cum = cum2[:, :LANES] + cum2[:, LANES:]
        yield
        e_neg = jnp.exp(-cum)
        at = alpha * jnp.exp(cum - logw)
        rt = r * jnp.exp(cum)
        bt = beta * e_neg
        kt = kmod * e_neg
        yield
        chunks = [slice(g * c, (g + 1) * c) for g in range(group)]
        cum_c = [cum[cs.stop - 1:cs.stop, :] for cs in chunks]
        e_end = [jnp.exp(cum_c[g] - cum[cs]) for g, cs in enumerate(chunks)]
        per = lambda x: [x[cs] for cs in chunks]
        return dict(
            at=per(at), rt=per(rt), bt=per(bt), kt=per(kt), v=per(v),
            bt_end=[beta[cs] * e_end[g] for g, cs in enumerate(chunks)],
            kt_end=[kmod[cs] * e_end[g] for g, cs in enumerate(chunks)],
            p_c=[jnp.exp(x) for x in cum_c], v_slab=v, rk=r * kmod * r_k)

    def chains(p):
        at, rt, bt, kt, bt_end, kt_end, v = (p[x] for x in ("at", "rt", "bt", "kt", "bt_end", "kt_end", "v"))
        n = range(group)
        lhs = [jnp.concatenate([at[g], rt[g]], axis=0) for g in n]
        xbk = [_mm(lhs[g], jnp.concatenate([bd(bt[g]), bd(kt[g])], axis=0), _dot_nt) for g in n]
        yield
        a_rb = [jnp.where(incl, xbk[g][c:, :LANES], 0.0) for g in n]
        akrk = [jnp.concatenate([jnp.where(strict, xbk[g][:c, LANES:], 0.0),
                                 jnp.where(incl, xbk[g][c:, LANES:], 0.0)], axis=0) for g in n]
        av = [_mm(akrk[g], bd(v[g])) for g in n]
        yield
        apow = [jnp.where(strict, xbk[g][:c, :LANES], 0.0) for g in n]
        tinv = [jnp.where(diag, 1.0, 0.0) + apow[g] for g in n]
        nfac = int(math.log2(c))
        for i in range(1, nfac):
            rhs = [[bd(apow[g])] + ([bd(tinv[g])] if i > 1 else []) for g in n]
            d = [_mm(apow[g], jnp.concatenate(rhs[g], axis=1)) for g in n]
            if i > 1:
                tinv = [tinv[g] + d[g][:, LANES:] for g in n]
            apow = [d[g][:, :LANES] for g in n]
            yield
        tinv = [tinv[g] + _mm(apow[g], bd(tinv[g])) for g in n]
        yield
        x = [_mm(tinv[g], jnp.concatenate([bd(at[g]), bd(av[g][:c])], axis=1)) for g in n]
        wmat = [x[g][:, :LANES] for g in n]
        u0 = [x[g][:, LANES:] for g in n]
        yield
        d2 = [_mm(a_rb[g], jnp.concatenate([bd(wmat[g]), bd(u0[g])], axis=1)) for g in n]
        rp = [rt[g] + d2[g][:, :LANES] for g in n]
        y0 = [d2[g][:, LANES:] + av[g][c:] for g in n]
        yield
        fold = lambda z: jnp.where(head0, z[:HEAD_DIM], z[HEAD_DIM:])
        gp = [fold(_mm(wmat[g], bt_end[g], _dot_tn)) for g in n]
        npart = [fold(_mm(jnp.concatenate([u0[g], v[g]], axis=0),
                          jnp.concatenate([bt_end[g], kt_end[g]], axis=0), _dot_tn)) for g in n]
        yield
        spans = [[(jnp.where(diag, gp[g] + p["p_c"][g], gp[g]), npart[g]) for g in n]]
        while len(spans[-1]) > 1:
            prev = spans[-1]
            nxt = []
            for i in range(0, len(prev), 2):
                (ma, na), (mb, nb) = prev[i], prev[i + 1]
                prod = _mm(jnp.concatenate([ma, na], axis=0), bd(mb))
                nxt.append((prod[:HEAD_DIM], prod[HEAD_DIM:] + nb))
            spans.append(nxt)
            yield
        return dict(rp=rp, y0=y0, spans=spans, v_slab=p["v_slab"], rk=p["rk"])

    def finish(t, bj, sl):
        spans = t["spans"]
        states = {0: s_ref[bj]}
        top = len(spans) - 1
        m_all, n_all = spans[top][0]
        s_ref[bj] = _mm(states[0], bd(m_all)) + n_all
        yield
        for level in range(top, 0, -1):
            width = 1 << level
            for lo in range(0, group, width):
                ma, na = spans[level - 1][lo >> (level - 1)]
                states[lo + width // 2] = _mm(states[lo], bd(ma)) + na
            yield
        y = jnp.concatenate(
            [_mm(t["rp"][g], bd(states[g]), _dot_nt) + t["y0"][g] for g in range(group)], axis=0)
        yield
        mean = seg_sum(y) * (1.0 / HEAD_DIM)
        yc = y - mean
        var = seg_sum(yc * yc) * (1.0 / HEAD_DIM)
        yield
        yn = yc * lax.rsqrt(var + LNX_EPS) * lnx_w + lnx_b
        bonus = seg_sum(t["rk"]) * t["v_slab"]
        y_ref[bj, sl, :] = (yn + bonus).astype(y_ref.dtype)

    def run(*gens, lead_stages=1):
        results = [None] * len(gens)
        live = list(range(len(gens)))
        while live:
            for i in list(live):
                for _ in range(lead_stages if i == 0 else 1):
                    if i not in live:
                        break
                    try:
                        next(gens[i])
                    except StopIteration as stop:
                        results[i] = stop.value
                        live.remove(i)
        return results

    nslabs = zr_ref.shape[1] // rows
    units = [(bj, slice(t * rows, (t + 1) * rows)) for t in range(nslabs) for bj in range(zr_ref.shape[0])]
    prepared, = run(prepare(*units[0]))
    done = None
    for u in range(len(units)):
        gens = [chains(prepared)]
        if u + 1 < len(units):
            gens.append(prepare(*units[u + 1]))
        if done is not None:
            gens.append(finish(done, *units[u - 1]))
        out = run(*gens, lead_stages=RWKV_CHAIN_LEAD_STAGES)
        done = out[0]
        if u + 1 < len(units):
            prepared = out[1]
    run(finish(done, *units[-1]))


def _rwkv(z3d, mu, w0, a0, k_k, k_a, r_k, lnx_w, lnx_b, w2, a2):
    b, s, a_shift = z3d.shape
    br_w = (a_shift - 2 * LORA_W) // 3
    npairs = br_w // LANES
    vec = lambda off: pl.BlockSpec((1, LANES), lambda bi, hp: (0, off + hp))
    lora = pl.BlockSpec((LORA_W, LANES), lambda bi, hp: (0, hp))
    nb = RWKV_BATCH
    assert b % nb == 0 and s % (RWKV_CHUNK * RWKV_GROUP) == 0
    zspec = lambda off: pl.BlockSpec((nb, s, LANES), lambda bi, hp: (bi, 0, off + hp))
    return pl.pallas_call(
        functools.partial(_rwkv_body, group=RWKV_GROUP),
        grid=(b // nb, npairs),
        in_specs=[
            zspec(0), zspec(npairs), zspec(2 * npairs),
            pl.BlockSpec((nb, s, LANES), lambda bi, hp: (bi, 0, 3 * npairs)),
            vec(0), vec(npairs), vec(2 * npairs),
            pl.BlockSpec((1, LANES), lambda bi, hp: (0, 3 * npairs)),
            vec(0), vec(0), vec(0), vec(0), vec(0), vec(0), vec(0), lora, lora,
        ],
        out_specs=pl.BlockSpec((nb, s, LANES), lambda bi, hp: (bi, 0, hp)),
        out_shape=jax.ShapeDtypeStruct((b, s, br_w), ACT_DTYPE),
        scratch_shapes=[pltpu.VMEM((nb, HEAD_DIM, LANES), F32), pltpu.VMEM((nb, SUBLANES, LANES), F32)],
        compiler_params=pltpu.CompilerParams(
            dimension_semantics=("arbitrary", "arbitrary"), vmem_limit_bytes=VMEM_LIMIT_BYTES),
        name="rwkv7_scan",
    )(z3d, z3d, z3d, z3d, mu, mu, mu, mu, w0, a0, k_k, k_a, r_k, lnx_w, lnx_b, w2, a2)


def _alibi_slopes(n):
    def pow2(m):
        start = 2.0 ** (-8.0 / m)
        return [start ** (i + 1) for i in range(m)]
    if math.log2(n).is_integer():
        return pow2(n)
    cl = 2 ** int(math.floor(math.log2(n)))
    return pow2(cl) + pow2(2 * cl)[0::2][: n - cl]


def _diff_attn_body(q_ref, k_ref, v_ref, slope_ref, lq1_ref, lk1_ref, lq2_ref, lk2_ref, sub_ref, o_ref,
                    kb_ref, vt_ref, band_ref, sa_ref, sb_ref, acc_ref, *, tq, lam_init):
    bi = pl.program_id(1)
    nb = q_ref.shape[0]
    tk = tq
    half = tq // 2
    nq = k_ref.shape[1] // tq
    log2e = math.log2(math.e)
    slope2 = slope_ref[0][:, :1] * log2e
    aug_lane = lax.broadcasted_iota(jnp.int32, (tk, LANES), 1)

    for bj in range(nb):
        for t in range(nq):
            rows = slice(t * tk, (t + 1) * tk)
            kb_ref[bj, rows, :LANES] = k_ref[bj, rows, :].astype(BF16)
            vt_ref[bj, :LANES, rows] = v_ref[bj, rows, :].astype(F32).T.astype(BF16)

    @pl.when(bi == 0)
    def _():
        ones_row = lax.broadcasted_iota(jnp.int32, (ATT_V_PAD, tk), 0) == 0
        for t in range(nq):
            rows = slice(t * tk, (t + 1) * tk)
            pos = (lax.broadcasted_iota(jnp.int32, (tk, LANES), 0) + t * tk).astype(F32)
            pieces = _split(slope2 * pos, ATT_BIAS_PIECES)
            aug = jnp.zeros((tk, LANES), F32)
            for i, piece in enumerate(pieces):
                aug = jnp.where(aug_lane == i, piece.astype(F32), aug)
            for bj in range(nb):
                kb_ref[bj, rows, LANES:] = aug.astype(BF16)
                vt_ref[bj, LANES:, rows] = jnp.where(ones_row, 1.0, 0.0).astype(BF16)
        kpos = lax.broadcasted_iota(jnp.int32, (tk, 2 * tq), 0).astype(F32)
        lane2 = lax.broadcasted_iota(jnp.int32, (1, 2 * tq), 1)
        qry = jnp.where(lane2 < tq, lane2, lane2 - tq)
        qlim = ((qry // ATT_CHUNK + 1) * ATT_CHUNK).astype(F32)
        band_ref[...] = jnp.where(
            kpos < qlim, (-2.0 * slope2) * jnp.maximum(kpos - qry.astype(F32), 0.0), MASK_VALUE)

    lane = lax.broadcasted_iota(jnp.int32, (2 * tq, LANES), 1)
    head0 = lane < HEAD_DIM
    first = lax.broadcasted_iota(jnp.int32, (2 * tq, LANES), 0) < tq
    ones_cols = jnp.where(lane < ATT_BIAS_PIECES, 1.0, 0.0)
    lam = (jnp.exp(jnp.sum(lq1_ref[...] * lk1_ref[...], axis=-1, keepdims=True))
           - jnp.exp(jnp.sum(lq2_ref[...] * lk2_ref[...], axis=-1, keepdims=True)) + lam_init)
    late = lambda x: jnp.concatenate([x[..., half:tq], x[..., tq + half:]], axis=-1)

    def update(p_rows, vt, m, m_new, cols):
        pv = _dot(vt, p_rows.astype(BF16))
        off = 0
        for cs in cols:
            w = cs.stop - cs.start
            if m is None:
                acc_ref[:, cs] = pv[:, off:off + w]
            else:
                acc_ref[:, cs] = jnp.exp2(m - m_new)[:, off:off + w] * acc_ref[:, cs] + pv[:, off:off + w]
            off += w

    def tiles_of(n):
        past = [(slice(j * tk, (j + 1) * tk), None, None) for j in range(n)]
        own0 = (slice(n * tq, n * tq + half), None, slice(0, half))
        own1 = (slice(n * tq + half, (n + 1) * tq), "late", slice(half, tq))
        return past + [own0, own1]

    all_cols = [slice(0, 2 * tq)]
    late_cols = [slice(half, tq), slice(tq + half, 2 * tq)]
    bufs = (sa_ref, sb_ref)

    def query_operands(bj, n):
        q = q_ref[bj, n * tq:(n + 1) * tq, :].astype(F32) * (HEAD_DIM ** -0.5 * log2e)
        q2x = jnp.concatenate([q, q], axis=0)
        qcat = jnp.concatenate([jnp.where(first == head0, q2x, 0.0), ones_cols],
                               axis=1).astype(BF16)
        return qcat, jnp.concatenate([qcat[half:tq], qcat[tq + half:]], axis=0)

    def scores(bj, operands, tile, dst_ref):
        keys, which, _ = tile
        nk = keys.stop - keys.start
        if which is None:
            dst_ref[:nk, :] = _dot_nt(kb_ref[bj, keys, :], operands[0])
        else:
            dst_ref[:nk, :tq] = _dot_nt(kb_ref[bj, keys, :], operands[1])

    work = [(bj, n) for bj in range(nb) for n in range(nq)]
    slot = 0
    operands = query_operands(*work[0])
    scores(work[0][0], operands, tiles_of(work[0][1])[0], bufs[slot])
    for w, (bj, n) in enumerate(work):
        tiles = tiles_of(n)
        m = None
        for i, (keys, which, brows) in enumerate(tiles):
            if i + 1 < len(tiles):
                scores(bj, operands, tiles[i + 1], bufs[1 - slot])
            elif w + 1 < len(work):
                operands = query_operands(*work[w + 1])
                scores(work[w + 1][0], operands, tiles_of(work[w + 1][1])[0], bufs[1 - slot])
            src_ref = bufs[slot]
            slot = 1 - slot
            nk = keys.stop - keys.start
            vt = vt_ref[bj, :, keys]
            if which is None:
                s = src_ref[:nk, :]
                if brows is not None:
                    s = s + band_ref[brows, :]
                m_new = jnp.max(s, axis=0, keepdims=True)
                if m is not None:
                    m_new = jnp.maximum(m, m_new)
                update(jnp.exp2(s - m_new), vt, m, m_new, all_cols)
                m = m_new
            else:
                s = src_ref[:nk, :tq] + late(band_ref[brows, :])
                ml = late(m)
                m_new = jnp.maximum(ml, jnp.max(s, axis=0, keepdims=True))
                update(jnp.exp2(s - m_new), vt, ml, m_new, late_cols)
        on = acc_ref[:LANES, :] * (1.0 / acc_ref[LANES:LANES + 1, :])
        o = on[:, :tq] - lam * on[:, tq:]
        ms = jnp.mean(o * o, axis=0, keepdims=True)
        o = o * lax.rsqrt(ms + NORM_EPS) * sub_ref[...] * (1.0 - lam_init)
        o_ref[bj, n * tq:(n + 1) * tq, :] = o.T.astype(o_ref.dtype)


def _diff_attn(q3d, k3d, v3d, slopes, lam_vectors, subln_col, lam_init, tq):
    b, s, br_w = q3d.shape
    nh = br_w // LANES
    nb = ATT_BATCH
    assert b % nb == 0 and s % tq == 0
    lam_spec = pl.BlockSpec((1, HEAD_DIM), lambda h, bi: (0, 0))
    seq_spec = pl.BlockSpec((nb, s, LANES), lambda h, bi: (bi, 0, h))
    return pl.pallas_call(
        functools.partial(_diff_attn_body, tq=tq, lam_init=lam_init),
        grid=(nh, b // nb),
        in_specs=[
            seq_spec, seq_spec, seq_spec,
            pl.BlockSpec((1, 1, LANES), lambda h, bi: (h, 0, 0)),
            lam_spec, lam_spec, lam_spec, lam_spec,
            pl.BlockSpec((LANES, 1), lambda h, bi: (0, 0)),
        ],
        out_specs=seq_spec,
        out_shape=jax.ShapeDtypeStruct((b, s, br_w), ACT_DTYPE),
        scratch_shapes=[pltpu.VMEM((nb, s, 2 * LANES), BF16), pltpu.VMEM((nb, LANES + ATT_V_PAD, s), BF16),
                        pltpu.VMEM((tq, 2 * tq), F32),
                        pltpu.VMEM((tq, 2 * tq), F32), pltpu.VMEM((tq, 2 * tq), F32),
                        pltpu.VMEM((LANES + ATT_V_PAD, 2 * tq), F32)],
        compiler_params=pltpu.CompilerParams(
            dimension_semantics=("arbitrary", "arbitrary"),
            vmem_limit_bytes=VMEM_LIMIT_BYTES),
        name="diff_attn",
    )(q3d, k3d, v3d, slopes, *lam_vectors, subln_col)


def _out_body(ybr_ref, gate_ref, qm_ref, gm_ref, x_ref, km_ref, vm_ref, w_ref, pn_ref, o_ref, wb_ref, *, br_w):
    @pl.when(pl.program_id(0) == 0)
    def _():
        wb_ref[...] = w_ref[0].astype(BF16)

    tm = x_ref.shape[0]
    ml = km_ref.shape[2] // MEM_HEADS
    subs = [slice(r * OUT_SUB_ROWS, (r + 1) * OUT_SUB_ROWS) for r in range(tm // OUT_SUB_ROWS)]
    qscale = HEAD_DIM ** -0.5 * math.log2(math.e)
    s = [_dot_nt((qm_ref[rs, :].astype(F32) * qscale).astype(BF16), km_ref[0, 0]) for rs in subs]
    y_br = (ybr_ref[...].astype(F32) * _silu(gate_ref[...].astype(F32))).astype(BF16)
    pcat = []
    for sr in s:
        ps = []
        for hd in range(MEM_HEADS):
            sh = sr[:, hd * ml:(hd + 1) * ml]
            p = jnp.exp2(sh - jnp.max(sh, axis=-1, keepdims=True))
            ps.append((p * (1.0 / jnp.sum(p, axis=-1, keepdims=True))).astype(BF16))
        pcat.append(jnp.concatenate(ps, axis=1))
    y_mem = [_dot(pc, vm_ref[0, 0]) for pc in pcat]
    y_mem = (jnp.concatenate(y_mem, axis=0) * _silu(gm_ref[...].astype(F32))).astype(BF16)
    y = _dot(y_br, wb_ref[:br_w, :]) + _dot(y_mem, wb_ref[br_w:, :])
    o_ref[...] = x_ref[...] + _rms(y, pn_ref[0])


def _out_proj(ybr, gate, qm, gm, x2d, k_mem, v_mem, layer, w, post_g, seq, tm):
    m, d = x2d.shape
    br_w = ybr.shape[1]
    ml = k_mem.shape[2]
    per_b = seq // tm
    mem_spec = pl.BlockSpec((1, 1, ml, MEM_W), lambda i: (layer, i // per_b, 0, 0))
    row = lambda w: pl.BlockSpec((tm, w), lambda i: (i, 0))
    return pl.pallas_call(
        functools.partial(_out_body, br_w=br_w),
        grid=(m // tm,),
        in_specs=[
            row(br_w), row(br_w), row(MEM_W), row(MEM_W), row(d),
            mem_spec, mem_spec,
            pl.BlockSpec((1, d, d), lambda i: (layer, 0, 0)),
            pl.BlockSpec((1, 1, d), lambda i: (layer, 0, 0)),
        ],
        out_specs=row(d),
        out_shape=jax.ShapeDtypeStruct((m, d), F32),
        scratch_shapes=[pltpu.VMEM((d, d), BF16)],
        compiler_params=pltpu.CompilerParams(
            dimension_semantics=("arbitrary",), vmem_limit_bytes=VMEM_LIMIT_BYTES),
        name="out_proj",
    )(ybr, gate, qm, gm, x2d, k_mem, v_mem, w, post_g)


def kernel(x, mem, pre_norm, post_norm, w_out, mem_norm, w_mem_kv, a_w_in, a_shift_mu, a_w0, a_w2,
           a_a0, a_a2, a_k_k, a_k_a, a_r_k, a_lnx_w, a_lnx_b, kv_norm, w_kv, b_w_in, b_lam_q1,
           b_lam_k1, b_lam_q2, b_lam_k2, b_subln):
    bsz, seq, d = x.shape
    depth = pre_norm.shape[0]
    n_a = a_w_in.shape[0]
    br_w = d - MEM_W
    a_shift = 3 * br_w + 2 * LORA_W
    m = bsz * seq
    tm = 1024
    x2d = x.reshape(m, d)
    slopes = jnp.asarray(
        np.repeat(np.array(_alibi_slopes(br_w // LANES), np.float32)[:, None, None], LANES, axis=2))

    k_mem, v_mem = _mem_kv(mem, mem_norm.reshape(depth, 1, d), w_mem_kv)
    for l in range(depth):
        if l < n_a:
            i = l
            cols = [(0, a_shift), (a_shift, a_shift + br_w), (a_shift + br_w, a_shift + br_w + MEM_W),
                    (a_shift + br_w + MEM_W, a_shift + br_w + 2 * MEM_W)]
            z, gate, q_mem, g_mem = _norm_proj(
                x2d, [((pre_norm.reshape(depth, 1, d), l), a_w_in[i], cols)], tm)
            row = lambda p: p[i].reshape(1, -1)
            y_br = _rwkv(z.reshape(bsz, seq, a_shift), row(a_shift_mu), row(a_w0), row(a_a0), row(a_k_k),
                         row(a_k_a), row(a_r_k), row(a_lnx_w), row(a_lnx_b), a_w2[i], a_a2[i]).reshape(m, br_w)
        else:
            i = l - n_a
            cols = [(0, br_w), (br_w, 2 * br_w), (2 * br_w, 2 * br_w + MEM_W),
                    (2 * br_w + MEM_W, 2 * br_w + 2 * MEM_W)]
            groups = [((pre_norm.reshape(depth, 1, d), l), b_w_in[i], cols)]
            if l == n_a:
                groups.append(((kv_norm.reshape(1, 1, d), 0), w_kv, [(0, br_w), (br_w, 2 * br_w)]))
                q, gate, q_mem, g_mem, k_sh, v_sh = _norm_proj(x2d, groups, tm)
                k_sh = k_sh.reshape(bsz, seq, br_w)
                v_sh = v_sh.reshape(bsz, seq, br_w)
            else:
                q, gate, q_mem, g_mem = _norm_proj(x2d, groups, tm)
            lam_init = 0.8 - 0.6 * math.exp(-0.3 * l)
            lam_vectors = [p[i].reshape(1, HEAD_DIM) for p in (b_lam_q1, b_lam_k1, b_lam_q2, b_lam_k2)]
            y_br = _diff_attn(q.reshape(bsz, seq, br_w), k_sh, v_sh, slopes, lam_vectors,
                              b_subln[i].reshape(LANES, 1), lam_init, ATT_Q_TILE).reshape(m, br_w)
        x2d = _out_proj(y_br, gate, q_mem, g_mem, x2d, k_mem, v_mem, l, w_out,
                        post_norm.reshape(depth, 1, d), seq, OUT_ROWS)
    return x2d.reshape(bsz, seq, d)
```

```python
import functools
import math

import numpy as np
import jax
import jax.numpy as jnp
from jax import lax
from jax.experimental import pallas as pl
from jax.experimental.pallas import tpu as pltpu

F32 = jnp.float32
BF16 = jnp.bfloat16
ACT_DTYPE = BF16

HEAD_DIM = 64
LANES = 128
SUBLANES = 8
MEM_HEADS = 4
MEM_W = MEM_HEADS * HEAD_DIM
LORA_W = 64
ATT_CHUNK = 64
ATT_Q_TILE = 512
ATT_BATCH = 2
ATT_BIAS_PIECES = 3
ATT_V_PAD = 16
RWKV_CHUNK = 64
RWKV_GROUP = 16
RWKV_CUMSUM_ROWS = 128
RWKV_BATCH = 2
RWKV_CHAIN_LEAD_STAGES = 2
OUT_ROWS = 1024
OUT_SUB_ROWS = 512
NORM_EPS = 1e-6
LNX_EPS = 64e-5
KK_NORM_FLOOR = 1e-12
MASK_VALUE = -1e30
VMEM_LIMIT_BYTES = 56 * 1024 * 1024


def _dot(a, b):
    return jnp.dot(a, b, preferred_element_type=F32)


def _dot_nt(a, b):
    return lax.dot_general(a, b, (((1,), (1,)), ((), ())), preferred_element_type=F32)


def _dot_tn(a, b):
    return lax.dot_general(a, b, (((0,), (0,)), ((), ())), preferred_element_type=F32)


def _split(x, pieces):
    out = []
    rem = x
    for i in range(pieces):
        p = rem.astype(BF16)
        out.append(p)
        if i + 1 < pieces:
            rem = rem - p.astype(F32)
    return out


def _mm(a, b, fn=_dot, pa=1, pb=1):
    aps = _split(a, pa)
    bps = _split(b, pb)
    order = max(pa, pb)
    acc = None
    for i, ap in enumerate(aps):
        for j, bp in enumerate(bps):
            if i + j < order:
                t = fn(ap, bp)
                acc = t if acc is None else acc + t
    return acc


def _rms(x, g):
    ms = jnp.mean(x * x, axis=-1, keepdims=True)
    return (x * lax.rsqrt(ms + NORM_EPS)) * g


def _silu(x):
    h = 0.5 * x
    return h + h * jnp.tanh(h)


def _norm_proj_body(x_ref, *refs, groups):
    ng = len(groups)
    nout = sum(len(cols) for cols in groups)
    o_refs = refs[2 * ng:2 * ng + nout]
    wb_refs = refs[2 * ng + nout:]

    @pl.when(pl.program_id(0) == 0)
    def _():
        for gi in range(ng):
            wb_refs[gi][...] = refs[2 * gi + 1][...].astype(BF16)

    x = x_ref[...]
    ms = jnp.mean(x * x, axis=-1, keepdims=True)
    xn = x * lax.rsqrt(ms + NORM_EPS)
    k = 0
    for gi, cols in enumerate(groups):
        h = (xn * refs[2 * gi][0]).astype(BF16)
        for lo, hi in cols:
            o_refs[k][...] = _dot(h, wb_refs[gi][:, lo:hi]).astype(o_refs[k].dtype)
            k += 1


def _norm_proj(x2d, groups, tm):
    m, d = x2d.shape
    cols = [c for _, _, cs in groups for c in cs]
    in_specs = [pl.BlockSpec((tm, d), lambda i: (i, 0))]
    operands = [x2d]
    for (g, row), w, _ in groups:
        in_specs += [pl.BlockSpec((1, 1, d), lambda i, row=row: (row, 0, 0)), pl.BlockSpec(w.shape, lambda i: (0, 0))]
        operands += [g, w]
    return pl.pallas_call(
        functools.partial(_norm_proj_body, groups=tuple(tuple(cs) for _, _, cs in groups)),
        grid=(m // tm,),
        in_specs=in_specs,
        out_specs=[pl.BlockSpec((tm, hi - lo), lambda i: (i, 0)) for lo, hi in cols],
        out_shape=[jax.ShapeDtypeStruct((m, hi - lo), ACT_DTYPE) for lo, hi in cols],
        scratch_shapes=[pltpu.VMEM(w.shape, BF16) for _, w, _ in groups],
        compiler_params=pltpu.CompilerParams(
            dimension_semantics=("arbitrary",), vmem_limit_bytes=VMEM_LIMIT_BYTES),
        name="norm_proj",
    )(*operands)


def _mem_kv_body(mem_ref, g_ref, w_ref, k_ref, v_ref):
    nb, ml, d = mem_ref.shape
    h = _rms(mem_ref[...].reshape(nb * ml, d), g_ref[0]).astype(BF16)
    kv = _dot(h, w_ref[0].astype(BF16))
    lane = lax.broadcasted_iota(jnp.int32, (ml, MEM_W), 1)
    for bj in range(nb):
        rows = slice(bj * ml, (bj + 1) * ml)
        for hd in range(MEM_HEADS):
            in_head = (lane >= hd * HEAD_DIM) & (lane < (hd + 1) * HEAD_DIM)
            k_ref[0, bj, hd * ml:(hd + 1) * ml, :] = jnp.where(in_head, kv[rows, :MEM_W], 0.0).astype(k_ref.dtype)
            v_ref[0, bj, hd * ml:(hd + 1) * ml, :] = jnp.where(in_head, kv[rows, MEM_W:], 0.0).astype(v_ref.dtype)


def _mem_kv(mem, gains, w):
    b, ml, d = mem.shape
    nl = gains.shape[0]
    out_spec = pl.BlockSpec((1, b, MEM_HEADS * ml, MEM_W), lambda l: (l, 0, 0, 0))
    return pl.pallas_call(
        _mem_kv_body,
        grid=(nl,),
        in_specs=[
            pl.BlockSpec((b, ml, d), lambda l: (0, 0, 0)),
            pl.BlockSpec((1, 1, d), lambda l: (l, 0, 0)),
            pl.BlockSpec((1, d, 2 * MEM_W), lambda l: (l, 0, 0)),
        ],
        out_specs=[out_spec] * 2,
        out_shape=[jax.ShapeDtypeStruct((nl, b, MEM_HEADS * ml, MEM_W), ACT_DTYPE)] * 2,
        compiler_params=pltpu.CompilerParams(
            dimension_semantics=("arbitrary",), vmem_limit_bytes=VMEM_LIMIT_BYTES),
        name="mem_kv",
    )(mem, gains, w)


def _rwkv_body(zr_ref, zk_ref, zv_ref, zwa_ref, mur_ref, muk_ref, muv_ref, muwa_ref, w0_ref, a0_ref,
               kk_ref, ka_ref, rk_ref, lnw_ref, lnb_ref, w2_ref, a2_ref, y_ref, s_ref, prev_ref, *, group):
    c = RWKV_CHUNK
    rows = group * c
    lane = lax.broadcasted_iota(jnp.int32, (c, LANES), 1)
    row = lax.broadcasted_iota(jnp.int32, (c, LANES), 0)
    head0 = lane < HEAD_DIM
    scol = jnp.where(head0, lane, lane - HEAD_DIM)
    strict = scol < row
    incl = scol <= row
    diag = scol == row
    r2 = lax.broadcasted_iota(jnp.int32, (LANES, LANES), 0)
    c2 = lax.broadcasted_iota(jnp.int32, (LANES, LANES), 1)
    blockmask = (r2 < HEAD_DIM) == (c2 < HEAD_DIM)
    blockones = jnp.where(blockmask, 1.0, 0.0).astype(BF16)
    cum_rows = min(rows, RWKV_CUMSUM_ROWS)
    tr = lax.broadcasted_iota(jnp.int32, (cum_rows, cum_rows), 0)
    tc = lax.broadcasted_iota(jnp.int32, (cum_rows, cum_rows), 1)
    tril_ones = jnp.where((tc <= tr) & (tc // c == tr // c), 1.0, 0.0).astype(BF16)
    slab_lane = lax.broadcasted_iota(jnp.int32, (rows, LANES), 1)
    slab_head0 = slab_lane < HEAD_DIM
    slab_first = lax.broadcasted_iota(jnp.int32, (rows, LANES), 0) == 0

    def bd(x):
        return jnp.concatenate([jnp.where(head0, x, 0.0), jnp.where(head0, 0.0, x)], axis=0)

    def seg_sum(x):
        return _dot(x.astype(BF16), blockones)

    mu_r, mu_k, mu_v, mu_wa = mur_ref[...], muk_ref[...], muv_ref[...], muwa_ref[...]
    w0, a0 = w0_ref[...], a0_ref[...]
    k_k, k_a, r_k = kk_ref[...], ka_ref[...], rk_ref[...]
    lnx_w, lnx_b = lnw_ref[...], lnb_ref[...]
    zeros = jnp.zeros((LORA_W, LANES), F32)
    w2a2 = jnp.concatenate([jnp.concatenate([w2_ref[...], zeros], axis=1),
                            jnp.concatenate([zeros, a2_ref[...]], axis=1)], axis=0)

    s_ref[...] = jnp.zeros_like(s_ref)
    prev_ref[...] = jnp.zeros_like(prev_ref)

    def shift_mix(ref, bj, slot, sl, mu):
        z = ref[bj, sl, :].astype(F32)
        zp = pltpu.roll(z, 1, axis=0)
        zp = jnp.where(slab_first, prev_ref[bj, slot:slot + 1, :], zp)
        prev_ref[bj, slot:slot + 1, :] = z[rows - 1:rows, :]
        return z + (zp - z) * mu


    def prepare(bj, sl):
        r = shift_mix(zr_ref, bj, 0, sl, mu_r)
        k = shift_mix(zk_ref, bj, 1, sl, mu_k)
        yield
        v = shift_mix(zv_ref, bj, 2, sl, mu_v)
        wa = shift_mix(zwa_ref, bj, 3, sl, mu_wa)
        lora = _mm(jnp.where(slab_head0, jnp.tanh(wa), wa), w2a2)
        yield
        wlog = w0 + lora[:, :LANES]
        nw = -wlog
        w = -(jnp.maximum(nw, 0.0) + jnp.log(1.0 + jnp.exp(-jnp.abs(nw)))) - 0.5
        logw = -jnp.exp(w)
        a = 1.0 / (1.0 + jnp.exp(-(a0 + lora[:, LANES:])))
        kk = k * k_k
        kk = kk * lax.rsqrt(jnp.maximum(seg_sum(kk * kk), KK_NORM_FLOOR))
        yield
        kmod = k * (1.0 + (a - 1.0) * k_a)
        alpha = -kk
        beta = kk * a
        l2 = jnp.concatenate(_split(logw, 2), axis=1)
        cum2 = jnp.concatenate(
            [_dot(tril_ones, l2[i:i + cum_rows]) for i in range(0, rows, cum_rows)], axis=0)
        cum = cum2[:, :LANES] + cum2[:, LANES:]
        yield
        e_neg = jnp.exp(-cum)
        at = alpha * jnp.exp(cum - logw)
        rt = r * jnp.exp(cum)
        bt = beta * e_neg
        kt = kmod * e_neg
        yield
        chunks = [slice(g * c, (g + 1) * c) for g in range(group)]
        cum_c = [cum[cs.stop - 1:cs.stop, :] for cs in chunks]
        e_end = [jnp.exp(cum_c[g] - cum[cs]) for g, cs in enumerate(chunks)]
        per = lambda x: [x[cs] for cs in chunks]
        return dict(
            at=per(at), rt=per(rt), bt=per(bt), kt=per(kt), v=per(v),
            bt_end=[beta[cs] * e_end[g] for g, cs in enumerate(chunks)],
            kt_end=[kmod[cs] * e_end[g] for g, cs in enumerate(chunks)],
            p_c=[jnp.exp(x) for x in cum_c], v_slab=v, rk=r * kmod * r_k)

    def chains(p):
        at, rt, bt, kt, bt_end, kt_end, v = (p[x] for x in ("at", "rt", "bt", "kt", "bt_end", "kt_end", "v"))
        n = range(group)
        lhs = [jnp.concatenate([at[g], rt[g]], axis=0) for g in n]
        xbk = [_mm(lhs[g], jnp.concatenate([bd(bt[g]), bd(kt[g])], axis=0), _dot_nt) for g in n]
        yield
        a_rb = [jnp.where(incl, xbk[g][c:, :LANES], 0.0) for g in n]
        akrk = [jnp.concatenate([jnp.where(strict, xbk[g][:c, LANES:], 0.0),
                                 jnp.where(incl, xbk[g][c:, LANES:], 0.0)], axis=0) for g in n]
        av = [_mm(akrk[g], bd(v[g])) for g in n]
        yield
        apow = [jnp.where(strict, xbk[g][:c, :LANES], 0.0) for g in n]
        tinv = [jnp.where(diag, 1.0, 0.0) + apow[g] for g in n]
        nfac = int(math.log2(c))
        for i in range(1, nfac):
            rhs = [[bd(apow[g])] + ([bd(tinv[g])] if i > 1 else []) for g in n]
            d = [_mm(apow[g], jnp.concatenate(rhs[g], axis=1)) for g in n]
            if i > 1:
                tinv = [tinv[g] + d[g][:, LANES:] for g in n]
            apow = [d[g][:, :LANES] for g in n]
            yield
        tinv = [tinv[g] + _mm(apow[g], bd(tinv[g])) for g in n]
        yield
        x = [_mm(tinv[g], jnp.concatenate([bd(at[g]), bd(av[g][:c])], axis=1)) for g in n]
        wmat = [x[g][:, :LANES] for g in n]
        u0 = [x[g][:, LANES:] for g in n]
        yield
        d2 = [_mm(a_rb[g], jnp.concatenate([bd(wmat[g]), bd(u0[g])], axis=1)) for g in n]
        rp = [rt[g] + d2[g][:, :LANES] for g in n]
        y0 = [d2[g][:, LANES:] + av[g][c:] for g in n]
        yield
        fold = lambda z: jnp.where(head0, z[:HEAD_DIM], z[HEAD_DIM:])
        gp = [fold(_mm(wmat[g], bt_end[g], _dot_tn)) for g in n]
        npart = [fold(_mm(jnp.concatenate([u0[g], v[g]], axis=0),
                          jnp.concatenate([bt_end[g], kt_end[g]], axis=0), _dot_tn)) for g in n]
        yield
        spans = [[(jnp.where(diag, gp[g] + p["p_c"][g], gp[g]), npart[g]) for g in n]]
        while len(spans[-1]) > 1:
            prev = spans[-1]
            nxt = []
            for i in range(0, len(prev), 2):
                (ma, na), (mb, nb) = prev[i], prev[i + 1]
                prod = _mm(jnp.concatenate([ma, na], axis=0), bd(mb))
                nxt.append((prod[:HEAD_DIM], prod[HEAD_DIM:] + nb))
            spans.append(nxt)
            yield
        return dict(rp=rp, y0=y0, spans=spans, v_slab=p["v_slab"], rk=p["rk"])

    def finish(t, bj, sl):
        spans = t["spans"]
        states = {0: s_ref[bj]}
        top = len(spans) - 1
        m_all, n_all = spans[top][0]
        s_ref[bj] = _mm(states[0], bd(m_all)) + n_all
        yield
        for level in range(top, 0, -1):
            width = 1 << level
            for lo in range(0, group, width):
                ma, na = spans[level - 1][lo >> (level - 1)]
                states[lo + width // 2] = _mm(states[lo], bd(ma)) + na
            yield
        y = jnp.concatenate(
            [_mm(t["rp"][g], bd(states[g]), _dot_nt) + t["y0"][g] for g in range(group)], axis=0)
        yield
        mean = seg_sum(y) * (1.0 / HEAD_DIM)
        yc = y - mean
        var = seg_sum(yc * yc) * (1.0 / HEAD_DIM)
        yield
        yn = yc * lax.rsqrt(var + LNX_EPS) * lnx_w + lnx_b
        bonus = seg_sum(t["rk"]) * t["v_slab"]
        y_ref[bj, sl, :] = (yn + bonus).astype(y_ref.dtype)

    def run(*gens, lead_stages=1):
        results = [None] * len(gens)
        live = list(range(len(gens)))
        while live:
            for i in list(live):
                for _ in range(lead_stages if i == 0 else 1):
                    if i not in live:
                        break
                    try:
                        next(gens[i])
                    except StopIteration as stop:
                        results[i] = stop.value
                        live.remove(i)
        return results

    nslabs = zr_ref.shape[1] // rows
    units = [(bj, slice(t * rows, (t + 1) * rows)) for t in range(nslabs) for bj in range(zr_ref.shape[0])]
    prepared, = run(prepare(*units[0]))
    done = None
    for u in range(len(units)):
        gens = [chains(prepared)]
        if u + 1 < len(units):
            gens.append(prepare(*units[u + 1]))
        if done is not None:
            gens.append(finish(done, *units[u - 1]))
        out = run(*gens, lead_stages=RWKV_CHAIN_LEAD_STAGES)
        done = out[0]
        if u + 1 < len(units):
            prepared = out[1]
    run(finish(done, *units[-1]))


def _rwkv(z3d, mu, w0, a0, k_k, k_a, r_k, lnx_w, lnx_b, w2, a2):
    b, s, a_shift = z3d.shape
    br_w = (a_shift - 2 * LORA_W) // 3
    npairs = br_w // LANES
    vec = lambda off: pl.BlockSpec((1, LANES), lambda bi, hp: (0, off + hp))
    lora = pl.BlockSpec((LORA_W, LANES), lambda bi, hp: (0, hp))
    nb = RWKV_BATCH
    assert b % nb == 0 and s % (RWKV_CHUNK * RWKV_GROUP) == 0
    zspec = lambda off: pl.BlockSpec((nb, s, LANES), lambda bi, hp: (bi, 0, off + hp))
    return pl.pallas_call(
        functools.partial(_rwkv_body, group=RWKV_GROUP),
        grid=(b // nb, npairs),
        in_specs=[
            zspec(0), zspec(npairs), zspec(2 * npairs),
            pl.BlockSpec((nb, s, LANES), lambda bi, hp: (bi, 0, 3 * npairs)),
            vec(0), vec(npairs), vec(2 * npairs),
            pl.BlockSpec((1, LANES), lambda bi, hp: (0, 3 * npairs)),
            vec(0), vec(0), vec(0), vec(0), vec(0), vec(0), vec(0), lora, lora,
        ],
        out_specs=pl.BlockSpec((nb, s, LANES), lambda bi, hp: (bi, 0, hp)),
        out_shape=jax.ShapeDtypeStruct((b, s, br_w), ACT_DTYPE),
        scratch_shapes=[pltpu.VMEM((nb, HEAD_DIM, LANES), F32), pltpu.VMEM((nb, SUBLANES, LANES), F32)],
        compiler_params=pltpu.CompilerParams(
            dimension_semantics=("arbitrary", "arbitrary"), vmem_limit_bytes=VMEM_LIMIT_BYTES),
        name="rwkv7_scan",
    )(z3d, z3d, z3d, z3d, mu, mu, mu, mu, w0, a0, k_k, k_a, r_k, lnx_w, lnx_b, w2, a2)


def _alibi_slopes(n):
    def pow2(m):
        start = 2.0 ** (-8.0 / m)
        return [start ** (i + 1) for i in range(m)]
    if math.log2(n).is_integer():
        return pow2(n)
    cl = 2 ** int(math.floor(math.log2(n)))
    return pow2(cl) + pow2(2 * cl)[0::2][: n - cl]


def _diff_attn_body(q_ref, k_ref, v_ref, slope_ref, lq1_ref, lk1_ref, lq2_ref, lk2_ref, sub_ref, o_ref,
                    kb_ref, vt_ref, band_ref, sa_ref, sb_ref, acc_ref, *, tq, lam_init):
    bi = pl.program_id(1)
    nb = q_ref.shape[0]
    tk = tq
    half = tq // 2
    nq = k_ref.shape[1] // tq
    log2e = math.log2(math.e)
    slope2 = slope_ref[0][:, :1] * log2e
    aug_lane = lax.broadcasted_iota(jnp.int32, (tk, LANES), 1)

    def stage_keys(bj, t):
        rows = slice(t * tk, (t + 1) * tk)
        kb_ref[bj, rows, :LANES] = k_ref[bj, rows, :].astype(BF16)
        vt_ref[bj, :LANES, rows] = v_ref[bj, rows, :].astype(F32).T.astype(BF16)

    @pl.when(bi == 0)
    def _():
        ones_row = lax.broadcasted_iota(jnp.int32, (ATT_V_PAD, tk), 0) == 0
        for t in range(nq):
            rows = slice(t * tk, (t + 1) * tk)
            pos = (lax.broadcasted_iota(jnp.int32, (tk, LANES), 0) + t * tk).astype(F32)
            pieces = _split(slope2 * pos, ATT_BIAS_PIECES)
            aug = jnp.zeros((tk, LANES), F32)
            for i, piece in enumerate(pieces):
                aug = jnp.where(aug_lane == i, piece.astype(F32), aug)
            for bj in range(nb):
                kb_ref[bj, rows, LANES:] = aug.astype(BF16)
                vt_ref[bj, LANES:, rows] = jnp.where(ones_row, 1.0, 0.0).astype(BF16)
        kpos = lax.broadcasted_iota(jnp.int32, (tk, 2 * tq), 0).astype(F32)
        lane2 = lax.broadcasted_iota(jnp.int32, (1, 2 * tq), 1)
        qry = jnp.where(lane2 < tq, lane2, lane2 - tq)
        qlim = ((qry // ATT_CHUNK + 1) * ATT_CHUNK).astype(F32)
        band_ref[...] = jnp.where(
            kpos < qlim, (-2.0 * slope2) * jnp.maximum(kpos - qry.astype(F32), 0.0), MASK_VALUE)

    lane = lax.broadcasted_iota(jnp.int32, (2 * tq, LANES), 1)
    head0 = lane < HEAD_DIM
    first = lax.broadcasted_iota(jnp.int32, (2 * tq, LANES), 0) < tq
    ones_cols = jnp.where(lane < ATT_BIAS_PIECES, 1.0, 0.0)
    lam = (jnp.exp(jnp.sum(lq1_ref[...] * lk1_ref[...], axis=-1, keepdims=True))
           - jnp.exp(jnp.sum(lq2_ref[...] * lk2_ref[...], axis=-1, keepdims=True)) + lam_init)
    late = lambda x: jnp.concatenate([x[..., half:tq], x[..., tq + half:]], axis=-1)

    def update(p_rows, vt, m, m_new, cols):
        pv = _dot(vt, p_rows.astype(BF16))
        off = 0
        for cs in cols:
            w = cs.stop - cs.start
            if m is None:
                acc_ref[:, cs] = pv[:, off:off + w]
            else:
                acc_ref[:, cs] = jnp.exp2(m - m_new)[:, off:off + w] * acc_ref[:, cs] + pv[:, off:off + w]
            off += w

    def tiles_of(n):
        past = [(slice(j * tk, (j + 1) * tk), None, None) for j in range(n)]
        own0 = (slice(n * tq, n * tq + half), None, slice(0, half))
        own1 = (slice(n * tq + half, (n + 1) * tq), "late", slice(half, tq))
        return past + [own0, own1]

    all_cols = [slice(0, 2 * tq)]
    late_cols = [slice(half, tq), slice(tq + half, 2 * tq)]
    bufs = (sa_ref, sb_ref)

    def query_operands(bj, n):
        q = q_ref[bj, n * tq:(n + 1) * tq, :].astype(F32) * (HEAD_DIM ** -0.5 * log2e)
        q2x = jnp.concatenate([q, q], axis=0)
        qcat = jnp.concatenate([jnp.where(first == head0, q2x, 0.0), ones_cols],
                               axis=1).astype(BF16)
        return qcat, jnp.concatenate([qcat[half:tq], qcat[tq + half:]], axis=0)

    def scores(bj, operands, tile, dst_ref):
        keys, which, _ = tile
        nk = keys.stop - keys.start
        if which is None:
            dst_ref[:nk, :] = _dot_nt(kb_ref[bj, keys, :], operands[0])
        else:
            dst_ref[:nk, :tq] = _dot_nt(kb_ref[bj, keys, :], operands[1])

    work = [(bj, n) for bj in range(nb) for n in range(nq)]
    slot = 0
    stage_keys(*work[0])
    operands = query_operands(*work[0])
    scores(work[0][0], operands, tiles_of(work[0][1])[0], bufs[slot])
    for w, (bj, n) in enumerate(work):
        tiles = tiles_of(n)
        m = None
        if w + 1 < len(work):
            stage_keys(*work[w + 1])
        for i, (keys, which, brows) in enumerate(tiles):
            if i + 1 < len(tiles):
                scores(bj, operands, tiles[i + 1], bufs[1 - slot])
            elif w + 1 < len(work):
                operands = query_operands(*work[w + 1])
                scores(work[w + 1][0], operands, tiles_of(work[w + 1][1])[0], bufs[1 - slot])
            src_ref = bufs[slot]
            slot = 1 - slot
            nk = keys.stop - keys.start
            vt = vt_ref[bj, :, keys]
            if which is None:
                s = src_ref[:nk, :]
                if brows is not None:
                    s = s + band_ref[brows, :]
                m_new = jnp.max(s, axis=0, keepdims=True)
                if m is not None:
                    m_new = jnp.maximum(m, m_new)
                update(jnp.exp2(s - m_new), vt, m, m_new, all_cols)
                m = m_new
            else:
                s = src_ref[:nk, :tq] + late(band_ref[brows, :])
                ml = late(m)
                m_new = jnp.maximum(ml, jnp.max(s, axis=0, keepdims=True))
                update(jnp.exp2(s - m_new), vt, ml, m_new, late_cols)
        on = acc_ref[:LANES, :] * (1.0 / acc_ref[LANES:LANES + 1, :])
        o = on[:, :tq] - lam * on[:, tq:]
        ms = jnp.mean(o * o, axis=0, keepdims=True)
        o = o * lax.rsqrt(ms + NORM_EPS) * sub_ref[...] * (1.0 - lam_init)
        o_ref[bj, n * tq:(n + 1) * tq, :] = o.T.astype(o_ref.dtype)


def _diff_attn(q3d, k3d, v3d, slopes, lam_vectors, subln_col, lam_init, tq):
    b, s, br_w = q3d.shape
    nh = br_w // LANES
    nb = ATT_BATCH
    assert b % nb == 0 and s % tq == 0
    lam_spec = pl.BlockSpec((1, HEAD_DIM), lambda h, bi: (0, 0))
    seq_spec = pl.BlockSpec((nb, s, LANES), lambda h, bi: (bi, 0, h))
    return pl.pallas_call(
        functools.partial(_diff_attn_body, tq=tq, lam_init=lam_init),
        grid=(nh, b // nb),
        in_specs=[
            seq_spec, seq_spec, seq_spec,
            pl.BlockSpec((1, 1, LANES), lambda h, bi: (h, 0, 0)),
            lam_spec, lam_spec, lam_spec, lam_spec,
            pl.BlockSpec((LANES, 1), lambda h, bi: (0, 0)),
        ],
        out_specs=seq_spec,
        out_shape=jax.ShapeDtypeStruct((b, s, br_w), ACT_DTYPE),
        scratch_shapes=[pltpu.VMEM((nb, s, 2 * LANES), BF16), pltpu.VMEM((nb, LANES + ATT_V_PAD, s), BF16),
                        pltpu.VMEM((tq, 2 * tq), F32),
                        pltpu.VMEM((tq, 2 * tq), F32), pltpu.VMEM((tq, 2 * tq), F32),
                        pltpu.VMEM((LANES + ATT_V_PAD, 2 * tq), F32)],
        compiler_params=pltpu.CompilerParams(
            dimension_semantics=("arbitrary", "arbitrary"),
            vmem_limit_bytes=VMEM_LIMIT_BYTES),
        name="diff_attn",
    )(q3d, k3d, v3d, slopes, *lam_vectors, subln_col)


def _out_body(ybr_ref, gate_ref, qm_ref, gm_ref, x_ref, km_ref, vm_ref, w_ref, pn_ref, o_ref, wb_ref, *, br_w):
    @pl.when(pl.program_id(0) == 0)
    def _():
        wb_ref[...] = w_ref[0].astype(BF16)

    tm = x_ref.shape[0]
    ml = km_ref.shape[2] // MEM_HEADS
    subs = [slice(r * OUT_SUB_ROWS, (r + 1) * OUT_SUB_ROWS) for r in range(tm // OUT_SUB_ROWS)]
    qscale = HEAD_DIM ** -0.5 * math.log2(math.e)
    s = [_dot_nt((qm_ref[rs, :].astype(F32) * qscale).astype(BF16), km_ref[0, 0]) for rs in subs]
    y_br = (ybr_ref[...].astype(F32) * _silu(gate_ref[...].astype(F32))).astype(BF16)
    pcat = []
    for sr in s:
        ps = []
        for hd in range(MEM_HEADS):
            sh = sr[:, hd * ml:(hd + 1) * ml]
            p = jnp.exp2(sh - jnp.max(sh, axis=-1, keepdims=True))
            ps.append((p * (1.0 / jnp.sum(p, axis=-1, keepdims=True))).astype(BF16))
        pcat.append(jnp.concatenate(ps, axis=1))
    y_mem = [_dot(pc, vm_ref[0, 0]) for pc in pcat]
    y_mem = (jnp.concatenate(y_mem, axis=0) * _silu(gm_ref[...].astype(F32))).astype(BF16)
    y = _dot(y_br, wb_ref[:br_w, :]) + _dot(y_mem, wb_ref[br_w:, :])
    o_ref[...] = x_ref[...] + _rms(y, pn_ref[0])


def _out_proj(ybr, gate, qm, gm, x2d, k_mem, v_mem, layer, w, post_g, seq, tm):
    m, d = x2d.shape
    br_w = ybr.shape[1]
    ml = k_mem.shape[2]
    per_b = seq // tm
    mem_spec = pl.BlockSpec((1, 1, ml, MEM_W), lambda i: (layer, i // per_b, 0, 0))
    row = lambda w: pl.BlockSpec((tm, w), lambda i: (i, 0))
    return pl.pallas_call(
        functools.partial(_out_body, br_w=br_w),
        grid=(m // tm,),
        in_specs=[
            row(br_w), row(br_w), row(MEM_W), row(MEM_W), row(d),
            mem_spec, mem_spec,
            pl.BlockSpec((1, d, d), lambda i: (layer, 0, 0)),
            pl.BlockSpec((1, 1, d), lambda i: (layer, 0, 0)),
        ],
        out_specs=row(d),
        out_shape=jax.ShapeDtypeStruct((m, d), F32),
        scratch_shapes=[pltpu.VMEM((d, d), BF16)],
        compiler_params=pltpu.CompilerParams(
            dimension_semantics=("arbitrary",), vmem_limit_bytes=VMEM_LIMIT_BYTES),
        name="out_proj",
    )(ybr, gate, qm, gm, x2d, k_mem, v_mem, w, post_g)


def kernel(x, mem, pre_norm, post_norm, w_out, mem_norm, w_mem_kv, a_w_in, a_shift_mu, a_w0, a_w2,
           a_a0, a_a2, a_k_k, a_k_a, a_r_k, a_lnx_w, a_lnx_b, kv_norm, w_kv, b_w_in, b_lam_q1,
           b_lam_k1, b_lam_q2, b_lam_k2, b_subln):
    bsz, seq, d = x.shape
    depth = pre_norm.shape[0]
    n_a = a_w_in.shape[0]
    br_w = d - MEM_W
    a_shift = 3 * br_w + 2 * LORA_W
    m = bsz * seq
    tm = 1024
    x2d = x.reshape(m, d)
    slopes = jnp.asarray(
        np.repeat(np.array(_alibi_slopes(br_w // LANES), np.float32)[:, None, None], LANES, axis=2))

    k_mem, v_mem = _mem_kv(mem, mem_norm.reshape(depth, 1, d), w_mem_kv)
    for l in range(depth):
        if l < n_a:
            i = l
            cols = [(0, a_shift), (a_shift, a_shift + br_w), (a_shift + br_w, a_shift + br_w + MEM_W),
                    (a_shift + br_w + MEM_W, a_shift + br_w + 2 * MEM_W)]
            z, gate, q_mem, g_mem = _norm_proj(
                x2d, [((pre_norm.reshape(depth, 1, d), l), a_w_in[i], cols)], tm)
            row = lambda p: p[i].reshape(1, -1)
            y_br = _rwkv(z.reshape(bsz, seq, a_shift), row(a_shift_mu), row(a_w0), row(a_a0), row(a_k_k),
                         row(a_k_a), row(a_r_k), row(a_lnx_w), row(a_lnx_b), a_w2[i], a_a2[i]).reshape(m, br_w)
        else:
            i = l - n_a
            cols = [(0, br_w), (br_w, 2 * br_w), (2 * br_w, 2 * br_w + MEM_W),
                    (2 * br_w + MEM_W, 2 * br_w + 2 * MEM_W)]
            groups = [((pre_norm.reshape(depth, 1, d), l), b_w_in[i], cols)]
            if l == n_a:
                groups.append(((kv_norm.reshape(1, 1, d), 0), w_kv, [(0, br_w), (br_w, 2 * br_w)]))
                q, gate, q_mem, g_mem, k_sh, v_sh = _norm_proj(x2d, groups, tm)
                k_sh = k_sh.reshape(bsz, seq, br_w)
                v_sh = v_sh.reshape(bsz, seq, br_w)
            else:
                q, gate, q_mem, g_mem = _norm_proj(x2d, groups, tm)
            lam_init = 0.8 - 0.6 * math.exp(-0.3 * l)
            lam_vectors = [p[i].reshape(1, HEAD_DIM) for p in (b_lam_q1, b_lam_k1, b_lam_q2, b_lam_k2)]
            y_br = _diff_attn(q.reshape(bsz, seq, br_w), k_sh, v_sh, slopes, lam_vectors,
                              b_subln[i].reshape(LANES, 1), lam_init, ATT_Q_TILE).reshape(m, br_w)
        x2d = _out_proj(y_br, gate, q_mem, g_mem, x2d, k_mem, v_mem, l, w_out,
                        post_norm.reshape(depth, 1, d), seq, OUT_ROWS)
    return x2d.reshape(bsz, seq, d)
```

```python
import functools
import math

import numpy as np
import jax
import jax.numpy as jnp
from jax import lax
from jax.experimental import pallas as pl
from jax.experimental.pallas import tpu as pltpu

F32 = jnp.float32
BF16 = jnp.bfloat16
ACT_DTYPE = BF16

HEAD_DIM = 64
LANES = 128
SUBLANES = 8
MEM_HEADS = 4
MEM_W = MEM_HEADS * HEAD_DIM
LORA_W = 64
ATT_CHUNK = 64
ATT_Q_TILE = 512
ATT_BATCH = 2
ATT_BIAS_PIECES = 3
ATT_V_PAD = 16
RWKV_CHUNK = 64
RWKV_GROUP = 16
RWKV_CUMSUM_ROWS = 128
RWKV_BATCH = 2
RWKV_CHAIN_LEAD_STAGES = 2
OUT_ROWS = 1024
OUT_SUB_ROWS = 512
NORM_EPS = 1e-6
LNX_EPS = 64e-5
KK_NORM_FLOOR = 1e-12
MASK_VALUE = -1e30
VMEM_LIMIT_BYTES = 56 * 1024 * 1024


def _dot(a, b):
    return jnp.dot(a, b, preferred_element_type=F32)


def _dot_nt(a, b):
    return lax.dot_general(a, b, (((1,), (1,)), ((), ())), preferred_element_type=F32)


def _dot_tn(a, b):
    return lax.dot_general(a, b, (((0,), (0,)), ((), ())), preferred_element_type=F32)


def _split(x, pieces):
    out = []
    rem = x
    for i in range(pieces):
        p = rem.astype(BF16)
        out.append(p)
        if i + 1 < pieces:
            rem = rem - p.astype(F32)
    return out


def _mm(a, b, fn=_dot, pa=1, pb=1):
    aps = _split(a, pa)
    bps = _split(b, pb)
    order = max(pa, pb)
    acc = None
    for i, ap in enumerate(aps):
        for j, bp in enumerate(bps):
            if i + j < order:
                t = fn(ap, bp)
                acc = t if acc is None else acc + t
    return acc


def _rms(x, g):
    ms = jnp.mean(x * x, axis=-1, keepdims=True)
    return (x * lax.rsqrt(ms + NORM_EPS)) * g


def _silu(x):
    h = 0.5 * x
    return h + h * jnp.tanh(h)


def _norm_proj_body(x_ref, *refs, groups):
    ng = len(groups)
    nout = sum(len(cols) for cols in groups)
    o_refs = refs[2 * ng:2 * ng + nout]
    wb_refs = refs[2 * ng + nout:]

    @pl.when(pl.program_id(0) == 0)
    def _():
        for gi in range(ng):
            wb_refs[gi][...] = refs[2 * gi + 1][...].astype(BF16)

    x = x_ref[...]
    ms = jnp.mean(x * x, axis=-1, keepdims=True)
    xn = x * lax.rsqrt(ms + NORM_EPS)
    k = 0
    for gi, cols in enumerate(groups):
        h = (xn * refs[2 * gi][0]).astype(BF16)
        for lo, hi in cols:
            o_refs[k][...] = _dot(h, wb_refs[gi][:, lo:hi]).astype(o_refs[k].dtype)
            k += 1


def _norm_proj(x2d, groups, tm):
    m, d = x2d.shape
    cols = [c for _, _, cs in groups for c in cs]
    in_specs = [pl.BlockSpec((tm, d), lambda i: (i, 0))]
    operands = [x2d]
    for (g, row), w, _ in groups:
        in_specs += [pl.BlockSpec((1, 1, d), lambda i, row=row: (row, 0, 0)), pl.BlockSpec(w.shape, lambda i: (0, 0))]
        operands += [g, w]
    return pl.pallas_call(
        functools.partial(_norm_proj_body, groups=tuple(tuple(cs) for _, _, cs in groups)),
        grid=(m // tm,),
        in_specs=in_specs,
        out_specs=[pl.BlockSpec((tm, hi - lo), lambda i: (i, 0)) for lo, hi in cols],
        out_shape=[jax.ShapeDtypeStruct((m, hi - lo), ACT_DTYPE) for lo, hi in cols],
        scratch_shapes=[pltpu.VMEM(w.shape, BF16) for _, w, _ in groups],
        compiler_params=pltpu.CompilerParams(
            dimension_semantics=("arbitrary",), vmem_limit_bytes=VMEM_LIMIT_BYTES),
        name="norm_proj",
    )(*operands)


def _mem_kv_body(mem_ref, g_ref, w_ref, k_ref, v_ref):
    nb, ml, d = mem_ref.shape
    h = _rms(mem_ref[...].reshape(nb * ml, d), g_ref[0]).astype(BF16)
    kv = _dot(h, w_ref[0].astype(BF16))
    lane = lax.broadcasted_iota(jnp.int32, (ml, MEM_W), 1)
    for bj in range(nb):
        rows = slice(bj * ml, (bj + 1) * ml)
        for hd in range(MEM_HEADS):
            in_head = (lane >= hd * HEAD_DIM) & (lane < (hd + 1) * HEAD_DIM)
            k_ref[0, bj, hd * ml:(hd + 1) * ml, :] = jnp.where(in_head, kv[rows, :MEM_W], 0.0).astype(k_ref.dtype)
            v_ref[0, bj, hd * ml:(hd + 1) * ml, :] = jnp.where(in_head, kv[rows, MEM_W:], 0.0).astype(v_ref.dtype)


def _mem_kv(mem, gains, w):
    b, ml, d = mem.shape
    nl = gains.shape[0]
    out_spec = pl.BlockSpec((1, b, MEM_HEADS * ml, MEM_W), lambda l: (l, 0, 0, 0))
    return pl.pallas_call(
        _mem_kv_body,
        grid=(nl,),
        in_specs=[
            pl.BlockSpec((b, ml, d), lambda l: (0, 0, 0)),
            pl.BlockSpec((1, 1, d), lambda l: (l, 0, 0)),
            pl.BlockSpec((1, d, 2 * MEM_W), lambda l: (l, 0, 0)),
        ],
        out_specs=[out_spec] * 2,
        out_shape=[jax.ShapeDtypeStruct((nl, b, MEM_HEADS * ml, MEM_W), ACT_DTYPE)] * 2,
        compiler_params=pltpu.CompilerParams(
            dimension_semantics=("arbitrary",), vmem_limit_bytes=VMEM_LIMIT_BYTES),
        name="mem_kv",
    )(mem, gains, w)


def _rwkv_body(zr_ref, zk_ref, zv_ref, zwa_ref, mur_ref, muk_ref, muv_ref, muwa_ref, w0_ref, a0_ref,
               kk_ref, ka_ref, rk_ref, lnw_ref, lnb_ref, w2_ref, a2_ref, y_ref, s_ref, prev_ref, *, group):
    c = RWKV_CHUNK
    rows = group * c
    lane = lax.broadcasted_iota(jnp.int32, (c, LANES), 1)
    row = lax.broadcasted_iota(jnp.int32, (c, LANES), 0)
    head0 = lane < HEAD_DIM
    scol = jnp.where(head0, lane, lane - HEAD_DIM)
    strict = scol < row
    incl = scol <= row
    diag = scol == row
    r2 = lax.broadcasted_iota(jnp.int32, (LANES, LANES), 0)
    c2 = lax.broadcasted_iota(jnp.int32, (LANES, LANES), 1)
    blockmask = (r2 < HEAD_DIM) == (c2 < HEAD_DIM)
    blockones = jnp.where(blockmask, 1.0, 0.0).astype(BF16)
    cum_rows = min(rows, RWKV_CUMSUM_ROWS)
    tr = lax.broadcasted_iota(jnp.int32, (cum_rows, cum_rows), 0)
    tc = lax.broadcasted_iota(jnp.int32, (cum_rows, cum_rows), 1)
    tril_ones = jnp.where((tc <= tr) & (tc // c == tr // c), 1.0, 0.0).astype(BF16)
    slab_lane = lax.broadcasted_iota(jnp.int32, (rows, LANES), 1)
    slab_head0 = slab_lane < HEAD_DIM
    slab_first = lax.broadcasted_iota(jnp.int32, (rows, LANES), 0) == 0

    head0_packed = jnp.where(head0, 1.0, 0.0).astype(BF16) > 0

    def bd(x):
        xb = x.astype(BF16)
        zero = jnp.zeros_like(xb)
        return jnp.concatenate([jnp.where(head0_packed, xb, zero), jnp.where(head0_packed, zero, xb)], axis=0)

    def seg_sum(x):
        return _dot(x.astype(BF16), blockones)

    mu_r, mu_k, mu_v, mu_wa = mur_ref[...], muk_ref[...], muv_ref[...], muwa_ref[...]
    w0, a0 = w0_ref[...], a0_ref[...]
    k_k, k_a, r_k = kk_ref[...], ka_ref[...], rk_ref[...]
    lnx_w, lnx_b = lnw_ref[...], lnb_ref[...]
    zeros = jnp.zeros((LORA_W, LANES), F32)
    w2a2 = jnp.concatenate([jnp.concatenate([w2_ref[...], zeros], axis=1),
                            jnp.concatenate([zeros, a2_ref[...]], axis=1)], axis=0)

    s_ref[...] = jnp.zeros_like(s_ref)
    prev_ref[...] = jnp.zeros_like(prev_ref)

    def shift_mix(ref, bj, slot, sl, mu):
        z = ref[bj, sl, :].astype(F32)
        zp = pltpu.roll(z, 1, axis=0)
        zp = jnp.where(slab_first, prev_ref[bj, slot:slot + 1, :], zp)
        prev_ref[bj, slot:slot + 1, :] = z[rows - 1:rows, :]
        return z + (zp - z) * mu


    def prepare(bj, sl):
        r = shift_mix(zr_ref, bj, 0, sl, mu_r)
        k = shift_mix(zk_ref, bj, 1, sl, mu_k)
        yield
        v = shift_mix(zv_ref, bj, 2, sl, mu_v)
        wa = shift_mix(zwa_ref, bj, 3, sl, mu_wa)
        lora = _mm(jnp.where(slab_head0, jnp.tanh(wa), wa), w2a2)
        yield
        wlog = w0 + lora[:, :LANES]
        nw = -wlog
        w = -(jnp.maximum(nw, 0.0) + jnp.log(1.0 + jnp.exp(-jnp.abs(nw)))) - 0.5
        logw = -jnp.exp(w)
        a = 1.0 / (1.0 + jnp.exp(-(a0 + lora[:, LANES:])))
        kk = k * k_k
        kk = kk * lax.rsqrt(jnp.maximum(seg_sum(kk * kk), KK_NORM_FLOOR))
        yield
        kmod = k * (1.0 + (a - 1.0) * k_a)
        alpha = -kk
        beta = kk * a
        l2 = jnp.concatenate(_split(logw, 2), axis=1)
        cum2 = jnp.concatenate(
            [_dot(tril_ones, l2[i:i + cum_rows]) for i in range(0, rows, cum_rows)], axis=0)
        cum = cum2[:, :LANES] + cum2[:, LANES:]
        yield
        e_neg = jnp.exp(-cum)
        at = alpha * jnp.exp(cum - logw)
        rt = r * jnp.exp(cum)
        bt = beta * e_neg
        kt = kmod * e_neg
        yield
        chunks = [slice(g * c, (g + 1) * c) for g in range(group)]
        cum_c = [cum[cs.stop - 1:cs.stop, :] for cs in chunks]
        e_end = [jnp.exp(cum_c[g] - cum[cs]) for g, cs in enumerate(chunks)]
        per = lambda x: [x[cs] for cs in chunks]
        return dict(
            at=per(at), rt=per(rt), bt=per(bt), kt=per(kt), v=per(v),
            bt_end=[beta[cs] * e_end[g] for g, cs in enumerate(chunks)],
            kt_end=[kmod[cs] * e_end[g] for g, cs in enumerate(chunks)],
            p_c=[jnp.exp(x) for x in cum_c], v_slab=v, rk=r * kmod * r_k)

    def chains(p):
        at, rt, bt, kt, bt_end, kt_end, v = (p[x] for x in ("at", "rt", "bt", "kt", "bt_end", "kt_end", "v"))
        n = range(group)
        lhs = [jnp.concatenate([at[g], rt[g]], axis=0) for g in n]
        xbk = [_mm(lhs[g], jnp.concatenate([bd(bt[g]), bd(kt[g])], axis=0), _dot_nt) for g in n]
        yield
        a_rb = [jnp.where(incl, xbk[g][c:, :LANES], 0.0) for g in n]
        akrk = [jnp.concatenate([jnp.where(strict, xbk[g][:c, LANES:], 0.0),
                                 jnp.where(incl, xbk[g][c:, LANES:], 0.0)], axis=0) for g in n]
        av = [_mm(akrk[g], bd(v[g])) for g in n]
        yield
        apow = [jnp.where(strict, xbk[g][:c, :LANES], 0.0) for g in n]
        tinv = [jnp.where(diag, 1.0, 0.0) + apow[g] for g in n]
        nfac = int(math.log2(c))
        for i in range(1, nfac):
            rhs = [[bd(apow[g])] + ([bd(tinv[g])] if i > 1 else []) for g in n]
            d = [_mm(apow[g], jnp.concatenate(rhs[g], axis=1)) for g in n]
            if i > 1:
                tinv = [tinv[g] + d[g][:, LANES:] for g in n]
            apow = [d[g][:, :LANES] for g in n]
            yield
        tinv = [tinv[g] + _mm(apow[g], bd(tinv[g])) for g in n]
        yield
        x = [_mm(tinv[g], jnp.concatenate([bd(at[g]), bd(av[g][:c])], axis=1)) for g in n]
        wmat = [x[g][:, :LANES] for g in n]
        u0 = [x[g][:, LANES:] for g in n]
        yield
        d2 = [_mm(a_rb[g], jnp.concatenate([bd(wmat[g]), bd(u0[g])], axis=1)) for g in n]
        rp = [rt[g] + d2[g][:, :LANES] for g in n]
        y0 = [d2[g][:, LANES:] + av[g][c:] for g in n]
        yield
        fold = lambda z: jnp.where(head0, z[:HEAD_DIM], z[HEAD_DIM:])
        gp = [fold(_mm(wmat[g], bt_end[g], _dot_tn)) for g in n]
        npart = [fold(_mm(jnp.concatenate([u0[g], v[g]], axis=0),
                          jnp.concatenate([bt_end[g], kt_end[g]], axis=0), _dot_tn)) for g in n]
        yield
        spans = [[(jnp.where(diag, gp[g] + p["p_c"][g], gp[g]), npart[g]) for g in n]]
        while len(spans[-1]) > 1:
            prev = spans[-1]
            nxt = []
            for i in range(0, len(prev), 2):
                (ma, na), (mb, nb) = prev[i], prev[i + 1]
                prod = _mm(jnp.concatenate([ma, na], axis=0), bd(mb))
                nxt.append((prod[:HEAD_DIM], prod[HEAD_DIM:] + nb))
            spans.append(nxt)
            yield
        return dict(rp=rp, y0=y0, spans=spans, v_slab=p["v_slab"], rk=p["rk"])

    def finish(t, bj, sl):
        spans = t["spans"]
        states = {0: s_ref[bj]}
        top = len(spans) - 1
        m_all, n_all = spans[top][0]
        s_ref[bj] = _mm(states[0], bd(m_all)) + n_all
        yield
        for level in range(top, 0, -1):
            width = 1 << level
            for lo in range(0, group, width):
                ma, na = spans[level - 1][lo >> (level - 1)]
                states[lo + width // 2] = _mm(states[lo], bd(ma)) + na
            yield
        y = jnp.concatenate(
            [_mm(t["rp"][g], bd(states[g]), _dot_nt) + t["y0"][g] for g in range(group)], axis=0)
        yield
        mean = seg_sum(y) * (1.0 / HEAD_DIM)
        yc = y - mean
        var = seg_sum(yc * yc) * (1.0 / HEAD_DIM)
        yield
        yn = yc * lax.rsqrt(var + LNX_EPS) * lnx_w + lnx_b
        bonus = seg_sum(t["rk"]) * t["v_slab"]
        y_ref[bj, sl, :] = (yn + bonus).astype(y_ref.dtype)

    def run(*gens, lead_stages=1):
        results = [None] * len(gens)
        live = list(range(len(gens)))
        while live:
            for i in list(live):
                for _ in range(lead_stages if i == 0 else 1):
                    if i not in live:
                        break
                    try:
                        next(gens[i])
                    except StopIteration as stop:
                        results[i] = stop.value
                        live.remove(i)
        return results

    nslabs = zr_ref.shape[1] // rows
    units = [(bj, slice(t * rows, (t + 1) * rows)) for t in range(nslabs) for bj in range(zr_ref.shape[0])]
    prepared, = run(prepare(*units[0]))
    done = None
    for u in range(len(units)):
        gens = [chains(prepared)]
        if u + 1 < len(units):
            gens.append(prepare(*units[u + 1]))
        if done is not None:
            gens.append(finish(done, *units[u - 1]))
        out = run(*gens, lead_stages=RWKV_CHAIN_LEAD_STAGES)
        done = out[0]
        if u + 1 < len(units):
            prepared = out[1]
    run(finish(done, *units[-1]))


def _rwkv(z3d, mu, w0, a0, k_k, k_a, r_k, lnx_w, lnx_b, w2, a2):
    b, s, a_shift = z3d.shape
    br_w = (a_shift - 2 * LORA_W) // 3
    npairs = br_w // LANES
    vec = lambda off: pl.BlockSpec((1, LANES), lambda bi, hp: (0, off + hp))
    lora = pl.BlockSpec((LORA_W, LANES), lambda bi, hp: (0, hp))
    nb = RWKV_BATCH
    assert b % nb == 0 and s % (RWKV_CHUNK * RWKV_GROUP) == 0
    zspec = lambda off: pl.BlockSpec((nb, s, LANES), lambda bi, hp: (bi, 0, off + hp))
    return pl.pallas_call(
        functools.partial(_rwkv_body, group=RWKV_GROUP),
        grid=(b // nb, npairs),
        in_specs=[
            zspec(0), zspec(npairs), zspec(2 * npairs),
            pl.BlockSpec((nb, s, LANES), lambda bi, hp: (bi, 0, 3 * npairs)),
            vec(0), vec(npairs), vec(2 * npairs),
            pl.BlockSpec((1, LANES), lambda bi, hp: (0, 3 * npairs)),
            vec(0), vec(0), vec(0), vec(0), vec(0), vec(0), vec(0), lora, lora,
        ],
        out_specs=pl.BlockSpec((nb, s, LANES), lambda bi, hp: (bi, 0, hp)),
        out_shape=jax.ShapeDtypeStruct((b, s, br_w), ACT_DTYPE),
        scratch_shapes=[pltpu.VMEM((nb, HEAD_DIM, LANES), F32), pltpu.VMEM((nb, SUBLANES, LANES), F32)],
        compiler_params=pltpu.CompilerParams(
            dimension_semantics=("arbitrary", "arbitrary"), vmem_limit_bytes=VMEM_LIMIT_BYTES),
        name="rwkv7_scan",
    )(z3d, z3d, z3d, z3d, mu, mu, mu, mu, w0, a0, k_k, k_a, r_k, lnx_w, lnx_b, w2, a2)


def _alibi_slopes(n):
    def pow2(m):
        start = 2.0 ** (-8.0 / m)
        return [start ** (i + 1) for i in range(m)]
    if math.log2(n).is_integer():
        return pow2(n)
    cl = 2 ** int(math.floor(math.log2(n)))
    return pow2(cl) + pow2(2 * cl)[0::2][: n - cl]


def _diff_attn_body(q_ref, k_ref, v_ref, slope_ref, lq1_ref, lk1_ref, lq2_ref, lk2_ref, sub_ref, o_ref,
                    kb_ref, vt_ref, band_ref, sa_ref, sb_ref, acc_ref, *, tq, lam_init):
    bi = pl.program_id(1)
    nb = q_ref.shape[0]
    tk = tq
    half = tq // 2
    nq = k_ref.shape[1] // tq
    log2e = math.log2(math.e)
    slope2 = slope_ref[0][:, :1] * log2e
    aug_lane = lax.broadcasted_iota(jnp.int32, (tk, LANES), 1)

    def stage_keys(bj, t):
        rows = slice(t * tk, (t + 1) * tk)
        kb_ref[bj, rows, :LANES] = k_ref[bj, rows, :].astype(BF16)
        vt_ref[bj, :LANES, rows] = v_ref[bj, rows, :].astype(F32).T.astype(BF16)

    @pl.when(bi == 0)
    def _():
        ones_row = lax.broadcasted_iota(jnp.int32, (ATT_V_PAD, tk), 0) == 0
        for t in range(nq):
            rows = slice(t * tk, (t + 1) * tk)
            pos = (lax.broadcasted_iota(jnp.int32, (tk, LANES), 0) + t * tk).astype(F32)
            pieces = _split(slope2 * pos, ATT_BIAS_PIECES)
            aug = jnp.zeros((tk, LANES), F32)
            for i, piece in enumerate(pieces):
                aug = jnp.where(aug_lane == i, piece.astype(F32), aug)
            for bj in range(nb):
                kb_ref[bj, rows, LANES:] = aug.astype(BF16)
                vt_ref[bj, LANES:, rows] = jnp.where(ones_row, 1.0, 0.0).astype(BF16)
        kpos = lax.broadcasted_iota(jnp.int32, (tk, 2 * tq), 0).astype(F32)
        lane2 = lax.broadcasted_iota(jnp.int32, (1, 2 * tq), 1)
        qry = jnp.where(lane2 < tq, lane2, lane2 - tq)
        qlim = ((qry // ATT_CHUNK + 1) * ATT_CHUNK).astype(F32)
        band_ref[...] = jnp.where(
            kpos < qlim, (-2.0 * slope2) * jnp.maximum(kpos - qry.astype(F32), 0.0), MASK_VALUE)

    lane = lax.broadcasted_iota(jnp.int32, (2 * tq, LANES), 1)
    head0 = lane < HEAD_DIM
    first = lax.broadcasted_iota(jnp.int32, (2 * tq, LANES), 0) < tq
    ones_cols = jnp.where(lane < ATT_BIAS_PIECES, 1.0, 0.0)
    lam = (jnp.exp(jnp.sum(lq1_ref[...] * lk1_ref[...], axis=-1, keepdims=True))
           - jnp.exp(jnp.sum(lq2_ref[...] * lk2_ref[...], axis=-1, keepdims=True)) + lam_init)
    late = lambda x: jnp.concatenate([x[..., half:tq], x[..., tq + half:]], axis=-1)

    def update(p_rows, vt, m, m_new, cols):
        pv = _dot(vt, p_rows.astype(BF16))
        off = 0
        for cs in cols:
            w = cs.stop - cs.start
            if m is None:
                acc_ref[:, cs] = pv[:, off:off + w]
            else:
                acc_ref[:, cs] = jnp.exp2(m - m_new)[:, off:off + w] * acc_ref[:, cs] + pv[:, off:off + w]
            off += w

    def tiles_of(n):
        past = [(slice(j * tk, (j + 1) * tk), None, None) for j in range(n)]
        own0 = (slice(n * tq, n * tq + half), None, slice(0, half))
        own1 = (slice(n * tq + half, (n + 1) * tq), "late", slice(half, tq))
        return past + [own0, own1]

    all_cols = [slice(0, 2 * tq)]
    late_cols = [slice(half, tq), slice(tq + half, 2 * tq)]
    bufs = (sa_ref, sb_ref)

    def query_operands(bj, n):
        q = q_ref[bj, n * tq:(n + 1) * tq, :].astype(F32) * (HEAD_DIM ** -0.5 * log2e)
        q2x = jnp.concatenate([q, q], axis=0)
        qcat = jnp.concatenate([jnp.where(first == head0, q2x, 0.0), ones_cols],
                               axis=1).astype(BF16)
        return qcat, jnp.concatenate([qcat[half:tq], qcat[tq + half:]], axis=0)

    def scores(bj, operands, tile, dst_ref):
        keys, which, _ = tile
        nk = keys.stop - keys.start
        if which is None:
            dst_ref[:nk, :] = _dot_nt(kb_ref[bj, keys, :], operands[0])
        else:
            dst_ref[:nk, :tq] = _dot_nt(kb_ref[bj, keys, :], operands[1])

    work = [(bj, n) for bj in range(nb) for n in range(nq)]
    slot = 0
    stage_keys(*work[0])
    operands = query_operands(*work[0])
    scores(work[0][0], operands, tiles_of(work[0][1])[0], bufs[slot])
    for w, (bj, n) in enumerate(work):
        tiles = tiles_of(n)
        m = None
        if w + 1 < len(work):
            stage_keys(*work[w + 1])
        for i, (keys, which, brows) in enumerate(tiles):
            if i + 1 < len(tiles):
                scores(bj, operands, tiles[i + 1], bufs[1 - slot])
            elif w + 1 < len(work):
                operands = query_operands(*work[w + 1])
                scores(work[w + 1][0], operands, tiles_of(work[w + 1][1])[0], bufs[1 - slot])
            src_ref = bufs[slot]
            slot = 1 - slot
            nk = keys.stop - keys.start
            vt = vt_ref[bj, :, keys]
            if which is None:
                s = src_ref[:nk, :]
                if brows is not None:
                    s = s + band_ref[brows, :]
                m_new = jnp.max(s, axis=0, keepdims=True)
                if m is not None:
                    m_new = jnp.maximum(m, m_new)
                update(jnp.exp2(s - m_new), vt, m, m_new, all_cols)
                m = m_new
            else:
                s = src_ref[:nk, :tq] + late(band_ref[brows, :])
                ml = late(m)
                m_new = jnp.maximum(ml, jnp.max(s, axis=0, keepdims=True))
                update(jnp.exp2(s - m_new), vt, ml, m_new, late_cols)
        on = acc_ref[:LANES, :] * (1.0 / acc_ref[LANES:LANES + 1, :])
        o = on[:, :tq] - lam * on[:, tq:]
        ms = jnp.mean(o * o, axis=0, keepdims=True)
        o = o * lax.rsqrt(ms + NORM_EPS) * sub_ref[...] * (1.0 - lam_init)
        o_ref[bj, n * tq:(n + 1) * tq, :] = o.T.astype(o_ref.dtype)


def _diff_attn(q3d, k3d, v3d, slopes, lam_vectors, subln_col, lam_init, tq):
    b, s, br_w = q3d.shape
    nh = br_w // LANES
    nb = ATT_BATCH
    assert b % nb == 0 and s % tq == 0
    lam_spec = pl.BlockSpec((1, HEAD_DIM), lambda h, bi: (0, 0))
    seq_spec = pl.BlockSpec((nb, s, LANES), lambda h, bi: (bi, 0, h))
    return pl.pallas_call(
        functools.partial(_diff_attn_body, tq=tq, lam_init=lam_init),
        grid=(nh, b // nb),
        in_specs=[
            seq_spec, seq_spec, seq_spec,
            pl.BlockSpec((1, 1, LANES), lambda h, bi: (h, 0, 0)),
            lam_spec, lam_spec, lam_spec, lam_spec,
            pl.BlockSpec((LANES, 1), lambda h, bi: (0, 0)),
        ],
        out_specs=seq_spec,
        out_shape=jax.ShapeDtypeStruct((b, s, br_w), ACT_DTYPE),
        scratch_shapes=[pltpu.VMEM((nb, s, 2 * LANES), BF16), pltpu.VMEM((nb, LANES + ATT_V_PAD, s), BF16),
                        pltpu.VMEM((tq, 2 * tq), F32),
                        pltpu.VMEM((tq, 2 * tq), F32), pltpu.VMEM((tq, 2 * tq), F32),
                        pltpu.VMEM((LANES + ATT_V_PAD, 2 * tq), F32)],
        compiler_params=pltpu.CompilerParams(
            dimension_semantics=("arbitrary", "arbitrary"),
            vmem_limit_bytes=VMEM_LIMIT_BYTES),
        name="diff_attn",
    )(q3d, k3d, v3d, slopes, *lam_vectors, subln_col)


def _out_body(ybr_ref, gate_ref, qm_ref, gm_ref, x_ref, km_ref, vm_ref, w_ref, pn_ref, o_ref, wb_ref, *, br_w):
    @pl.when(pl.program_id(0) == 0)
    def _():
        wb_ref[...] = w_ref[0].astype(BF16)

    tm = x_ref.shape[0]
    ml = km_ref.shape[2] // MEM_HEADS
    subs = [slice(r * OUT_SUB_ROWS, (r + 1) * OUT_SUB_ROWS) for r in range(tm // OUT_SUB_ROWS)]
    qscale = HEAD_DIM ** -0.5 * math.log2(math.e)
    s = [_dot_nt((qm_ref[rs, :].astype(F32) * qscale).astype(BF16), km_ref[0, 0]) for rs in subs]
    y_br = (ybr_ref[...].astype(F32) * _silu(gate_ref[...].astype(F32))).astype(BF16)
    pcat = []
    for sr in s:
        ps = []
        for hd in range(MEM_HEADS):
            sh = sr[:, hd * ml:(hd + 1) * ml]
            p = jnp.exp2(sh - jnp.max(sh, axis=-1, keepdims=True))
            ps.append((p * (1.0 / jnp.sum(p, axis=-1, keepdims=True))).astype(BF16))
        pcat.append(jnp.concatenate(ps, axis=1))
    y_mem = [_dot(pc, vm_ref[0, 0]) for pc in pcat]
    y_mem = (jnp.concatenate(y_mem, axis=0) * _silu(gm_ref[...].astype(F32))).astype(BF16)
    y = _dot(y_br, wb_ref[:br_w, :]) + _dot(y_mem, wb_ref[br_w:, :])
    o_ref[...] = x_ref[...] + _rms(y, pn_ref[0])


def _out_proj(ybr, gate, qm, gm, x2d, k_mem, v_mem, layer, w, post_g, seq, tm):
    m, d = x2d.shape
    br_w = ybr.shape[1]
    ml = k_mem.shape[2]
    per_b = seq // tm
    mem_spec = pl.BlockSpec((1, 1, ml, MEM_W), lambda i: (layer, i // per_b, 0, 0))
    row = lambda w: pl.BlockSpec((tm, w), lambda i: (i, 0))
    return pl.pallas_call(
        functools.partial(_out_body, br_w=br_w),
        grid=(m // tm,),
        in_specs=[
            row(br_w), row(br_w), row(MEM_W), row(MEM_W), row(d),
            mem_spec, mem_spec,
            pl.BlockSpec((1, d, d), lambda i: (layer, 0, 0)),
            pl.BlockSpec((1, 1, d), lambda i: (layer, 0, 0)),
        ],
        out_specs=row(d),
        out_shape=jax.ShapeDtypeStruct((m, d), F32),
        scratch_shapes=[pltpu.VMEM((d, d), BF16)],
        compiler_params=pltpu.CompilerParams(
            dimension_semantics=("arbitrary",), vmem_limit_bytes=VMEM_LIMIT_BYTES),
        name="out_proj",
    )(ybr, gate, qm, gm, x2d, k_mem, v_mem, w, post_g)


def kernel(x, mem, pre_norm, post_norm, w_out, mem_norm, w_mem_kv, a_w_in, a_shift_mu, a_w0, a_w2,
           a_a0, a_a2, a_k_k, a_k_a, a_r_k, a_lnx_w, a_lnx_b, kv_norm, w_kv, b_w_in, b_lam_q1,
           b_lam_k1, b_lam_q2, b_lam_k2, b_subln):
    bsz, seq, d = x.shape
    depth = pre_norm.shape[0]
    n_a = a_w_in.shape[0]
    br_w = d - MEM_W
    a_shift = 3 * br_w + 2 * LORA_W
    m = bsz * seq
    tm = 1024
    x2d = x.reshape(m, d)
    slopes = jnp.asarray(
        np.repeat(np.array(_alibi_slopes(br_w // LANES), np.float32)[:, None, None], LANES, axis=2))

    k_mem, v_mem = _mem_kv(mem, mem_norm.reshape(depth, 1, d), w_mem_kv)
    for l in range(depth):
        if l < n_a:
            i = l
            cols = [(0, a_shift), (a_shift, a_shift + br_w), (a_shift + br_w, a_shift + br_w + MEM_W),
                    (a_shift + br_w + MEM_W, a_shift + br_w + 2 * MEM_W)]
            z, gate, q_mem, g_mem = _norm_proj(
                x2d, [((pre_norm.reshape(depth, 1, d), l), a_w_in[i], cols)], tm)
            row = lambda p: p[i].reshape(1, -1)
            y_br = _rwkv(z.reshape(bsz, seq, a_shift), row(a_shift_mu), row(a_w0), row(a_a0), row(a_k_k),
                         row(a_k_a), row(a_r_k), row(a_lnx_w), row(a_lnx_b), a_w2[i], a_a2[i]).reshape(m, br_w)
        else:
            i = l - n_a
            cols = [(0, br_w), (br_w, 2 * br_w), (2 * br_w, 2 * br_w + MEM_W),
                    (2 * br_w + MEM_W, 2 * br_w + 2 * MEM_W)]
            groups = [((pre_norm.reshape(depth, 1, d), l), b_w_in[i], cols)]
            if l == n_a:
                groups.append(((kv_norm.reshape(1, 1, d), 0), w_kv, [(0, br_w), (br_w, 2 * br_w)]))
                q, gate, q_mem, g_mem, k_sh, v_sh = _norm_proj(x2d, groups, tm)
                k_sh = k_sh.reshape(bsz, seq, br_w)
                v_sh = v_sh.reshape(bsz, seq, br_w)
            else:
                q, gate, q_mem, g_mem = _norm_proj(x2d, groups, tm)
            lam_init = 0.8 - 0.6 * math.exp(-0.3 * l)
            lam_vectors = [p[i].reshape(1, HEAD_DIM) for p in (b_lam_q1, b_lam_k1, b_lam_q2, b_lam_k2)]
            y_br = _diff_attn(q.reshape(bsz, seq, br_w), k_sh, v_sh, slopes, lam_vectors,
                              b_subln[i].reshape(LANES, 1), lam_init, ATT_Q_TILE).reshape(m, br_w)
        x2d = _out_proj(y_br, gate, q_mem, g_mem, x2d, k_mem, v_mem, l, w_out,
                        post_norm.reshape(depth, 1, d), seq, OUT_ROWS)
    return x2d.reshape(bsz, seq, d)
```
